```python
import math
import jax, jax.numpy as jnp
from jax import lax
import numpy as np

D_MODEL = 1024
BATCH = 2
SEQ = 16384
DEPTH = 1
DEC_BATCH = 16
DEC_SEQ = 2048
PAST_LEN = 128

HEAD_DIM = 64
N_Q_HEADS = 8
N_KV_HEADS = 2
Q_PER_KV = N_Q_HEADS // N_KV_HEADS
ATTN_WIDTH = N_Q_HEADS * HEAD_DIM
KV_WIDTH = N_KV_HEADS * HEAD_DIM
WINDOW = 128
BLOCK = 128
ROPE_THETA = 500000.0
ROPE_DIM = HEAD_DIM // 4

SSM_WIDTH = D_MODEL // 2
SSM_GROUP = 16
SSM_N_GROUPS = SSM_WIDTH // SSM_GROUP
SSM_STATE = 64
N_DIR = 2

MIX_WIDTH = ATTN_WIDTH + SSM_WIDTH
IN_WIDTH = ATTN_WIDTH + 2 * KV_WIDTH + SSM_WIDTH

N_EXPERTS = 32
TOP_K = 4
D_EXPERT = D_MODEL
SWIGLU_LIMIT = 7.0
SWIGLU_ALPHA = 1.702
MOE_BLOCK = 256

EPS = 1e-5

kernel_name = 'hymba_s5_swa_moe_encoder'


def _rmsnorm(x, g):
    xf = x.astype(jnp.float32)
    y = xf * lax.rsqrt(jnp.mean(xf * xf, axis=-1, keepdims=True) + EPS)
    return (y * g.astype(jnp.float32)).astype(x.dtype)


def _rotary(x):
    L = x.shape[1]
    half = ROPE_DIM // 2
    inv_freq = ROPE_THETA ** (-jnp.arange(half, dtype=jnp.float32) * 2.0 / ROPE_DIM)
    ang = jnp.arange(L, dtype=jnp.float32)[:, None] * inv_freq[None, :]
    cos = jnp.cos(ang)[None, :, None, :]
    sin = jnp.sin(ang)[None, :, None, :]
    xf = x.astype(jnp.float32)
    x1 = xf[..., :half]
    x2 = xf[..., half:ROPE_DIM]
    out = jnp.concatenate([x1 * cos - x2 * sin, x2 * cos + x1 * sin, xf[..., ROPE_DIM:]], axis=-1)
    return out.astype(x.dtype)


def _banded_attention(q, k, v, sink):
    N, L = q.shape[0], q.shape[1]
    nb = L // BLOCK
    qb = q.reshape(N, nb, BLOCK, N_KV_HEADS, Q_PER_KV, HEAD_DIM)

    def windows(t):
        tp = jnp.pad(t, ((0, 0), (BLOCK, BLOCK), (0, 0), (0, 0)))
        tp = tp.reshape(N, nb + 2, BLOCK, N_KV_HEADS, HEAD_DIM)
        return jnp.concatenate([tp[:, :-2], tp[:, 1:-1], tp[:, 2:]], axis=2)

    kw = windows(k)
    vw = windows(v)
    s = jnp.einsum('nbqhgd,nbkhd->nbhgqk', qb, kw).astype(jnp.float32) * (HEAD_DIM ** -0.5)
    qi = jnp.arange(BLOCK)
    kj = jnp.arange(3 * BLOCK)
    rel = kj[None, :] - BLOCK - qi[:, None]
    kpos = jnp.arange(nb)[:, None] * BLOCK - BLOCK + kj[None, :]
    valid = (jnp.abs(rel) <= WINDOW)[None, :, :] & ((kpos >= 0) & (kpos < L))[:, None, :]
    s = jnp.where(valid[None, :, None, None], s, -1e30)
    sink_l = sink.astype(jnp.float32).reshape(N_KV_HEADS, Q_PER_KV)[None, None, :, :, None, None]
    m = jnp.maximum(jnp.max(s, axis=-1, keepdims=True), sink_l)
    p = jnp.exp(s - m)
    denom = jnp.sum(p, axis=-1, keepdims=True) + jnp.exp(sink_l - m)
    p = (p / denom).astype(v.dtype)
    o = jnp.einsum('nbhgqk,nbkhd->nbqhgd', p, vw)
    return o.reshape(N, L, ATTN_WIDTH)


def _s5_direction(ug, a_re, a_im, log_dt, b_re, b_im, c_re, c_im, reverse):
    a_re = a_re.astype(jnp.float32)
    a_im = a_im.astype(jnp.float32)
    dt = jnp.exp(log_dt.astype(jnp.float32))[:, None]
    mag = jnp.exp(a_re * dt)
    ab_re = mag * jnp.cos(a_im * dt)
    ab_im = mag * jnp.sin(a_im * dt)
    den = a_re * a_re + a_im * a_im
    num_re = ab_re - 1.0
    num_im = ab_im
    f_re = (num_re * a_re + num_im * a_im) / den
    f_im = (num_im * a_re - num_re * a_im) / den
    b_re = b_re.astype(jnp.float32)
    b_im = b_im.astype(jnp.float32)
    bb_re = f_re[..., None] * b_re - f_im[..., None] * b_im
    bb_im = f_re[..., None] * b_im + f_im[..., None] * b_re
    x_re = jnp.einsum('nlgc,gpc->nlgp', ug, bb_re)
    x_im = jnp.einsum('nlgc,gpc->nlgp', ug, bb_im)
    a_full_re = jnp.broadcast_to(ab_re, x_re.shape)
    a_full_im = jnp.broadcast_to(ab_im, x_re.shape)

    def combine(e1, e2):
        a1r, a1i, b1r, b1i = e1
        a2r, a2i, b2r, b2i = e2
        return (a1r * a2r - a1i * a2i,
                a1r * a2i + a1i * a2r,
                a2r * b1r - a2i * b1i + b2r,
                a2r * b1i + a2i * b1r + b2i)

    _, _, h_re, h_im = lax.associative_scan(combine, (a_full_re, a_full_im, x_re, x_im), axis=1, reverse=reverse)
    return (jnp.einsum('nlgp,gcp->nlgc', h_re, c_re.astype(jnp.float32))
            - jnp.einsum('nlgp,gcp->nlgc', h_im, c_im.astype(jnp.float32)))


def _s5(u, a_re, a_im, log_dt, b_re, b_im, c_re, c_im, ssm_d, glu_w, glu_b):
    N, L = u.shape[0], u.shape[1]
    ug = u.astype(jnp.float32).reshape(N, L, SSM_N_GROUPS, SSM_GROUP)
    y = ssm_d.astype(jnp.float32) * ug
    for d in range(N_DIR):
        y = y + _s5_direction(ug, a_re[d], a_im[d], log_dt[d], b_re[d], b_im[d], c_re[d], c_im[d], reverse=(d == 1))
    y = jax.nn.gelu(y.reshape(N, L, SSM_WIDTH)).astype(u.dtype)
    gate = jax.nn.sigmoid(jnp.einsum('nle,ef->nlf', y, glu_w) + glu_b)
    return y * gate


def _mixer(h, w_in, attn_sink, a_re, a_im, log_dt, b_re, b_im, c_re, c_im, ssm_d, glu_w, glu_b, attn_out_g, ssm_out_g, w_out):
    N, L = h.shape[0], h.shape[1]
    proj = jnp.einsum('nld,de->nle', h, w_in)
    q, k, v, u = jnp.split(proj, [ATTN_WIDTH, ATTN_WIDTH + KV_WIDTH, ATTN_WIDTH + 2 * KV_WIDTH], axis=-1)
    q = _rotary(q.reshape(N, L, N_Q_HEADS, HEAD_DIM))
    k = _rotary(k.reshape(N, L, N_KV_HEADS, HEAD_DIM))
    v = v.reshape(N, L, N_KV_HEADS, HEAD_DIM)
    attn = _banded_attention(q, k, v, attn_sink)
    ssm = _s5(u, a_re, a_im, log_dt, b_re, b_im, c_re, c_im, ssm_d, glu_w, glu_b)
    mixed = jnp.concatenate([_rmsnorm(attn, attn_out_g), _rmsnorm(ssm, ssm_out_g)], axis=-1)
    return jnp.einsum('nle,ed->nld', mixed, w_out)


def _moe(h, router_w, router_b, w_gate_up, b_gate_up, w_down, b_down):
    N, L, D = h.shape
    T = N * L
    xt = h.reshape(T, D)
    logits = (xt @ router_w + router_b).astype(jnp.float32)
    top_logit, top_e = lax.top_k(logits, TOP_K)
    gates = jax.nn.softmax(top_logit, axis=-1)
    n_slot = T * TOP_K
    flat_e = top_e.reshape(-1).astype(jnp.int32)
    order = jnp.argsort(flat_e, stable=True).astype(jnp.int32)
    sorted_e = flat_e[order]
    slot_tok = order // TOP_K
    slot_gate = gates.reshape(-1)[order]
    counts = jnp.zeros((N_EXPERTS,), jnp.int32).at[flat_e].add(1)
    padded = (counts + MOE_BLOCK - 1) // MOE_BLOCK * MOE_BLOCK
    start = jnp.cumsum(counts) - counts
    pend = jnp.cumsum(padded)
    pstart = pend - padded
    dest = pstart[sorted_e] + jnp.arange(n_slot, dtype=jnp.int32) - start[sorted_e]
    n_blocks = -(-n_slot // MOE_BLOCK) + N_EXPERTS
    n_pad = n_blocks * MOE_BLOCK
    buf_tok = jnp.full((n_pad,), T, jnp.int32).at[dest].set(slot_tok)
    buf_gate = jnp.zeros((n_pad,), jnp.float32).at[dest].set(slot_gate)
    block_e = jnp.minimum(jnp.searchsorted(pend, jnp.arange(n_blocks, dtype=jnp.int32) * MOE_BLOCK, side='right'), N_EXPERTS - 1)
    x_pad = jnp.concatenate([xt, jnp.zeros((1, D), xt.dtype)], axis=0)
    xb = x_pad[buf_tok].reshape(n_blocks, MOE_BLOCK, D)

    def expert_block(args):
        xblk, e = args
        gu = xblk @ w_gate_up[e] + b_gate_up[e]
        gate, up = jnp.split(gu, 2, axis=-1)
        gate = jnp.minimum(gate, SWIGLU_LIMIT)
        up = jnp.clip(up, -SWIGLU_LIMIT, SWIGLU_LIMIT)
        act = (up + 1.0) * (gate * jax.nn.sigmoid(SWIGLU_ALPHA * gate))
        return act @ w_down[e] + b_down[e]

    yb = lax.map(expert_block, (xb, block_e)).reshape(n_pad, D)
    out = jnp.zeros((T + 1, D), yb.dtype).at[buf_tok].add(yb * buf_gate[:, None].astype(yb.dtype))
    return out[:T].reshape(N, L, D).astype(h.dtype)


def _layer(x, norm1_g, w_in, attn_sink, a_re, a_im, log_dt, b_re, b_im, c_re, c_im, ssm_d, glu_w, glu_b,
           attn_out_g, ssm_out_g, w_out, norm2_g, router_w, router_b, w_gate_up, b_gate_up, w_down, b_down):
    h = _rmsnorm(x, norm1_g)
    x = x + _mixer(h, w_in, attn_sink, a_re, a_im, log_dt, b_re, b_im, c_re, c_im, ssm_d, glu_w, glu_b,
                   attn_out_g, ssm_out_g, w_out)
    x = x + _moe(_rmsnorm(x, norm2_g), router_w, router_b, w_gate_up, b_gate_up, w_down, b_down)
    return x


def setup_inputs(seed: int = 0) -> dict:
    key = jax.random.key(seed)
    ks = jax.random.split(key, 32)
    f32 = jnp.float32

    def nrm(k, shape, scale):
        return jax.random.normal(k, shape, f32) * scale

    G, P, C = SSM_N_GROUPS, SSM_STATE, SSM_GROUP
    n_idx = jnp.arange(P, dtype=f32)
    return {
        'x_prompt': nrm(ks[0], (BATCH, SEQ, D_MODEL), 1.0),
        'x_sample': nrm(ks[1], (DEC_BATCH, DEC_SEQ, D_MODEL), 1.0),
        'norm1_g': 1.0 + nrm(ks[2], (DEPTH, D_MODEL), 0.01),
        'w_in': nrm(ks[3], (DEPTH, D_MODEL, IN_WIDTH), D_MODEL ** -0.5),
        'attn_sink': nrm(ks[4], (DEPTH, N_Q_HEADS), 0.5),
        'ssm_a_re': -0.5 + nrm(ks[5], (DEPTH, N_DIR, G, P), 0.01),
        'ssm_a_im': math.pi * n_idx + nrm(ks[6], (DEPTH, N_DIR, G, P), 0.01),
        'ssm_log_dt': jax.random.uniform(ks[7], (DEPTH, N_DIR, G), f32, math.log(1e-3), math.log(1e-1)),
        'ssm_b_re': nrm(ks[8], (DEPTH, N_DIR, G, P, C), (2.0 * C) ** -0.5),
        'ssm_b_im': nrm(ks[9], (DEPTH, N_DIR, G, P, C), (2.0 * C) ** -0.5),
        'ssm_c_re': nrm(ks[10], (DEPTH, N_DIR, G, C, P), (2.0 * P) ** -0.5),
        'ssm_c_im': nrm(ks[11], (DEPTH, N_DIR, G, C, P), (2.0 * P) ** -0.5),
        'ssm_d': nrm(ks[12], (DEPTH, G, C), 1.0),
        'glu_w': nrm(ks[13], (DEPTH, SSM_WIDTH, SSM_WIDTH), SSM_WIDTH ** -0.5),
        'glu_b': nrm(ks[14], (DEPTH, SSM_WIDTH), 0.01),
        'attn_out_g': 1.0 + nrm(ks[15], (DEPTH, ATTN_WIDTH), 0.01),
        'ssm_out_g': 1.0 + nrm(ks[16], (DEPTH, SSM_WIDTH), 0.01),
        'w_out': nrm(ks[17], (DEPTH, MIX_WIDTH, D_MODEL), MIX_WIDTH ** -0.5),
        'norm2_g': 1.0 + nrm(ks[18], (DEPTH, D_MODEL), 0.01),
        'router_w': nrm(ks[19], (DEPTH, D_MODEL, N_EXPERTS), D_MODEL ** -0.5),
        'router_b': nrm(ks[20], (DEPTH, N_EXPERTS), 0.01),
        'w_gate_up': nrm(ks[21], (DEPTH, N_EXPERTS, D_MODEL, 2 * D_EXPERT), D_MODEL ** -0.5),
        'b_gate_up': nrm(ks[22], (DEPTH, N_EXPERTS, 2 * D_EXPERT), 0.01),
        'w_down': nrm(ks[23], (DEPTH, N_EXPERTS, D_EXPERT, D_MODEL), D_EXPERT ** -0.5),
        'b_down': nrm(ks[24], (DEPTH, N_EXPERTS, D_MODEL), 0.01),
        'final_g': 1.0 + nrm(ks[25], (D_MODEL,), 0.01),
    }


def reference(x_prompt, x_sample, norm1_g, w_in, attn_sink, ssm_a_re, ssm_a_im, ssm_log_dt, ssm_b_re, ssm_b_im,
              ssm_c_re, ssm_c_im, ssm_d, glu_w, glu_b, attn_out_g, ssm_out_g, w_out, norm2_g, router_w, router_b,
              w_gate_up, b_gate_up, w_down, b_down, final_g):
    def trunk(x):
        for l in range(DEPTH):
            x = _layer(x, norm1_g[l], w_in[l], attn_sink[l], ssm_a_re[l], ssm_a_im[l], ssm_log_dt[l],
                       ssm_b_re[l], ssm_b_im[l], ssm_c_re[l], ssm_c_im[l], ssm_d[l], glu_w[l], glu_b[l],
                       attn_out_g[l], ssm_out_g[l], w_out[l], norm2_g[l], router_w[l], router_b[l],
                       w_gate_up[l], b_gate_up[l], w_down[l], b_down[l])
        return _rmsnorm(x, final_g)

    y_prompt = trunk(x_prompt)
    y_sample = trunk(x_sample)
    return (y_prompt, y_sample)
```

```python
import functools
import math

import jax
import jax.numpy as jnp
from jax import lax
from jax.experimental import pallas as pl
from jax.experimental.pallas import tpu as pltpu

F32 = jnp.float32
BF16 = jnp.bfloat16

D_MODEL = 1024
HEAD_DIM = 64
N_Q_HEADS = 8
N_KV_HEADS = 2
ATTN_WIDTH = N_Q_HEADS * HEAD_DIM
KV_WIDTH = N_KV_HEADS * HEAD_DIM
WINDOW = 128
ATT_BLOCK = 128
ROPE_THETA = 500000.0
ROPE_DIM = HEAD_DIM // 4
ROPE_HALF = ROPE_DIM // 2
SSM_WIDTH = 512
SSM_GROUP = 16
SSM_N_GROUPS = SSM_WIDTH // SSM_GROUP
SSM_STATE = 64
N_EXPERTS = 32
TOP_K = 4
SWIGLU_LIMIT = 7.0
SWIGLU_ALPHA = 1.702
EPS = 1e-5

LANES = 128
SSM_CHUNK = 32
TOEP = SSM_CHUNK * SSM_GROUP
MOE_ROWS = 512
VMEM_LIMIT = 52 * 1024 * 1024


def _params(sem, vmem=VMEM_LIMIT):
    return pltpu.CompilerParams(dimension_semantics=sem, vmem_limit_bytes=vmem)


def _rms(x, g):
    ms = jnp.mean(x * x, axis=-1, keepdims=True)
    return x * lax.rsqrt(ms + EPS) * g


def _qkv_kernel(x_ref, g_ref, w_ref, c_ref, s1_ref, s2_ref, q_ref, ka_ref, kb_ref, va_ref, vb_ref):
    h = _rms(x_ref[...], g_ref[...]).astype(BF16)
    p = jnp.dot(h, w_ref[...], preferred_element_type=F32)
    c = c_ref[...]
    s1 = s1_ref[...]
    s2 = s2_ref[...]

    def rot(t):
        return t * c + pltpu.roll(t, ROPE_HALF, 1) * s1 + pltpu.roll(t, LANES - ROPE_HALF, 1) * s2

    for j in range(ATTN_WIDTH // LANES):
        q_ref[:, j * LANES:(j + 1) * LANES] = (rot(p[:, j * LANES:(j + 1) * LANES]) * (HEAD_DIM ** -0.5)).astype(BF16)
    ka_ref[...] = rot(p[:, 512:640]).astype(BF16)
    kb_ref[...] = rot(p[:, 640:768]).astype(BF16)
    va_ref[...] = p[:, 768:896].astype(BF16)
    vb_ref[...] = p[:, 896:1024].astype(BF16)


def _qkv(x2d, g, w, c, s1, s2, seq_len, tm=512):
    t = x2d.shape[0]
    nlb = seq_len // tm
    row = lambda i: (i, 0)
    tab = lambda i: (i % nlb, 0)
    full = lambda i: (0, 0)
    kv = jax.ShapeDtypeStruct((t, LANES), BF16)
    return pl.pallas_call(
        _qkv_kernel,
        grid=(t // tm,),
        in_specs=[pl.BlockSpec((tm, D_MODEL), row), pl.BlockSpec((1, D_MODEL), full),
                  pl.BlockSpec((D_MODEL, 1024), full),
                  pl.BlockSpec((tm, LANES), tab), pl.BlockSpec((tm, LANES), tab), pl.BlockSpec((tm, LANES), tab)],
        out_specs=[pl.BlockSpec((tm, ATTN_WIDTH), row)] + [pl.BlockSpec((tm, LANES), row)] * 4,
        out_shape=[jax.ShapeDtypeStruct((t, ATTN_WIDTH), BF16), kv, kv, kv, kv],
        compiler_params=_params(("parallel",)),
        name="qkv_rotary",
    )(x2d, g, w, c, s1, s2)


def _attn_kernel(sink_ref, q_ref, kap, kac, kan, kbp, kbc, kbn, vap, vac, van, vbp, vbc, vbn, g_ref, o_ref, *, bps):
    i = pl.program_id(0)
    first = (i % bps) == 0
    last = (i % bps) == bps - 1
    qi = lax.broadcasted_iota(jnp.int32, (ATT_BLOCK, 3 * ATT_BLOCK), 0)
    kj = lax.broadcasted_iota(jnp.int32, (ATT_BLOCK, 3 * ATT_BLOCK), 1)
    rel = kj - ATT_BLOCK - qi
    valid = (jnp.abs(rel) <= WINDOW)
    valid = valid & ((kj >= ATT_BLOCK) | jnp.logical_not(first))
    valid = valid & ((kj < 2 * ATT_BLOCK) | jnp.logical_not(last))
    k_a = jnp.concatenate([kap[...], kac[...], kan[...]], axis=0)
    k_b = jnp.concatenate([kbp[...], kbc[...], kbn[...]], axis=0)
    v_a = jnp.concatenate([vap[...], vac[...], van[...]], axis=0)
    v_b = jnp.concatenate([vbp[...], vbc[...], vbn[...]], axis=0)
    lo = lax.broadcasted_iota(jnp.int32, (ATT_BLOCK, LANES), 1) < HEAD_DIM
    zero = jnp.zeros((ATT_BLOCK, LANES), BF16)
    tiles = []
    for j in range(ATTN_WIDTH // LANES):
        kvh = j // 2
        qt = q_ref[:, j * LANES:(j + 1) * LANES]
        halves = []
        for par in range(2):
            qm = jnp.where(lo if par == 0 else jnp.logical_not(lo), qt, zero)
            k = k_a if kvh == par else k_b
            v = v_a if kvh == par else v_b
            s = lax.dot_general(qm, k, (((1,), (1,)), ((), ())), preferred_element_type=F32)
            s = jnp.where(valid, s, -1e30)
            sk = sink_ref[2 * j + par]
            m = jnp.maximum(jnp.max(s, axis=-1, keepdims=True), sk)
            p = jnp.exp(s - m)
            den = jnp.sum(p, axis=-1, keepdims=True) + jnp.exp(sk - m)
            o = jnp.dot(p.astype(BF16), v, preferred_element_type=F32)
            halves.append(o / den)
        tiles.append(jnp.where(lo, halves[0], halves[1]))
    o = jnp.concatenate(tiles, axis=1)
    o_ref[...] = _rms(o, g_ref[...]).astype(BF16)


def _attention(q, ka, kb, va, vb, sink, g, seq_len):
    t = q.shape[0]
    nblk = t // ATT_BLOCK
    bps = seq_len // ATT_BLOCK
    cur = lambda i, s: (i, 0)
    prv = lambda i, s: (jnp.maximum(i - 1, 0), 0)
    nxt = lambda i, s: (jnp.minimum(i + 1, nblk - 1), 0)
    kvs = lambda: [pl.BlockSpec((ATT_BLOCK, LANES), prv), pl.BlockSpec((ATT_BLOCK, LANES), cur),
                   pl.BlockSpec((ATT_BLOCK, LANES), nxt)]
    grid_spec = pltpu.PrefetchScalarGridSpec(
        num_scalar_prefetch=1,
        grid=(nblk,),
        in_specs=[pl.BlockSpec((ATT_BLOCK, ATTN_WIDTH), cur)] + kvs() + kvs() + kvs() + kvs()
                 + [pl.BlockSpec((1, ATTN_WIDTH), lambda i, s: (0, 0))],
        out_specs=pl.BlockSpec((ATT_BLOCK, ATTN_WIDTH), cur),
    )
    return pl.pallas_call(
        functools.partial(_attn_kernel, bps=bps),
        grid_spec=grid_spec,
        out_shape=jax.ShapeDtypeStruct((t, ATTN_WIDTH), BF16),
        compiler_params=_params(("parallel",)),
        name="banded_attention",
    )(sink, q, ka, ka, ka, kb, kb, kb, va, va, va, vb, vb, vb, g)


def _uproj_kernel(x_ref, g_ref, w_ref, o_ref):
    h = _rms(x_ref[...], g_ref[...]).astype(BF16)
    ut = lax.dot_general(w_ref[...], h, (((1,), (1,)), ((), ())), preferred_element_type=F32)
    o_ref[0] = ut.astype(BF16)


def _uproj(x2d, g, w_t, lt):
    t = x2d.shape[0]
    nc = t // SSM_CHUNK
    xv = x2d.reshape(nc, SSM_CHUNK * D_MODEL)
    return pl.pallas_call(
        _uproj_kernel,
        grid=(nc // lt, SSM_CHUNK),
        in_specs=[pl.BlockSpec((lt, D_MODEL), lambda i, b: (i, b)),
                  pl.BlockSpec((1, D_MODEL), lambda i, b: (0, 0)),
                  pl.BlockSpec((SSM_WIDTH, D_MODEL), lambda i, b: (0, 0))],
        out_specs=pl.BlockSpec((1, SSM_WIDTH, lt), lambda i, b: (b, 0, i)),
        out_shape=jax.ShapeDtypeStruct((SSM_CHUNK, SSM_WIDTH, nc), BF16),
        compiler_params=_params(("parallel", "parallel")),
        name="u_projection",
    )(xv, g, w_t)


def _gelu_tanh(x):
    return 0.5 * x * (1.0 + jnp.tanh(math.sqrt(2.0 / math.pi) * (x + 0.044715 * (x * x * x))))


def _ssm_kernel(a_ref, wt_ref, ws_ref, wc_ref, pw_ref, y_ref, *, chunks_per_seq, nsteps):
    nc = a_ref.shape[2]
    a = a_ref[...].reshape(TOEP, nc)
    y = jnp.dot(wt_ref[0], a, preferred_element_type=F32)
    s = jnp.dot(ws_ref[0], a, preferred_element_type=F32)
    pos = lax.broadcasted_iota(jnp.int32, (SSM_STATE, nc), 1) % chunks_per_seq
    carries = []
    for d in range(2):
        hr = s[2 * d * SSM_STATE:(2 * d + 1) * SSM_STATE]
        hi = s[(2 * d + 1) * SSM_STATE:(2 * d + 2) * SSM_STATE]
        for k in range(nsteps):
            sh = 1 << k
            pr = pw_ref[0, 2 * d * SSM_STATE:(2 * d + 1) * SSM_STATE, k:k + 1]
            pi = pw_ref[0, (2 * d + 1) * SSM_STATE:(2 * d + 2) * SSM_STATE, k:k + 1]
            if d == 0:
                ok = pos >= sh
                sr = pltpu.roll(hr, sh, 1)
                si = pltpu.roll(hi, sh, 1)
            else:
                ok = pos < chunks_per_seq - sh
                sr = pltpu.roll(hr, nc - sh, 1)
                si = pltpu.roll(hi, nc - sh, 1)
            hr, hi = (hr + jnp.where(ok, pr * sr - pi * si, 0.0),
                      hi + jnp.where(ok, pr * si + pi * sr, 0.0))
        if d == 0:
            ok = pos >= 1
            cr = pltpu.roll(hr, 1, 1)
            ci = pltpu.roll(hi, 1, 1)
        else:
            ok = pos < chunks_per_seq - 1
            cr = pltpu.roll(hr, nc - 1, 1)
            ci = pltpu.roll(hi, nc - 1, 1)
        carries += [jnp.where(ok, cr, 0.0), jnp.where(ok, ci, 0.0)]
    carry = jnp.concatenate(carries, axis=0).astype(BF16)
    y = y + jnp.dot(wc_ref[0], carry, preferred_element_type=F32)
    y_ref[...] = _gelu_tanh(y).reshape(SSM_CHUNK, SSM_GROUP, nc)


def _ssm(ut, w_toep, w_state, w_carry, pw, chunks_per_seq):
    nc = ut.shape[2]
    nsteps = max(1, (chunks_per_seq - 1).bit_length())
    g3 = lambda g: (g, 0, 0)
    return pl.pallas_call(
        functools.partial(_ssm_kernel, chunks_per_seq=chunks_per_seq, nsteps=nsteps),
        grid=(SSM_N_GROUPS,),
        in_specs=[pl.BlockSpec((SSM_CHUNK, SSM_GROUP, nc), lambda g: (0, g, 0)),
                  pl.BlockSpec((1, TOEP, TOEP), g3),
                  pl.BlockSpec((1, 4 * SSM_STATE, TOEP), g3),
                  pl.BlockSpec((1, TOEP, 4 * SSM_STATE), g3),
                  pl.BlockSpec((1, 4 * SSM_STATE, pw.shape[2]), g3)],
        out_specs=pl.BlockSpec((SSM_CHUNK, SSM_GROUP, nc), lambda g: (0, g, 0)),
        out_shape=jax.ShapeDtypeStruct((SSM_CHUNK, SSM_WIDTH, nc), F32),
        compiler_params=_params(("parallel",)),
        name="s5_core",
    )(ut, w_toep, w_state, w_carry, pw)


def _glu_kernel(y_ref, w_ref, b_ref, g_ref, o_ref):
    y = y_ref[0]
    z = jnp.dot(w_ref[...], y.astype(BF16), preferred_element_type=F32) + b_ref[...]
    s = y * jax.nn.sigmoid(z)
    ms = jnp.mean(s * s, axis=0, keepdims=True)
    sn = s * lax.rsqrt(ms + EPS) * g_ref[...]
    o_ref[...] = sn.T.astype(BF16)


def _glu(yt, w_t, b_col, g_col, lt):
    nc = yt.shape[2]
    out = pl.pallas_call(
        _glu_kernel,
        grid=(nc // lt, SSM_CHUNK),
        in_specs=[pl.BlockSpec((1, SSM_WIDTH, lt), lambda i, b: (b, 0, i)),
                  pl.BlockSpec((SSM_WIDTH, SSM_WIDTH), lambda i, b: (0, 0)),
                  pl.BlockSpec((SSM_WIDTH, 1), lambda i, b: (0, 0)),
                  pl.BlockSpec((SSM_WIDTH, 1), lambda i, b: (0, 0))],
        out_specs=pl.BlockSpec((lt, SSM_WIDTH), lambda i, b: (i, b)),
        out_shape=jax.ShapeDtypeStruct((nc, SSM_CHUNK * SSM_WIDTH), BF16),
        compiler_params=_params(("parallel", "parallel")),
        name="glu_norm",
    )(yt, w_t, b_col, g_col)
    return out.reshape(nc * SSM_CHUNK, SSM_WIDTH)


def _outproj_kernel(x_ref, a_ref, s_ref, wa_ref, ws_ref, g_ref, wr_ref, br_ref, x2_ref, h2_ref, lg_ref):
    x2 = (x_ref[...] + jnp.dot(a_ref[...], wa_ref[...], preferred_element_type=F32)
          + jnp.dot(s_ref[...], ws_ref[...], preferred_element_type=F32))
    x2_ref[...] = x2
    h2 = _rms(x2, g_ref[...]).astype(BF16)
    h2_ref[...] = h2
    lg_ref[...] = jnp.dot(h2, wr_ref[...], preferred_element_type=F32) + br_ref[...]


def _outproj(x2d, attn_n, ssm_n, w_a, w_s, g, w_r, b_r, tm=512):
    t = x2d.shape[0]
    row = lambda i: (i, 0)
    full = lambda i: (0, 0)
    return pl.pallas_call(
        _outproj_kernel,
        grid=(t // tm,),
        in_specs=[pl.BlockSpec((tm, D_MODEL), row), pl.BlockSpec((tm, ATTN_WIDTH), row),
                  pl.BlockSpec((tm, SSM_WIDTH), row),
                  pl.BlockSpec((ATTN_WIDTH, D_MODEL), full), pl.BlockSpec((SSM_WIDTH, D_MODEL), full),
                  pl.BlockSpec((1, D_MODEL), full),
                  pl.BlockSpec((D_MODEL, LANES), full), pl.BlockSpec((1, LANES), full)],
        out_specs=[pl.BlockSpec((tm, D_MODEL), row), pl.BlockSpec((tm, D_MODEL), row),
                   pl.BlockSpec((tm, LANES), row)],
        out_shape=[jax.ShapeDtypeStruct((t, D_MODEL), F32), jax.ShapeDtypeStruct((t, D_MODEL), BF16),
                   jax.ShapeDtypeStruct((t, LANES), F32)],
        compiler_params=_params(("parallel",)),
        name="out_projection_router",
    )(x2d, attn_n, ssm_n, w_a, w_s, g, w_r, b_r)


def _expert_kernel(be_ref, nv_ref, x_ref, wgu_ref, bgu_ref, wd_ref, bd_ref, o_ref):
    i = pl.program_id(0)

    @pl.when(i < nv_ref[0])
    def _():
        gu = jnp.dot(x_ref[...], wgu_ref[0], preferred_element_type=F32) + bgu_ref[0]
        gate = jnp.minimum(gu[:, :D_MODEL], SWIGLU_LIMIT)
        up = jnp.clip(gu[:, D_MODEL:], -SWIGLU_LIMIT, SWIGLU_LIMIT)
        act = (up + 1.0) * (gate * jax.nn.sigmoid(SWIGLU_ALPHA * gate))
        o_ref[...] = jnp.dot(act.astype(BF16), wd_ref[0], preferred_element_type=F32) + bd_ref[0]

    @pl.when(i >= nv_ref[0])
    def _():
        o_ref[...] = jnp.zeros_like(o_ref)


def _experts(block_e, n_valid, xb, wgu, bgu, wd, bd):
    n_pad = xb.shape[0]
    nb = n_pad // MOE_ROWS
    e3 = lambda i, be, nv: (be[i], 0, 0)
    row = lambda i, be, nv: (i, 0)
    grid_spec = pltpu.PrefetchScalarGridSpec(
        num_scalar_prefetch=2,
        grid=(nb,),
        in_specs=[pl.BlockSpec((MOE_ROWS, D_MODEL), row),
                  pl.BlockSpec((1, D_MODEL, 2 * D_MODEL), e3), pl.BlockSpec((1, 1, 2 * D_MODEL), e3),
                  pl.BlockSpec((1, D_MODEL, D_MODEL), e3), pl.BlockSpec((1, 1, D_MODEL), e3)],
        out_specs=pl.BlockSpec((MOE_ROWS, D_MODEL), row),
    )
    return pl.pallas_call(
        _expert_kernel,
        grid_spec=grid_spec,
        out_shape=jax.ShapeDtypeStruct((n_pad, D_MODEL), F32),
        compiler_params=_params(("arbitrary",)),
        name="routed_experts",
    )(block_e, n_valid, xb, wgu, bgu, wd, bd)


def _combine_kernel(x_ref, y_ref, gt_ref, g_ref, o_ref):
    acc = x_ref[...]
    gt = gt_ref[...]
    for k in range(TOP_K):
        acc = acc + y_ref[:, k * D_MODEL:(k + 1) * D_MODEL] * gt[:, k:k + 1]
    o_ref[...] = _rms(acc, g_ref[...])


def _combine(x2, ysel, gates, g, tm=256):
    t = x2.shape[0]
    row = lambda i: (i, 0)
    return pl.pallas_call(
        _combine_kernel,
        grid=(t // tm,),
        in_specs=[pl.BlockSpec((tm, D_MODEL), row), pl.BlockSpec((tm, TOP_K * D_MODEL), row),
                  pl.BlockSpec((tm, TOP_K), row), pl.BlockSpec((1, D_MODEL), lambda i: (0, 0))],
        out_specs=pl.BlockSpec((tm, D_MODEL), row),
        out_shape=jax.ShapeDtypeStruct((t, D_MODEL), F32),
        compiler_params=_params(("parallel",)),
        name="combine_final_norm",
    )(x2, ysel, gates, g)


def _rope_tables(seq_len):
    inv_freq = ROPE_THETA ** (-jnp.arange(ROPE_HALF, dtype=F32) * 2.0 / ROPE_DIM)
    ang = jnp.arange(seq_len, dtype=F32)[:, None] * inv_freq[None, :]
    cos, sin = jnp.cos(ang), jnp.sin(ang)
    pad = HEAD_DIM - ROPE_DIM
    ones = jnp.ones((seq_len, pad), F32)
    zer_h = jnp.zeros((seq_len, ROPE_HALF), F32)
    zer_p = jnp.zeros((seq_len, pad), F32)
    c = jnp.concatenate([cos, cos, ones], axis=1)
    s1 = jnp.concatenate([zer_h, sin, zer_p], axis=1)
    s2 = jnp.concatenate([-sin, zer_h, zer_p], axis=1)
    rep = LANES // HEAD_DIM
    return jnp.tile(c, (1, rep)), jnp.tile(s1, (1, rep)), jnp.tile(s2, (1, rep))


def _ssm_weights(a_re, a_im, log_dt, b_re, b_im, c_re, c_im, ssm_d, nsteps):
    r = SSM_CHUNK
    dt = jnp.exp(log_dt)[..., None]
    lr, li = a_re * dt, a_im * dt

    def cpow(n):
        n = jnp.asarray(n, F32)[..., None, None, None]
        mag = jnp.exp(n * lr)
        return mag * jnp.cos(n * li), mag * jnp.sin(n * li)

    ab_re, ab_im = cpow(jnp.ones(()))
    den = a_re * a_re + a_im * a_im
    num_re, num_im = ab_re - 1.0, ab_im
    f_re = (num_re * a_re + num_im * a_im) / den
    f_im = (num_im * a_re - num_re * a_im) / den
    bb_re = f_re[..., None] * b_re - f_im[..., None] * b_im
    bb_im = f_re[..., None] * b_im + f_im[..., None] * b_re

    taus = jnp.arange(r + 1, dtype=F32)
    p_re, p_im = cpow(taus)
    m_re = p_re[..., None] * bb_re - p_im[..., None] * bb_im
    m_im = p_re[..., None] * bb_im + p_im[..., None] * bb_re
    kern = (jnp.einsum('dgcp,tdgpk->tdgck', c_re, m_re) - jnp.einsum('dgcp,tdgpk->tdgck', c_im, m_im))
    bi = jnp.arange(r)
    lag = bi[:, None] - bi[None, :]
    kf = kern[jnp.clip(lag, 0, r), 0] * (lag >= 0)[:, :, None, None, None]
    kb = kern[jnp.clip(-lag, 0, r), 1] * (lag <= 0)[:, :, None, None, None]
    eye = (lag == 0)[:, :, None, None, None] * (jnp.eye(SSM_GROUP, dtype=F32) * ssm_d[:, :, None])[None, None]
    w_toep = (kf + kb + eye).transpose(2, 0, 3, 1, 4).reshape(SSM_N_GROUPS, TOEP, TOEP)

    ef = (r - 1 - bi)
    st_f_re, st_f_im = m_re[ef, 0], m_im[ef, 0]
    st_b_re, st_b_im = m_re[bi, 1], m_im[bi, 1]
    w_state = jnp.stack([st_f_re, st_f_im, st_b_re, st_b_im], axis=0)
    w_state = w_state.transpose(2, 0, 3, 1, 4).reshape(SSM_N_GROUPS, 4 * SSM_STATE, TOEP)

    def c_times_pow(d, expo):
        pr, pi = p_re[expo, d], p_im[expo, d]
        zr = c_re[d][None] * pr[:, :, None, :] - c_im[d][None] * pi[:, :, None, :]
        zi = c_re[d][None] * pi[:, :, None, :] + c_im[d][None] * pr[:, :, None, :]
        return zr, -zi
    cf_re, cf_im = c_times_pow(0, bi + 1)
    cb_re, cb_im = c_times_pow(1, r - bi)
    w_carry = jnp.stack([cf_re, cf_im, cb_re, cb_im], axis=3)
    w_carry = w_carry.transpose(1, 0, 2, 3, 4).reshape(SSM_N_GROUPS, TOEP, 4 * SSM_STATE)

    qr, qi = p_re[r], p_im[r]
    cols = []
    for _ in range(nsteps):
        cols.append(jnp.stack([qr[0], qi[0], qr[1], qi[1]], axis=1))
        qr, qi = qr * qr - qi * qi, 2.0 * qr * qi
    pw = jnp.stack(cols, axis=-1).reshape(SSM_N_GROUPS, 4 * SSM_STATE, nsteps)
    return w_toep.astype(BF16), w_state.astype(BF16), w_carry.astype(BF16), pw


def _front(x, prm):
    n, seq_len, _ = x.shape
    t = n * seq_len
    x2d = x.reshape(t, D_MODEL)
    q, ka, kb, va, vb = _qkv(x2d, prm['norm1_g'], prm['w_qkv'], *_rope_tables(seq_len), seq_len)
    attn_n = _attention(q, ka, kb, va, vb, prm['sink'], prm['attn_out_g'], seq_len)

    chunks_per_seq = seq_len // SSM_CHUNK
    nsteps = max(1, (chunks_per_seq - 1).bit_length())
    nc = t // SSM_CHUNK
    lt = min(256, nc)
    ut = _uproj(x2d, prm['norm1_g'], prm['w_u_t'], lt)
    yt = _ssm(ut, prm['w_toep'], prm['w_state'], prm['w_carry'], prm['pw'][:, :, :nsteps], chunks_per_seq)
    ssm_n = _glu(yt, prm['glu_w_t'], prm['glu_b'], prm['ssm_out_g'], lt)
    return _outproj(x2d, attn_n, ssm_n, prm['w_out_a'], prm['w_out_s'], prm['norm2_g'],
                    prm['router_w'], prm['router_b'])


def _route(logits):
    t = logits.shape[0]
    top_logit, top_e = lax.top_k(logits, TOP_K)
    gates = jax.nn.softmax(top_logit, axis=-1)
    n_slot = t * TOP_K
    flat_e = top_e.reshape(-1).astype(jnp.int32)
    order = jnp.argsort(flat_e, stable=True).astype(jnp.int32)
    sorted_e = flat_e[order]
    counts = jnp.zeros((N_EXPERTS,), jnp.int32).at[flat_e].add(1)
    padded = (counts + MOE_ROWS - 1) // MOE_ROWS * MOE_ROWS
    start = jnp.cumsum(counts) - counts
    pend = jnp.cumsum(padded)
    pstart = pend - padded
    dest = pstart[sorted_e] + jnp.arange(n_slot, dtype=jnp.int32) - start[sorted_e]
    nb = n_slot // MOE_ROWS + N_EXPERTS
    n_pad = nb * MOE_ROWS
    tok_of_row = jnp.zeros((n_pad,), jnp.int32).at[dest].set(order // TOP_K)
    row_of_slot = jnp.zeros((n_slot,), jnp.int32).at[order].set(dest)
    block_e = jnp.minimum(jnp.searchsorted(pend, jnp.arange(nb, dtype=jnp.int32) * MOE_ROWS, side='right'),
                          N_EXPERTS - 1).astype(jnp.int32)
    n_valid = (pend[-1] // MOE_ROWS).astype(jnp.int32).reshape(1)
    return gates, tok_of_row, row_of_slot, block_e, n_valid


def kernel(x_prompt, x_sample, norm1_g, w_in, attn_sink, ssm_a_re, ssm_a_im, ssm_log_dt, ssm_b_re, ssm_b_im, ssm_c_re, ssm_c_im, ssm_d, glu_w, glu_b, attn_out_g, ssm_out_g, w_out, norm2_g, router_w, router_b, w_gate_up, b_gate_up, w_down, b_down, final_g):
    depth = norm1_g.shape[0]
    xs = [x_prompt, x_sample]
    max_chunks = max(x.shape[1] for x in xs) // SSM_CHUNK
    max_steps = max(1, (max_chunks - 1).bit_length())
    for l in range(depth):
        wq, wk, wv, wu = jnp.split(w_in[l], [ATTN_WIDTH, ATTN_WIDTH + KV_WIDTH, ATTN_WIDTH + 2 * KV_WIDTH], axis=1)
        swap = lambda w: jnp.concatenate([w[:, HEAD_DIM:], w[:, :HEAD_DIM]], axis=1)
        w_toep, w_state, w_carry, pw = _ssm_weights(ssm_a_re[l], ssm_a_im[l], ssm_log_dt[l], ssm_b_re[l], ssm_b_im[l],
                                                    ssm_c_re[l], ssm_c_im[l], ssm_d[l], max_steps)
        prm = dict(
            norm1_g=norm1_g[l].reshape(1, D_MODEL),
            w_qkv=jnp.concatenate([wq, wk, swap(wk), wv, swap(wv)], axis=1).astype(BF16),
            w_u_t=wu.T.astype(BF16),
            sink=attn_sink[l].astype(F32),
            attn_out_g=attn_out_g[l].reshape(1, ATTN_WIDTH),
            w_toep=w_toep, w_state=w_state, w_carry=w_carry, pw=pw,
            glu_w_t=glu_w[l].T.astype(BF16),
            glu_b=glu_b[l].reshape(SSM_WIDTH, 1),
            ssm_out_g=ssm_out_g[l].reshape(SSM_WIDTH, 1),
            w_out_a=w_out[l][:ATTN_WIDTH].astype(BF16),
            w_out_s=w_out[l][ATTN_WIDTH:].astype(BF16),
            norm2_g=norm2_g[l].reshape(1, D_MODEL),
            router_w=jnp.pad(router_w[l], ((0, 0), (0, LANES - N_EXPERTS))).astype(BF16),
            router_b=jnp.pad(router_b[l], (0, LANES - N_EXPERTS)).reshape(1, LANES),
        )
        fronts = [_front(x, prm) for x in xs]
        sizes = [f[0].shape[0] for f in fronts]
        h2 = jnp.concatenate([f[1] for f in fronts], axis=0)
        logits = jnp.concatenate([f[2][:, :N_EXPERTS] for f in fronts], axis=0)
        gates, tok_of_row, row_of_slot, block_e, n_valid = _route(logits)
        xb = h2[tok_of_row]
        yb = _experts(block_e, n_valid, xb, w_gate_up[l].astype(BF16), b_gate_up[l][:, None, :],
                      w_down[l].astype(BF16), b_down[l][:, None, :])
        ysel = yb[row_of_slot].reshape(-1, TOP_K * D_MODEL)
        outs, off = [], 0
        last = l == depth - 1
        for x, f, sz in zip(xs, fronts, sizes):
            gfin = final_g.reshape(1, D_MODEL)
            if not last:
                raise NotImplementedError("depth > 1 needs a combine without the final norm")
            o = _combine(f[0], ysel[off:off + sz], gates[off:off + sz], gfin)
            outs.append(o.reshape(x.shape))
            off += sz
        xs = outs
    return tuple(xs)
```

```python
import functools
import math

import jax
import jax.numpy as jnp
from jax import lax
from jax.experimental import pallas as pl
from jax.experimental.pallas import tpu as pltpu

F32 = jnp.float32
BF16 = jnp.bfloat16
I32 = jnp.int32

D_MODEL = 1024
HEAD_DIM = 64
N_Q_HEADS = 8
N_KV_HEADS = 2
ATTN_WIDTH = N_Q_HEADS * HEAD_DIM
KV_WIDTH = N_KV_HEADS * HEAD_DIM
WINDOW = 128
ATT_BLOCK = 128
ROPE_THETA = 500000.0
ROPE_DIM = HEAD_DIM // 4
ROPE_HALF = ROPE_DIM // 2
SSM_WIDTH = 512
SSM_GROUP = 16
SSM_N_GROUPS = SSM_WIDTH // SSM_GROUP
SSM_STATE = 64
N_EXPERTS = 32
TOP_K = 4
SWIGLU_LIMIT = 7.0
SWIGLU_ALPHA = 1.702
EPS = 1e-5

LANES = 128
SUBLANES = 8
SSM_CHUNK = 32
TOEP = SSM_CHUNK * SSM_GROUP
MOE_ROWS = 512
ROUTE_TM = 512
COMBINE_TM = 128
VMEM_LIMIT = 52 * 1024 * 1024
NEG_BIG = -1e30


def _params(sem, vmem=VMEM_LIMIT):
    return pltpu.CompilerParams(dimension_semantics=sem, vmem_limit_bytes=vmem)


def _rms(x, g):
    ms = jnp.mean(x * x, axis=-1, keepdims=True)
    return x * lax.rsqrt(ms + EPS) * g


def _tile_rows_load(ref, rows):
    return jnp.concatenate([ref[pl.ds(j, rows, stride=SUBLANES), :] for j in range(SUBLANES)], axis=1)


def _tile_rows_store(ref, val):
    rows = val.shape[0]
    for j in range(SUBLANES):
        ref[pl.ds(j, rows, stride=SUBLANES), :] = val[:, j * LANES:(j + 1) * LANES]


def _tile_row(ref, row):
    return ref.at[pl.ds(pl.multiple_of(row * SUBLANES, SUBLANES), SUBLANES)]


def _qkv_kernel(x_ref, g_ref, w_ref, c_ref, s1_ref, s2_ref, q_ref, ka_ref, kb_ref, va_ref, vb_ref):
    h = _rms(x_ref[...], g_ref[...]).astype(BF16)
    p = jnp.dot(h, w_ref[...], preferred_element_type=F32)
    c = c_ref[...]
    s1 = s1_ref[...]
    s2 = s2_ref[...]

    def rot(t):
        return t * c + pltpu.roll(t, ROPE_HALF, 1) * s1 + pltpu.roll(t, LANES - ROPE_HALF, 1) * s2

    for j in range(ATTN_WIDTH // LANES):
        q_ref[:, j * LANES:(j + 1) * LANES] = (rot(p[:, j * LANES:(j + 1) * LANES]) * (HEAD_DIM ** -0.5)).astype(BF16)
    ka_ref[...] = rot(p[:, 512:640]).astype(BF16)
    kb_ref[...] = rot(p[:, 640:768]).astype(BF16)
    va_ref[...] = p[:, 768:896].astype(BF16)
    vb_ref[...] = p[:, 896:1024].astype(BF16)


def _qkv(x2d, g, w, c, s1, s2, seq_len, tm=512):
    t = x2d.shape[0]
    nlb = seq_len // tm
    row = lambda i: (i, 0)
    tab = lambda i: (i % nlb, 0)
    full = lambda i: (0, 0)
    kv = jax.ShapeDtypeStruct((t, LANES), BF16)
    return pl.pallas_call(
        _qkv_kernel,
        grid=(t // tm,),
        in_specs=[pl.BlockSpec((tm, D_MODEL), row), pl.BlockSpec((1, D_MODEL), full),
                  pl.BlockSpec((D_MODEL, 1024), full),
                  pl.BlockSpec((tm, LANES), tab), pl.BlockSpec((tm, LANES), tab), pl.BlockSpec((tm, LANES), tab)],
        out_specs=[pl.BlockSpec((tm, ATTN_WIDTH), row)] + [pl.BlockSpec((tm, LANES), row)] * 4,
        out_shape=[jax.ShapeDtypeStruct((t, ATTN_WIDTH), BF16), kv, kv, kv, kv],
        compiler_params=_params(("parallel",)),
        name="qkv_rotary",
    )(x2d, g, w, c, s1, s2)


def _attn_kernel(sink_ref, q_ref, kap, kac, kan, kbp, kbc, kbn, vap, vac, van, vbp, vbc, vbn, g_ref, o_ref, *, bps):
    i = pl.program_id(0)
    first = (i % bps) == 0
    last = (i % bps) == bps - 1
    qi = lax.broadcasted_iota(I32, (ATT_BLOCK, 3 * ATT_BLOCK), 0)
    kj = lax.broadcasted_iota(I32, (ATT_BLOCK, 3 * ATT_BLOCK), 1)
    rel = kj - ATT_BLOCK - qi
    valid = (jnp.abs(rel) <= WINDOW)
    valid = valid & ((kj >= ATT_BLOCK) | jnp.logical_not(first))
    valid = valid & ((kj < 2 * ATT_BLOCK) | jnp.logical_not(last))
    k_a = jnp.concatenate([kap[...], kac[...], kan[...]], axis=0)
    k_b = jnp.concatenate([kbp[...], kbc[...], kbn[...]], axis=0)
    v_a = jnp.concatenate([vap[...], vac[...], van[...]], axis=0)
    v_b = jnp.concatenate([vbp[...], vbc[...], vbn[...]], axis=0)
    lo = lax.broadcasted_iota(I32, (ATT_BLOCK, LANES), 1) < HEAD_DIM
    zero = jnp.zeros((ATT_BLOCK, LANES), BF16)
    tiles = []
    for j in range(ATTN_WIDTH // LANES):
        kvh = j // 2
        qt = q_ref[:, j * LANES:(j + 1) * LANES]
        halves = []
        for par in range(2):
            qm = jnp.where(lo if par == 0 else jnp.logical_not(lo), qt, zero)
            k = k_a if kvh == par else k_b
            v = v_a if kvh == par else v_b
            s = lax.dot_general(qm, k, (((1,), (1,)), ((), ())), preferred_element_type=F32)
            s = jnp.where(valid, s, NEG_BIG)
            sk = sink_ref[2 * j + par]
            m = jnp.maximum(jnp.max(s, axis=-1, keepdims=True), sk)
            p = jnp.exp(s - m)
            den = jnp.sum(p, axis=-1, keepdims=True) + jnp.exp(sk - m)
            o = jnp.dot(p.astype(BF16), v, preferred_element_type=F32)
            halves.append(o / den)
        tiles.append(jnp.where(lo, halves[0], halves[1]))
    o = jnp.concatenate(tiles, axis=1)
    o_ref[...] = _rms(o, g_ref[...]).astype(BF16)


def _attention(q, ka, kb, va, vb, sink, g, seq_len):
    t = q.shape[0]
    nblk = t // ATT_BLOCK
    bps = seq_len // ATT_BLOCK
    cur = lambda i, s: (i, 0)
    prv = lambda i, s: (jnp.maximum(i - 1, 0), 0)
    nxt = lambda i, s: (jnp.minimum(i + 1, nblk - 1), 0)
    kvs = lambda: [pl.BlockSpec((ATT_BLOCK, LANES), prv), pl.BlockSpec((ATT_BLOCK, LANES), cur),
                   pl.BlockSpec((ATT_BLOCK, LANES), nxt)]
    grid_spec = pltpu.PrefetchScalarGridSpec(
        num_scalar_prefetch=1,
        grid=(nblk,),
        in_specs=[pl.BlockSpec((ATT_BLOCK, ATTN_WIDTH), cur)] + kvs() + kvs() + kvs() + kvs()
                 + [pl.BlockSpec((1, ATTN_WIDTH), lambda i, s: (0, 0))],
        out_specs=pl.BlockSpec((ATT_BLOCK, ATTN_WIDTH), cur),
    )
    return pl.pallas_call(
        functools.partial(_attn_kernel, bps=bps),
        grid_spec=grid_spec,
        out_shape=jax.ShapeDtypeStruct((t, ATTN_WIDTH), BF16),
        compiler_params=_params(("parallel",)),
        name="banded_attention",
    )(sink, q, ka, ka, ka, kb, kb, kb, va, va, va, vb, vb, vb, g)


def _uproj_kernel(x_ref, g_ref, w_ref, o_ref, stage_ref):
    lt = x_ref.shape[0]
    h = _rms(x_ref[...].reshape(lt * SUBLANES, D_MODEL), g_ref[...])
    for j in range(D_MODEL // LANES):
        stage_ref[j] = h[:, j * LANES:(j + 1) * LANES]
    for bl in range(SUBLANES):
        hb = jnp.concatenate([stage_ref[j, pl.ds(bl, lt, stride=SUBLANES), :] for j in range(D_MODEL // LANES)],
                             axis=1).astype(BF16)
        ut = lax.dot_general(w_ref[...], hb, (((1,), (1,)), ((), ())), preferred_element_type=F32)
        o_ref[bl] = ut.astype(BF16)


def _uproj(x2d, g, w_t, lt):
    t = x2d.shape[0]
    nc = t // SSM_CHUNK
    xv = x2d.reshape(nc, SSM_CHUNK // SUBLANES, SUBLANES, D_MODEL)
    return pl.pallas_call(
        _uproj_kernel,
        grid=(nc // lt, SSM_CHUNK // SUBLANES),
        in_specs=[pl.BlockSpec((lt, None, SUBLANES, D_MODEL), lambda i, b: (i, b, 0, 0)),
                  pl.BlockSpec((1, D_MODEL), lambda i, b: (0, 0)),
                  pl.BlockSpec((SSM_WIDTH, D_MODEL), lambda i, b: (0, 0))],
        out_specs=pl.BlockSpec((SUBLANES, SSM_WIDTH, lt), lambda i, b: (b, 0, i)),
        out_shape=jax.ShapeDtypeStruct((SSM_CHUNK, SSM_WIDTH, nc), BF16),
        scratch_shapes=[pltpu.VMEM((D_MODEL // LANES, lt * SUBLANES, LANES), F32)],
        compiler_params=_params(("parallel", "parallel")),
        name="u_projection",
    )(xv, g, w_t)


def _gelu_tanh(x):
    return 0.5 * x * (1.0 + jnp.tanh(math.sqrt(2.0 / math.pi) * (x + 0.044715 * (x * x * x))))


def _ssm_kernel(a_ref, wt_ref, ws_ref, wc_ref, pw_ref, y_ref, *, chunks_per_seq, nsteps):
    nc = a_ref.shape[2]
    a = a_ref[...].reshape(TOEP, nc)
    y = jnp.dot(wt_ref[0], a, preferred_element_type=F32)
    s = jnp.dot(ws_ref[0], a, preferred_element_type=F32)
    pos = lax.broadcasted_iota(I32, (SSM_STATE, nc), 1) % chunks_per_seq
    carries = []
    for d in range(2):
        hr = s[2 * d * SSM_STATE:(2 * d + 1) * SSM_STATE]
        hi = s[(2 * d + 1) * SSM_STATE:(2 * d + 2) * SSM_STATE]
        for k in range(nsteps):
            sh = 1 << k
            pr = pw_ref[0, 2 * d * SSM_STATE:(2 * d + 1) * SSM_STATE, k:k + 1]
            pi = pw_ref[0, (2 * d + 1) * SSM_STATE:(2 * d + 2) * SSM_STATE, k:k + 1]
            if d == 0:
                ok = pos >= sh
                sr = pltpu.roll(hr, sh, 1)
                si = pltpu.roll(hi, sh, 1)
            else:
                ok = pos < chunks_per_seq - sh
                sr = pltpu.roll(hr, nc - sh, 1)
                si = pltpu.roll(hi, nc - sh, 1)
            hr, hi = (hr + jnp.where(ok, pr * sr - pi * si, 0.0),
                      hi + jnp.where(ok, pr * si + pi * sr, 0.0))
        if d == 0:
            ok = pos >= 1
            cr = pltpu.roll(hr, 1, 1)
            ci = pltpu.roll(hi, 1, 1)
        else:
            ok = pos < chunks_per_seq - 1
            cr = pltpu.roll(hr, nc - 1, 1)
            ci = pltpu.roll(hi, nc - 1, 1)
        carries += [jnp.where(ok, cr, 0.0), jnp.where(ok, ci, 0.0)]
    carry = jnp.concatenate(carries, axis=0).astype(BF16)
    y = y + jnp.dot(wc_ref[0], carry, preferred_element_type=F32)
    y_ref[...] = _gelu_tanh(y).reshape(SSM_CHUNK, SSM_GROUP, nc)


def _ssm(ut, w_toep, w_state, w_carry, pw, chunks_per_seq):
    nc = ut.shape[2]
    nsteps = max(1, (chunks_per_seq - 1).bit_length())
    g3 = lambda g: (g, 0, 0)
    return pl.pallas_call(
        functools.partial(_ssm_kernel, chunks_per_seq=chunks_per_seq, nsteps=nsteps),
        grid=(SSM_N_GROUPS,),
        in_specs=[pl.BlockSpec((SSM_CHUNK, SSM_GROUP, nc), lambda g: (0, g, 0)),
                  pl.BlockSpec((1, TOEP, TOEP), g3),
                  pl.BlockSpec((1, 4 * SSM_STATE, TOEP), g3),
                  pl.BlockSpec((1, TOEP, 4 * SSM_STATE), g3),
                  pl.BlockSpec((1, 4 * SSM_STATE, pw.shape[2]), g3)],
        out_specs=pl.BlockSpec((SSM_CHUNK, SSM_GROUP, nc), lambda g: (0, g, 0)),
        out_shape=jax.ShapeDtypeStruct((SSM_CHUNK, SSM_WIDTH, nc), F32),
        compiler_params=_params(("parallel",)),
        name="s5_core",
    )(ut, w_toep, w_state, w_carry, pw)


def _glu_kernel(y_ref, w_ref, b_ref, g_ref, o_ref, stage_ref):
    lt = o_ref.shape[0]
    for bl in range(SUBLANES):
        y = y_ref[bl]
        z = jnp.dot(w_ref[...], y.astype(BF16), preferred_element_type=F32) + b_ref[...]
        s = y * jax.nn.sigmoid(z)
        ms = jnp.mean(s * s, axis=0, keepdims=True)
        sn = (s * lax.rsqrt(ms + EPS) * g_ref[...]).T
        for j in range(SSM_WIDTH // LANES):
            stage_ref[j, pl.ds(bl, lt, stride=SUBLANES), :] = sn[:, j * LANES:(j + 1) * LANES]
    for j in range(SSM_WIDTH // LANES):
        o_ref[:, :, j * LANES:(j + 1) * LANES] = stage_ref[j].reshape(lt, SUBLANES, LANES)


def _glu(yt, w_t, b_col, g_col, lt):
    nc = yt.shape[2]
    out = pl.pallas_call(
        _glu_kernel,
        grid=(nc // lt, SSM_CHUNK // SUBLANES),
        in_specs=[pl.BlockSpec((SUBLANES, SSM_WIDTH, lt), lambda i, b: (b, 0, i)),
                  pl.BlockSpec((SSM_WIDTH, SSM_WIDTH), lambda i, b: (0, 0)),
                  pl.BlockSpec((SSM_WIDTH, 1), lambda i, b: (0, 0)),
                  pl.BlockSpec((SSM_WIDTH, 1), lambda i, b: (0, 0))],
        out_specs=pl.BlockSpec((lt, None, SUBLANES, SSM_WIDTH), lambda i, b: (i, b, 0, 0)),
        out_shape=jax.ShapeDtypeStruct((nc, SSM_CHUNK // SUBLANES, SUBLANES, SSM_WIDTH), F32),
        scratch_shapes=[pltpu.VMEM((SSM_WIDTH // LANES, lt * SUBLANES, LANES), F32)],
        compiler_params=_params(("parallel", "parallel")),
        name="glu_norm",
    )(yt, w_t, b_col, g_col)
    return out.reshape(nc * SSM_CHUNK, SSM_WIDTH)


def _outproj_kernel(x_ref, a_ref, s_ref, wa_ref, ws_ref, g_ref, wr_ref, br_ref, tri_ref,
                    x2_ref, h2_ref, rt_ref, gt_ref, cnt_ref, run_ref):
    i = pl.program_id(0)

    @pl.when(i == 0)
    def _():
        run_ref[...] = jnp.zeros_like(run_ref)

    x2 = (x_ref[...] + jnp.dot(a_ref[...], wa_ref[...], preferred_element_type=F32)
          + jnp.dot(s_ref[...].astype(BF16), ws_ref[...], preferred_element_type=F32))
    x2_ref[...] = x2
    h2 = _rms(x2, g_ref[...])
    _tile_rows_store(h2_ref, h2)
    logits = jnp.dot(h2.astype(BF16), wr_ref[...], preferred_element_type=F32) + br_ref[...]

    tm = logits.shape[0]
    lane = lax.broadcasted_iota(I32, (tm, LANES), 1)
    lane_f = lane.astype(F32)
    work = logits
    sel = jnp.zeros((tm, LANES), F32)
    top_v, top_i = [], []
    for _ in range(TOP_K):
        m = jnp.max(work, axis=-1, keepdims=True)
        idx = jnp.min(jnp.where(work == m, lane_f, float(LANES)), axis=-1, keepdims=True).astype(I32)
        hit = lane == idx
        sel = jnp.where(hit, 1.0, sel)
        work = jnp.where(hit, -jnp.inf, work)
        top_v.append(m)
        top_i.append(idx)
    ex = [jnp.exp(v - top_v[0]) for v in top_v]
    den = ex[0] + ex[1] + ex[2] + ex[3]
    before = jnp.dot(tri_ref[...], sel.astype(BF16), preferred_element_type=F32) + run_ref[0:1, :]
    rt = jnp.zeros((tm, LANES), I32)
    gt = jnp.zeros((tm, LANES), F32)
    for k in range(TOP_K):
        rank = jnp.sum(jnp.where(lane == top_i[k], before, 0.0), axis=-1, keepdims=True).astype(I32)
        rt = jnp.where(lane == k, top_i[k], rt)
        rt = jnp.where(lane == TOP_K + k, rank, rt)
        gt = jnp.where(lane == k, ex[k] / den, gt)
    rt_ref[...] = rt
    gt_ref[...] = gt
    run = run_ref[...] + jnp.sum(sel, axis=0, keepdims=True)
    run_ref[...] = run
    cnt_ref[...] = run


def _outproj(x2d, attn_n, ssm_n, w_a, w_s, g, w_r, b_r, tri):
    t = x2d.shape[0]
    tm = ROUTE_TM
    row = lambda i: (i, 0)
    full = lambda i: (0, 0)
    return pl.pallas_call(
        _outproj_kernel,
        grid=(t // tm,),
        in_specs=[pl.BlockSpec((tm, D_MODEL), row), pl.BlockSpec((tm, ATTN_WIDTH), row),
                  pl.BlockSpec((tm, SSM_WIDTH), row),
                  pl.BlockSpec((ATTN_WIDTH, D_MODEL), full), pl.BlockSpec((SSM_WIDTH, D_MODEL), full),
                  pl.BlockSpec((1, D_MODEL), full),
                  pl.BlockSpec((D_MODEL, LANES), full), pl.BlockSpec((1, LANES), full),
                  pl.BlockSpec((tm, tm), full)],
        out_specs=[pl.BlockSpec((tm, D_MODEL), row), pl.BlockSpec((tm * SUBLANES, LANES), row),
                   pl.BlockSpec((tm, LANES), row), pl.BlockSpec((tm, LANES), row),
                   pl.BlockSpec((SUBLANES, LANES), full)],
        out_shape=[jax.ShapeDtypeStruct((t, D_MODEL), F32), jax.ShapeDtypeStruct((t * SUBLANES, LANES), F32),
                   jax.ShapeDtypeStruct((t, LANES), I32), jax.ShapeDtypeStruct((t, LANES), F32),
                   jax.ShapeDtypeStruct((SUBLANES, LANES), F32)],
        scratch_shapes=[pltpu.VMEM((SUBLANES, LANES), F32)],
        compiler_params=_params(("arbitrary",)),
        name="out_projection_router",
    )(x2d, attn_n, ssm_n, w_a, w_s, g, w_r, b_r, tri)


DMA_GROUP = 4


def _scatter_kernel(pad_start_ref, pad_cnt_ref, rows_ref, h_hbm, xb_hbm, zero_ref, sems, zsem, *, tm):
    i = pl.program_id(0)
    nt = pl.num_programs(0)
    n_dma = tm * TOP_K

    def tile_wait(slot):
        whole = xb_hbm.at[pl.ds(0, n_dma * SUBLANES)]
        pltpu.make_async_copy(whole, whole, sems.at[slot]).wait()

    @pl.when(i == 0)
    def _():
        zero_ref[...] = jnp.zeros_like(zero_ref)

        def fill(row, n_rows):
            return pltpu.make_async_copy(zero_ref.at[pl.ds(0, n_rows * SUBLANES)],
                                         xb_hbm.at[pl.ds(pl.multiple_of(row * SUBLANES, SUBLANES), n_rows * SUBLANES)],
                                         zsem)

        for phase in range(2):
            def per_expert(e, c):
                cnt = pad_cnt_ref[e]
                row = pad_start_ref[e]
                run = MOE_ROWS // 2
                while run >= 1:
                    below = cnt & ~(2 * run - 1)

                    @pl.when((cnt & run) != 0)
                    def _(run=run, below=below):
                        cp = fill(row + below, run)
                        cp.start() if phase == 0 else cp.wait()
                    run //= 2
                return c
            lax.fori_loop(0, N_EXPERTS, per_expert, 0)

            def per_tail_block(b, c):
                cp = fill(pad_start_ref[N_EXPERTS] + b * MOE_ROWS, MOE_ROWS)
                cp.start() if phase == 0 else cp.wait()
                return c
            lax.fori_loop(0, pad_cnt_ref[N_EXPERTS] // MOE_ROWS, per_tail_block, 0)

    slot = i % 2

    def issue(grp, c):
        r0 = grp * DMA_GROUP
        dst = [rows_ref[0, 0, (r0 + r) * TOP_K + k] for r in range(DMA_GROUP) for k in range(TOP_K)]
        for r in range(DMA_GROUP):
            src = _tile_row(h_hbm, i * tm + r0 + r)
            for k in range(TOP_K):
                pltpu.make_async_copy(src, _tile_row(xb_hbm, dst[r * TOP_K + k]), sems.at[slot]).start()
        return c
    lax.fori_loop(0, tm // DMA_GROUP, issue, 0)

    @pl.when(i > 0)
    def _():
        tile_wait(1 - slot)

    @pl.when(i == nt - 1)
    def _():
        tile_wait(slot)


def _scatter_rows(pad_start, pad_cnt, rows, h2, n_pad):
    tm = ROUTE_TM
    nt = h2.shape[0] // SUBLANES // tm
    grid_spec = pltpu.PrefetchScalarGridSpec(
        num_scalar_prefetch=2,
        grid=(nt,),
        in_specs=[pl.BlockSpec((1, 1, tm * TOP_K), lambda i, a, b: (i, 0, 0), memory_space=pltpu.SMEM),
                  pl.BlockSpec(memory_space=pl.ANY)],
        out_specs=pl.BlockSpec(memory_space=pl.ANY),
        scratch_shapes=[pltpu.VMEM((MOE_ROWS * SUBLANES, LANES), F32), pltpu.SemaphoreType.DMA((2,)),
                        pltpu.SemaphoreType.DMA(())],
    )
    return pl.pallas_call(
        functools.partial(_scatter_kernel, tm=tm),
        grid_spec=grid_spec,
        out_shape=jax.ShapeDtypeStruct((n_pad * SUBLANES, LANES), F32),
        compiler_params=_params(("arbitrary",)),
        name="scatter_rows",
    )(pad_start, pad_cnt, rows.reshape(nt, 1, tm * TOP_K), h2)


def _expert_kernel(be_ref, grp_ref, ia_ref, ib_ref, nv_ref, xa_ref, xb_ref, wgu_ref, bgu_ref, wd_ref, bd_ref,
                   o_ref, x_scr):
    i = pl.program_id(0)
    live = i < nv_ref[0]

    @pl.when(live & (grp_ref[i] == 0))
    def _():
        x_scr[...] = _tile_rows_load(xa_ref, MOE_ROWS).astype(BF16)

    @pl.when(live & (grp_ref[i] != 0))
    def _():
        x_scr[...] = _tile_rows_load(xb_ref, MOE_ROWS).astype(BF16)

    @pl.when(live)
    def _():
        gu = jnp.dot(x_scr[...], wgu_ref[0], preferred_element_type=F32) + bgu_ref[0]
        gate = jnp.minimum(gu[:, :D_MODEL], SWIGLU_LIMIT)
        up = jnp.clip(gu[:, D_MODEL:], -SWIGLU_LIMIT, SWIGLU_LIMIT)
        act = (up + 1.0) * (gate * jax.nn.sigmoid(SWIGLU_ALPHA * gate))
        _tile_rows_store(o_ref, jnp.dot(act.astype(BF16), wd_ref[0], preferred_element_type=F32) + bd_ref[0])

    @pl.when(i >= nv_ref[0])
    def _():
        o_ref[...] = jnp.zeros_like(o_ref)


def _experts(block_e, block_grp, blk_a, blk_b, n_valid, xa, xb, wgu, bgu, wd, bd):
    nb = block_e.shape[0]
    e3 = lambda i, be, gr, ia, ib, nv: (be[i], 0, 0)
    grid_spec = pltpu.PrefetchScalarGridSpec(
        num_scalar_prefetch=5,
        grid=(nb,),
        in_specs=[pl.BlockSpec((MOE_ROWS * SUBLANES, LANES), lambda i, be, gr, ia, ib, nv: (ia[i], 0)),
                  pl.BlockSpec((MOE_ROWS * SUBLANES, LANES), lambda i, be, gr, ia, ib, nv: (ib[i], 0)),
                  pl.BlockSpec((1, D_MODEL, 2 * D_MODEL), e3), pl.BlockSpec((1, 1, 2 * D_MODEL), e3),
                  pl.BlockSpec((1, D_MODEL, D_MODEL), e3), pl.BlockSpec((1, 1, D_MODEL), e3)],
        out_specs=pl.BlockSpec((MOE_ROWS * SUBLANES, LANES), lambda i, be, gr, ia, ib, nv: (i, 0)),
        scratch_shapes=[pltpu.VMEM((MOE_ROWS, D_MODEL), BF16)],
    )
    return pl.pallas_call(
        _expert_kernel,
        grid_spec=grid_spec,
        out_shape=jax.ShapeDtypeStruct((nb * MOE_ROWS * SUBLANES, LANES), F32),
        compiler_params=_params(("arbitrary",)),
        name="routed_experts",
    )(block_e, block_grp, blk_a, blk_b, n_valid, xa, xb, wgu, bgu, wd, bd)


def _combine_kernel(rows_cur, rows_nxt, x_ref, gt_ref, g_ref, y_hbm, o_ref, buf, sems, *, tm):
    i = pl.program_id(0)
    nt = pl.num_programs(0)
    slot = i % 2

    def issue(rows_ref, dst_slot):
        def body(grp, c):
            r0 = grp * DMA_GROUP
            src = [rows_ref[0, 0, (r0 + r) * TOP_K + k] for r in range(DMA_GROUP) for k in range(TOP_K)]
            for r in range(DMA_GROUP):
                for k in range(TOP_K):
                    pltpu.make_async_copy(_tile_row(y_hbm, src[r * TOP_K + k]),
                                          _tile_row(buf.at[dst_slot, k], r0 + r), sems.at[dst_slot]).start()
            return c
        lax.fori_loop(0, tm // DMA_GROUP, body, 0)

    @pl.when(i == 0)
    def _():
        issue(rows_cur, 0)

    @pl.when(i + 1 < nt)
    def _():
        issue(rows_nxt, 1 - slot)

    pltpu.make_async_copy(buf.at[slot], buf.at[slot], sems.at[slot]).wait()
    acc = x_ref[...]
    gt = gt_ref[...]
    for k in range(TOP_K):
        acc = acc + _tile_rows_load(buf.at[slot, k], tm) * gt[:, k:k + 1]
    o_ref[...] = _rms(acc, g_ref[...])


def _combine(rows, x2, gates, g, yb):
    t = x2.shape[0]
    tm = COMBINE_TM
    nt = t // tm
    rows3 = rows.reshape(nt, 1, tm * TOP_K)
    row = lambda i: (i, 0)
    return pl.pallas_call(
        functools.partial(_combine_kernel, tm=tm),
        grid=(nt,),
        in_specs=[pl.BlockSpec((1, 1, tm * TOP_K), lambda i: (i, 0, 0), memory_space=pltpu.SMEM),
                  pl.BlockSpec((1, 1, tm * TOP_K), lambda i: (jnp.minimum(i + 1, nt - 1), 0, 0),
                               memory_space=pltpu.SMEM),
                  pl.BlockSpec((tm, D_MODEL), row), pl.BlockSpec((tm, LANES), row),
                  pl.BlockSpec((1, D_MODEL), lambda i: (0, 0)),
                  pl.BlockSpec(memory_space=pl.ANY)],
        out_specs=pl.BlockSpec((tm, D_MODEL), row),
        out_shape=jax.ShapeDtypeStruct((t, D_MODEL), F32),
        scratch_shapes=[pltpu.VMEM((2, TOP_K, tm * SUBLANES, LANES), F32), pltpu.SemaphoreType.DMA((2,))],
        compiler_params=_params(("arbitrary",)),
        name="combine_final_norm",
    )(rows3, rows3, x2, gates, g, yb)


def _rope_tables(seq_len):
    inv_freq = ROPE_THETA ** (-jnp.arange(ROPE_HALF, dtype=F32) * 2.0 / ROPE_DIM)
    ang = jnp.arange(seq_len, dtype=F32)[:, None] * inv_freq[None, :]
    cos, sin = jnp.cos(ang), jnp.sin(ang)
    pad = HEAD_DIM - ROPE_DIM
    ones = jnp.ones((seq_len, pad), F32)
    zer_h = jnp.zeros((seq_len, ROPE_HALF), F32)
    zer_p = jnp.zeros((seq_len, pad), F32)
    c = jnp.concatenate([cos, cos, ones], axis=1)
    s1 = jnp.concatenate([zer_h, sin, zer_p], axis=1)
    s2 = jnp.concatenate([-sin, zer_h, zer_p], axis=1)
    rep = LANES // HEAD_DIM
    return jnp.tile(c, (1, rep)), jnp.tile(s1, (1, rep)), jnp.tile(s2, (1, rep))


def _ssm_weights(a_re, a_im, log_dt, b_re, b_im, c_re, c_im, ssm_d, nsteps):
    r = SSM_CHUNK
    dt = jnp.exp(log_dt)[..., None]
    lr, li = a_re * dt, a_im * dt

    def cpow(n):
        n = jnp.asarray(n, F32)[..., None, None, None]
        mag = jnp.exp(n * lr)
        return mag * jnp.cos(n * li), mag * jnp.sin(n * li)

    ab_re, ab_im = cpow(jnp.ones(()))
    den = a_re * a_re + a_im * a_im
    num_re, num_im = ab_re - 1.0, ab_im
    f_re = (num_re * a_re + num_im * a_im) / den
    f_im = (num_im * a_re - num_re * a_im) / den
    bb_re = f_re[..., None] * b_re - f_im[..., None] * b_im
    bb_im = f_re[..., None] * b_im + f_im[..., None] * b_re

    taus = jnp.arange(r + 1, dtype=F32)
    p_re, p_im = cpow(taus)
    m_re = p_re[..., None] * bb_re - p_im[..., None] * bb_im
    m_im = p_re[..., None] * bb_im + p_im[..., None] * bb_re
    kern = (jnp.einsum('dgcp,tdgpk->tdgck', c_re, m_re) - jnp.einsum('dgcp,tdgpk->tdgck', c_im, m_im))
    bi = jnp.arange(r)
    lag = bi[:, None] - bi[None, :]
    kf = kern[jnp.clip(lag, 0, r), 0] * (lag >= 0)[:, :, None, None, None]
    kb = kern[jnp.clip(-lag, 0, r), 1] * (lag <= 0)[:, :, None, None, None]
    eye = (lag == 0)[:, :, None, None, None] * (jnp.eye(SSM_GROUP, dtype=F32) * ssm_d[:, :, None])[None, None]
    w_toep = (kf + kb + eye).transpose(2, 0, 3, 1, 4).reshape(SSM_N_GROUPS, TOEP, TOEP)

    ef = (r - 1 - bi)
    st_f_re, st_f_im = m_re[ef, 0], m_im[ef, 0]
    st_b_re, st_b_im = m_re[bi, 1], m_im[bi, 1]
    w_state = jnp.stack([st_f_re, st_f_im, st_b_re, st_b_im], axis=0)
    w_state = w_state.transpose(2, 0, 3, 1, 4).reshape(SSM_N_GROUPS, 4 * SSM_STATE, TOEP)

    def c_times_pow(d, expo):
        pr, pi = p_re[expo, d], p_im[expo, d]
        zr = c_re[d][None] * pr[:, :, None, :] - c_im[d][None] * pi[:, :, None, :]
        zi = c_re[d][None] * pi[:, :, None, :] + c_im[d][None] * pr[:, :, None, :]
        return zr, -zi
    cf_re, cf_im = c_times_pow(0, bi + 1)
    cb_re, cb_im = c_times_pow(1, r - bi)
    w_carry = jnp.stack([cf_re, cf_im, cb_re, cb_im], axis=3)
    w_carry = w_carry.transpose(1, 0, 2, 3, 4).reshape(SSM_N_GROUPS, TOEP, 4 * SSM_STATE)

    qr, qi = p_re[r], p_im[r]
    cols = []
    for _ in range(nsteps):
        cols.append(jnp.stack([qr[0], qi[0], qr[1], qi[1]], axis=1))
        qr, qi = qr * qr - qi * qi, 2.0 * qr * qi
    pw = jnp.stack(cols, axis=-1).reshape(SSM_N_GROUPS, 4 * SSM_STATE, nsteps)
    return w_toep.astype(BF16), w_state.astype(BF16), w_carry.astype(BF16), pw


def _front(x, prm):
    n, seq_len, _ = x.shape
    t = n * seq_len
    x2d = x.reshape(t, D_MODEL)
    q, ka, kb, va, vb = _qkv(x2d, prm['norm1_g'], prm['w_qkv'], *_rope_tables(seq_len), seq_len)
    attn_n = _attention(q, ka, kb, va, vb, prm['sink'], prm['attn_out_g'], seq_len)

    chunks_per_seq = seq_len // SSM_CHUNK
    nsteps = max(1, (chunks_per_seq - 1).bit_length())
    nc = t // SSM_CHUNK
    lt = min(256, nc)
    ut = _uproj(x2d, prm['norm1_g'], prm['w_u_t'], lt)
    yt = _ssm(ut, prm['w_toep'], prm['w_state'], prm['w_carry'], prm['pw'][:, :, :nsteps], chunks_per_seq)
    ssm_n = _glu(yt, prm['glu_w_t'], prm['glu_b'], prm['ssm_out_g'], lt)
    return _outproj(x2d, attn_n, ssm_n, prm['w_out_a'], prm['w_out_s'], prm['norm2_g'],
                    prm['router_w'], prm['router_b'], prm['tri'])


def _by_expert(idx, table):
    hit = idx[..., None] == jnp.arange(N_EXPERTS, dtype=I32)
    return jnp.sum(jnp.where(hit, table, 0), axis=-1)


def kernel(x_prompt, x_sample, norm1_g, w_in, attn_sink, ssm_a_re, ssm_a_im, ssm_log_dt, ssm_b_re, ssm_b_im, ssm_c_re, ssm_c_im, ssm_d, glu_w, glu_b, attn_out_g, ssm_out_g, w_out, norm2_g, router_w, router_b, w_gate_up, b_gate_up, w_down, b_down, final_g):
    assert norm1_g.shape[0] == 1, "single-layer problem"
    l = 0
    xs = [x_prompt, x_sample]
    max_chunks = max(x.shape[1] for x in xs) // SSM_CHUNK
    max_steps = max(1, (max_chunks - 1).bit_length())
    wq, wk, wv, wu = jnp.split(w_in[l], [ATTN_WIDTH, ATTN_WIDTH + KV_WIDTH, ATTN_WIDTH + 2 * KV_WIDTH], axis=1)
    swap = lambda w: jnp.concatenate([w[:, HEAD_DIM:], w[:, :HEAD_DIM]], axis=1)
    w_toep, w_state, w_carry, pw = _ssm_weights(ssm_a_re[l], ssm_a_im[l], ssm_log_dt[l], ssm_b_re[l], ssm_b_im[l],
                                                ssm_c_re[l], ssm_c_im[l], ssm_d[l], max_steps)
    tri_i = lax.broadcasted_iota(I32, (ROUTE_TM, ROUTE_TM), 0)
    tri_j = lax.broadcasted_iota(I32, (ROUTE_TM, ROUTE_TM), 1)
    prm = dict(
        norm1_g=norm1_g[l].reshape(1, D_MODEL),
        w_qkv=jnp.concatenate([wq, wk, swap(wk), wv, swap(wv)], axis=1).astype(BF16),
        w_u_t=wu.T.astype(BF16),
        sink=attn_sink[l].astype(F32),
        attn_out_g=attn_out_g[l].reshape(1, ATTN_WIDTH),
        w_toep=w_toep, w_state=w_state, w_carry=w_carry, pw=pw,
        glu_w_t=glu_w[l].T.astype(BF16),
        glu_b=glu_b[l].reshape(SSM_WIDTH, 1),
        ssm_out_g=ssm_out_g[l].reshape(SSM_WIDTH, 1),
        w_out_a=w_out[l][:ATTN_WIDTH].astype(BF16),
        w_out_s=w_out[l][ATTN_WIDTH:].astype(BF16),
        norm2_g=norm2_g[l].reshape(1, D_MODEL),
        router_w=jnp.pad(router_w[l], ((0, 0), (0, LANES - N_EXPERTS))).astype(BF16),
        router_b=jnp.pad(router_b[l], (0, LANES - N_EXPERTS), constant_values=NEG_BIG).reshape(1, LANES),
        tri=(tri_j < tri_i).astype(BF16),
    )
    fronts = [_front(x, prm) for x in xs]

    cnts = [f[4][0, :N_EXPERTS].astype(I32) for f in fronts]
    padded = [(c + MOE_ROWS - 1) // MOE_ROWS * MOE_ROWS for c in cnts]
    pends = [jnp.cumsum(p) for p in padded]
    pstarts = [pe - p for pe, p in zip(pends, padded)]
    nbs = [f[0].shape[0] * TOP_K // MOE_ROWS + N_EXPERTS for f in fronts]
    seg_blocks = jnp.stack([p // MOE_ROWS for p in padded], axis=1).reshape(-1)
    seg_end = jnp.cumsum(seg_blocks)
    seg_start = seg_end - seg_blocks
    nb = sum(nbs)
    bi = jnp.arange(nb, dtype=I32)
    seg = jnp.minimum(jnp.searchsorted(seg_end, bi, side='right'), 2 * N_EXPERTS - 1).astype(I32)
    block_e = seg // 2
    block_grp = seg % 2
    n_valid = seg_end[-1].astype(I32).reshape(1)
    live = bi < n_valid[0]
    src_start = jnp.stack([ps // MOE_ROWS for ps in pstarts], axis=1).reshape(-1)
    src_blk = src_start[seg] + bi - seg_start[seg]
    blk = [lax.cummax(jnp.where(live & (block_grp == g), src_blk, 0)).astype(I32) for g in range(2)]

    xbufs, yrows = [], []
    for g, f in enumerate(fronts):
        e_idx = f[2][:, :TOP_K]
        rank = f[2][:, TOP_K:2 * TOP_K]
        n_pad = nbs[g] * MOE_ROWS
        pad_start = jnp.concatenate([pstarts[g] + cnts[g], pends[g][-1:]]).astype(I32)
        pad_cnt = jnp.concatenate([padded[g] - cnts[g], n_pad - pends[g][-1:]]).astype(I32)
        xrows = (_by_expert(e_idx, pstarts[g]) + rank).astype(I32)
        xbufs.append(_scatter_rows(pad_start, pad_cnt, xrows, f[1], n_pad))
        yrows.append((_by_expert(e_idx, seg_start[g::2] * MOE_ROWS) + rank).astype(I32))
    yb = _experts(block_e, block_grp, blk[0], blk[1], n_valid, xbufs[0], xbufs[1],
                  w_gate_up[l].astype(BF16), b_gate_up[l][:, None, :], w_down[l].astype(BF16), b_down[l][:, None, :])
    gfin = final_g.reshape(1, D_MODEL)
    outs = [_combine(r, f[0], f[3], gfin, yb).reshape(x.shape) for x, f, r in zip(xs, fronts, yrows)]
    return tuple(outs)
```

```python
import functools
import math

import jax
import jax.numpy as jnp
from jax import lax
from jax.experimental import pallas as pl
from jax.experimental.pallas import tpu as pltpu

F32 = jnp.float32
BF16 = jnp.bfloat16
I32 = jnp.int32

D_MODEL = 1024
HEAD_DIM = 64
N_Q_HEADS = 8
N_KV_HEADS = 2
Q_PER_KV = N_Q_HEADS // N_KV_HEADS
ATTN_WIDTH = N_Q_HEADS * HEAD_DIM
KV_WIDTH = N_KV_HEADS * HEAD_DIM
WINDOW = 128
ATT_BLOCK = 128
ROPE_THETA = 500000.0
ROPE_DIM = HEAD_DIM // 4
ROPE_HALF = ROPE_DIM // 2
SSM_WIDTH = 512
SSM_GROUP = 16
SSM_N_GROUPS = SSM_WIDTH // SSM_GROUP
SSM_STATE = 64
N_EXPERTS = 32
TOP_K = 4
SWIGLU_LIMIT = 7.0
SWIGLU_ALPHA = 1.702
EPS = 1e-5

LANES = 128
SUBLANES = 8
SSM_CHUNK = 32
TOEP = SSM_CHUNK * SSM_GROUP
MOE_ROWS = 512
ROUTE_TM = 512
COMBINE_TM = 128
VMEM_LIMIT = 52 * 1024 * 1024
NEG_BIG = -1e30


def _params(sem, vmem=VMEM_LIMIT):
    return pltpu.CompilerParams(dimension_semantics=sem, vmem_limit_bytes=vmem)


def _rms(x, g):
    ms = jnp.mean(x * x, axis=-1, keepdims=True)
    return x * lax.rsqrt(ms + EPS) * g


def _tile_rows_load(ref, rows):
    return jnp.concatenate([ref[pl.ds(j, rows, stride=SUBLANES), :] for j in range(SUBLANES)], axis=1)


def _tile_rows_store(ref, val):
    rows = val.shape[0]
    for j in range(SUBLANES):
        ref[pl.ds(j, rows, stride=SUBLANES), :] = val[:, j * LANES:(j + 1) * LANES]


def _tile_row(ref, row):
    return ref.at[pl.ds(pl.multiple_of(row * SUBLANES, SUBLANES), SUBLANES)]


def _qkv_kernel(x_ref, g_ref, w_ref, c_ref, s1_ref, s2_ref, q_ref, ka_ref, kb_ref, va_ref, vb_ref):
    h = _rms(x_ref[...], g_ref[...]).astype(BF16)
    p = jnp.dot(h, w_ref[...], preferred_element_type=F32)
    c = c_ref[...]
    s1 = s1_ref[...]
    s2 = s2_ref[...]

    def rot(t):
        return t * c + pltpu.roll(t, ROPE_HALF, 1) * s1 + pltpu.roll(t, LANES - ROPE_HALF, 1) * s2

    for j in range(ATTN_WIDTH // LANES):
        q_ref[:, j * LANES:(j + 1) * LANES] = (rot(p[:, j * LANES:(j + 1) * LANES]) * (HEAD_DIM ** -0.5)).astype(BF16)
    ka_ref[...] = rot(p[:, 512:640]).astype(BF16)
    kb_ref[...] = rot(p[:, 640:768]).astype(BF16)
    va_ref[...] = p[:, 768:896].astype(BF16)
    vb_ref[...] = p[:, 896:1024].astype(BF16)


def _qkv(x2d, g, w, c, s1, s2, seq_len, tm=512):
    t = x2d.shape[0]
    nlb = seq_len // tm
    row = lambda i: (i, 0)
    tab = lambda i: (i % nlb, 0)
    full = lambda i: (0, 0)
    kv = jax.ShapeDtypeStruct((t, LANES), BF16)
    return pl.pallas_call(
        _qkv_kernel,
        grid=(t // tm,),
        in_specs=[pl.BlockSpec((tm, D_MODEL), row), pl.BlockSpec((1, D_MODEL), full),
                  pl.BlockSpec((D_MODEL, 1024), full),
                  pl.BlockSpec((tm, LANES), tab), pl.BlockSpec((tm, LANES), tab), pl.BlockSpec((tm, LANES), tab)],
        out_specs=[pl.BlockSpec((tm, ATTN_WIDTH), row)] + [pl.BlockSpec((tm, LANES), row)] * 4,
        out_shape=[jax.ShapeDtypeStruct((t, ATTN_WIDTH), BF16), kv, kv, kv, kv],
        compiler_params=_params(("parallel",)),
        name="qkv_rotary",
    )(x2d, g, w, c, s1, s2)


def _attn_kernel(sink_ref, q_ref, kap, kac, kan, kbp, kbc, kbn, vap, vac, van, vbp, vbc, vbn, g_ref, o_ref, *, bps):
    i = pl.program_id(0)
    first = (i % bps) == 0
    last = (i % bps) == bps - 1
    qi = lax.broadcasted_iota(I32, (ATT_BLOCK, 3 * ATT_BLOCK), 0)
    kj = lax.broadcasted_iota(I32, (ATT_BLOCK, 3 * ATT_BLOCK), 1)
    rel = kj - ATT_BLOCK - qi
    valid = (jnp.abs(rel) <= WINDOW)
    valid = valid & ((kj >= ATT_BLOCK) | jnp.logical_not(first))
    valid = valid & ((kj < 2 * ATT_BLOCK) | jnp.logical_not(last))
    ks = (jnp.concatenate([kap[...], kac[...], kan[...]], axis=0), jnp.concatenate([kbp[...], kbc[...], kbn[...]], axis=0))
    vs = (jnp.concatenate([vap[...], vac[...], van[...]], axis=0), jnp.concatenate([vbp[...], vbc[...], vbn[...]], axis=0))
    lo = lax.broadcasted_iota(I32, (ATT_BLOCK, LANES), 1) < HEAD_DIM
    zero = jnp.zeros((ATT_BLOCK, LANES), BF16)
    tiles = []
    for j in range(ATTN_WIDTH // LANES):
        kvh = j // 2
        qt = q_ref[:, j * LANES:(j + 1) * LANES]
        halves = []
        for par in range(2):
            qm = jnp.where(lo if par == 0 else jnp.logical_not(lo), qt, zero)
            s = lax.dot_general(qm, ks[kvh], (((1,), (1,)), ((), ())), preferred_element_type=F32)
            s = jnp.where(valid, s, NEG_BIG)
            sk = sink_ref[2 * j + par]
            m = jnp.maximum(jnp.max(s, axis=-1, keepdims=True), sk)
            p = jnp.exp(s - m)
            den = jnp.sum(p, axis=-1, keepdims=True) + jnp.exp(sk - m)
            o = jnp.dot(p.astype(BF16), vs[kvh], preferred_element_type=F32)
            halves.append(o / den)
        tiles.append(jnp.where(lo, halves[0], halves[1]))
    o = jnp.concatenate(tiles, axis=1)
    o_ref[...] = _rms(o, g_ref[...]).astype(BF16)


def _attention(q, ka, kb, va, vb, sink, g, seq_len):
    t = q.shape[0]
    nblk = t // ATT_BLOCK
    bps = seq_len // ATT_BLOCK
    cur = lambda i, s: (i, 0)
    prv = lambda i, s: (jnp.maximum(i - 1, 0), 0)
    nxt = lambda i, s: (jnp.minimum(i + 1, nblk - 1), 0)
    kvs = lambda: [pl.BlockSpec((ATT_BLOCK, LANES), prv), pl.BlockSpec((ATT_BLOCK, LANES), cur),
                   pl.BlockSpec((ATT_BLOCK, LANES), nxt)]
    grid_spec = pltpu.PrefetchScalarGridSpec(
        num_scalar_prefetch=1,
        grid=(nblk,),
        in_specs=[pl.BlockSpec((ATT_BLOCK, ATTN_WIDTH), cur)] + kvs() + kvs() + kvs() + kvs()
                 + [pl.BlockSpec((1, ATTN_WIDTH), lambda i, s: (0, 0))],
        out_specs=pl.BlockSpec((ATT_BLOCK, ATTN_WIDTH), cur),
    )
    return pl.pallas_call(
        functools.partial(_attn_kernel, bps=bps),
        grid_spec=grid_spec,
        out_shape=jax.ShapeDtypeStruct((t, ATTN_WIDTH), BF16),
        compiler_params=_params(("parallel",)),
        name="banded_attention",
    )(sink, q, ka, ka, ka, kb, kb, kb, va, va, va, vb, vb, vb, g)


def _uproj_kernel(x_ref, g_ref, w_ref, o_ref, stage_ref):
    lt = x_ref.shape[0]
    h = _rms(x_ref[...].reshape(lt * SUBLANES, D_MODEL), g_ref[...])
    for j in range(D_MODEL // LANES):
        stage_ref[j] = h[:, j * LANES:(j + 1) * LANES]
    for bl in range(SUBLANES):
        hb = jnp.concatenate([stage_ref[j, pl.ds(bl, lt, stride=SUBLANES), :] for j in range(D_MODEL // LANES)],
                             axis=1).astype(BF16)
        ut = lax.dot_general(w_ref[...], hb, (((1,), (1,)), ((), ())), preferred_element_type=F32)
        o_ref[bl] = ut.astype(BF16)


def _uproj(x2d, g, w_t, lt):
    t = x2d.shape[0]
    nc = t // SSM_CHUNK
    xv = x2d.reshape(nc, SSM_CHUNK // SUBLANES, SUBLANES, D_MODEL)
    return pl.pallas_call(
        _uproj_kernel,
        grid=(nc // lt, SSM_CHUNK // SUBLANES),
        in_specs=[pl.BlockSpec((lt, None, SUBLANES, D_MODEL), lambda i, b: (i, b, 0, 0)),
                  pl.BlockSpec((1, D_MODEL), lambda i, b: (0, 0)),
                  pl.BlockSpec((SSM_WIDTH, D_MODEL), lambda i, b: (0, 0))],
        out_specs=pl.BlockSpec((SUBLANES, SSM_WIDTH, lt), lambda i, b: (b, 0, i)),
        out_shape=jax.ShapeDtypeStruct((SSM_CHUNK, SSM_WIDTH, nc), BF16),
        scratch_shapes=[pltpu.VMEM((D_MODEL // LANES, lt * SUBLANES, LANES), F32)],
        compiler_params=_params(("parallel", "parallel")),
        name="u_projection",
    )(xv, g, w_t)


def _gelu_tanh(x):
    return 0.5 * x * (1.0 + jnp.tanh(math.sqrt(2.0 / math.pi) * (x + 0.044715 * (x * x * x))))


def _ssm_kernel(a_ref, wt_ref, ws_ref, wc_ref, pw_ref, y_ref, *, chunks_per_seq, nsteps):
    nc = a_ref.shape[2]
    a = a_ref[...].reshape(TOEP, nc)
    y = jnp.dot(wt_ref[0], a, preferred_element_type=F32)
    s = jnp.dot(ws_ref[0], a, preferred_element_type=F32)
    pos = lax.broadcasted_iota(I32, (SSM_STATE, nc), 1) % chunks_per_seq
    carries = []
    for d in range(2):
        hr = s[2 * d * SSM_STATE:(2 * d + 1) * SSM_STATE]
        hi = s[(2 * d + 1) * SSM_STATE:(2 * d + 2) * SSM_STATE]
        for k in range(nsteps):
            sh = 1 << k
            pr = pw_ref[0, 2 * d * SSM_STATE:(2 * d + 1) * SSM_STATE, k:k + 1]
            pi = pw_ref[0, (2 * d + 1) * SSM_STATE:(2 * d + 2) * SSM_STATE, k:k + 1]
            if d == 0:
                ok = pos >= sh
                sr = pltpu.roll(hr, sh, 1)
                si = pltpu.roll(hi, sh, 1)
            else:
                ok = pos < chunks_per_seq - sh
                sr = pltpu.roll(hr, nc - sh, 1)
                si = pltpu.roll(hi, nc - sh, 1)
            hr, hi = (hr + jnp.where(ok, pr * sr - pi * si, 0.0),
                      hi + jnp.where(ok, pr * si + pi * sr, 0.0))
        if d == 0:
            ok = pos >= 1
            cr = pltpu.roll(hr, 1, 1)
            ci = pltpu.roll(hi, 1, 1)
        else:
            ok = pos < chunks_per_seq - 1
            cr = pltpu.roll(hr, nc - 1, 1)
            ci = pltpu.roll(hi, nc - 1, 1)
        carries += [jnp.where(ok, cr, 0.0), jnp.where(ok, ci, 0.0)]
    carry = jnp.concatenate(carries, axis=0).astype(BF16)
    y = y + jnp.dot(wc_ref[0], carry, preferred_element_type=F32)
    y_ref[...] = _gelu_tanh(y).reshape(SSM_CHUNK, SSM_GROUP, nc)


def _ssm(ut, w_toep, w_state, w_carry, pw, chunks_per_seq):
    nc = ut.shape[2]
    nsteps = max(1, (chunks_per_seq - 1).bit_length())
    g3 = lambda g: (g, 0, 0)
    return pl.pallas_call(
        functools.partial(_ssm_kernel, chunks_per_seq=chunks_per_seq, nsteps=nsteps),
        grid=(SSM_N_GROUPS,),
        in_specs=[pl.BlockSpec((SSM_CHUNK, SSM_GROUP, nc), lambda g: (0, g, 0)),
                  pl.BlockSpec((1, TOEP, TOEP), g3),
                  pl.BlockSpec((1, 4 * SSM_STATE, TOEP), g3),
                  pl.BlockSpec((1, TOEP, 4 * SSM_STATE), g3),
                  pl.BlockSpec((1, 4 * SSM_STATE, pw.shape[2]), g3)],
        out_specs=pl.BlockSpec((SSM_CHUNK, SSM_GROUP, nc), lambda g: (0, g, 0)),
        out_shape=jax.ShapeDtypeStruct((SSM_CHUNK, SSM_WIDTH, nc), F32),
        compiler_params=_params(("parallel",)),
        name="s5_core",
    )(ut, w_toep, w_state, w_carry, pw)


def _glu_kernel(y_ref, w_ref, b_ref, g_ref, o_ref, stage_ref):
    lt = o_ref.shape[0]
    for bl in range(SUBLANES):
        y = y_ref[bl]
        z = jnp.dot(w_ref[...], y.astype(BF16), preferred_element_type=F32) + b_ref[...]
        s = y * jax.nn.sigmoid(z)
        ms = jnp.mean(s * s, axis=0, keepdims=True)
        sn = (s * lax.rsqrt(ms + EPS) * g_ref[...]).T
        for j in range(SSM_WIDTH // LANES):
            stage_ref[j, pl.ds(bl, lt, stride=SUBLANES), :] = sn[:, j * LANES:(j + 1) * LANES]
    for j in range(SSM_WIDTH // LANES):
        o_ref[:, :, j * LANES:(j + 1) * LANES] = stage_ref[j].reshape(lt, SUBLANES, LANES)


def _glu(yt, w_t, b_col, g_col, lt):
    nc = yt.shape[2]
    out = pl.pallas_call(
        _glu_kernel,
        grid=(nc // lt, SSM_CHUNK // SUBLANES),
        in_specs=[pl.BlockSpec((SUBLANES, SSM_WIDTH, lt), lambda i, b: (b, 0, i)),
                  pl.BlockSpec((SSM_WIDTH, SSM_WIDTH), lambda i, b: (0, 0)),
                  pl.BlockSpec((SSM_WIDTH, 1), lambda i, b: (0, 0)),
                  pl.BlockSpec((SSM_WIDTH, 1), lambda i, b: (0, 0))],
        out_specs=pl.BlockSpec((lt, None, SUBLANES, SSM_WIDTH), lambda i, b: (i, b, 0, 0)),
        out_shape=jax.ShapeDtypeStruct((nc, SSM_CHUNK // SUBLANES, SUBLANES, SSM_WIDTH), F32),
        scratch_shapes=[pltpu.VMEM((SSM_WIDTH // LANES, lt * SUBLANES, LANES), F32)],
        compiler_params=_params(("parallel", "parallel")),
        name="glu_norm",
    )(yt, w_t, b_col, g_col)
    return out.reshape(nc * SSM_CHUNK, SSM_WIDTH)


def _outproj_kernel(x_ref, a_ref, s_ref, wa_ref, ws_ref, g_ref, wr_ref, br_ref, tri_ref,
                    x2_ref, h2_ref, rt_ref, gt_ref, cnt_ref, run_ref):
    i = pl.program_id(0)

    @pl.when(i == 0)
    def _():
        run_ref[...] = jnp.zeros_like(run_ref)

    x2 = (x_ref[...] + jnp.dot(a_ref[...], wa_ref[...], preferred_element_type=F32)
          + jnp.dot(s_ref[...].astype(BF16), ws_ref[...], preferred_element_type=F32))
    x2_ref[...] = x2
    h2 = _rms(x2, g_ref[...])
    _tile_rows_store(h2_ref, h2)
    logits = jnp.dot(h2.astype(BF16), wr_ref[...], preferred_element_type=F32) + br_ref[...]

    tm = logits.shape[0]
    lane = lax.broadcasted_iota(I32, (tm, LANES), 1)
    lane_f = lane.astype(F32)
    work = logits
    sel = jnp.zeros((tm, LANES), F32)
    top_v, top_i = [], []
    for _ in range(TOP_K):
        m = jnp.max(work, axis=-1, keepdims=True)
        idx = jnp.min(jnp.where(work == m, lane_f, float(LANES)), axis=-1, keepdims=True).astype(I32)
        hit = lane == idx
        sel = jnp.where(hit, 1.0, sel)
        work = jnp.where(hit, -jnp.inf, work)
        top_v.append(m)
        top_i.append(idx)
    ex = [jnp.exp(v - top_v[0]) for v in top_v]
    den = ex[0] + ex[1] + ex[2] + ex[3]
    before = jnp.dot(tri_ref[...], sel.astype(BF16), preferred_element_type=F32) + run_ref[0:1, :]
    rt = jnp.zeros((tm, LANES), I32)
    gt = jnp.zeros((tm, LANES), F32)
    for k in range(TOP_K):
        rank = jnp.sum(jnp.where(lane == top_i[k], before, 0.0), axis=-1, keepdims=True).astype(I32)
        rt = jnp.where(lane == k, top_i[k], rt)
        rt = jnp.where(lane == TOP_K + k, rank, rt)
        gt = jnp.where(lane == k, ex[k] / den, gt)
    rt_ref[...] = rt
    gt_ref[...] = gt
    run = run_ref[...] + jnp.sum(sel, axis=0, keepdims=True)
    run_ref[...] = run
    cnt_ref[...] = run


def _outproj(x2d, attn_n, ssm_n, w_a, w_s, g, w_r, b_r, tri):
    t = x2d.shape[0]
    tm = ROUTE_TM
    row = lambda i: (i, 0)
    full = lambda i: (0, 0)
    return pl.pallas_call(
        _outproj_kernel,
        grid=(t // tm,),
        in_specs=[pl.BlockSpec((tm, D_MODEL), row), pl.BlockSpec((tm, ATTN_WIDTH), row),
                  pl.BlockSpec((tm, SSM_WIDTH), row),
                  pl.BlockSpec((ATTN_WIDTH, D_MODEL), full), pl.BlockSpec((SSM_WIDTH, D_MODEL), full),
                  pl.BlockSpec((1, D_MODEL), full),
                  pl.BlockSpec((D_MODEL, LANES), full), pl.BlockSpec((1, LANES), full),
                  pl.BlockSpec((tm, tm), full)],
        out_specs=[pl.BlockSpec((tm, D_MODEL), row), pl.BlockSpec((tm * SUBLANES, LANES), row),
                   pl.BlockSpec((tm, LANES), row), pl.BlockSpec((tm, LANES), row),
                   pl.BlockSpec((SUBLANES, LANES), full)],
        out_shape=[jax.ShapeDtypeStruct((t, D_MODEL), F32), jax.ShapeDtypeStruct((t * SUBLANES, LANES), F32),
                   jax.ShapeDtypeStruct((t, LANES), I32), jax.ShapeDtypeStruct((t, LANES), F32),
                   jax.ShapeDtypeStruct((SUBLANES, LANES), F32)],
        scratch_shapes=[pltpu.VMEM((SUBLANES, LANES), F32)],
        compiler_params=_params(("arbitrary",)),
        name="out_projection_router",
    )(x2d, attn_n, ssm_n, w_a, w_s, g, w_r, b_r, tri)


DMA_GROUP = 4


def _scatter_kernel(pad_start_ref, pad_cnt_ref, rows_ref, h_ref, xb_hbm, zero_ref, stage_ref, sems, zsem, *, tm):
    i = pl.program_id(0)
    nt = pl.num_programs(0)
    n_dma = tm * TOP_K

    def tile_wait(slot):
        whole = xb_hbm.at[pl.ds(0, n_dma * SUBLANES)]
        pltpu.make_async_copy(whole, whole, sems.at[slot]).wait()

    @pl.when(i == 0)
    def _():
        zero_ref[...] = jnp.zeros_like(zero_ref)

        def fill(row, n_rows):
            return pltpu.make_async_copy(zero_ref.at[pl.ds(0, n_rows * SUBLANES)],
                                         xb_hbm.at[pl.ds(pl.multiple_of(row * SUBLANES, SUBLANES), n_rows * SUBLANES)],
                                         zsem)

        for phase in range(2):
            def per_expert(e, c):
                cnt = pad_cnt_ref[e]
                row = pad_start_ref[e]
                run = MOE_ROWS // 2
                while run >= 1:
                    below = cnt & ~(2 * run - 1)

                    @pl.when((cnt & run) != 0)
                    def _(run=run, below=below):
                        cp = fill(row + below, run)
                        cp.start() if phase == 0 else cp.wait()
                    run //= 2
                return c
            lax.fori_loop(0, N_EXPERTS, per_expert, 0)

            def per_tail_block(b, c):
                cp = fill(pad_start_ref[N_EXPERTS] + b * MOE_ROWS, MOE_ROWS)
                cp.start() if phase == 0 else cp.wait()
                return c
            lax.fori_loop(0, pad_cnt_ref[N_EXPERTS] // MOE_ROWS, per_tail_block, 0)

    slot = i % 2
    stage_ref[slot] = h_ref[...]

    def issue(grp, c):
        r0 = grp * DMA_GROUP
        dst = [rows_ref[0, 0, (r0 + r) * TOP_K + k] for r in range(DMA_GROUP) for k in range(TOP_K)]
        for r in range(DMA_GROUP):
            src = _tile_row(stage_ref.at[slot], r0 + r)
            for k in range(TOP_K):
                pltpu.make_async_copy(src, _tile_row(xb_hbm, dst[r * TOP_K + k]), sems.at[slot]).start()
        return c
    lax.fori_loop(0, tm // DMA_GROUP, issue, 0)

    @pl.when(i > 0)
    def _():
        tile_wait(1 - slot)

    @pl.when(i == nt - 1)
    def _():
        tile_wait(slot)


def _scatter_rows(pad_start, pad_cnt, rows, h2, n_pad):
    tm = ROUTE_TM
    nt = h2.shape[0] // SUBLANES // tm
    grid_spec = pltpu.PrefetchScalarGridSpec(
        num_scalar_prefetch=2,
        grid=(nt,),
        in_specs=[pl.BlockSpec((1, 1, tm * TOP_K), lambda i, a, b: (i, 0, 0), memory_space=pltpu.SMEM),
                  pl.BlockSpec((tm * SUBLANES, LANES), lambda i, a, b: (i, 0))],
        out_specs=pl.BlockSpec(memory_space=pl.ANY),
        scratch_shapes=[pltpu.VMEM((MOE_ROWS * SUBLANES, LANES), F32),
                        pltpu.VMEM((2, tm * SUBLANES, LANES), F32),
                        pltpu.SemaphoreType.DMA((2,)), pltpu.SemaphoreType.DMA(())],
    )
    return pl.pallas_call(
        functools.partial(_scatter_kernel, tm=tm),
        grid_spec=grid_spec,
        out_shape=jax.ShapeDtypeStruct((n_pad * SUBLANES, LANES), F32),
        compiler_params=_params(("arbitrary",)),
        name="scatter_rows",
    )(pad_start, pad_cnt, rows.reshape(nt, 1, tm * TOP_K), h2)


def _expert_kernel(be_ref, grp_ref, ia_ref, ib_ref, nv_ref, xa_ref, xb_ref, wgu_ref, bgu_ref, wd_ref, bd_ref,
                   o_ref, x_scr):
    i = pl.program_id(0)
    live = i < nv_ref[0]

    @pl.when(live & (grp_ref[i] == 0))
    def _():
        x_scr[...] = _tile_rows_load(xa_ref, MOE_ROWS).astype(BF16)

    @pl.when(live & (grp_ref[i] != 0))
    def _():
        x_scr[...] = _tile_rows_load(xb_ref, MOE_ROWS).astype(BF16)

    @pl.when(live)
    def _():
        gu = jnp.dot(x_scr[...], wgu_ref[0], preferred_element_type=F32) + bgu_ref[0]
        gate = jnp.minimum(gu[:, :D_MODEL], SWIGLU_LIMIT)
        up = jnp.clip(gu[:, D_MODEL:], -SWIGLU_LIMIT, SWIGLU_LIMIT)
        act = (up + 1.0) * (gate * jax.nn.sigmoid(SWIGLU_ALPHA * gate))
        _tile_rows_store(o_ref, jnp.dot(act.astype(BF16), wd_ref[0], preferred_element_type=F32) + bd_ref[0])

    @pl.when(i >= nv_ref[0])
    def _():
        o_ref[...] = jnp.zeros_like(o_ref)


def _experts(block_e, block_grp, blk_a, blk_b, n_valid, xa, xb, wgu, bgu, wd, bd):
    nb = block_e.shape[0]
    e3 = lambda i, be, gr, ia, ib, nv: (be[i], 0, 0)
    grid_spec = pltpu.PrefetchScalarGridSpec(
        num_scalar_prefetch=5,
        grid=(nb,),
        in_specs=[pl.BlockSpec((MOE_ROWS * SUBLANES, LANES), lambda i, be, gr, ia, ib, nv: (ia[i], 0)),
                  pl.BlockSpec((MOE_ROWS * SUBLANES, LANES), lambda i, be, gr, ia, ib, nv: (ib[i], 0)),
                  pl.BlockSpec((1, D_MODEL, 2 * D_MODEL), e3), pl.BlockSpec((1, 1, 2 * D_MODEL), e3),
                  pl.BlockSpec((1, D_MODEL, D_MODEL), e3), pl.BlockSpec((1, 1, D_MODEL), e3)],
        out_specs=pl.BlockSpec((MOE_ROWS * SUBLANES, LANES), lambda i, be, gr, ia, ib, nv: (i, 0)),
        scratch_shapes=[pltpu.VMEM((MOE_ROWS, D_MODEL), BF16)],
    )
    return pl.pallas_call(
        _expert_kernel,
        grid_spec=grid_spec,
        out_shape=jax.ShapeDtypeStruct((nb * MOE_ROWS * SUBLANES, LANES), F32),
        compiler_params=_params(("arbitrary",)),
        name="routed_experts",
    )(block_e, block_grp, blk_a, blk_b, n_valid, xa, xb, wgu, bgu, wd, bd)


def _combine_kernel(rows_cur, rows_nxt, x_ref, gt_ref, g_ref, y_hbm, o_ref, buf, sems, *, tm):
    i = pl.program_id(0)
    nt = pl.num_programs(0)
    slot = i % 2

    def issue(rows_ref, dst_slot):
        def body(grp, c):
            r0 = grp * DMA_GROUP
            src = [rows_ref[0, 0, (r0 + r) * TOP_K + k] for r in range(DMA_GROUP) for k in range(TOP_K)]
            for r in range(DMA_GROUP):
                for k in range(TOP_K):
                    pltpu.make_async_copy(_tile_row(y_hbm, src[r * TOP_K + k]),
                                          _tile_row(buf.at[dst_slot, k], r0 + r), sems.at[dst_slot]).start()
            return c
        lax.fori_loop(0, tm // DMA_GROUP, body, 0)

    @pl.when(i == 0)
    def _():
        issue(rows_cur, 0)

    @pl.when(i + 1 < nt)
    def _():
        issue(rows_nxt, 1 - slot)

    pltpu.make_async_copy(buf.at[slot], buf.at[slot], sems.at[slot]).wait()
    acc = x_ref[...]
    gt = gt_ref[...]
    for k in range(TOP_K):
        acc = acc + _tile_rows_load(buf.at[slot, k], tm) * gt[:, k:k + 1]
    o_ref[...] = _rms(acc, g_ref[...])


def _combine(rows, x2, gates, g, yb):
    t = x2.shape[0]
    tm = COMBINE_TM
    nt = t // tm
    rows3 = rows.reshape(nt, 1, tm * TOP_K)
    row = lambda i: (i, 0)
    return pl.pallas_call(
        functools.partial(_combine_kernel, tm=tm),
        grid=(nt,),
        in_specs=[pl.BlockSpec((1, 1, tm * TOP_K), lambda i: (i, 0, 0), memory_space=pltpu.SMEM),
                  pl.BlockSpec((1, 1, tm * TOP_K), lambda i: (jnp.minimum(i + 1, nt - 1), 0, 0),
                               memory_space=pltpu.SMEM),
                  pl.BlockSpec((tm, D_MODEL), row), pl.BlockSpec((tm, LANES), row),
                  pl.BlockSpec((1, D_MODEL), lambda i: (0, 0)),
                  pl.BlockSpec(memory_space=pl.ANY)],
        out_specs=pl.BlockSpec((tm, D_MODEL), row),
        out_shape=jax.ShapeDtypeStruct((t, D_MODEL), F32),
        scratch_shapes=[pltpu.VMEM((2, TOP_K, tm * SUBLANES, LANES), F32), pltpu.SemaphoreType.DMA((2,))],
        compiler_params=_params(("arbitrary",)),
        name="combine_final_norm",
    )(rows3, rows3, x2, gates, g, yb)


def _rope_tables(seq_len):
    inv_freq = ROPE_THETA ** (-jnp.arange(ROPE_HALF, dtype=F32) * 2.0 / ROPE_DIM)
    ang = jnp.arange(seq_len, dtype=F32)[:, None] * inv_freq[None, :]
    cos, sin = jnp.cos(ang), jnp.sin(ang)
    pad = HEAD_DIM - ROPE_DIM
    ones = jnp.ones((seq_len, pad), F32)
    zer_h = jnp.zeros((seq_len, ROPE_HALF), F32)
    zer_p = jnp.zeros((seq_len, pad), F32)
    c = jnp.concatenate([cos, cos, ones], axis=1)
    s1 = jnp.concatenate([zer_h, sin, zer_p], axis=1)
    s2 = jnp.concatenate([-sin, zer_h, zer_p], axis=1)
    rep = LANES // HEAD_DIM
    return jnp.tile(c, (1, rep)), jnp.tile(s1, (1, rep)), jnp.tile(s2, (1, rep))


def _ssm_weights(a_re, a_im, log_dt, b_re, b_im, c_re, c_im, ssm_d, nsteps):
    r = SSM_CHUNK
    dt = jnp.exp(log_dt)[..., None]
    lr, li = a_re * dt, a_im * dt

    def cpow(n):
        n = jnp.asarray(n, F32)[..., None, None, None]
        mag = jnp.exp(n * lr)
        return mag * jnp.cos(n * li), mag * jnp.sin(n * li)

    ab_re, ab_im = cpow(jnp.ones(()))
    den = a_re * a_re + a_im * a_im
    num_re, num_im = ab_re - 1.0, ab_im
    f_re = (num_re * a_re + num_im * a_im) / den
    f_im = (num_im * a_re - num_re * a_im) / den
    bb_re = f_re[..., None] * b_re - f_im[..., None] * b_im
    bb_im = f_re[..., None] * b_im + f_im[..., None] * b_re

    taus = jnp.arange(r + 1, dtype=F32)
    p_re, p_im = cpow(taus)
    m_re = p_re[..., None] * bb_re - p_im[..., None] * bb_im
    m_im = p_re[..., None] * bb_im + p_im[..., None] * bb_re
    kern = (jnp.einsum('dgcp,tdgpk->tdgck', c_re, m_re) - jnp.einsum('dgcp,tdgpk->tdgck', c_im, m_im))
    bi = jnp.arange(r)
    lag = bi[:, None] - bi[None, :]
    kf = kern[jnp.clip(lag, 0, r), 0] * (lag >= 0)[:, :, None, None, None]
    kb = kern[jnp.clip(-lag, 0, r), 1] * (lag <= 0)[:, :, None, None, None]
    eye = (lag == 0)[:, :, None, None, None] * (jnp.eye(SSM_GROUP, dtype=F32) * ssm_d[:, :, None])[None, None]
    w_toep = (kf + kb + eye).transpose(2, 0, 3, 1, 4).reshape(SSM_N_GROUPS, TOEP, TOEP)

    ef = (r - 1 - bi)
    st_f_re, st_f_im = m_re[ef, 0], m_im[ef, 0]
    st_b_re, st_b_im = m_re[bi, 1], m_im[bi, 1]
    w_state = jnp.stack([st_f_re, st_f_im, st_b_re, st_b_im], axis=0)
    w_state = w_state.transpose(2, 0, 3, 1, 4).reshape(SSM_N_GROUPS, 4 * SSM_STATE, TOEP)

    def c_times_pow(d, expo):
        pr, pi = p_re[expo, d], p_im[expo, d]
        zr = c_re[d][None] * pr[:, :, None, :] - c_im[d][None] * pi[:, :, None, :]
        zi = c_re[d][None] * pi[:, :, None, :] + c_im[d][None] * pr[:, :, None, :]
        return zr, -zi
    cf_re, cf_im = c_times_pow(0, bi + 1)
    cb_re, cb_im = c_times_pow(1, r - bi)
    w_carry = jnp.stack([cf_re, cf_im, cb_re, cb_im], axis=3)
    w_carry = w_carry.transpose(1, 0, 2, 3, 4).reshape(SSM_N_GROUPS, TOEP, 4 * SSM_STATE)

    qr, qi = p_re[r], p_im[r]
    cols = []
    for _ in range(nsteps):
        cols.append(jnp.stack([qr[0], qi[0], qr[1], qi[1]], axis=1))
        qr, qi = qr * qr - qi * qi, 2.0 * qr * qi
    pw = jnp.stack(cols, axis=-1).reshape(SSM_N_GROUPS, 4 * SSM_STATE, nsteps)
    return w_toep.astype(BF16), w_state.astype(BF16), w_carry.astype(BF16), pw


def _front(x, prm):
    n, seq_len, _ = x.shape
    t = n * seq_len
    x2d = x.reshape(t, D_MODEL)
    q, ka, kb, va, vb = _qkv(x2d, prm['norm1_g'], prm['w_qkv'], *_rope_tables(seq_len), seq_len)
    attn_n = _attention(q, ka, kb, va, vb, prm['sink'], prm['attn_out_g'], seq_len)

    chunks_per_seq = seq_len // SSM_CHUNK
    nsteps = max(1, (chunks_per_seq - 1).bit_length())
    nc = t // SSM_CHUNK
    lt = min(256, nc)
    ut = _uproj(x2d, prm['norm1_g'], prm['w_u_t'], lt)
    yt = _ssm(ut, prm['w_toep'], prm['w_state'], prm['w_carry'], prm['pw'][:, :, :nsteps], chunks_per_seq)
    ssm_n = _glu(yt, prm['glu_w_t'], prm['glu_b'], prm['ssm_out_g'], lt)
    return _outproj(x2d, attn_n, ssm_n, prm['w_out_a'], prm['w_out_s'], prm['norm2_g'],
                    prm['router_w'], prm['router_b'], prm['tri'])


def _cumsum_small(x):
    n = x.shape[0]
    keep = jnp.arange(n)[None, :] <= jnp.arange(n)[:, None]
    return jnp.sum(jnp.where(keep, x[None, :], 0), axis=1)


def _by_expert(idx, table):
    hit = idx[..., None] == jnp.arange(N_EXPERTS, dtype=I32)
    return jnp.sum(jnp.where(hit, table, 0), axis=-1)


def kernel(x_prompt, x_sample, norm1_g, w_in, attn_sink, ssm_a_re, ssm_a_im, ssm_log_dt, ssm_b_re, ssm_b_im, ssm_c_re, ssm_c_im, ssm_d, glu_w, glu_b, attn_out_g, ssm_out_g, w_out, norm2_g, router_w, router_b, w_gate_up, b_gate_up, w_down, b_down, final_g):
    assert norm1_g.shape[0] == 1, "single-layer problem"
    l = 0
    xs = [x_prompt, x_sample]
    max_chunks = max(x.shape[1] for x in xs) // SSM_CHUNK
    max_steps = max(1, (max_chunks - 1).bit_length())
    wq, wk, wv, wu = jnp.split(w_in[l], [ATTN_WIDTH, ATTN_WIDTH + KV_WIDTH, ATTN_WIDTH + 2 * KV_WIDTH], axis=1)
    dup = lambda w: jnp.concatenate([w[:, :HEAD_DIM], w[:, :HEAD_DIM], w[:, HEAD_DIM:], w[:, HEAD_DIM:]], axis=1)
    w_toep, w_state, w_carry, pw = _ssm_weights(ssm_a_re[l], ssm_a_im[l], ssm_log_dt[l], ssm_b_re[l], ssm_b_im[l],
                                                ssm_c_re[l], ssm_c_im[l], ssm_d[l], max_steps)
    tri_i = lax.broadcasted_iota(I32, (ROUTE_TM, ROUTE_TM), 0)
    tri_j = lax.broadcasted_iota(I32, (ROUTE_TM, ROUTE_TM), 1)
    prm = dict(
        norm1_g=norm1_g[l].reshape(1, D_MODEL),
        w_qkv=jnp.concatenate([wq, dup(wk), dup(wv)], axis=1).astype(BF16),
        w_u_t=wu.T.astype(BF16),
        sink=attn_sink[l].astype(F32),
        attn_out_g=attn_out_g[l].reshape(1, ATTN_WIDTH),
        w_toep=w_toep, w_state=w_state, w_carry=w_carry, pw=pw,
        glu_w_t=glu_w[l].T.astype(BF16),
        glu_b=glu_b[l].reshape(SSM_WIDTH, 1),
        ssm_out_g=ssm_out_g[l].reshape(SSM_WIDTH, 1),
        w_out_a=w_out[l][:ATTN_WIDTH].astype(BF16),
        w_out_s=w_out[l][ATTN_WIDTH:].astype(BF16),
        norm2_g=norm2_g[l].reshape(1, D_MODEL),
        router_w=jnp.pad(router_w[l], ((0, 0), (0, LANES - N_EXPERTS))).astype(BF16),
        router_b=jnp.pad(router_b[l], (0, LANES - N_EXPERTS), constant_values=NEG_BIG).reshape(1, LANES),
        tri=(tri_j < tri_i).astype(BF16),
    )
    fronts = [_front(x, prm) for x in xs]

    cnts = [f[4][0, :N_EXPERTS].astype(I32) for f in fronts]
    padded = [(c + MOE_ROWS - 1) // MOE_ROWS * MOE_ROWS for c in cnts]
    pends = [_cumsum_small(p) for p in padded]
    pstarts = [pe - p for pe, p in zip(pends, padded)]
    nbs = [f[0].shape[0] * TOP_K // MOE_ROWS + N_EXPERTS for f in fronts]
    seg_blocks = jnp.stack([p // MOE_ROWS for p in padded], axis=1).reshape(-1)
    seg_end = _cumsum_small(seg_blocks)
    seg_start = seg_end - seg_blocks
    nb = sum(nbs)
    bi = jnp.arange(nb, dtype=I32)
    seg = jnp.minimum(jnp.sum(seg_end[None, :] <= bi[:, None], axis=1), 2 * N_EXPERTS - 1).astype(I32)
    block_e = seg // 2
    block_grp = seg % 2
    n_valid = seg_end[-1].astype(I32).reshape(1)
    live = bi < n_valid[0]
    src_start = jnp.stack([ps // MOE_ROWS for ps in pstarts], axis=1).reshape(-1)
    in_seg = seg[:, None] == jnp.arange(2 * N_EXPERTS, dtype=I32)[None, :]
    src_blk = jnp.sum(jnp.where(in_seg, (src_start - seg_start)[None, :], 0), axis=1) + bi
    not_after = bi[None, :] <= bi[:, None]
    blk = [jnp.max(jnp.where(not_after & (live & (block_grp == g))[None, :], src_blk[None, :], 0), axis=1).astype(I32)
           for g in range(2)]

    xbufs, yrows = [], []
    for g, f in enumerate(fronts):
        e_idx = f[2][:, :TOP_K]
        rank = f[2][:, TOP_K:2 * TOP_K]
        n_pad = nbs[g] * MOE_ROWS
        pad_start = jnp.concatenate([pstarts[g] + cnts[g], pends[g][-1:]]).astype(I32)
        pad_cnt = jnp.concatenate([padded[g] - cnts[g], n_pad - pends[g][-1:]]).astype(I32)
        xrows = (_by_expert(e_idx, pstarts[g]) + rank).astype(I32)
        xbufs.append(_scatter_rows(pad_start, pad_cnt, xrows, f[1], n_pad))
        yrows.append((_by_expert(e_idx, seg_start[g::2] * MOE_ROWS) + rank).astype(I32))
    yb = _experts(block_e, block_grp, blk[0], blk[1], n_valid, xbufs[0], xbufs[1],
                  w_gate_up[l].astype(BF16), b_gate_up[l][:, None, :], w_down[l].astype(BF16), b_down[l][:, None, :])
    gfin = final_g.reshape(1, D_MODEL)
    outs = [_combine(r, f[0], f[3], gfin, yb).reshape(x.shape) for x, f, r in zip(xs, fronts, yrows)]
    return tuple(outs)
```

```python
import functools
import math

import jax
import jax.numpy as jnp
from jax import lax
from jax.experimental import pallas as pl
from jax.experimental.pallas import tpu as pltpu

F32 = jnp.float32
BF16 = jnp.bfloat16
I32 = jnp.int32

D_MODEL = 1024
HEAD_DIM = 64
N_Q_HEADS = 8
N_KV_HEADS = 2
Q_PER_KV = N_Q_HEADS // N_KV_HEADS
ATTN_WIDTH = N_Q_HEADS * HEAD_DIM
KV_WIDTH = N_KV_HEADS * HEAD_DIM
WINDOW = 128
ATT_BLOCK = 128
ROPE_THETA = 500000.0
ROPE_DIM = HEAD_DIM // 4
ROPE_HALF = ROPE_DIM // 2
SSM_WIDTH = 512
SSM_GROUP = 16
SSM_N_GROUPS = SSM_WIDTH // SSM_GROUP
SSM_STATE = 64
N_EXPERTS = 32
TOP_K = 4
SWIGLU_LIMIT = 7.0
SWIGLU_ALPHA = 1.702
EPS = 1e-5

LANES = 128
SUBLANES = 8
SSM_CHUNK = 32
TOEP = SSM_CHUNK * SSM_GROUP
MOE_ROWS = 512
ROUTE_TM = 512
COMBINE_TM = 128
VMEM_LIMIT = 52 * 1024 * 1024
NEG_BIG = -1e30


def _params(sem, vmem=VMEM_LIMIT):
    return pltpu.CompilerParams(dimension_semantics=sem, vmem_limit_bytes=vmem)


def _rms(x, g):
    ms = jnp.mean(x * x, axis=-1, keepdims=True)
    return x * lax.rsqrt(ms + EPS) * g


def _tile_rows_load(ref, rows):
    return jnp.concatenate([ref[pl.ds(j, rows, stride=SUBLANES), :] for j in range(SUBLANES)], axis=1)


def _tile_rows_store(ref, val):
    rows = val.shape[0]
    for j in range(SUBLANES):
        ref[pl.ds(j, rows, stride=SUBLANES), :] = val[:, j * LANES:(j + 1) * LANES]


def _tile_row(ref, row):
    return ref.at[pl.ds(pl.multiple_of(row * SUBLANES, SUBLANES), SUBLANES)]


KV_COLS = 4 * LANES


def _qkv_kernel(x_ref, g_ref, w_ref, c_ref, s1_ref, s2_ref, q_ref, kv_ref):
    h = _rms(x_ref[...], g_ref[...]).astype(BF16)
    p = jnp.dot(h, w_ref[...], preferred_element_type=F32)
    c = c_ref[...]
    s1 = s1_ref[...]
    s2 = s2_ref[...]

    def rot(t):
        return t * c + pltpu.roll(t, ROPE_HALF, 1) * s1 + pltpu.roll(t, LANES - ROPE_HALF, 1) * s2

    for j in range(ATTN_WIDTH // LANES):
        q_ref[:, j * LANES:(j + 1) * LANES] = (rot(p[:, j * LANES:(j + 1) * LANES]) * (HEAD_DIM ** -0.5)).astype(BF16)
    for j in range(N_KV_HEADS):
        col = ATTN_WIDTH + j * LANES
        kv_ref[:, j * LANES:(j + 1) * LANES] = rot(p[:, col:col + LANES]).astype(BF16)
    kv_ref[:, N_KV_HEADS * LANES:] = p[:, ATTN_WIDTH + N_KV_HEADS * LANES:].astype(BF16)


def _qkv(x2d, g, w, c, s1, s2, seq_len, tm=512):
    t = x2d.shape[0]
    nlb = seq_len // tm
    row = lambda i: (i, 0)
    tab = lambda i: (i % nlb, 0)
    full = lambda i: (0, 0)
    return pl.pallas_call(
        _qkv_kernel,
        grid=(t // tm,),
        in_specs=[pl.BlockSpec((tm, D_MODEL), row), pl.BlockSpec((1, D_MODEL), full),
                  pl.BlockSpec((D_MODEL, ATTN_WIDTH + KV_COLS), full),
                  pl.BlockSpec((tm, LANES), tab), pl.BlockSpec((tm, LANES), tab), pl.BlockSpec((tm, LANES), tab)],
        out_specs=[pl.BlockSpec((tm, ATTN_WIDTH), row), pl.BlockSpec((tm, KV_COLS), row)],
        out_shape=[jax.ShapeDtypeStruct((t, ATTN_WIDTH), BF16), jax.ShapeDtypeStruct((t, KV_COLS), BF16)],
        compiler_params=_params(("parallel",)),
        name="qkv_rotary",
    )(x2d, g, w, c, s1, s2)


def _attn_kernel(sink_ref, q_ref, kvp, kvc, kvn, g_ref, o_ref, *, bps):
    i = pl.program_id(0)
    first = (i % bps) == 0
    last = (i % bps) == bps - 1
    qi = lax.broadcasted_iota(I32, (ATT_BLOCK, 3 * ATT_BLOCK), 0)
    kj = lax.broadcasted_iota(I32, (ATT_BLOCK, 3 * ATT_BLOCK), 1)
    rel = kj - ATT_BLOCK - qi
    valid = (jnp.abs(rel) <= WINDOW)
    valid = valid & ((kj >= ATT_BLOCK) | jnp.logical_not(first))
    valid = valid & ((kj < 2 * ATT_BLOCK) | jnp.logical_not(last))
    kv = jnp.concatenate([kvp[...], kvc[...], kvn[...]], axis=0)
    ks = [kv[:, h * LANES:(h + 1) * LANES] for h in range(N_KV_HEADS)]
    vs = [kv[:, (N_KV_HEADS + h) * LANES:(N_KV_HEADS + h + 1) * LANES] for h in range(N_KV_HEADS)]
    lo = lax.broadcasted_iota(I32, (ATT_BLOCK, LANES), 1) < HEAD_DIM
    zero = jnp.zeros((ATT_BLOCK, LANES), BF16)
    heads = [(j, par) for j in range(ATTN_WIDTH // LANES) for par in range(2)]
    nt_dims = (((1,), (1,)), ((), ()))
    scores = []
    for j, par in heads:
        qt = q_ref[:, j * LANES:(j + 1) * LANES]
        qm = jnp.where(lo if par == 0 else jnp.logical_not(lo), qt, zero)
        scores.append(lax.dot_general(qm, ks[j // 2], nt_dims, preferred_element_type=F32))
    scores = [jnp.where(valid, s, NEG_BIG) for s in scores]
    sinks = [sink_ref[2 * j + par] for j, par in heads]
    maxes = [jnp.maximum(jnp.max(s, axis=-1, keepdims=True), sk) for s, sk in zip(scores, sinks)]
    probs = [jnp.exp(s - m) for s, m in zip(scores, maxes)]
    dens = [jnp.sum(p, axis=-1, keepdims=True) + jnp.exp(sk - m) for p, m, sk in zip(probs, maxes, sinks)]
    outs = [jnp.dot(p.astype(BF16), vs[j // 2], preferred_element_type=F32) for p, (j, par) in zip(probs, heads)]
    outs = [o / d for o, d in zip(outs, dens)]
    tiles = [jnp.where(lo, outs[2 * j], outs[2 * j + 1]) for j in range(ATTN_WIDTH // LANES)]
    o = jnp.concatenate(tiles, axis=1)
    o_ref[...] = _rms(o, g_ref[...]).astype(BF16)


def _attention(q, kv, sink, g, seq_len):
    t = q.shape[0]
    nblk = t // ATT_BLOCK
    bps = seq_len // ATT_BLOCK
    cur = lambda i, s: (i, 0)
    prv = lambda i, s: (jnp.maximum(i - 1, 0), 0)
    nxt = lambda i, s: (jnp.minimum(i + 1, nblk - 1), 0)
    grid_spec = pltpu.PrefetchScalarGridSpec(
        num_scalar_prefetch=1,
        grid=(nblk,),
        in_specs=[pl.BlockSpec((ATT_BLOCK, ATTN_WIDTH), cur),
                  pl.BlockSpec((ATT_BLOCK, KV_COLS), prv), pl.BlockSpec((ATT_BLOCK, KV_COLS), cur),
                  pl.BlockSpec((ATT_BLOCK, KV_COLS), nxt),
                  pl.BlockSpec((1, ATTN_WIDTH), lambda i, s: (0, 0))],
        out_specs=pl.BlockSpec((ATT_BLOCK, ATTN_WIDTH), cur),
    )
    return pl.pallas_call(
        functools.partial(_attn_kernel, bps=bps),
        grid_spec=grid_spec,
        out_shape=jax.ShapeDtypeStruct((t, ATTN_WIDTH), BF16),
        compiler_params=_params(("parallel",)),
        name="banded_attention",
    )(sink, q, kv, kv, kv, g)


def _uproj_kernel(x_ref, g_ref, w_ref, o_ref, stage_ref):
    lt = x_ref.shape[0]
    h = _rms(x_ref[...].reshape(lt * SUBLANES, D_MODEL), g_ref[...])
    for j in range(D_MODEL // LANES):
        stage_ref[j] = h[:, j * LANES:(j + 1) * LANES]
    for bl in range(SUBLANES):
        hb = jnp.concatenate([stage_ref[j, pl.ds(bl, lt, stride=SUBLANES), :] for j in range(D_MODEL // LANES)],
                             axis=1).astype(BF16)
        ut = lax.dot_general(w_ref[...], hb, (((1,), (1,)), ((), ())), preferred_element_type=F32)
        o_ref[bl] = ut.astype(BF16)


def _uproj(x2d, g, w_t, lt):
    t = x2d.shape[0]
    nc = t // SSM_CHUNK
    xv = x2d.reshape(nc, SSM_CHUNK // SUBLANES, SUBLANES, D_MODEL)
    return pl.pallas_call(
        _uproj_kernel,
        grid=(nc // lt, SSM_CHUNK // SUBLANES),
        in_specs=[pl.BlockSpec((lt, None, SUBLANES, D_MODEL), lambda i, b: (i, b, 0, 0)),
                  pl.BlockSpec((1, D_MODEL), lambda i, b: (0, 0)),
                  pl.BlockSpec((SSM_WIDTH, D_MODEL), lambda i, b: (0, 0))],
        out_specs=pl.BlockSpec((SUBLANES, SSM_WIDTH, lt), lambda i, b: (b, 0, i)),
        out_shape=jax.ShapeDtypeStruct((SSM_CHUNK, SSM_WIDTH, nc), BF16),
        scratch_shapes=[pltpu.VMEM((D_MODEL // LANES, lt * SUBLANES, LANES), F32)],
        compiler_params=_params(("parallel", "parallel")),
        name="u_projection",
    )(xv, g, w_t)


def _gelu_tanh(x):
    return 0.5 * x * (1.0 + jnp.tanh(math.sqrt(2.0 / math.pi) * (x + 0.044715 * (x * x * x))))


def _ssm_kernel(a_ref, wt_ref, ws_ref, wc_ref, pw_ref, y_ref, *, chunks_per_seq, nsteps):
    nc = a_ref.shape[2]
    a = a_ref[...].reshape(TOEP, nc)
    y = jnp.dot(wt_ref[0], a, preferred_element_type=F32)
    s = jnp.dot(ws_ref[0], a, preferred_element_type=F32)
    pos = lax.broadcasted_iota(I32, (SSM_STATE, nc), 1) % chunks_per_seq
    carries = []
    for d in range(2):
        hr = s[2 * d * SSM_STATE:(2 * d + 1) * SSM_STATE]
        hi = s[(2 * d + 1) * SSM_STATE:(2 * d + 2) * SSM_STATE]
        for k in range(nsteps):
            sh = 1 << k
            pr = pw_ref[0, 2 * d * SSM_STATE:(2 * d + 1) * SSM_STATE, k:k + 1]
            pi = pw_ref[0, (2 * d + 1) * SSM_STATE:(2 * d + 2) * SSM_STATE, k:k + 1]
            if d == 0:
                ok = pos >= sh
                sr = pltpu.roll(hr, sh, 1)
                si = pltpu.roll(hi, sh, 1)
            else:
                ok = pos < chunks_per_seq - sh
                sr = pltpu.roll(hr, nc - sh, 1)
                si = pltpu.roll(hi, nc - sh, 1)
            hr, hi = (hr + jnp.where(ok, pr * sr - pi * si, 0.0),
                      hi + jnp.where(ok, pr * si + pi * sr, 0.0))
        if d == 0:
            ok = pos >= 1
            cr = pltpu.roll(hr, 1, 1)
            ci = pltpu.roll(hi, 1, 1)
        else:
            ok = pos < chunks_per_seq - 1
            cr = pltpu.roll(hr, nc - 1, 1)
            ci = pltpu.roll(hi, nc - 1, 1)
        carries += [jnp.where(ok, cr, 0.0), jnp.where(ok, ci, 0.0)]
    carry = jnp.concatenate(carries, axis=0).astype(BF16)
    y = y + jnp.dot(wc_ref[0], carry, preferred_element_type=F32)
    y_ref[...] = _gelu_tanh(y).reshape(SSM_CHUNK, SSM_GROUP, nc)


def _ssm(ut, w_toep, w_state, w_carry, pw, chunks_per_seq):
    nc = ut.shape[2]
    nsteps = max(1, (chunks_per_seq - 1).bit_length())
    g3 = lambda g: (g, 0, 0)
    return pl.pallas_call(
        functools.partial(_ssm_kernel, chunks_per_seq=chunks_per_seq, nsteps=nsteps),
        grid=(SSM_N_GROUPS,),
        in_specs=[pl.BlockSpec((SSM_CHUNK, SSM_GROUP, nc), lambda g: (0, g, 0)),
                  pl.BlockSpec((1, TOEP, TOEP), g3),
                  pl.BlockSpec((1, 4 * SSM_STATE, TOEP), g3),
                  pl.BlockSpec((1, TOEP, 4 * SSM_STATE), g3),
                  pl.BlockSpec((1, 4 * SSM_STATE, pw.shape[2]), g3)],
        out_specs=pl.BlockSpec((SSM_CHUNK, SSM_GROUP, nc), lambda g: (0, g, 0)),
        out_shape=jax.ShapeDtypeStruct((SSM_CHUNK, SSM_WIDTH, nc), F32),
        compiler_params=_params(("parallel",)),
        name="s5_core",
    )(ut, w_toep, w_state, w_carry, pw)


def _glu_kernel(y_ref, w_ref, b_ref, g_ref, o_ref, stage_ref):
    lt = o_ref.shape[0]
    for bl in range(SUBLANES):
        y = y_ref[bl]
        z = jnp.dot(w_ref[...], y.astype(BF16), preferred_element_type=F32) + b_ref[...]
        s = y * jax.nn.sigmoid(z)
        ms = jnp.mean(s * s, axis=0, keepdims=True)
        sn = (s * lax.rsqrt(ms + EPS) * g_ref[...]).T
        for j in range(SSM_WIDTH // LANES):
            stage_ref[j, pl.ds(bl, lt, stride=SUBLANES), :] = sn[:, j * LANES:(j + 1) * LANES]
    for j in range(SSM_WIDTH // LANES):
        o_ref[:, :, j * LANES:(j + 1) * LANES] = stage_ref[j].reshape(lt, SUBLANES, LANES)


def _glu(yt, w_t, b_col, g_col, lt):
    nc = yt.shape[2]
    out = pl.pallas_call(
        _glu_kernel,
        grid=(nc // lt, SSM_CHUNK // SUBLANES),
        in_specs=[pl.BlockSpec((SUBLANES, SSM_WIDTH, lt), lambda i, b: (b, 0, i)),
                  pl.BlockSpec((SSM_WIDTH, SSM_WIDTH), lambda i, b: (0, 0)),
                  pl.BlockSpec((SSM_WIDTH, 1), lambda i, b: (0, 0)),
                  pl.BlockSpec((SSM_WIDTH, 1), lambda i, b: (0, 0))],
        out_specs=pl.BlockSpec((lt, None, SUBLANES, SSM_WIDTH), lambda i, b: (i, b, 0, 0)),
        out_shape=jax.ShapeDtypeStruct((nc, SSM_CHUNK // SUBLANES, SUBLANES, SSM_WIDTH), F32),
        scratch_shapes=[pltpu.VMEM((SSM_WIDTH // LANES, lt * SUBLANES, LANES), F32)],
        compiler_params=_params(("parallel", "parallel")),
        name="glu_norm",
    )(yt, w_t, b_col, g_col)
    return out.reshape(nc * SSM_CHUNK, SSM_WIDTH)


def _outproj_kernel(x_ref, a_ref, s_ref, wa_ref, ws_ref, g_ref, wr_ref, br_ref, tri_ref,
                    x2_ref, h2_ref, rt_ref, gt_ref, cnt_ref, run_ref):
    i = pl.program_id(0)

    @pl.when(i == 0)
    def _():
        run_ref[...] = jnp.zeros_like(run_ref)

    x2 = (x_ref[...] + jnp.dot(a_ref[...], wa_ref[...], preferred_element_type=F32)
          + jnp.dot(s_ref[...].astype(BF16), ws_ref[...], preferred_element_type=F32))
    x2_ref[...] = x2
    h2 = _rms(x2, g_ref[...])
    _tile_rows_store(h2_ref, h2)
    logits = jnp.dot(h2.astype(BF16), wr_ref[...], preferred_element_type=F32) + br_ref[...]

    tm = logits.shape[0]
    lane = lax.broadcasted_iota(I32, (tm, LANES), 1)
    lane_f = lane.astype(F32)
    work = logits
    sel = jnp.zeros((tm, LANES), F32)
    top_v, top_i = [], []
    for _ in range(TOP_K):
        m = jnp.max(work, axis=-1, keepdims=True)
        idx = jnp.min(jnp.where(work == m, lane_f, float(LANES)), axis=-1, keepdims=True).astype(I32)
        hit = lane == idx
        sel = jnp.where(hit, 1.0, sel)
        work = jnp.where(hit, -jnp.inf, work)
        top_v.append(m)
        top_i.append(idx)
    ex = [jnp.exp(v - top_v[0]) for v in top_v]
    den = ex[0] + ex[1] + ex[2] + ex[3]
    before = jnp.dot(tri_ref[...], sel.astype(BF16), preferred_element_type=F32) + run_ref[0:1, :]
    rt = jnp.zeros((tm, LANES), I32)
    gt = jnp.zeros((tm, LANES), F32)
    for k in range(TOP_K):
        rank = jnp.sum(jnp.where(lane == top_i[k], before, 0.0), axis=-1, keepdims=True).astype(I32)
        rt = jnp.where(lane == k, top_i[k], rt)
        rt = jnp.where(lane == TOP_K + k, rank, rt)
        gt = jnp.where(lane == k, ex[k] / den, gt)
    rt_ref[...] = rt
    gt_ref[...] = gt
    run = run_ref[...] + jnp.sum(sel, axis=0, keepdims=True)
    run_ref[...] = run
    cnt_ref[...] = run


def _outproj(x2d, attn_n, ssm_n, w_a, w_s, g, w_r, b_r, tri):
    t = x2d.shape[0]
    tm = ROUTE_TM
    row = lambda i: (i, 0)
    full = lambda i: (0, 0)
    return pl.pallas_call(
        _outproj_kernel,
        grid=(t // tm,),
        in_specs=[pl.BlockSpec((tm, D_MODEL), row), pl.BlockSpec((tm, ATTN_WIDTH), row),
                  pl.BlockSpec((tm, SSM_WIDTH), row),
                  pl.BlockSpec((ATTN_WIDTH, D_MODEL), full), pl.BlockSpec((SSM_WIDTH, D_MODEL), full),
                  pl.BlockSpec((1, D_MODEL), full),
                  pl.BlockSpec((D_MODEL, LANES), full), pl.BlockSpec((1, LANES), full),
                  pl.BlockSpec((tm, tm), full)],
        out_specs=[pl.BlockSpec((tm, D_MODEL), row), pl.BlockSpec((tm * SUBLANES, LANES), row),
                   pl.BlockSpec((tm, LANES), row), pl.BlockSpec((tm, LANES), row),
                   pl.BlockSpec((SUBLANES, LANES), full)],
        out_shape=[jax.ShapeDtypeStruct((t, D_MODEL), F32), jax.ShapeDtypeStruct((t * SUBLANES, LANES), F32),
                   jax.ShapeDtypeStruct((t, LANES), I32), jax.ShapeDtypeStruct((t, LANES), F32),
                   jax.ShapeDtypeStruct((SUBLANES, LANES), F32)],
        scratch_shapes=[pltpu.VMEM((SUBLANES, LANES), F32)],
        compiler_params=_params(("arbitrary",)),
        name="out_projection_router",
    )(x2d, attn_n, ssm_n, w_a, w_s, g, w_r, b_r, tri)


DMA_GROUP = 4


def _scatter_kernel(pad_start_ref, pad_cnt_ref, rows_ref, h_ref, xb_hbm, zero_ref, stage_ref, sems, zsem, *, tm):
    i = pl.program_id(0)
    nt = pl.num_programs(0)
    n_dma = tm * TOP_K

    def tile_wait(slot):
        whole = xb_hbm.at[pl.ds(0, n_dma * SUBLANES)]
        pltpu.make_async_copy(whole, whole, sems.at[slot]).wait()

    @pl.when(i == 0)
    def _():
        zero_ref[...] = jnp.zeros_like(zero_ref)

        def fill(row, n_rows):
            return pltpu.make_async_copy(zero_ref.at[pl.ds(0, n_rows * SUBLANES)],
                                         xb_hbm.at[pl.ds(pl.multiple_of(row * SUBLANES, SUBLANES), n_rows * SUBLANES)],
                                         zsem)

        for phase in range(2):
            def per_expert(e, c):
                cnt = pad_cnt_ref[e]
                row = pad_start_ref[e]
                run = MOE_ROWS // 2
                while run >= 1:
                    below = cnt & ~(2 * run - 1)

                    @pl.when((cnt & run) != 0)
                    def _(run=run, below=below):
                        cp = fill(row + below, run)
                        cp.start() if phase == 0 else cp.wait()
                    run //= 2
                return c
            lax.fori_loop(0, N_EXPERTS, per_expert, 0)

            def per_tail_block(b, c):
                cp = fill(pad_start_ref[N_EXPERTS] + b * MOE_ROWS, MOE_ROWS)
                cp.start() if phase == 0 else cp.wait()
                return c
            lax.fori_loop(0, pad_cnt_ref[N_EXPERTS] // MOE_ROWS, per_tail_block, 0)

    for slot in range(2):
        @pl.when(i > 0)
        def _(slot=slot):
            tile_wait(slot)

        stage_ref[slot] = h_ref[pl.ds(slot * tm * SUBLANES, tm * SUBLANES), :]

        def issue(grp, c, slot=slot):
            r0 = grp * DMA_GROUP
            dst = [rows_ref[0, 0, (slot * tm + r0 + r) * TOP_K + k] for r in range(DMA_GROUP) for k in range(TOP_K)]
            for r in range(DMA_GROUP):
                src = _tile_row(stage_ref.at[slot], r0 + r)
                for k in range(TOP_K):
                    pltpu.make_async_copy(src, _tile_row(xb_hbm, dst[r * TOP_K + k]), sems.at[slot]).start()
            return c
        lax.fori_loop(0, tm // DMA_GROUP, issue, 0)

    @pl.when(i == nt - 1)
    def _():
        tile_wait(0)
        tile_wait(1)


def _scatter_rows(pad_start, pad_cnt, rows, h2, n_pad):
    tm = ROUTE_TM
    nt = h2.shape[0] // SUBLANES // (2 * tm)
    grid_spec = pltpu.PrefetchScalarGridSpec(
        num_scalar_prefetch=2,
        grid=(nt,),
        in_specs=[pl.BlockSpec((1, 1, 2 * tm * TOP_K), lambda i, a, b: (i, 0, 0), memory_space=pltpu.SMEM),
                  pl.BlockSpec((2 * tm * SUBLANES, LANES), lambda i, a, b: (i, 0))],
        out_specs=pl.BlockSpec(memory_space=pl.ANY),
        scratch_shapes=[pltpu.VMEM((MOE_ROWS * SUBLANES, LANES), F32),
                        pltpu.VMEM((2, tm * SUBLANES, LANES), F32),
                        pltpu.SemaphoreType.DMA((2,)), pltpu.SemaphoreType.DMA(())],
    )
    return pl.pallas_call(
        functools.partial(_scatter_kernel, tm=tm),
        grid_spec=grid_spec,
        out_shape=jax.ShapeDtypeStruct((n_pad * SUBLANES, LANES), F32),
        compiler_params=_params(("arbitrary",)),
        name="scatter_rows",
    )(pad_start, pad_cnt, rows.reshape(nt, 1, 2 * tm * TOP_K), h2)


def _expert_kernel(be_ref, grp_ref, ia_ref, ib_ref, nv_ref, xa_ref, xb_ref, wgu_ref, bgu_ref, wd_ref, bd_ref,
                   o_ref, x_scr):
    i = pl.program_id(0)
    live = i < nv_ref[0]

    @pl.when(live & (grp_ref[i] == 0))
    def _():
        x_scr[...] = _tile_rows_load(xa_ref, MOE_ROWS).astype(BF16)

    @pl.when(live & (grp_ref[i] != 0))
    def _():
        x_scr[...] = _tile_rows_load(xb_ref, MOE_ROWS).astype(BF16)

    @pl.when(live)
    def _():
        gu = jnp.dot(x_scr[...], wgu_ref[0], preferred_element_type=F32) + bgu_ref[0]
        gate = jnp.minimum(gu[:, :D_MODEL], SWIGLU_LIMIT)
        up = jnp.clip(gu[:, D_MODEL:], -SWIGLU_LIMIT, SWIGLU_LIMIT)
        act = (up + 1.0) * (gate * jax.nn.sigmoid(SWIGLU_ALPHA * gate))
        _tile_rows_store(o_ref, jnp.dot(act.astype(BF16), wd_ref[0], preferred_element_type=F32) + bd_ref[0])

    @pl.when(i >= nv_ref[0])
    def _():
        o_ref[...] = jnp.zeros_like(o_ref)


def _experts(block_e, block_grp, blk_a, blk_b, n_valid, xa, xb, wgu, bgu, wd, bd):
    nb = block_e.shape[0]
    e3 = lambda i, be, gr, ia, ib, nv: (be[i], 0, 0)
    grid_spec = pltpu.PrefetchScalarGridSpec(
        num_scalar_prefetch=5,
        grid=(nb,),
        in_specs=[pl.BlockSpec((MOE_ROWS * SUBLANES, LANES), lambda i, be, gr, ia, ib, nv: (ia[i], 0)),
                  pl.BlockSpec((MOE_ROWS * SUBLANES, LANES), lambda i, be, gr, ia, ib, nv: (ib[i], 0)),
                  pl.BlockSpec((1, D_MODEL, 2 * D_MODEL), e3), pl.BlockSpec((1, 1, 2 * D_MODEL), e3),
                  pl.BlockSpec((1, D_MODEL, D_MODEL), e3), pl.BlockSpec((1, 1, D_MODEL), e3)],
        out_specs=pl.BlockSpec((MOE_ROWS * SUBLANES, LANES), lambda i, be, gr, ia, ib, nv: (i, 0)),
        scratch_shapes=[pltpu.VMEM((MOE_ROWS, D_MODEL), BF16)],
    )
    return pl.pallas_call(
        _expert_kernel,
        grid_spec=grid_spec,
        out_shape=jax.ShapeDtypeStruct((nb * MOE_ROWS * SUBLANES, LANES), F32),
        compiler_params=_params(("arbitrary",)),
        name="routed_experts",
    )(block_e, block_grp, blk_a, blk_b, n_valid, xa, xb, wgu, bgu, wd, bd)


def _combine_kernel(rows_cur, rows_nxt, x_ref, gt_ref, g_ref, y_hbm, o_ref, buf, sems, *, tm):
    i = pl.program_id(0)
    nt = pl.num_programs(0)

    def issue(rows_ref, half, slot):
        def body(grp, c):
            r0 = grp * DMA_GROUP
            src = [rows_ref[0, 0, (half * tm + r0 + r) * TOP_K + k] for r in range(DMA_GROUP) for k in range(TOP_K)]
            for r in range(DMA_GROUP):
                for k in range(TOP_K):
                    pltpu.make_async_copy(_tile_row(y_hbm, src[r * TOP_K + k]),
                                          _tile_row(buf.at[slot, k], r0 + r), sems.at[slot]).start()
            return c
        lax.fori_loop(0, tm // DMA_GROUP, body, 0)

    def finish(half):
        slot = half
        pltpu.make_async_copy(buf.at[slot], buf.at[slot], sems.at[slot]).wait()
        acc = x_ref[half * tm:(half + 1) * tm, :]
        gt = gt_ref[half * tm:(half + 1) * tm, :]
        for k in range(TOP_K):
            acc = acc + _tile_rows_load(buf.at[slot, k], tm) * gt[:, k:k + 1]
        o_ref[half * tm:(half + 1) * tm, :] = _rms(acc, g_ref[...])

    @pl.when(i == 0)
    def _():
        issue(rows_cur, 0, 0)

    issue(rows_cur, 1, 1)
    finish(0)

    @pl.when(i + 1 < nt)
    def _():
        issue(rows_nxt, 0, 0)

    finish(1)


def _combine(rows, x2, gates, g, yb):
    t = x2.shape[0]
    tm = COMBINE_TM
    nt = t // (2 * tm)
    rows3 = rows.reshape(nt, 1, 2 * tm * TOP_K)
    row = lambda i: (i, 0)
    return pl.pallas_call(
        functools.partial(_combine_kernel, tm=tm),
        grid=(nt,),
        in_specs=[pl.BlockSpec((1, 1, 2 * tm * TOP_K), lambda i: (i, 0, 0), memory_space=pltpu.SMEM),
                  pl.BlockSpec((1, 1, 2 * tm * TOP_K), lambda i: (jnp.minimum(i + 1, nt - 1), 0, 0),
                               memory_space=pltpu.SMEM),
                  pl.BlockSpec((2 * tm, D_MODEL), row), pl.BlockSpec((2 * tm, LANES), row),
                  pl.BlockSpec((1, D_MODEL), lambda i: (0, 0)),
                  pl.BlockSpec(memory_space=pl.ANY)],
        out_specs=pl.BlockSpec((2 * tm, D_MODEL), row),
        out_shape=jax.ShapeDtypeStruct((t, D_MODEL), F32),
        scratch_shapes=[pltpu.VMEM((2, TOP_K, tm * SUBLANES, LANES), F32), pltpu.SemaphoreType.DMA((2,))],
        compiler_params=_params(("arbitrary",)),
        name="combine_final_norm",
    )(rows3, rows3, x2, gates, g, yb)


def _rope_tables(seq_len):
    inv_freq = ROPE_THETA ** (-jnp.arange(ROPE_HALF, dtype=F32) * 2.0 / ROPE_DIM)
    ang = jnp.arange(seq_len, dtype=F32)[:, None] * inv_freq[None, :]
    cos, sin = jnp.cos(ang), jnp.sin(ang)
    pad = HEAD_DIM - ROPE_DIM
    ones = jnp.ones((seq_len, pad), F32)
    zer_h = jnp.zeros((seq_len, ROPE_HALF), F32)
    zer_p = jnp.zeros((seq_len, pad), F32)
    c = jnp.concatenate([cos, cos, ones], axis=1)
    s1 = jnp.concatenate([zer_h, sin, zer_p], axis=1)
    s2 = jnp.concatenate([-sin, zer_h, zer_p], axis=1)
    rep = LANES // HEAD_DIM
    return jnp.tile(c, (1, rep)), jnp.tile(s1, (1, rep)), jnp.tile(s2, (1, rep))


def _ssm_weights(a_re, a_im, log_dt, b_re, b_im, c_re, c_im, ssm_d, nsteps):
    r = SSM_CHUNK
    dt = jnp.exp(log_dt)[..., None]
    lr, li = a_re * dt, a_im * dt

    def cpow(n):
        n = jnp.asarray(n, F32)[..., None, None, None]
        mag = jnp.exp(n * lr)
        return mag * jnp.cos(n * li), mag * jnp.sin(n * li)

    ab_re, ab_im = cpow(jnp.ones(()))
    den = a_re * a_re + a_im * a_im
    num_re, num_im = ab_re - 1.0, ab_im
    f_re = (num_re * a_re + num_im * a_im) / den
    f_im = (num_im * a_re - num_re * a_im) / den
    bb_re = f_re[..., None] * b_re - f_im[..., None] * b_im
    bb_im = f_re[..., None] * b_im + f_im[..., None] * b_re

    taus = jnp.arange(r + 1, dtype=F32)
    p_re, p_im = cpow(taus)
    m_re = p_re[..., None] * bb_re - p_im[..., None] * bb_im
    m_im = p_re[..., None] * bb_im + p_im[..., None] * bb_re
    kern = (jnp.einsum('dgcp,tdgpk->tdgck', c_re, m_re) - jnp.einsum('dgcp,tdgpk->tdgck', c_im, m_im))
    center = kern[0, 0] + kern[0, 1] + jnp.eye(SSM_GROUP, dtype=F32) * ssm_d[:, :, None]
    lags = jnp.concatenate([kern[r - 1:0:-1, 0], center[None], kern[1:r, 1]], axis=0)
    rows = jnp.stack([lags[r - 1 - b:2 * r - 1 - b] for b in range(r)], axis=0)
    w_toep = rows.transpose(2, 0, 3, 1, 4).reshape(SSM_N_GROUPS, TOEP, TOEP)

    st_f_re, st_f_im = m_re[r - 1::-1, 0], m_im[r - 1::-1, 0]
    st_b_re, st_b_im = m_re[:r, 1], m_im[:r, 1]
    w_state = jnp.stack([st_f_re, st_f_im, st_b_re, st_b_im], axis=0)
    w_state = w_state.transpose(2, 0, 3, 1, 4).reshape(SSM_N_GROUPS, 4 * SSM_STATE, TOEP)

    def c_times_pow(d, pr, pi):
        zr = c_re[d][None] * pr[:, :, None, :] - c_im[d][None] * pi[:, :, None, :]
        zi = c_re[d][None] * pi[:, :, None, :] + c_im[d][None] * pr[:, :, None, :]
        return zr, -zi
    cf_re, cf_im = c_times_pow(0, p_re[1:r + 1, 0], p_im[1:r + 1, 0])
    cb_re, cb_im = c_times_pow(1, p_re[r:0:-1, 1], p_im[r:0:-1, 1])
    w_carry = jnp.stack([cf_re, cf_im, cb_re, cb_im], axis=3)
    w_carry = w_carry.transpose(1, 0, 2, 3, 4).reshape(SSM_N_GROUPS, TOEP, 4 * SSM_STATE)

    qr, qi = p_re[r], p_im[r]
    cols = []
    for _ in range(nsteps):
        cols.append(jnp.stack([qr[0], qi[0], qr[1], qi[1]], axis=1))
        qr, qi = qr * qr - qi * qi, 2.0 * qr * qi
    pw = jnp.stack(cols, axis=-1).reshape(SSM_N_GROUPS, 4 * SSM_STATE, nsteps)
    return w_toep.astype(BF16), w_state.astype(BF16), w_carry.astype(BF16), pw


def _front(x, prm):
    n, seq_len, _ = x.shape
    t = n * seq_len
    x2d = x.reshape(t, D_MODEL)
    q, kv = _qkv(x2d, prm['norm1_g'], prm['w_qkv'], *_rope_tables(seq_len), seq_len)
    attn_n = _attention(q, kv, prm['sink'], prm['attn_out_g'], seq_len)

    chunks_per_seq = seq_len // SSM_CHUNK
    nsteps = max(1, (chunks_per_seq - 1).bit_length())
    nc = t // SSM_CHUNK
    lt = min(256, nc)
    ut = _uproj(x2d, prm['norm1_g'], prm['w_u_t'], lt)
    yt = _ssm(ut, prm['w_toep'], prm['w_state'], prm['w_carry'], prm['pw'][:, :, :nsteps], chunks_per_seq)
    ssm_n = _glu(yt, prm['glu_w_t'], prm['glu_b'], prm['ssm_out_g'], lt)
    return _outproj(x2d, attn_n, ssm_n, prm['w_out_a'], prm['w_out_s'], prm['norm2_g'],
                    prm['router_w'], prm['router_b'], prm['tri'])


def _cumsum_small(x):
    n = x.shape[0]
    keep = jnp.arange(n)[None, :] <= jnp.arange(n)[:, None]
    return jnp.sum(jnp.where(keep, x[None, :], 0), axis=1)


def _by_expert(idx, table):
    hit = idx[..., None] == jnp.arange(N_EXPERTS, dtype=I32)
    return jnp.sum(jnp.where(hit, table, 0), axis=-1)


def kernel(x_prompt, x_sample, norm1_g, w_in, attn_sink, ssm_a_re, ssm_a_im, ssm_log_dt, ssm_b_re, ssm_b_im, ssm_c_re, ssm_c_im, ssm_d, glu_w, glu_b, attn_out_g, ssm_out_g, w_out, norm2_g, router_w, router_b, w_gate_up, b_gate_up, w_down, b_down, final_g):
    assert norm1_g.shape[0] == 1, "single-layer problem"
    l = 0
    xs = [x_prompt, x_sample]
    max_chunks = max(x.shape[1] for x in xs) // SSM_CHUNK
    max_steps = max(1, (max_chunks - 1).bit_length())
    wq, wk, wv, wu = jnp.split(w_in[l], [ATTN_WIDTH, ATTN_WIDTH + KV_WIDTH, ATTN_WIDTH + 2 * KV_WIDTH], axis=1)
    dup = lambda w: jnp.concatenate([w[:, :HEAD_DIM], w[:, :HEAD_DIM], w[:, HEAD_DIM:], w[:, HEAD_DIM:]], axis=1)
    w_toep, w_state, w_carry, pw = _ssm_weights(ssm_a_re[l], ssm_a_im[l], ssm_log_dt[l], ssm_b_re[l], ssm_b_im[l],
                                                ssm_c_re[l], ssm_c_im[l], ssm_d[l], max_steps)
    tri_i = lax.broadcasted_iota(I32, (ROUTE_TM, ROUTE_TM), 0)
    tri_j = lax.broadcasted_iota(I32, (ROUTE_TM, ROUTE_TM), 1)
    prm = dict(
        norm1_g=norm1_g[l].reshape(1, D_MODEL),
        w_qkv=jnp.concatenate([wq, dup(wk), dup(wv)], axis=1).astype(BF16),
        w_u_t=wu.T.astype(BF16),
        sink=attn_sink[l].astype(F32),
        attn_out_g=attn_out_g[l].reshape(1, ATTN_WIDTH),
        w_toep=w_toep, w_state=w_state, w_carry=w_carry, pw=pw,
        glu_w_t=glu_w[l].T.astype(BF16),
        glu_b=glu_b[l].reshape(SSM_WIDTH, 1),
        ssm_out_g=ssm_out_g[l].reshape(SSM_WIDTH, 1),
        w_out_a=w_out[l][:ATTN_WIDTH].astype(BF16),
        w_out_s=w_out[l][ATTN_WIDTH:].astype(BF16),
        norm2_g=norm2_g[l].reshape(1, D_MODEL),
        router_w=jnp.pad(router_w[l], ((0, 0), (0, LANES - N_EXPERTS))).astype(BF16),
        router_b=jnp.pad(router_b[l], (0, LANES - N_EXPERTS), constant_values=NEG_BIG).reshape(1, LANES),
        tri=(tri_j < tri_i).astype(BF16),
    )
    fronts = [_front(x, prm) for x in xs]

    cnts = [f[4][0, :N_EXPERTS].astype(I32) for f in fronts]
    padded = [(c + MOE_ROWS - 1) // MOE_ROWS * MOE_ROWS for c in cnts]
    pends = [_cumsum_small(p) for p in padded]
    pstarts = [pe - p for pe, p in zip(pends, padded)]
    nbs = [f[0].shape[0] * TOP_K // MOE_ROWS + N_EXPERTS for f in fronts]
    seg_blocks = jnp.stack([p // MOE_ROWS for p in padded], axis=1).reshape(-1)
    seg_end = _cumsum_small(seg_blocks)
    seg_start = seg_end - seg_blocks
    nb = sum(nbs)
    bi = jnp.arange(nb, dtype=I32)
    seg = jnp.minimum(jnp.sum(seg_end[None, :] <= bi[:, None], axis=1), 2 * N_EXPERTS - 1).astype(I32)
    block_e = seg // 2
    block_grp = seg % 2
    n_valid = seg_end[-1].astype(I32).reshape(1)
    live = bi < n_valid[0]
    src_start = jnp.stack([ps // MOE_ROWS for ps in pstarts], axis=1).reshape(-1)
    in_seg = seg[:, None] == jnp.arange(2 * N_EXPERTS, dtype=I32)[None, :]
    src_blk = jnp.sum(jnp.where(in_seg, (src_start - seg_start)[None, :], 0), axis=1) + bi
    not_after = bi[None, :] <= bi[:, None]
    blk = [jnp.max(jnp.where(not_after & (live & (block_grp == g))[None, :], src_blk[None, :], 0), axis=1).astype(I32)
           for g in range(2)]

    xbufs, yrows = [], []
    for g, f in enumerate(fronts):
        e_idx = f[2][:, :TOP_K]
        rank = f[2][:, TOP_K:2 * TOP_K]
        n_pad = nbs[g] * MOE_ROWS
        pad_start = jnp.concatenate([pstarts[g] + cnts[g], pends[g][-1:]]).astype(I32)
        pad_cnt = jnp.concatenate([padded[g] - cnts[g], n_pad - pends[g][-1:]]).astype(I32)
        xrows = (_by_expert(e_idx, pstarts[g]) + rank).astype(I32)
        xbufs.append(_scatter_rows(pad_start, pad_cnt, xrows, f[1], n_pad))
        yrows.append((_by_expert(e_idx, seg_start[g::2] * MOE_ROWS) + rank).astype(I32))
    yb = _experts(block_e, block_grp, blk[0], blk[1], n_valid, xbufs[0], xbufs[1],
                  w_gate_up[l].astype(BF16), b_gate_up[l][:, None, :], w_down[l].astype(BF16), b_down[l][:, None, :])
    gfin = final_g.reshape(1, D_MODEL)
    outs = [_combine(r, f[0], f[3], gfin, yb).reshape(x.shape) for x, f, r in zip(xs, fronts, yrows)]
    return tuple(outs)
```

```python
import functools
import math

import jax
import jax.numpy as jnp
from jax import lax
from jax.experimental import pallas as pl
from jax.experimental.pallas import tpu as pltpu

F32 = jnp.float32
BF16 = jnp.bfloat16
I32 = jnp.int32

D_MODEL = 1024
HEAD_DIM = 64
N_Q_HEADS = 8
N_KV_HEADS = 2
Q_PER_KV = N_Q_HEADS // N_KV_HEADS
ATTN_WIDTH = N_Q_HEADS * HEAD_DIM
KV_WIDTH = N_KV_HEADS * HEAD_DIM
WINDOW = 128
ATT_BLOCK = 128
ROPE_THETA = 500000.0
ROPE_DIM = HEAD_DIM // 4
ROPE_HALF = ROPE_DIM // 2
SSM_WIDTH = 512
SSM_GROUP = 16
SSM_N_GROUPS = SSM_WIDTH // SSM_GROUP
SSM_STATE = 64
N_EXPERTS = 32
TOP_K = 4
SWIGLU_LIMIT = 7.0
SWIGLU_ALPHA = 1.702
EPS = 1e-5

LANES = 128
SUBLANES = 8
SSM_CHUNK = 32
TOEP = SSM_CHUNK * SSM_GROUP
MOE_ROWS = 512
ROUTE_TM = 512
COMBINE_TM = 128
VMEM_LIMIT = 52 * 1024 * 1024
EXPERT_VMEM_LIMIT = 60 * 1024 * 1024
NEG_BIG = -1e30


def _params(sem, vmem=VMEM_LIMIT):
    return pltpu.CompilerParams(dimension_semantics=sem, vmem_limit_bytes=vmem)


def _rms(x, g):
    ms = jnp.mean(x * x, axis=-1, keepdims=True)
    return x * lax.rsqrt(ms + EPS) * g


def _tile_rows_load(ref, rows):
    return jnp.concatenate([ref[pl.ds(j, rows, stride=SUBLANES), :] for j in range(SUBLANES)], axis=1)


def _tile_rows_store(ref, val):
    rows = val.shape[0]
    for j in range(SUBLANES):
        ref[pl.ds(j, rows, stride=SUBLANES), :] = val[:, j * LANES:(j + 1) * LANES]


def _tile_row(ref, row):
    return ref.at[pl.ds(pl.multiple_of(row * SUBLANES, SUBLANES), SUBLANES)]


KV_COLS = 4 * LANES


def _qkv_kernel(x_ref, g_ref, w_ref, c_ref, s1_ref, s2_ref, q_ref, kv_ref):
    h = _rms(x_ref[...], g_ref[...]).astype(BF16)
    p = jnp.dot(h, w_ref[...], preferred_element_type=F32)
    c = c_ref[...]
    s1 = s1_ref[...]
    s2 = s2_ref[...]

    def rot(t):
        return t * c + pltpu.roll(t, ROPE_HALF, 1) * s1 + pltpu.roll(t, LANES - ROPE_HALF, 1) * s2

    for j in range(ATTN_WIDTH // LANES):
        q_ref[:, j * LANES:(j + 1) * LANES] = (rot(p[:, j * LANES:(j + 1) * LANES]) * (HEAD_DIM ** -0.5)).astype(BF16)
    for j in range(N_KV_HEADS):
        col = ATTN_WIDTH + j * LANES
        kv_ref[:, j * LANES:(j + 1) * LANES] = rot(p[:, col:col + LANES]).astype(BF16)
    kv_ref[:, N_KV_HEADS * LANES:] = p[:, ATTN_WIDTH + N_KV_HEADS * LANES:].astype(BF16)


def _qkv(x2d, g, w, c, s1, s2, seq_len, tm=512):
    t = x2d.shape[0]
    nlb = seq_len // tm
    row = lambda i: (i, 0)
    tab = lambda i: (i % nlb, 0)
    full = lambda i: (0, 0)
    return pl.pallas_call(
        _qkv_kernel,
        grid=(t // tm,),
        in_specs=[pl.BlockSpec((tm, D_MODEL), row), pl.BlockSpec((1, D_MODEL), full),
                  pl.BlockSpec((D_MODEL, ATTN_WIDTH + KV_COLS), full),
                  pl.BlockSpec((tm, LANES), tab), pl.BlockSpec((tm, LANES), tab), pl.BlockSpec((tm, LANES), tab)],
        out_specs=[pl.BlockSpec((tm, ATTN_WIDTH), row), pl.BlockSpec((tm, KV_COLS), row)],
        out_shape=[jax.ShapeDtypeStruct((t, ATTN_WIDTH), BF16), jax.ShapeDtypeStruct((t, KV_COLS), BF16)],
        compiler_params=_params(("parallel",)),
        name="qkv_rotary",
    )(x2d, g, w, c, s1, s2)


def _attn_kernel(sink_ref, q_ref, kvp, kvc, kvn, g_ref, o_ref, *, bps):
    i = pl.program_id(0)
    first = (i % bps) == 0
    last = (i % bps) == bps - 1
    qi = lax.broadcasted_iota(I32, (ATT_BLOCK, 3 * ATT_BLOCK), 0)
    kj = lax.broadcasted_iota(I32, (ATT_BLOCK, 3 * ATT_BLOCK), 1)
    rel = kj - ATT_BLOCK - qi
    valid = (jnp.abs(rel) <= WINDOW)
    valid = valid & ((kj >= ATT_BLOCK) | jnp.logical_not(first))
    valid = valid & ((kj < 2 * ATT_BLOCK) | jnp.logical_not(last))
    kv = jnp.concatenate([kvp[...], kvc[...], kvn[...]], axis=0)
    ks = [kv[:, h * LANES:(h + 1) * LANES] for h in range(N_KV_HEADS)]
    vs = [kv[:, (N_KV_HEADS + h) * LANES:(N_KV_HEADS + h + 1) * LANES] for h in range(N_KV_HEADS)]
    lo = lax.broadcasted_iota(I32, (ATT_BLOCK, LANES), 1) < HEAD_DIM
    zero = jnp.zeros((ATT_BLOCK, LANES), BF16)
    heads = [(j, par) for j in range(ATTN_WIDTH // LANES) for par in range(2)]
    nt_dims = (((1,), (1,)), ((), ()))
    scores = []
    for j, par in heads:
        qt = q_ref[:, j * LANES:(j + 1) * LANES]
        qm = jnp.where(lo if par == 0 else jnp.logical_not(lo), qt, zero)
        scores.append(lax.dot_general(qm, ks[j // 2], nt_dims, preferred_element_type=F32))
    scores = [jnp.where(valid, s, NEG_BIG) for s in scores]
    sinks = [sink_ref[2 * j + par] for j, par in heads]
    maxes = [jnp.maximum(jnp.max(s, axis=-1, keepdims=True), sk) for s, sk in zip(scores, sinks)]
    probs = [jnp.exp(s - m) for s, m in zip(scores, maxes)]
    dens = [jnp.sum(p, axis=-1, keepdims=True) + jnp.exp(sk - m) for p, m, sk in zip(probs, maxes, sinks)]
    outs = [jnp.dot(p.astype(BF16), vs[j // 2], preferred_element_type=F32) for p, (j, par) in zip(probs, heads)]
    outs = [o / d for o, d in zip(outs, dens)]
    tiles = [jnp.where(lo, outs[2 * j], outs[2 * j + 1]) for j in range(ATTN_WIDTH // LANES)]
    o = jnp.concatenate(tiles, axis=1)
    o_ref[...] = _rms(o, g_ref[...]).astype(BF16)


def _attention(q, kv, sink, g, seq_len):
    t = q.shape[0]
    nblk = t // ATT_BLOCK
    bps = seq_len // ATT_BLOCK
    cur = lambda i, s: (i, 0)
    prv = lambda i, s: (jnp.maximum(i - 1, 0), 0)
    nxt = lambda i, s: (jnp.minimum(i + 1, nblk - 1), 0)
    grid_spec = pltpu.PrefetchScalarGridSpec(
        num_scalar_prefetch=1,
        grid=(nblk,),
        in_specs=[pl.BlockSpec((ATT_BLOCK, ATTN_WIDTH), cur),
                  pl.BlockSpec((ATT_BLOCK, KV_COLS), prv), pl.BlockSpec((ATT_BLOCK, KV_COLS), cur),
                  pl.BlockSpec((ATT_BLOCK, KV_COLS), nxt),
                  pl.BlockSpec((1, ATTN_WIDTH), lambda i, s: (0, 0))],
        out_specs=pl.BlockSpec((ATT_BLOCK, ATTN_WIDTH), cur),
    )
    return pl.pallas_call(
        functools.partial(_attn_kernel, bps=bps),
        grid_spec=grid_spec,
        out_shape=jax.ShapeDtypeStruct((t, ATTN_WIDTH), BF16),
        compiler_params=_params(("parallel",)),
        name="banded_attention",
    )(sink, q, kv, kv, kv, g)


def _uproj_kernel(x_ref, g_ref, w_ref, o_ref, stage_ref):
    lt = x_ref.shape[0]
    h = _rms(x_ref[...].reshape(lt * SUBLANES, D_MODEL), g_ref[...])
    for j in range(D_MODEL // LANES):
        stage_ref[j] = h[:, j * LANES:(j + 1) * LANES]
    for bl in range(SUBLANES):
        hb = jnp.concatenate([stage_ref[j, pl.ds(bl, lt, stride=SUBLANES), :] for j in range(D_MODEL // LANES)],
                             axis=1).astype(BF16)
        ut = lax.dot_general(w_ref[...], hb, (((1,), (1,)), ((), ())), preferred_element_type=F32)
        o_ref[bl] = ut.astype(BF16)


def _uproj(x2d, g, w_t, lt):
    t = x2d.shape[0]
    nc = t // SSM_CHUNK
    xv = x2d.reshape(nc, SSM_CHUNK // SUBLANES, SUBLANES, D_MODEL)
    return pl.pallas_call(
        _uproj_kernel,
        grid=(nc // lt, SSM_CHUNK // SUBLANES),
        in_specs=[pl.BlockSpec((lt, None, SUBLANES, D_MODEL), lambda i, b: (i, b, 0, 0)),
                  pl.BlockSpec((1, D_MODEL), lambda i, b: (0, 0)),
                  pl.BlockSpec((SSM_WIDTH, D_MODEL), lambda i, b: (0, 0))],
        out_specs=pl.BlockSpec((SUBLANES, SSM_WIDTH, lt), lambda i, b: (b, 0, i)),
        out_shape=jax.ShapeDtypeStruct((SSM_CHUNK, SSM_WIDTH, nc), BF16),
        scratch_shapes=[pltpu.VMEM((D_MODEL // LANES, lt * SUBLANES, LANES), F32)],
        compiler_params=_params(("parallel", "parallel")),
        name="u_projection",
    )(xv, g, w_t)


def _gelu_tanh(x):
    return 0.5 * x * (1.0 + jnp.tanh(math.sqrt(2.0 / math.pi) * (x + 0.044715 * (x * x * x))))


def _ssm_kernel(a_ref, wt_ref, ws_ref, wc_ref, pw_ref, y_ref, *, chunks_per_seq, nsteps):
    nc = a_ref.shape[2]
    a = a_ref[...].reshape(TOEP, nc)
    y = jnp.dot(wt_ref[0], a, preferred_element_type=F32)
    s = jnp.dot(ws_ref[0], a, preferred_element_type=F32)
    pos = lax.broadcasted_iota(I32, (SSM_STATE, nc), 1) % chunks_per_seq
    carries = []
    for d in range(2):
        hr = s[2 * d * SSM_STATE:(2 * d + 1) * SSM_STATE]
        hi = s[(2 * d + 1) * SSM_STATE:(2 * d + 2) * SSM_STATE]
        for k in range(nsteps):
            sh = 1 << k
            pr = pw_ref[0, 2 * d * SSM_STATE:(2 * d + 1) * SSM_STATE, k:k + 1]
            pi = pw_ref[0, (2 * d + 1) * SSM_STATE:(2 * d + 2) * SSM_STATE, k:k + 1]
            if d == 0:
                ok = pos >= sh
                sr = pltpu.roll(hr, sh, 1)
                si = pltpu.roll(hi, sh, 1)
            else:
                ok = pos < chunks_per_seq - sh
                sr = pltpu.roll(hr, nc - sh, 1)
                si = pltpu.roll(hi, nc - sh, 1)
            hr, hi = (hr + jnp.where(ok, pr * sr - pi * si, 0.0),
                      hi + jnp.where(ok, pr * si + pi * sr, 0.0))
        if d == 0:
            ok = pos >= 1
            cr = pltpu.roll(hr, 1, 1)
            ci = pltpu.roll(hi, 1, 1)
        else:
            ok = pos < chunks_per_seq - 1
            cr = pltpu.roll(hr, nc - 1, 1)
            ci = pltpu.roll(hi, nc - 1, 1)
        carries += [jnp.where(ok, cr, 0.0), jnp.where(ok, ci, 0.0)]
    carry = jnp.concatenate(carries, axis=0).astype(BF16)
    y = y + jnp.dot(wc_ref[0], carry, preferred_element_type=F32)
    y_ref[...] = _gelu_tanh(y).reshape(SSM_CHUNK, SSM_GROUP, nc)


def _ssm(ut, w_toep, w_state, w_carry, pw, chunks_per_seq):
    nc = ut.shape[2]
    nsteps = max(1, (chunks_per_seq - 1).bit_length())
    g3 = lambda g: (g, 0, 0)
    return pl.pallas_call(
        functools.partial(_ssm_kernel, chunks_per_seq=chunks_per_seq, nsteps=nsteps),
        grid=(SSM_N_GROUPS,),
        in_specs=[pl.BlockSpec((SSM_CHUNK, SSM_GROUP, nc), lambda g: (0, g, 0)),
                  pl.BlockSpec((1, TOEP, TOEP), g3),
                  pl.BlockSpec((1, 4 * SSM_STATE, TOEP), g3),
                  pl.BlockSpec((1, TOEP, 4 * SSM_STATE), g3),
                  pl.BlockSpec((1, 4 * SSM_STATE, pw.shape[2]), g3)],
        out_specs=pl.BlockSpec((SSM_CHUNK, SSM_GROUP, nc), lambda g: (0, g, 0)),
        out_shape=jax.ShapeDtypeStruct((SSM_CHUNK, SSM_WIDTH, nc), F32),
        compiler_params=_params(("parallel",)),
        name="s5_core",
    )(ut, w_toep, w_state, w_carry, pw)


def _glu_kernel(y_ref, w_ref, b_ref, g_ref, o_ref, stage_ref):
    lt = o_ref.shape[0]
    for bl in range(SUBLANES):
        y = y_ref[bl]
        z = jnp.dot(w_ref[...], y.astype(BF16), preferred_element_type=F32) + b_ref[...]
        s = y * jax.nn.sigmoid(z)
        ms = jnp.mean(s * s, axis=0, keepdims=True)
        sn = (s * lax.rsqrt(ms + EPS) * g_ref[...]).T
        for j in range(SSM_WIDTH // LANES):
            stage_ref[j, pl.ds(bl, lt, stride=SUBLANES), :] = sn[:, j * LANES:(j + 1) * LANES]
    for j in range(SSM_WIDTH // LANES):
        o_ref[:, :, j * LANES:(j + 1) * LANES] = stage_ref[j].reshape(lt, SUBLANES, LANES)


def _glu(yt, w_t, b_col, g_col, lt):
    nc = yt.shape[2]
    out = pl.pallas_call(
        _glu_kernel,
        grid=(nc // lt, SSM_CHUNK // SUBLANES),
        in_specs=[pl.BlockSpec((SUBLANES, SSM_WIDTH, lt), lambda i, b: (b, 0, i)),
                  pl.BlockSpec((SSM_WIDTH, SSM_WIDTH), lambda i, b: (0, 0)),
                  pl.BlockSpec((SSM_WIDTH, 1), lambda i, b: (0, 0)),
                  pl.BlockSpec((SSM_WIDTH, 1), lambda i, b: (0, 0))],
        out_specs=pl.BlockSpec((lt, None, SUBLANES, SSM_WIDTH), lambda i, b: (i, b, 0, 0)),
        out_shape=jax.ShapeDtypeStruct((nc, SSM_CHUNK // SUBLANES, SUBLANES, SSM_WIDTH), F32),
        scratch_shapes=[pltpu.VMEM((SSM_WIDTH // LANES, lt * SUBLANES, LANES), F32)],
        compiler_params=_params(("parallel", "parallel")),
        name="glu_norm",
    )(yt, w_t, b_col, g_col)
    return out.reshape(nc * SSM_CHUNK, SSM_WIDTH)


def _outproj_kernel(x_ref, a_ref, s_ref, wa_ref, ws_ref, g_ref, wr_ref, br_ref, tri_ref,
                    x2_ref, h2_ref, rt_ref, gt_ref, cnt_ref, run_ref):
    i = pl.program_id(0)

    @pl.when(i == 0)
    def _():
        run_ref[...] = jnp.zeros_like(run_ref)

    x2 = (x_ref[...] + jnp.dot(a_ref[...], wa_ref[...], preferred_element_type=F32)
          + jnp.dot(s_ref[...].astype(BF16), ws_ref[...], preferred_element_type=F32))
    x2_ref[...] = x2
    h2 = _rms(x2, g_ref[...])
    _tile_rows_store(h2_ref, h2)
    logits = jnp.dot(h2.astype(BF16), wr_ref[...], preferred_element_type=F32) + br_ref[...]

    tm = logits.shape[0]
    lane = lax.broadcasted_iota(I32, (tm, LANES), 1)
    lane_f = lane.astype(F32)
    work = logits
    sel = jnp.zeros((tm, LANES), F32)
    top_v, top_i = [], []
    for _ in range(TOP_K):
        m = jnp.max(work, axis=-1, keepdims=True)
        idx = jnp.min(jnp.where(work == m, lane_f, float(LANES)), axis=-1, keepdims=True).astype(I32)
        hit = lane == idx
        sel = jnp.where(hit, 1.0, sel)
        work = jnp.where(hit, -jnp.inf, work)
        top_v.append(m)
        top_i.append(idx)
    ex = [jnp.exp(v - top_v[0]) for v in top_v]
    den = ex[0] + ex[1] + ex[2] + ex[3]
    before = jnp.dot(tri_ref[...], sel.astype(BF16), preferred_element_type=F32) + run_ref[0:1, :]
    rt = jnp.zeros((tm, LANES), I32)
    gt = jnp.zeros((tm, LANES), F32)
    for k in range(TOP_K):
        rank = jnp.sum(jnp.where(lane == top_i[k], before, 0.0), axis=-1, keepdims=True).astype(I32)
        rt = jnp.where(lane == k, top_i[k], rt)
        rt = jnp.where(lane == TOP_K + k, rank, rt)
        gt = jnp.where(lane == k, ex[k] / den, gt)
    rt_ref[...] = rt
    gt_ref[...] = gt
    run = run_ref[...] + jnp.sum(sel, axis=0, keepdims=True)
    run_ref[...] = run
    cnt_ref[...] = run


def _outproj(x2d, attn_n, ssm_n, w_a, w_s, g, w_r, b_r, tri):
    t = x2d.shape[0]
    tm = ROUTE_TM
    row = lambda i: (i, 0)
    full = lambda i: (0, 0)
    return pl.pallas_call(
        _outproj_kernel,
        grid=(t // tm,),
        in_specs=[pl.BlockSpec((tm, D_MODEL), row), pl.BlockSpec((tm, ATTN_WIDTH), row),
                  pl.BlockSpec((tm, SSM_WIDTH), row),
                  pl.BlockSpec((ATTN_WIDTH, D_MODEL), full), pl.BlockSpec((SSM_WIDTH, D_MODEL), full),
                  pl.BlockSpec((1, D_MODEL), full),
                  pl.BlockSpec((D_MODEL, LANES), full), pl.BlockSpec((1, LANES), full),
                  pl.BlockSpec((tm, tm), full)],
        out_specs=[pl.BlockSpec((tm, D_MODEL), row), pl.BlockSpec((tm * SUBLANES, LANES), row),
                   pl.BlockSpec((tm, LANES), row), pl.BlockSpec((tm, LANES), row),
                   pl.BlockSpec((SUBLANES, LANES), full)],
        out_shape=[jax.ShapeDtypeStruct((t, D_MODEL), F32), jax.ShapeDtypeStruct((t * SUBLANES, LANES), F32),
                   jax.ShapeDtypeStruct((t, LANES), I32), jax.ShapeDtypeStruct((t, LANES), F32),
                   jax.ShapeDtypeStruct((SUBLANES, LANES), F32)],
        scratch_shapes=[pltpu.VMEM((SUBLANES, LANES), F32)],
        compiler_params=_params(("arbitrary",)),
        name="out_projection_router",
    )(x2d, attn_n, ssm_n, w_a, w_s, g, w_r, b_r, tri)


DMA_GROUP = 4


def _scatter_kernel(pad_start_ref, pad_cnt_ref, rows_ref, h_ref, xb_hbm, zero_ref, stage_ref, sems, zsem, *, tm):
    i = pl.program_id(0)
    nt = pl.num_programs(0)
    n_dma = tm * TOP_K

    def tile_wait(slot):
        whole = xb_hbm.at[pl.ds(0, n_dma * SUBLANES)]
        pltpu.make_async_copy(whole, whole, sems.at[slot]).wait()

    @pl.when(i == 0)
    def _():
        zero_ref[...] = jnp.zeros_like(zero_ref)

        def fill(row, n_rows):
            return pltpu.make_async_copy(zero_ref.at[pl.ds(0, n_rows * SUBLANES)],
                                         xb_hbm.at[pl.ds(pl.multiple_of(row * SUBLANES, SUBLANES), n_rows * SUBLANES)],
                                         zsem)

        for phase in range(2):
            def per_expert(e, c):
                cnt = pad_cnt_ref[e]
                row = pad_start_ref[e]
                run = MOE_ROWS // 2
                while run >= 1:
                    below = cnt & ~(2 * run - 1)

                    @pl.when((cnt & run) != 0)
                    def _(run=run, below=below):
                        cp = fill(row + below, run)
                        cp.start() if phase == 0 else cp.wait()
                    run //= 2
                return c
            lax.fori_loop(0, N_EXPERTS, per_expert, 0)

            def per_tail_block(b, c):
                cp = fill(pad_start_ref[N_EXPERTS] + b * MOE_ROWS, MOE_ROWS)
                cp.start() if phase == 0 else cp.wait()
                return c
            lax.fori_loop(0, pad_cnt_ref[N_EXPERTS] // MOE_ROWS, per_tail_block, 0)

    for slot in range(2):
        @pl.when(i > 0)
        def _(slot=slot):
            tile_wait(slot)

        stage_ref[slot] = h_ref[pl.ds(slot * tm * SUBLANES, tm * SUBLANES), :]

        def issue(grp, c, slot=slot):
            r0 = grp * DMA_GROUP
            dst = [rows_ref[0, 0, (slot * tm + r0 + r) * TOP_K + k] for r in range(DMA_GROUP) for k in range(TOP_K)]
            for r in range(DMA_GROUP):
                src = _tile_row(stage_ref.at[slot], r0 + r)
                for k in range(TOP_K):
                    pltpu.make_async_copy(src, _tile_row(xb_hbm, dst[r * TOP_K + k]),
                                          sems.at[slot]).start(priority=k % 2)
            return c
        lax.fori_loop(0, tm // DMA_GROUP, issue, 0)

    @pl.when(i == nt - 1)
    def _():
        tile_wait(0)
        tile_wait(1)


def _scatter_rows(pad_start, pad_cnt, rows, h2, n_pad):
    tm = ROUTE_TM
    nt = h2.shape[0] // SUBLANES // (2 * tm)
    grid_spec = pltpu.PrefetchScalarGridSpec(
        num_scalar_prefetch=2,
        grid=(nt,),
        in_specs=[pl.BlockSpec((1, 1, 2 * tm * TOP_K), lambda i, a, b: (i, 0, 0), memory_space=pltpu.SMEM),
                  pl.BlockSpec((2 * tm * SUBLANES, LANES), lambda i, a, b: (i, 0))],
        out_specs=pl.BlockSpec(memory_space=pl.ANY),
        scratch_shapes=[pltpu.VMEM((MOE_ROWS * SUBLANES, LANES), F32),
                        pltpu.VMEM((2, tm * SUBLANES, LANES), F32),
                        pltpu.SemaphoreType.DMA((2,)), pltpu.SemaphoreType.DMA(())],
    )
    return pl.pallas_call(
        functools.partial(_scatter_kernel, tm=tm),
        grid_spec=grid_spec,
        out_shape=jax.ShapeDtypeStruct((n_pad * SUBLANES, LANES), F32),
        compiler_params=_params(("arbitrary",)),
        name="scatter_rows",
    )(pad_start, pad_cnt, rows.reshape(nt, 1, 2 * tm * TOP_K), h2)


def _expert_kernel(be_ref, grp_ref, ia_ref, ib_ref, nv_ref, xa_ref, xb_ref, wgu_ref, bgu_ref, wd_ref, bd_ref,
                   o_ref, x_scr, wgu_scr, wd_scr):
    i = pl.program_id(0)
    live = i < nv_ref[0]

    @pl.when(live & ((i == 0) | (be_ref[i] != be_ref[jnp.maximum(i - 1, 0)])))
    def _():
        wgu_scr[...] = wgu_ref[0].astype(BF16)
        wd_scr[...] = wd_ref[0].astype(BF16)

    @pl.when(live & (grp_ref[i] == 0))
    def _():
        x_scr[...] = _tile_rows_load(xa_ref, MOE_ROWS).astype(BF16)

    @pl.when(live & (grp_ref[i] != 0))
    def _():
        x_scr[...] = _tile_rows_load(xb_ref, MOE_ROWS).astype(BF16)

    @pl.when(live)
    def _():
        gu = jnp.dot(x_scr[...], wgu_scr[...], preferred_element_type=F32) + bgu_ref[0]
        gate = jnp.minimum(gu[:, :D_MODEL], SWIGLU_LIMIT)
        up = jnp.clip(gu[:, D_MODEL:], -SWIGLU_LIMIT, SWIGLU_LIMIT)
        act = (up + 1.0) * (gate * jax.nn.sigmoid(SWIGLU_ALPHA * gate))
        _tile_rows_store(o_ref, jnp.dot(act.astype(BF16), wd_scr[...], preferred_element_type=F32) + bd_ref[0])

    @pl.when(i >= nv_ref[0])
    def _():
        o_ref[...] = jnp.zeros_like(o_ref)


def _experts(block_e, block_grp, blk_a, blk_b, n_valid, xa, xb, wgu, bgu, wd, bd):
    nb = block_e.shape[0]
    e3 = lambda i, be, gr, ia, ib, nv: (be[i], 0, 0)
    grid_spec = pltpu.PrefetchScalarGridSpec(
        num_scalar_prefetch=5,
        grid=(nb,),
        in_specs=[pl.BlockSpec((MOE_ROWS * SUBLANES, LANES), lambda i, be, gr, ia, ib, nv: (ia[i], 0)),
                  pl.BlockSpec((MOE_ROWS * SUBLANES, LANES), lambda i, be, gr, ia, ib, nv: (ib[i], 0)),
                  pl.BlockSpec((1, D_MODEL, 2 * D_MODEL), e3), pl.BlockSpec((1, 1, 2 * D_MODEL), e3),
                  pl.BlockSpec((1, D_MODEL, D_MODEL), e3), pl.BlockSpec((1, 1, D_MODEL), e3)],
        out_specs=pl.BlockSpec((MOE_ROWS * SUBLANES, LANES), lambda i, be, gr, ia, ib, nv: (i, 0)),
        scratch_shapes=[pltpu.VMEM((MOE_ROWS, D_MODEL), BF16), pltpu.VMEM((D_MODEL, 2 * D_MODEL), BF16),
                        pltpu.VMEM((D_MODEL, D_MODEL), BF16)],
    )
    return pl.pallas_call(
        _expert_kernel,
        grid_spec=grid_spec,
        out_shape=jax.ShapeDtypeStruct((nb * MOE_ROWS * SUBLANES, LANES), F32),
        compiler_params=_params(("arbitrary",), vmem=EXPERT_VMEM_LIMIT),
        name="routed_experts",
    )(block_e, block_grp, blk_a, blk_b, n_valid, xa, xb, wgu, bgu, wd, bd)


def _combine_kernel(rows_cur, rows_nxt, x_ref, gt_ref, g_ref, y_hbm, o_ref, buf, sems, *, tm):
    i = pl.program_id(0)
    nt = pl.num_programs(0)

    def issue(rows_ref, half, slot):
        def body(grp, c):
            r0 = grp * DMA_GROUP
            src = [rows_ref[0, 0, (half * tm + r0 + r) * TOP_K + k] for r in range(DMA_GROUP) for k in range(TOP_K)]
            for r in range(DMA_GROUP):
                for k in range(TOP_K):
                    pltpu.make_async_copy(_tile_row(y_hbm, src[r * TOP_K + k]),
                                          _tile_row(buf.at[slot, k], r0 + r), sems.at[slot]).start(priority=k % 2)
            return c
        lax.fori_loop(0, tm // DMA_GROUP, body, 0)

    def finish(half):
        slot = half
        pltpu.make_async_copy(buf.at[slot], buf.at[slot], sems.at[slot]).wait()
        acc = x_ref[half * tm:(half + 1) * tm, :]
        gt = gt_ref[half * tm:(half + 1) * tm, :]
        for k in range(TOP_K):
            acc = acc + _tile_rows_load(buf.at[slot, k], tm) * gt[:, k:k + 1]
        o_ref[half * tm:(half + 1) * tm, :] = _rms(acc, g_ref[...])

    @pl.when(i == 0)
    def _():
        issue(rows_cur, 0, 0)

    issue(rows_cur, 1, 1)
    finish(0)

    @pl.when(i + 1 < nt)
    def _():
        issue(rows_nxt, 0, 0)

    finish(1)


def _combine(rows, x2, gates, g, yb):
    t = x2.shape[0]
    tm = COMBINE_TM
    nt = t // (2 * tm)
    rows3 = rows.reshape(nt, 1, 2 * tm * TOP_K)
    row = lambda i: (i, 0)
    return pl.pallas_call(
        functools.partial(_combine_kernel, tm=tm),
        grid=(nt,),
        in_specs=[pl.BlockSpec((1, 1, 2 * tm * TOP_K), lambda i: (i, 0, 0), memory_space=pltpu.SMEM),
                  pl.BlockSpec((1, 1, 2 * tm * TOP_K), lambda i: (jnp.minimum(i + 1, nt - 1), 0, 0),
                               memory_space=pltpu.SMEM),
                  pl.BlockSpec((2 * tm, D_MODEL), row), pl.BlockSpec((2 * tm, LANES), row),
                  pl.BlockSpec((1, D_MODEL), lambda i: (0, 0)),
                  pl.BlockSpec(memory_space=pl.ANY)],
        out_specs=pl.BlockSpec((2 * tm, D_MODEL), row),
        out_shape=jax.ShapeDtypeStruct((t, D_MODEL), F32),
        scratch_shapes=[pltpu.VMEM((2, TOP_K, tm * SUBLANES, LANES), F32), pltpu.SemaphoreType.DMA((2,))],
        compiler_params=_params(("arbitrary",)),
        name="combine_final_norm",
    )(rows3, rows3, x2, gates, g, yb)


def _rope_tables(seq_len):
    inv_freq = ROPE_THETA ** (-jnp.arange(ROPE_HALF, dtype=F32) * 2.0 / ROPE_DIM)
    ang = jnp.arange(seq_len, dtype=F32)[:, None] * inv_freq[None, :]
    cos, sin = jnp.cos(ang), jnp.sin(ang)
    pad = HEAD_DIM - ROPE_DIM
    ones = jnp.ones((seq_len, pad), F32)
    zer_h = jnp.zeros((seq_len, ROPE_HALF), F32)
    zer_p = jnp.zeros((seq_len, pad), F32)
    c = jnp.concatenate([cos, cos, ones], axis=1)
    s1 = jnp.concatenate([zer_h, sin, zer_p], axis=1)
    s2 = jnp.concatenate([-sin, zer_h, zer_p], axis=1)
    rep = LANES // HEAD_DIM
    return jnp.tile(c, (1, rep)), jnp.tile(s1, (1, rep)), jnp.tile(s2, (1, rep))


def _ssm_weights(a_re, a_im, log_dt, b_re, b_im, c_re, c_im, ssm_d, nsteps):
    r = SSM_CHUNK
    dt = jnp.exp(log_dt)[..., None]
    lr, li = a_re * dt, a_im * dt

    def cpow(n):
        n = jnp.asarray(n, F32)[..., None, None, None]
        mag = jnp.exp(n * lr)
        return mag * jnp.cos(n * li), mag * jnp.sin(n * li)

    ab_re, ab_im = cpow(jnp.ones(()))
    den = a_re * a_re + a_im * a_im
    num_re, num_im = ab_re - 1.0, ab_im
    f_re = (num_re * a_re + num_im * a_im) / den
    f_im = (num_im * a_re - num_re * a_im) / den
    bb_re = f_re[..., None] * b_re - f_im[..., None] * b_im
    bb_im = f_re[..., None] * b_im + f_im[..., None] * b_re

    taus = jnp.arange(r + 1, dtype=F32)
    p_re, p_im = cpow(taus)
    m_re = p_re[..., None] * bb_re - p_im[..., None] * bb_im
    m_im = p_re[..., None] * bb_im + p_im[..., None] * bb_re
    kern = (jnp.einsum('dgcp,tdgpk->tdgck', c_re, m_re) - jnp.einsum('dgcp,tdgpk->tdgck', c_im, m_im))
    center = kern[0, 0] + kern[0, 1] + jnp.eye(SSM_GROUP, dtype=F32) * ssm_d[:, :, None]
    lags = jnp.concatenate([kern[r - 1:0:-1, 0], center[None], kern[1:r, 1]], axis=0)
    rows = jnp.stack([lags[r - 1 - b:2 * r - 1 - b] for b in range(r)], axis=0)
    w_toep = rows.transpose(2, 0, 3, 1, 4).reshape(SSM_N_GROUPS, TOEP, TOEP)

    st_f_re, st_f_im = m_re[r - 1::-1, 0], m_im[r - 1::-1, 0]
    st_b_re, st_b_im = m_re[:r, 1], m_im[:r, 1]
    w_state = jnp.stack([st_f_re, st_f_im, st_b_re, st_b_im], axis=0)
    w_state = w_state.transpose(2, 0, 3, 1, 4).reshape(SSM_N_GROUPS, 4 * SSM_STATE, TOEP)

    def c_times_pow(d, pr, pi):
        zr = c_re[d][None] * pr[:, :, None, :] - c_im[d][None] * pi[:, :, None, :]
        zi = c_re[d][None] * pi[:, :, None, :] + c_im[d][None] * pr[:, :, None, :]
        return zr, -zi
    cf_re, cf_im = c_times_pow(0, p_re[1:r + 1, 0], p_im[1:r + 1, 0])
    cb_re, cb_im = c_times_pow(1, p_re[r:0:-1, 1], p_im[r:0:-1, 1])
    w_carry = jnp.stack([cf_re, cf_im, cb_re, cb_im], axis=3)
    w_carry = w_carry.transpose(1, 0, 2, 3, 4).reshape(SSM_N_GROUPS, TOEP, 4 * SSM_STATE)

    qr, qi = p_re[r], p_im[r]
    cols = []
    for _ in range(nsteps):
        cols.append(jnp.stack([qr[0], qi[0], qr[1], qi[1]], axis=1))
        qr, qi = qr * qr - qi * qi, 2.0 * qr * qi
    pw = jnp.stack(cols, axis=-1).reshape(SSM_N_GROUPS, 4 * SSM_STATE, nsteps)
    return w_toep.astype(BF16), w_state.astype(BF16), w_carry.astype(BF16), pw


def _front(x, prm):
    n, seq_len, _ = x.shape
    t = n * seq_len
    x2d = x.reshape(t, D_MODEL)
    q, kv = _qkv(x2d, prm['norm1_g'], prm['w_qkv'], *_rope_tables(seq_len), seq_len)
    attn_n = _attention(q, kv, prm['sink'], prm['attn_out_g'], seq_len)

    chunks_per_seq = seq_len // SSM_CHUNK
    nsteps = max(1, (chunks_per_seq - 1).bit_length())
    nc = t // SSM_CHUNK
    lt = min(256, nc)
    ut = _uproj(x2d, prm['norm1_g'], prm['w_u_t'], lt)
    yt = _ssm(ut, prm['w_toep'], prm['w_state'], prm['w_carry'], prm['pw'][:, :, :nsteps], chunks_per_seq)
    ssm_n = _glu(yt, prm['glu_w_t'], prm['glu_b'], prm['ssm_out_g'], lt)
    return _outproj(x2d, attn_n, ssm_n, prm['w_out_a'], prm['w_out_s'], prm['norm2_g'],
                    prm['router_w'], prm['router_b'], prm['tri'])


def _cumsum_small(x):
    n = x.shape[0]
    keep = jnp.arange(n)[None, :] <= jnp.arange(n)[:, None]
    return jnp.sum(jnp.where(keep, x[None, :], 0), axis=1)


def _by_expert(idx, table):
    hit = idx[..., None] == jnp.arange(N_EXPERTS, dtype=I32)
    return jnp.sum(jnp.where(hit, table, 0), axis=-1)


def kernel(x_prompt, x_sample, norm1_g, w_in, attn_sink, ssm_a_re, ssm_a_im, ssm_log_dt, ssm_b_re, ssm_b_im, ssm_c_re, ssm_c_im, ssm_d, glu_w, glu_b, attn_out_g, ssm_out_g, w_out, norm2_g, router_w, router_b, w_gate_up, b_gate_up, w_down, b_down, final_g):
    assert norm1_g.shape[0] == 1, "single-layer problem"
    l = 0
    xs = [x_prompt, x_sample]
    max_chunks = max(x.shape[1] for x in xs) // SSM_CHUNK
    max_steps = max(1, (max_chunks - 1).bit_length())
    wq, wk, wv, wu = jnp.split(w_in[l], [ATTN_WIDTH, ATTN_WIDTH + KV_WIDTH, ATTN_WIDTH + 2 * KV_WIDTH], axis=1)
    dup = lambda w: jnp.concatenate([w[:, :HEAD_DIM], w[:, :HEAD_DIM], w[:, HEAD_DIM:], w[:, HEAD_DIM:]], axis=1)
    w_toep, w_state, w_carry, pw = _ssm_weights(ssm_a_re[l], ssm_a_im[l], ssm_log_dt[l], ssm_b_re[l], ssm_b_im[l],
                                                ssm_c_re[l], ssm_c_im[l], ssm_d[l], max_steps)
    tri_i = lax.broadcasted_iota(I32, (ROUTE_TM, ROUTE_TM), 0)
    tri_j = lax.broadcasted_iota(I32, (ROUTE_TM, ROUTE_TM), 1)
    prm = dict(
        norm1_g=norm1_g[l].reshape(1, D_MODEL),
        w_qkv=jnp.concatenate([wq, dup(wk), dup(wv)], axis=1).astype(BF16),
        w_u_t=wu.T.astype(BF16),
        sink=attn_sink[l].astype(F32),
        attn_out_g=attn_out_g[l].reshape(1, ATTN_WIDTH),
        w_toep=w_toep, w_state=w_state, w_carry=w_carry, pw=pw,
        glu_w_t=glu_w[l].T.astype(BF16),
        glu_b=glu_b[l].reshape(SSM_WIDTH, 1),
        ssm_out_g=ssm_out_g[l].reshape(SSM_WIDTH, 1),
        w_out_a=w_out[l][:ATTN_WIDTH].astype(BF16),
        w_out_s=w_out[l][ATTN_WIDTH:].astype(BF16),
        norm2_g=norm2_g[l].reshape(1, D_MODEL),
        router_w=jnp.pad(router_w[l], ((0, 0), (0, LANES - N_EXPERTS))).astype(BF16),
        router_b=jnp.pad(router_b[l], (0, LANES - N_EXPERTS), constant_values=NEG_BIG).reshape(1, LANES),
        tri=(tri_j < tri_i).astype(BF16),
    )
    fronts = [_front(x, prm) for x in xs]

    cnts = [f[4][0, :N_EXPERTS].astype(I32) for f in fronts]
    padded = [(c + MOE_ROWS - 1) // MOE_ROWS * MOE_ROWS for c in cnts]
    pends = [_cumsum_small(p) for p in padded]
    pstarts = [pe - p for pe, p in zip(pends, padded)]
    nbs = [f[0].shape[0] * TOP_K // MOE_ROWS + N_EXPERTS for f in fronts]
    seg_blocks = jnp.stack([p // MOE_ROWS for p in padded], axis=1).reshape(-1)
    seg_end = _cumsum_small(seg_blocks)
    seg_start = seg_end - seg_blocks
    nb = sum(nbs)
    bi = jnp.arange(nb, dtype=I32)
    seg = jnp.minimum(jnp.sum(seg_end[None, :] <= bi[:, None], axis=1), 2 * N_EXPERTS - 1).astype(I32)
    block_e = seg // 2
    block_grp = seg % 2
    n_valid = seg_end[-1].astype(I32).reshape(1)
    live = bi < n_valid[0]
    src_start = jnp.stack([ps // MOE_ROWS for ps in pstarts], axis=1).reshape(-1)
    in_seg = seg[:, None] == jnp.arange(2 * N_EXPERTS, dtype=I32)[None, :]
    src_blk = jnp.sum(jnp.where(in_seg, (src_start - seg_start)[None, :], 0), axis=1) + bi
    not_after = bi[None, :] <= bi[:, None]
    blk = [jnp.max(jnp.where(not_after & (live & (block_grp == g))[None, :], src_blk[None, :], 0), axis=1).astype(I32)
           for g in range(2)]

    xbufs, yrows = [], []
    for g, f in enumerate(fronts):
        e_idx = f[2][:, :TOP_K]
        rank = f[2][:, TOP_K:2 * TOP_K]
        n_pad = nbs[g] * MOE_ROWS
        pad_start = jnp.concatenate([pstarts[g] + cnts[g], pends[g][-1:]]).astype(I32)
        pad_cnt = jnp.concatenate([padded[g] - cnts[g], n_pad - pends[g][-1:]]).astype(I32)
        xrows = (_by_expert(e_idx, pstarts[g]) + rank).astype(I32)
        xbufs.append(_scatter_rows(pad_start, pad_cnt, xrows, f[1], n_pad))
        yrows.append((_by_expert(e_idx, seg_start[g::2] * MOE_ROWS) + rank).astype(I32))
    yb = _experts(block_e, block_grp, blk[0], blk[1], n_valid, xbufs[0], xbufs[1],
                  w_gate_up.reshape(N_EXPERTS, D_MODEL, 2 * D_MODEL), b_gate_up[l][:, None, :],
                  w_down.reshape(N_EXPERTS, D_MODEL, D_MODEL), b_down[l][:, None, :])
    gfin = final_g.reshape(1, D_MODEL)
    outs = [_combine(r, f[0], f[3], gfin, yb).reshape(x.shape) for x, f, r in zip(xs, fronts, yrows)]
    return tuple(outs)
```

```python
import functools
import math

import jax
import jax.numpy as jnp
from jax import lax
from jax.experimental import pallas as pl
from jax.experimental.pallas import tpu as pltpu

F32 = jnp.float32
BF16 = jnp.bfloat16
I32 = jnp.int32

D_MODEL = 1024
HEAD_DIM = 64
N_Q_HEADS = 8
N_KV_HEADS = 2
Q_PER_KV = N_Q_HEADS // N_KV_HEADS
ATTN_WIDTH = N_Q_HEADS * HEAD_DIM
KV_WIDTH = N_KV_HEADS * HEAD_DIM
WINDOW = 128
ATT_BLOCK = 128
ROPE_THETA = 500000.0
ROPE_DIM = HEAD_DIM // 4
ROPE_HALF = ROPE_DIM // 2
SSM_WIDTH = 512
SSM_GROUP = 16
SSM_N_GROUPS = SSM_WIDTH // SSM_GROUP
SSM_STATE = 64
N_EXPERTS = 32
TOP_K = 4
SWIGLU_LIMIT = 7.0
SWIGLU_ALPHA = 1.702
EPS = 1e-5

LANES = 128
SUBLANES = 8
SSM_CHUNK = 32
TOEP = SSM_CHUNK * SSM_GROUP
MOE_ROWS = 512
ROUTE_TM = 512
COMBINE_TM = 128
VMEM_LIMIT = 52 * 1024 * 1024
EXPERT_VMEM_LIMIT = 60 * 1024 * 1024
NEG_BIG = -1e30


def _params(sem, vmem=VMEM_LIMIT):
    return pltpu.CompilerParams(dimension_semantics=sem, vmem_limit_bytes=vmem)


def _rms(x, g):
    ms = jnp.mean(x * x, axis=-1, keepdims=True)
    return x * lax.rsqrt(ms + EPS) * g


def _tile_rows_load(ref, rows):
    return jnp.concatenate([ref[pl.ds(j, rows, stride=SUBLANES), :] for j in range(SUBLANES)], axis=1)


def _tile_rows_store(ref, val):
    rows = val.shape[0]
    for j in range(SUBLANES):
        ref[pl.ds(j, rows, stride=SUBLANES), :] = val[:, j * LANES:(j + 1) * LANES]


def _tile_row(ref, row):
    return ref.at[pl.ds(pl.multiple_of(row * SUBLANES, SUBLANES), SUBLANES)]


KV_COLS = 4 * LANES


def _qkv_kernel(x_ref, g_ref, w_ref, c_ref, s1_ref, s2_ref, q_ref, kv_ref):
    h = _rms(x_ref[...], g_ref[...]).astype(BF16)
    p = jnp.dot(h, w_ref[...], preferred_element_type=F32)
    c = c_ref[...]
    s1 = s1_ref[...]
    s2 = s2_ref[...]

    def rot(t):
        return t * c + pltpu.roll(t, ROPE_HALF, 1) * s1 + pltpu.roll(t, LANES - ROPE_HALF, 1) * s2

    for j in range(ATTN_WIDTH // LANES):
        q_ref[:, j * LANES:(j + 1) * LANES] = (rot(p[:, j * LANES:(j + 1) * LANES]) * (HEAD_DIM ** -0.5)).astype(BF16)
    for j in range(N_KV_HEADS):
        col = ATTN_WIDTH + j * LANES
        kv_ref[:, j * LANES:(j + 1) * LANES] = rot(p[:, col:col + LANES]).astype(BF16)
    kv_ref[:, N_KV_HEADS * LANES:] = p[:, ATTN_WIDTH + N_KV_HEADS * LANES:].astype(BF16)


def _qkv(x2d, g, w, c, s1, s2, seq_len, tm=512):
    t = x2d.shape[0]
    nlb = seq_len // tm
    row = lambda i: (i, 0)
    tab = lambda i: (i % nlb, 0)
    full = lambda i: (0, 0)
    return pl.pallas_call(
        _qkv_kernel,
        grid=(t // tm,),
        in_specs=[pl.BlockSpec((tm, D_MODEL), row), pl.BlockSpec((1, D_MODEL), full),
                  pl.BlockSpec((D_MODEL, ATTN_WIDTH + KV_COLS), full),
                  pl.BlockSpec((tm, LANES), tab), pl.BlockSpec((tm, LANES), tab), pl.BlockSpec((tm, LANES), tab)],
        out_specs=[pl.BlockSpec((tm, ATTN_WIDTH), row), pl.BlockSpec((tm, KV_COLS), row)],
        out_shape=[jax.ShapeDtypeStruct((t, ATTN_WIDTH), BF16), jax.ShapeDtypeStruct((t, KV_COLS), BF16)],
        compiler_params=_params(("parallel",)),
        name="qkv_rotary",
    )(x2d, g, w, c, s1, s2)


def _attn_kernel(sink_ref, q_ref, kvp, kvc, kvn, g_ref, o_ref, *, bps):
    i = pl.program_id(0)
    first = (i % bps) == 0
    last = (i % bps) == bps - 1
    qi = lax.broadcasted_iota(I32, (ATT_BLOCK, 3 * ATT_BLOCK), 0)
    kj = lax.broadcasted_iota(I32, (ATT_BLOCK, 3 * ATT_BLOCK), 1)
    rel = kj - ATT_BLOCK - qi
    valid = (jnp.abs(rel) <= WINDOW)
    valid = valid & ((kj >= ATT_BLOCK) | jnp.logical_not(first))
    valid = valid & ((kj < 2 * ATT_BLOCK) | jnp.logical_not(last))
    kv = jnp.concatenate([kvp[...], kvc[...], kvn[...]], axis=0)
    ks = [kv[:, h * LANES:(h + 1) * LANES] for h in range(N_KV_HEADS)]
    vs = [kv[:, (N_KV_HEADS + h) * LANES:(N_KV_HEADS + h + 1) * LANES] for h in range(N_KV_HEADS)]
    lo = lax.broadcasted_iota(I32, (ATT_BLOCK, LANES), 1) < HEAD_DIM
    zero = jnp.zeros((ATT_BLOCK, LANES), BF16)
    heads = [(j, par) for j in range(ATTN_WIDTH // LANES) for par in range(2)]
    nt_dims = (((1,), (1,)), ((), ()))
    scores = []
    for j, par in heads:
        qt = q_ref[:, j * LANES:(j + 1) * LANES]
        qm = jnp.where(lo if par == 0 else jnp.logical_not(lo), qt, zero)
        scores.append(lax.dot_general(qm, ks[j // 2], nt_dims, preferred_element_type=F32))
    scores = [jnp.where(valid, s, NEG_BIG) for s in scores]
    sinks = [sink_ref[2 * j + par] for j, par in heads]
    maxes = [jnp.maximum(jnp.max(s, axis=-1, keepdims=True), sk) for s, sk in zip(scores, sinks)]
    probs = [jnp.exp(s - m) for s, m in zip(scores, maxes)]
    dens = [jnp.sum(p, axis=-1, keepdims=True) + jnp.exp(sk - m) for p, m, sk in zip(probs, maxes, sinks)]
    outs = [jnp.dot(p.astype(BF16), vs[j // 2], preferred_element_type=F32) for p, (j, par) in zip(probs, heads)]
    outs = [o / d for o, d in zip(outs, dens)]
    tiles = [jnp.where(lo, outs[2 * j], outs[2 * j + 1]) for j in range(ATTN_WIDTH // LANES)]
    o = jnp.concatenate(tiles, axis=1)
    o_ref[...] = _rms(o, g_ref[...]).astype(BF16)


def _attention(q, kv, sink, g, seq_len):
    t = q.shape[0]
    nblk = t // ATT_BLOCK
    bps = seq_len // ATT_BLOCK
    cur = lambda i, s: (i, 0)
    prv = lambda i, s: (jnp.maximum(i - 1, 0), 0)
    nxt = lambda i, s: (jnp.minimum(i + 1, nblk - 1), 0)
    grid_spec = pltpu.PrefetchScalarGridSpec(
        num_scalar_prefetch=1,
        grid=(nblk,),
        in_specs=[pl.BlockSpec((ATT_BLOCK, ATTN_WIDTH), cur),
                  pl.BlockSpec((ATT_BLOCK, KV_COLS), prv), pl.BlockSpec((ATT_BLOCK, KV_COLS), cur),
                  pl.BlockSpec((ATT_BLOCK, KV_COLS), nxt),
                  pl.BlockSpec((1, ATTN_WIDTH), lambda i, s: (0, 0))],
        out_specs=pl.BlockSpec((ATT_BLOCK, ATTN_WIDTH), cur),
    )
    return pl.pallas_call(
        functools.partial(_attn_kernel, bps=bps),
        grid_spec=grid_spec,
        out_shape=jax.ShapeDtypeStruct((t, ATTN_WIDTH), BF16),
        compiler_params=_params(("parallel",)),
        name="banded_attention",
    )(sink, q, kv, kv, kv, g)


def _uproj_kernel(x_ref, g_ref, w_ref, o_ref, stage_ref):
    lt = x_ref.shape[0]
    h = _rms(x_ref[...].reshape(lt * SUBLANES, D_MODEL), g_ref[...])
    for j in range(D_MODEL // LANES):
        stage_ref[j] = h[:, j * LANES:(j + 1) * LANES]
    for bl in range(SUBLANES):
        hb = jnp.concatenate([stage_ref[j, pl.ds(bl, lt, stride=SUBLANES), :] for j in range(D_MODEL // LANES)],
                             axis=1).astype(BF16)
        ut = lax.dot_general(w_ref[...], hb, (((1,), (1,)), ((), ())), preferred_element_type=F32)
        o_ref[bl] = ut.astype(BF16)


def _uproj(x2d, g, w_t, lt):
    t = x2d.shape[0]
    nc = t // SSM_CHUNK
    xv = x2d.reshape(nc, SSM_CHUNK // SUBLANES, SUBLANES, D_MODEL)
    return pl.pallas_call(
        _uproj_kernel,
        grid=(nc // lt, SSM_CHUNK // SUBLANES),
        in_specs=[pl.BlockSpec((lt, None, SUBLANES, D_MODEL), lambda i, b: (i, b, 0, 0)),
                  pl.BlockSpec((1, D_MODEL), lambda i, b: (0, 0)),
                  pl.BlockSpec((SSM_WIDTH, D_MODEL), lambda i, b: (0, 0))],
        out_specs=pl.BlockSpec((SUBLANES, SSM_WIDTH, lt), lambda i, b: (b, 0, i)),
        out_shape=jax.ShapeDtypeStruct((SSM_CHUNK, SSM_WIDTH, nc), BF16),
        scratch_shapes=[pltpu.VMEM((D_MODEL // LANES, lt * SUBLANES, LANES), F32)],
        compiler_params=_params(("parallel", "parallel")),
        name="u_projection",
    )(xv, g, w_t)


def _gelu_tanh(x):
    return 0.5 * x * (1.0 + jnp.tanh(math.sqrt(2.0 / math.pi) * (x + 0.044715 * (x * x * x))))


def _ssm_kernel(a_ref, wt_ref, ws_ref, wc_ref, pw_ref, y_ref, *, chunks_per_seq, nsteps):
    nc = a_ref.shape[2]
    a = a_ref[...].reshape(TOEP, nc)
    y = jnp.dot(wt_ref[0], a, preferred_element_type=F32)
    s = jnp.dot(ws_ref[0], a, preferred_element_type=F32)
    pos = lax.broadcasted_iota(I32, (SSM_STATE, nc), 1) % chunks_per_seq
    carries = []
    for d in range(2):
        hr = s[2 * d * SSM_STATE:(2 * d + 1) * SSM_STATE]
        hi = s[(2 * d + 1) * SSM_STATE:(2 * d + 2) * SSM_STATE]
        for k in range(nsteps):
            sh = 1 << k
            pr = pw_ref[0, 2 * d * SSM_STATE:(2 * d + 1) * SSM_STATE, k:k + 1]
            pi = pw_ref[0, (2 * d + 1) * SSM_STATE:(2 * d + 2) * SSM_STATE, k:k + 1]
            if d == 0:
                ok = pos >= sh
                sr = pltpu.roll(hr, sh, 1)
                si = pltpu.roll(hi, sh, 1)
            else:
                ok = pos < chunks_per_seq - sh
                sr = pltpu.roll(hr, nc - sh, 1)
                si = pltpu.roll(hi, nc - sh, 1)
            hr, hi = (hr + jnp.where(ok, pr * sr - pi * si, 0.0),
                      hi + jnp.where(ok, pr * si + pi * sr, 0.0))
        if d == 0:
            ok = pos >= 1
            cr = pltpu.roll(hr, 1, 1)
            ci = pltpu.roll(hi, 1, 1)
        else:
            ok = pos < chunks_per_seq - 1
            cr = pltpu.roll(hr, nc - 1, 1)
            ci = pltpu.roll(hi, nc - 1, 1)
        carries += [jnp.where(ok, cr, 0.0), jnp.where(ok, ci, 0.0)]
    carry = jnp.concatenate(carries, axis=0).astype(BF16)
    y = y + jnp.dot(wc_ref[0], carry, preferred_element_type=F32)
    y_ref[...] = _gelu_tanh(y).reshape(SSM_CHUNK, SSM_GROUP, nc)


def _ssm(ut, w_toep, w_state, w_carry, pw, chunks_per_seq):
    nc = ut.shape[2]
    nsteps = max(1, (chunks_per_seq - 1).bit_length())
    g3 = lambda g: (g, 0, 0)
    return pl.pallas_call(
        functools.partial(_ssm_kernel, chunks_per_seq=chunks_per_seq, nsteps=nsteps),
        grid=(SSM_N_GROUPS,),
        in_specs=[pl.BlockSpec((SSM_CHUNK, SSM_GROUP, nc), lambda g: (0, g, 0)),
                  pl.BlockSpec((1, TOEP, TOEP), g3),
                  pl.BlockSpec((1, 4 * SSM_STATE, TOEP), g3),
                  pl.BlockSpec((1, TOEP, 4 * SSM_STATE), g3),
                  pl.BlockSpec((1, 4 * SSM_STATE, pw.shape[2]), g3)],
        out_specs=pl.BlockSpec((SSM_CHUNK, SSM_GROUP, nc), lambda g: (0, g, 0)),
        out_shape=jax.ShapeDtypeStruct((SSM_CHUNK, SSM_WIDTH, nc), F32),
        compiler_params=_params(("parallel",)),
        name="s5_core",
    )(ut, w_toep, w_state, w_carry, pw)


def _glu_kernel(y_ref, w_ref, b_ref, g_ref, o_ref, stage_ref):
    lt = o_ref.shape[0]
    for bl in range(SUBLANES):
        y = y_ref[bl]
        z = jnp.dot(w_ref[...], y.astype(BF16), preferred_element_type=F32) + b_ref[...]
        s = y * jax.nn.sigmoid(z)
        ms = jnp.mean(s * s, axis=0, keepdims=True)
        sn = (s * lax.rsqrt(ms + EPS) * g_ref[...]).T
        for j in range(SSM_WIDTH // LANES):
            stage_ref[j, pl.ds(bl, lt, stride=SUBLANES), :] = sn[:, j * LANES:(j + 1) * LANES]
    for j in range(SSM_WIDTH // LANES):
        o_ref[:, :, j * LANES:(j + 1) * LANES] = stage_ref[j].reshape(lt, SUBLANES, LANES)


def _glu(yt, w_t, b_col, g_col, lt):
    nc = yt.shape[2]
    out = pl.pallas_call(
        _glu_kernel,
        grid=(nc // lt, SSM_CHUNK // SUBLANES),
        in_specs=[pl.BlockSpec((SUBLANES, SSM_WIDTH, lt), lambda i, b: (b, 0, i)),
                  pl.BlockSpec((SSM_WIDTH, SSM_WIDTH), lambda i, b: (0, 0)),
                  pl.BlockSpec((SSM_WIDTH, 1), lambda i, b: (0, 0)),
                  pl.BlockSpec((SSM_WIDTH, 1), lambda i, b: (0, 0))],
        out_specs=pl.BlockSpec((lt, None, SUBLANES, SSM_WIDTH), lambda i, b: (i, b, 0, 0)),
        out_shape=jax.ShapeDtypeStruct((nc, SSM_CHUNK // SUBLANES, SUBLANES, SSM_WIDTH), F32),
        scratch_shapes=[pltpu.VMEM((SSM_WIDTH // LANES, lt * SUBLANES, LANES), F32)],
        compiler_params=_params(("parallel", "parallel")),
        name="glu_norm",
    )(yt, w_t, b_col, g_col)
    return out.reshape(nc * SSM_CHUNK, SSM_WIDTH)


def _outproj_kernel(x_ref, a_ref, s_ref, wa_ref, ws_ref, g_ref, wr_ref, br_ref, tri_ref,
                    x2_ref, h2_ref, rt_ref, gt_ref, cnt_ref, run_ref):
    i = pl.program_id(0)

    @pl.when(i == 0)
    def _():
        run_ref[...] = jnp.zeros_like(run_ref)

    x2 = (x_ref[...] + jnp.dot(a_ref[...], wa_ref[...], preferred_element_type=F32)
          + jnp.dot(s_ref[...].astype(BF16), ws_ref[...], preferred_element_type=F32))
    x2_ref[...] = x2
    h2 = _rms(x2, g_ref[...])
    _tile_rows_store(h2_ref, h2)
    logits = lax.dot_general(wr_ref[...], h2.astype(BF16), (((1,), (1,)), ((), ())),
                             preferred_element_type=F32) + br_ref[...]
    tm = logits.shape[1]
    sub = lax.broadcasted_iota(I32, (N_EXPERTS, tm), 0)
    sub_f = sub.astype(F32)
    work = logits
    sel = jnp.zeros((N_EXPERTS, tm), F32)
    top_v, top_i = [], []
    for _ in range(TOP_K):
        m = jnp.max(work, axis=0, keepdims=True)
        idx = jnp.min(jnp.where(work == m, sub_f, float(N_EXPERTS)), axis=0, keepdims=True).astype(I32)
        hit = sub == idx
        sel = jnp.where(hit, 1.0, sel)
        work = jnp.where(hit, -jnp.inf, work)
        top_v.append(m)
        top_i.append(idx)
    ex = [jnp.exp(v - top_v[0]) for v in top_v]
    den = ex[0] + ex[1] + ex[2] + ex[3]
    before = jnp.dot(sel.astype(BF16), tri_ref[...], preferred_element_type=F32) + run_ref[:, 0:1]
    ranks = [jnp.sum(jnp.where(sub == top_i[k], before, 0.0), axis=0, keepdims=True).astype(I32) for k in range(TOP_K)]
    rt_ref[...] = jnp.concatenate(top_i + ranks, axis=0)
    gt_ref[...] = jnp.concatenate([e / den for e in ex] + [jnp.zeros((SUBLANES - TOP_K, tm), F32)], axis=0)
    run = run_ref[...] + jnp.sum(sel, axis=1, keepdims=True)
    run_ref[...] = run
    cnt_ref[...] = run


def _outproj(x2d, attn_n, ssm_n, w_a, w_s, g, w_r, b_r, tri):
    t = x2d.shape[0]
    tm = ROUTE_TM
    row = lambda i: (i, 0)
    full = lambda i: (0, 0)
    return pl.pallas_call(
        _outproj_kernel,
        grid=(t // tm,),
        in_specs=[pl.BlockSpec((tm, D_MODEL), row), pl.BlockSpec((tm, ATTN_WIDTH), row),
                  pl.BlockSpec((tm, SSM_WIDTH), row),
                  pl.BlockSpec((ATTN_WIDTH, D_MODEL), full), pl.BlockSpec((SSM_WIDTH, D_MODEL), full),
                  pl.BlockSpec((1, D_MODEL), full),
                  pl.BlockSpec((N_EXPERTS, D_MODEL), full), pl.BlockSpec((N_EXPERTS, 1), full),
                  pl.BlockSpec((tm, tm), full)],
        out_specs=[pl.BlockSpec((tm, D_MODEL), row), pl.BlockSpec((tm * SUBLANES, LANES), row),
                   pl.BlockSpec((2 * TOP_K, tm), lambda i: (0, i)), pl.BlockSpec((SUBLANES, tm), lambda i: (0, i)),
                   pl.BlockSpec((N_EXPERTS, LANES), full)],
        out_shape=[jax.ShapeDtypeStruct((t, D_MODEL), F32), jax.ShapeDtypeStruct((t * SUBLANES, LANES), F32),
                   jax.ShapeDtypeStruct((2 * TOP_K, t), I32), jax.ShapeDtypeStruct((SUBLANES, t), F32),
                   jax.ShapeDtypeStruct((N_EXPERTS, LANES), F32)],
        scratch_shapes=[pltpu.VMEM((N_EXPERTS, LANES), F32)],
        compiler_params=_params(("arbitrary",)),
        name="out_projection_router",
    )(x2d, attn_n, ssm_n, w_a, w_s, g, w_r, b_r, tri)


DMA_GROUP = 4


def _rows_per_step(rows, tokens):
    nt = rows.shape[1] // tokens
    return rows.reshape(TOP_K, nt, tokens).transpose(1, 0, 2).reshape(nt, 1, TOP_K * tokens)


def _scatter_kernel(pad_start_ref, pad_cnt_ref, rows_ref, h_ref, xb_hbm, zero_ref, stage_ref, sems, zsem, *, tm):
    i = pl.program_id(0)
    nt = pl.num_programs(0)
    n_dma = tm * TOP_K

    def tile_wait(slot):
        whole = xb_hbm.at[pl.ds(0, n_dma * SUBLANES)]
        pltpu.make_async_copy(whole, whole, sems.at[slot]).wait()

    @pl.when(i == 0)
    def _():
        zero_ref[...] = jnp.zeros_like(zero_ref)

        def fill(row, n_rows):
            return pltpu.make_async_copy(zero_ref.at[pl.ds(0, n_rows * SUBLANES)],
                                         xb_hbm.at[pl.ds(pl.multiple_of(row * SUBLANES, SUBLANES), n_rows * SUBLANES)],
                                         zsem)

        for phase in range(2):
            def per_expert(e, c):
                cnt = pad_cnt_ref[e]
                row = pad_start_ref[e]
                run = MOE_ROWS // 2
                while run >= 1:
                    below = cnt & ~(2 * run - 1)

                    @pl.when((cnt & run) != 0)
                    def _(run=run, below=below):
                        cp = fill(row + below, run)
                        cp.start() if phase == 0 else cp.wait()
                    run //= 2
                return c
            lax.fori_loop(0, N_EXPERTS, per_expert, 0)

            def per_tail_block(b, c):
                cp = fill(pad_start_ref[N_EXPERTS] + b * MOE_ROWS, MOE_ROWS)
                cp.start() if phase == 0 else cp.wait()
                return c
            lax.fori_loop(0, pad_cnt_ref[N_EXPERTS] // MOE_ROWS, per_tail_block, 0)

    for slot in range(2):
        @pl.when(i > 0)
        def _(slot=slot):
            tile_wait(slot)

        stage_ref[slot] = h_ref[pl.ds(slot * tm * SUBLANES, tm * SUBLANES), :]

        def issue(grp, c, slot=slot):
            r0 = grp * DMA_GROUP
            dst = [rows_ref[0, 0, k * (2 * tm) + slot * tm + r0 + r] for r in range(DMA_GROUP) for k in range(TOP_K)]
            for r in range(DMA_GROUP):
                src = _tile_row(stage_ref.at[slot], r0 + r)
                for k in range(TOP_K):
                    pltpu.make_async_copy(src, _tile_row(xb_hbm, dst[r * TOP_K + k]),
                                          sems.at[slot]).start(priority=k % 2)
            return c
        lax.fori_loop(0, tm // DMA_GROUP, issue, 0)

    @pl.when(i == nt - 1)
    def _():
        tile_wait(0)
        tile_wait(1)


def _scatter_rows(pad_start, pad_cnt, rows, h2, n_pad):
    tm = ROUTE_TM
    nt = h2.shape[0] // SUBLANES // (2 * tm)
    grid_spec = pltpu.PrefetchScalarGridSpec(
        num_scalar_prefetch=2,
        grid=(nt,),
        in_specs=[pl.BlockSpec((1, 1, 2 * tm * TOP_K), lambda i, a, b: (i, 0, 0), memory_space=pltpu.SMEM),
                  pl.BlockSpec((2 * tm * SUBLANES, LANES), lambda i, a, b: (i, 0))],
        out_specs=pl.BlockSpec(memory_space=pl.ANY),
        scratch_shapes=[pltpu.VMEM((MOE_ROWS * SUBLANES, LANES), F32),
                        pltpu.VMEM((2, tm * SUBLANES, LANES), F32),
                        pltpu.SemaphoreType.DMA((2,)), pltpu.SemaphoreType.DMA(())],
    )
    return pl.pallas_call(
        functools.partial(_scatter_kernel, tm=tm),
        grid_spec=grid_spec,
        out_shape=jax.ShapeDtypeStruct((n_pad * SUBLANES, LANES), F32),
        compiler_params=_params(("arbitrary",)),
        name="scatter_rows",
    )(pad_start, pad_cnt, _rows_per_step(rows, 2 * tm), h2)


def _expert_kernel(be_ref, grp_ref, ia_ref, ib_ref, nv_ref, xa_ref, xb_ref, wgu_ref, bgu_ref, wd_ref, bd_ref,
                   o_ref, x_scr, wgu_scr, wd_scr):
    i = pl.program_id(0)
    live = i < nv_ref[0]

    @pl.when(live & ((i == 0) | (be_ref[i] != be_ref[jnp.maximum(i - 1, 0)])))
    def _():
        wgu_scr[...] = wgu_ref[0].astype(BF16)
        wd_scr[...] = wd_ref[0].astype(BF16)

    @pl.when(live & (grp_ref[i] == 0))
    def _():
        x_scr[...] = _tile_rows_load(xa_ref, MOE_ROWS).astype(BF16)

    @pl.when(live & (grp_ref[i] != 0))
    def _():
        x_scr[...] = _tile_rows_load(xb_ref, MOE_ROWS).astype(BF16)

    @pl.when(live)
    def _():
        gu = jnp.dot(x_scr[...], wgu_scr[...], preferred_element_type=F32) + bgu_ref[0]
        gate = jnp.minimum(gu[:, :D_MODEL], SWIGLU_LIMIT)
        up = jnp.clip(gu[:, D_MODEL:], -SWIGLU_LIMIT, SWIGLU_LIMIT)
        act = (up + 1.0) * (gate * jax.nn.sigmoid(SWIGLU_ALPHA * gate))
        _tile_rows_store(o_ref, jnp.dot(act.astype(BF16), wd_scr[...], preferred_element_type=F32) + bd_ref[0])

    @pl.when(i >= nv_ref[0])
    def _():
        o_ref[...] = jnp.zeros_like(o_ref)


def _experts(block_e, block_grp, blk_a, blk_b, n_valid, xa, xb, wgu, bgu, wd, bd):
    nb = block_e.shape[0]
    e3 = lambda i, be, gr, ia, ib, nv: (be[i], 0, 0)
    grid_spec = pltpu.PrefetchScalarGridSpec(
        num_scalar_prefetch=5,
        grid=(nb,),
        in_specs=[pl.BlockSpec((MOE_ROWS * SUBLANES, LANES), lambda i, be, gr, ia, ib, nv: (ia[i], 0)),
                  pl.BlockSpec((MOE_ROWS * SUBLANES, LANES), lambda i, be, gr, ia, ib, nv: (ib[i], 0)),
                  pl.BlockSpec((1, D_MODEL, 2 * D_MODEL), e3), pl.BlockSpec((1, 1, 2 * D_MODEL), e3),
                  pl.BlockSpec((1, D_MODEL, D_MODEL), e3), pl.BlockSpec((1, 1, D_MODEL), e3)],
        out_specs=pl.BlockSpec((MOE_ROWS * SUBLANES, LANES), lambda i, be, gr, ia, ib, nv: (i, 0)),
        scratch_shapes=[pltpu.VMEM((MOE_ROWS, D_MODEL), BF16), pltpu.VMEM((D_MODEL, 2 * D_MODEL), BF16),
                        pltpu.VMEM((D_MODEL, D_MODEL), BF16)],
    )
    return pl.pallas_call(
        _expert_kernel,
        grid_spec=grid_spec,
        out_shape=jax.ShapeDtypeStruct((nb * MOE_ROWS * SUBLANES, LANES), F32),
        compiler_params=_params(("arbitrary",), vmem=EXPERT_VMEM_LIMIT),
        name="routed_experts",
    )(block_e, block_grp, blk_a, blk_b, n_valid, xa, xb, wgu, bgu, wd, bd)


def _combine_kernel(rows_cur, rows_nxt, x_ref, gt_ref, g_ref, y_hbm, o_ref, buf, sems, *, tm):
    i = pl.program_id(0)
    nt = pl.num_programs(0)

    def issue(rows_ref, half, slot):
        def body(grp, c):
            r0 = grp * DMA_GROUP
            src = [rows_ref[0, 0, k * (2 * tm) + half * tm + r0 + r] for r in range(DMA_GROUP) for k in range(TOP_K)]
            for r in range(DMA_GROUP):
                for k in range(TOP_K):
                    pltpu.make_async_copy(_tile_row(y_hbm, src[r * TOP_K + k]),
                                          _tile_row(buf.at[slot, k], r0 + r), sems.at[slot]).start(priority=k % 2)
            return c
        lax.fori_loop(0, tm // DMA_GROUP, body, 0)

    def finish(half):
        slot = half
        pltpu.make_async_copy(buf.at[slot], buf.at[slot], sems.at[slot]).wait()
        acc = x_ref[half * tm:(half + 1) * tm, :]
        g8 = gt_ref[:, half * tm:(half + 1) * tm]
        gt = jnp.concatenate([g8, jnp.zeros((tm - SUBLANES, tm), F32)], axis=0).T
        for k in range(TOP_K):
            acc = acc + _tile_rows_load(buf.at[slot, k], tm) * gt[:, k:k + 1]
        o_ref[half * tm:(half + 1) * tm, :] = _rms(acc, g_ref[...])

    @pl.when(i == 0)
    def _():
        issue(rows_cur, 0, 0)

    issue(rows_cur, 1, 1)
    finish(0)

    @pl.when(i + 1 < nt)
    def _():
        issue(rows_nxt, 0, 0)

    finish(1)


def _combine(rows, x2, gates, g, yb):
    t = x2.shape[0]
    tm = COMBINE_TM
    nt = t // (2 * tm)
    rows3 = _rows_per_step(rows, 2 * tm)
    row = lambda i: (i, 0)
    return pl.pallas_call(
        functools.partial(_combine_kernel, tm=tm),
        grid=(nt,),
        in_specs=[pl.BlockSpec((1, 1, 2 * tm * TOP_K), lambda i: (i, 0, 0), memory_space=pltpu.SMEM),
                  pl.BlockSpec((1, 1, 2 * tm * TOP_K), lambda i: (jnp.minimum(i + 1, nt - 1), 0, 0),
                               memory_space=pltpu.SMEM),
                  pl.BlockSpec((2 * tm, D_MODEL), row), pl.BlockSpec((SUBLANES, 2 * tm), lambda i: (0, i)),
                  pl.BlockSpec((1, D_MODEL), lambda i: (0, 0)),
                  pl.BlockSpec(memory_space=pl.ANY)],
        out_specs=pl.BlockSpec((2 * tm, D_MODEL), row),
        out_shape=jax.ShapeDtypeStruct((t, D_MODEL), F32),
        scratch_shapes=[pltpu.VMEM((2, TOP_K, tm * SUBLANES, LANES), F32), pltpu.SemaphoreType.DMA((2,))],
        compiler_params=_params(("arbitrary",)),
        name="combine_final_norm",
    )(rows3, rows3, x2, gates, g, yb)


def _rope_tables(seq_len):
    inv_freq = ROPE_THETA ** (-jnp.arange(ROPE_HALF, dtype=F32) * 2.0 / ROPE_DIM)
    ang = jnp.arange(seq_len, dtype=F32)[:, None] * inv_freq[None, :]
    cos, sin = jnp.cos(ang), jnp.sin(ang)
    pad = HEAD_DIM - ROPE_DIM
    ones = jnp.ones((seq_len, pad), F32)
    zer_h = jnp.zeros((seq_len, ROPE_HALF), F32)
    zer_p = jnp.zeros((seq_len, pad), F32)
    c = jnp.concatenate([cos, cos, ones], axis=1)
    s1 = jnp.concatenate([zer_h, sin, zer_p], axis=1)
    s2 = jnp.concatenate([-sin, zer_h, zer_p], axis=1)
    rep = LANES // HEAD_DIM
    return jnp.tile(c, (1, rep)), jnp.tile(s1, (1, rep)), jnp.tile(s2, (1, rep))


def _ssm_weights(a_re, a_im, log_dt, b_re, b_im, c_re, c_im, ssm_d, nsteps):
    r = SSM_CHUNK
    dt = jnp.exp(log_dt)[..., None]
    lr, li = a_re * dt, a_im * dt

    def cpow(n):
        n = jnp.asarray(n, F32)[..., None, None, None]
        mag = jnp.exp(n * lr)
        return mag * jnp.cos(n * li), mag * jnp.sin(n * li)

    ab_re, ab_im = cpow(jnp.ones(()))
    den = a_re * a_re + a_im * a_im
    num_re, num_im = ab_re - 1.0, ab_im
    f_re = (num_re * a_re + num_im * a_im) / den
    f_im = (num_im * a_re - num_re * a_im) / den
    bb_re = f_re[..., None] * b_re - f_im[..., None] * b_im
    bb_im = f_re[..., None] * b_im + f_im[..., None] * b_re

    taus = jnp.arange(r + 1, dtype=F32)
    p_re, p_im = cpow(taus)
    m_re = p_re[..., None] * bb_re - p_im[..., None] * bb_im
    m_im = p_re[..., None] * bb_im + p_im[..., None] * bb_re
    kern = (jnp.einsum('dgcp,tdgpk->tdgck', c_re, m_re) - jnp.einsum('dgcp,tdgpk->tdgck', c_im, m_im))
    center = kern[0, 0] + kern[0, 1] + jnp.eye(SSM_GROUP, dtype=F32) * ssm_d[:, :, None]
    lags = jnp.concatenate([kern[r - 1:0:-1, 0], center[None], kern[1:r, 1]], axis=0)
    cyc = jnp.tile(jnp.concatenate([lags, jnp.zeros_like(lags[:1])], axis=0), (r, 1, 1, 1))
    rows = cyc[:r * (2 * r - 1)].reshape((r, 2 * r - 1) + lags.shape[1:])[:, r - 1:]
    w_toep = rows.transpose(2, 0, 3, 1, 4).reshape(SSM_N_GROUPS, TOEP, TOEP)

    st_f_re, st_f_im = m_re[r - 1::-1, 0], m_im[r - 1::-1, 0]
    st_b_re, st_b_im = m_re[:r, 1], m_im[:r, 1]
    w_state = jnp.stack([st_f_re, st_f_im, st_b_re, st_b_im], axis=0)
    w_state = w_state.transpose(2, 0, 3, 1, 4).reshape(SSM_N_GROUPS, 4 * SSM_STATE, TOEP)

    def c_times_pow(d, pr, pi):
        zr = c_re[d][None] * pr[:, :, None, :] - c_im[d][None] * pi[:, :, None, :]
        zi = c_re[d][None] * pi[:, :, None, :] + c_im[d][None] * pr[:, :, None, :]
        return zr, -zi
    cf_re, cf_im = c_times_pow(0, p_re[1:r + 1, 0], p_im[1:r + 1, 0])
    cb_re, cb_im = c_times_pow(1, p_re[r:0:-1, 1], p_im[r:0:-1, 1])
    w_carry = jnp.stack([cf_re, cf_im, cb_re, cb_im], axis=3)
    w_carry = w_carry.transpose(1, 0, 2, 3, 4).reshape(SSM_N_GROUPS, TOEP, 4 * SSM_STATE)

    qr, qi = p_re[r], p_im[r]
    cols = []
    for _ in range(nsteps):
        cols.append(jnp.stack([qr[0], qi[0], qr[1], qi[1]], axis=1))
        qr, qi = qr * qr - qi * qi, 2.0 * qr * qi
    pw = jnp.stack(cols, axis=-1).reshape(SSM_N_GROUPS, 4 * SSM_STATE, nsteps)
    return w_toep.astype(BF16), w_state.astype(BF16), w_carry.astype(BF16), pw


def _front(x, prm):
    n, seq_len, _ = x.shape
    t = n * seq_len
    x2d = x.reshape(t, D_MODEL)
    q, kv = _qkv(x2d, prm['norm1_g'], prm['w_qkv'], *_rope_tables(seq_len), seq_len)
    attn_n = _attention(q, kv, prm['sink'], prm['attn_out_g'], seq_len)

    chunks_per_seq = seq_len // SSM_CHUNK
    nsteps = max(1, (chunks_per_seq - 1).bit_length())
    nc = t // SSM_CHUNK
    lt = min(256, nc)
    ut = _uproj(x2d, prm['norm1_g'], prm['w_u_t'], lt)
    yt = _ssm(ut, prm['w_toep'], prm['w_state'], prm['w_carry'], prm['pw'][:, :, :nsteps], chunks_per_seq)
    ssm_n = _glu(yt, prm['glu_w_t'], prm['glu_b'], prm['ssm_out_g'], lt)
    return _outproj(x2d, attn_n, ssm_n, prm['w_out_a'], prm['w_out_s'], prm['norm2_g'],
                    prm['router_w'], prm['router_b'], prm['tri'])


def _cumsum_small(x):
    n = x.shape[0]
    keep = jnp.arange(n)[None, :] <= jnp.arange(n)[:, None]
    return jnp.sum(jnp.where(keep, x[None, :], 0), axis=1)


def _by_expert(idx, table):
    hit = idx[None] == jnp.arange(N_EXPERTS, dtype=I32)[:, None, None]
    return jnp.sum(jnp.where(hit, table[:, None, None], 0), axis=0)


def kernel(x_prompt, x_sample, norm1_g, w_in, attn_sink, ssm_a_re, ssm_a_im, ssm_log_dt, ssm_b_re, ssm_b_im, ssm_c_re, ssm_c_im, ssm_d, glu_w, glu_b, attn_out_g, ssm_out_g, w_out, norm2_g, router_w, router_b, w_gate_up, b_gate_up, w_down, b_down, final_g):
    assert norm1_g.shape[0] == 1, "single-layer problem"
    l = 0
    xs = [x_prompt, x_sample]
    max_chunks = max(x.shape[1] for x in xs) // SSM_CHUNK
    max_steps = max(1, (max_chunks - 1).bit_length())
    wq, wk, wv, wu = jnp.split(w_in[l], [ATTN_WIDTH, ATTN_WIDTH + KV_WIDTH, ATTN_WIDTH + 2 * KV_WIDTH], axis=1)
    dup = lambda w: jnp.concatenate([w[:, :HEAD_DIM], w[:, :HEAD_DIM], w[:, HEAD_DIM:], w[:, HEAD_DIM:]], axis=1)
    w_toep, w_state, w_carry, pw = _ssm_weights(ssm_a_re[l], ssm_a_im[l], ssm_log_dt[l], ssm_b_re[l], ssm_b_im[l],
                                                ssm_c_re[l], ssm_c_im[l], ssm_d[l], max_steps)
    tri_i = lax.broadcasted_iota(I32, (ROUTE_TM, ROUTE_TM), 0)
    tri_j = lax.broadcasted_iota(I32, (ROUTE_TM, ROUTE_TM), 1)
    prm = dict(
        norm1_g=norm1_g[l].reshape(1, D_MODEL),
        w_qkv=jnp.concatenate([wq, dup(wk), dup(wv)], axis=1).astype(BF16),
        w_u_t=wu.T.astype(BF16),
        sink=attn_sink[l].astype(F32),
        attn_out_g=attn_out_g[l].reshape(1, ATTN_WIDTH),
        w_toep=w_toep, w_state=w_state, w_carry=w_carry, pw=pw,
        glu_w_t=glu_w[l].T.astype(BF16),
        glu_b=glu_b[l].reshape(SSM_WIDTH, 1),
        ssm_out_g=ssm_out_g[l].reshape(SSM_WIDTH, 1),
        w_out_a=w_out[l][:ATTN_WIDTH].astype(BF16),
        w_out_s=w_out[l][ATTN_WIDTH:].astype(BF16),
        norm2_g=norm2_g[l].reshape(1, D_MODEL),
        router_w=router_w[l].T.astype(BF16),
        router_b=router_b[l].reshape(N_EXPERTS, 1),
        tri=(tri_i < tri_j).astype(BF16),
    )
    fronts = [_front(x, prm) for x in xs]

    cnts = [f[4][:, 0].astype(I32) for f in fronts]
    padded = [(c + MOE_ROWS - 1) // MOE_ROWS * MOE_ROWS for c in cnts]
    pends = [_cumsum_small(p) for p in padded]
    pstarts = [pe - p for pe, p in zip(pends, padded)]
    nbs = [f[0].shape[0] * TOP_K // MOE_ROWS + N_EXPERTS for f in fronts]
    seg_blocks = jnp.stack([p // MOE_ROWS for p in padded], axis=1).reshape(-1)
    seg_end = _cumsum_small(seg_blocks)
    seg_start = seg_end - seg_blocks
    nb = sum(nbs)
    bi = jnp.arange(nb, dtype=I32)
    seg = jnp.minimum(jnp.sum(seg_end[None, :] <= bi[:, None], axis=1), 2 * N_EXPERTS - 1).astype(I32)
    block_e = seg // 2
    block_grp = seg % 2
    n_valid = seg_end[-1].astype(I32).reshape(1)
    live = bi < n_valid[0]
    src_start = jnp.stack([ps // MOE_ROWS for ps in pstarts], axis=1).reshape(-1)
    in_seg = seg[:, None] == jnp.arange(2 * N_EXPERTS, dtype=I32)[None, :]
    src_blk = jnp.sum(jnp.where(in_seg, (src_start - seg_start)[None, :], 0), axis=1) + bi
    not_after = bi[None, :] <= bi[:, None]
    blk = [jnp.max(jnp.where(not_after & (live & (block_grp == g))[None, :], src_blk[None, :], 0), axis=1).astype(I32)
           for g in range(2)]

    xbufs, yrows = [], []
    for g, f in enumerate(fronts):
        e_idx = f[2][:TOP_K]
        rank = f[2][TOP_K:]
        n_pad = nbs[g] * MOE_ROWS
        pad_start = jnp.concatenate([pstarts[g] + cnts[g], pends[g][-1:]]).astype(I32)
        pad_cnt = jnp.concatenate([padded[g] - cnts[g], n_pad - pends[g][-1:]]).astype(I32)
        xrows = (_by_expert(e_idx, pstarts[g]) + rank).astype(I32)
        xbufs.append(_scatter_rows(pad_start, pad_cnt, xrows, f[1], n_pad))
        yrows.append((_by_expert(e_idx, seg_start[g::2] * MOE_ROWS) + rank).astype(I32))
    yb = _experts(block_e, block_grp, blk[0], blk[1], n_valid, xbufs[0], xbufs[1],
                  w_gate_up.reshape(N_EXPERTS, D_MODEL, 2 * D_MODEL), b_gate_up[l][:, None, :],
                  w_down.reshape(N_EXPERTS, D_MODEL, D_MODEL), b_down[l][:, None, :])
    gfin = final_g.reshape(1, D_MODEL)
    outs = [_combine(r, f[0], f[3], gfin, yb).reshape(x.shape) for x, f, r in zip(xs, fronts, yrows)]
    return tuple(outs)
```

```python
import functools
import math

import jax
import jax.numpy as jnp
from jax import lax
from jax.experimental import pallas as pl
from jax.experimental.pallas import tpu as pltpu

F32 = jnp.float32
BF16 = jnp.bfloat16
I32 = jnp.int32

D_MODEL = 1024
HEAD_DIM = 64
N_Q_HEADS = 8
N_KV_HEADS = 2
Q_PER_KV = N_Q_HEADS // N_KV_HEADS
ATTN_WIDTH = N_Q_HEADS * HEAD_DIM
KV_WIDTH = N_KV_HEADS * HEAD_DIM
WINDOW = 128
ATT_BLOCK = 128
ROPE_THETA = 500000.0
ROPE_DIM = HEAD_DIM // 4
ROPE_HALF = ROPE_DIM // 2
SSM_WIDTH = 512
SSM_GROUP = 16
SSM_N_GROUPS = SSM_WIDTH // SSM_GROUP
SSM_STATE = 64
N_EXPERTS = 32
TOP_K = 4
SWIGLU_LIMIT = 7.0
SWIGLU_ALPHA = 1.702
EPS = 1e-5

LANES = 128
SUBLANES = 8
SSM_CHUNK = 32
TOEP = SSM_CHUNK * SSM_GROUP
MOE_ROWS = 512
ROUTE_TM = 512
COMBINE_TM = 128
VMEM_LIMIT = 52 * 1024 * 1024
EXPERT_VMEM_LIMIT = 60 * 1024 * 1024
NEG_BIG = -1e30


def _params(sem, vmem=VMEM_LIMIT):
    return pltpu.CompilerParams(dimension_semantics=sem, vmem_limit_bytes=vmem)


def _rms(x, g):
    ms = jnp.mean(x * x, axis=-1, keepdims=True)
    return x * lax.rsqrt(ms + EPS) * g


def _tile_rows_load(ref, rows):
    return jnp.concatenate([ref[pl.ds(j, rows, stride=SUBLANES), :] for j in range(SUBLANES)], axis=1)


def _tile_rows_store(ref, val):
    rows = val.shape[0]
    for j in range(SUBLANES):
        ref[pl.ds(j, rows, stride=SUBLANES), :] = val[:, j * LANES:(j + 1) * LANES]


def _tile_row(ref, row):
    return ref.at[pl.ds(pl.multiple_of(row * SUBLANES, SUBLANES), SUBLANES)]


KV_COLS = 4 * LANES


def _qkv_kernel(x_ref, g_ref, w_ref, c_ref, s1_ref, s2_ref, q_ref, kv_ref):
    h = _rms(x_ref[...], g_ref[...]).astype(BF16)
    p = jnp.dot(h, w_ref[...], preferred_element_type=F32)
    c = c_ref[...]
    s1 = s1_ref[...]
    s2 = s2_ref[...]

    def rot(t):
        return t * c + pltpu.roll(t, ROPE_HALF, 1) * s1 + pltpu.roll(t, LANES - ROPE_HALF, 1) * s2

    for j in range(ATTN_WIDTH // LANES):
        q_ref[:, j * LANES:(j + 1) * LANES] = (rot(p[:, j * LANES:(j + 1) * LANES]) * (HEAD_DIM ** -0.5)).astype(BF16)
    for j in range(N_KV_HEADS):
        col = ATTN_WIDTH + j * LANES
        kv_ref[:, j * LANES:(j + 1) * LANES] = rot(p[:, col:col + LANES]).astype(BF16)
    kv_ref[:, N_KV_HEADS * LANES:] = p[:, ATTN_WIDTH + N_KV_HEADS * LANES:].astype(BF16)


def _qkv(x2d, g, w, c, s1, s2, seq_len, tm=512):
    t = x2d.shape[0]
    nlb = seq_len // tm
    row = lambda i: (i, 0)
    tab = lambda i: (i % nlb, 0)
    full = lambda i: (0, 0)
    return pl.pallas_call(
        _qkv_kernel,
        grid=(t // tm,),
        in_specs=[pl.BlockSpec((tm, D_MODEL), row), pl.BlockSpec((1, D_MODEL), full),
                  pl.BlockSpec((D_MODEL, ATTN_WIDTH + KV_COLS), full),
                  pl.BlockSpec((tm, LANES), tab), pl.BlockSpec((tm, LANES), tab), pl.BlockSpec((tm, LANES), tab)],
        out_specs=[pl.BlockSpec((tm, ATTN_WIDTH), row), pl.BlockSpec((tm, KV_COLS), row)],
        out_shape=[jax.ShapeDtypeStruct((t, ATTN_WIDTH), BF16), jax.ShapeDtypeStruct((t, KV_COLS), BF16)],
        compiler_params=_params(("parallel",)),
        name="qkv_rotary",
    )(x2d, g, w, c, s1, s2)


def _attn_kernel(sink_ref, q_ref, kvp, kvc, kvn, g_ref, o_ref, *, bps):
    i = pl.program_id(0)
    first = (i % bps) == 0
    last = (i % bps) == bps - 1
    qi = lax.broadcasted_iota(I32, (ATT_BLOCK, 3 * ATT_BLOCK), 0)
    kj = lax.broadcasted_iota(I32, (ATT_BLOCK, 3 * ATT_BLOCK), 1)
    rel = kj - ATT_BLOCK - qi
    valid = (jnp.abs(rel) <= WINDOW)
    valid = valid & ((kj >= ATT_BLOCK) | jnp.logical_not(first))
    valid = valid & ((kj < 2 * ATT_BLOCK) | jnp.logical_not(last))
    kv = jnp.concatenate([kvp[...], kvc[...], kvn[...]], axis=0)
    ks = [kv[:, h * LANES:(h + 1) * LANES] for h in range(N_KV_HEADS)]
    vs = [kv[:, (N_KV_HEADS + h) * LANES:(N_KV_HEADS + h + 1) * LANES] for h in range(N_KV_HEADS)]
    lo = lax.broadcasted_iota(I32, (ATT_BLOCK, LANES), 1) < HEAD_DIM
    zero = jnp.zeros((ATT_BLOCK, LANES), BF16)
    heads = [(j, par) for j in range(ATTN_WIDTH // LANES) for par in range(2)]
    nt_dims = (((1,), (1,)), ((), ()))
    scores = []
    for j, par in heads:
        qt = q_ref[:, j * LANES:(j + 1) * LANES]
        qm = jnp.where(lo if par == 0 else jnp.logical_not(lo), qt, zero)
        scores.append(lax.dot_general(qm, ks[j // 2], nt_dims, preferred_element_type=F32))
    scores = [jnp.where(valid, s, NEG_BIG) for s in scores]
    sinks = [sink_ref[2 * j + par] for j, par in heads]
    maxes = [jnp.maximum(jnp.max(s, axis=-1, keepdims=True), sk) for s, sk in zip(scores, sinks)]
    probs = [jnp.exp(s - m) for s, m in zip(scores, maxes)]
    dens = [jnp.sum(p, axis=-1, keepdims=True) + jnp.exp(sk - m) for p, m, sk in zip(probs, maxes, sinks)]
    outs = [jnp.dot(p.astype(BF16), vs[j // 2], preferred_element_type=F32) for p, (j, par) in zip(probs, heads)]
    outs = [o / d for o, d in zip(outs, dens)]
    tiles = [jnp.where(lo, outs[2 * j], outs[2 * j + 1]) for j in range(ATTN_WIDTH // LANES)]
    o = jnp.concatenate(tiles, axis=1)
    o_ref[...] = _rms(o, g_ref[...]).astype(BF16)


def _attention(q, kv, sink, g, seq_len):
    t = q.shape[0]
    nblk = t // ATT_BLOCK
    bps = seq_len // ATT_BLOCK
    cur = lambda i, s: (i, 0)
    prv = lambda i, s: (jnp.maximum(i - 1, 0), 0)
    nxt = lambda i, s: (jnp.minimum(i + 1, nblk - 1), 0)
    grid_spec = pltpu.PrefetchScalarGridSpec(
        num_scalar_prefetch=1,
        grid=(nblk,),
        in_specs=[pl.BlockSpec((ATT_BLOCK, ATTN_WIDTH), cur),
                  pl.BlockSpec((ATT_BLOCK, KV_COLS), prv), pl.BlockSpec((ATT_BLOCK, KV_COLS), cur),
                  pl.BlockSpec((ATT_BLOCK, KV_COLS), nxt),
                  pl.BlockSpec((1, ATTN_WIDTH), lambda i, s: (0, 0))],
        out_specs=pl.BlockSpec((ATT_BLOCK, ATTN_WIDTH), cur),
    )
    return pl.pallas_call(
        functools.partial(_attn_kernel, bps=bps),
        grid_spec=grid_spec,
        out_shape=jax.ShapeDtypeStruct((t, ATTN_WIDTH), BF16),
        compiler_params=_params(("parallel",)),
        name="banded_attention",
    )(sink, q, kv, kv, kv, g)


def _uproj_kernel(x_ref, g_ref, w_ref, o_ref, stage_ref):
    lt = x_ref.shape[0]
    h = _rms(x_ref[...].reshape(lt * SUBLANES, D_MODEL), g_ref[...])
    for j in range(D_MODEL // LANES):
        stage_ref[j] = h[:, j * LANES:(j + 1) * LANES]
    for bl in range(SUBLANES):
        hb = jnp.concatenate([stage_ref[j, pl.ds(bl, lt, stride=SUBLANES), :] for j in range(D_MODEL // LANES)],
                             axis=1).astype(BF16)
        ut = lax.dot_general(w_ref[...], hb, (((1,), (1,)), ((), ())), preferred_element_type=F32)
        o_ref[bl] = ut.astype(BF16)


def _uproj(x2d, g, w_t, lt):
    t = x2d.shape[0]
    nc = t // SSM_CHUNK
    xv = x2d.reshape(nc, SSM_CHUNK // SUBLANES, SUBLANES, D_MODEL)
    return pl.pallas_call(
        _uproj_kernel,
        grid=(nc // lt, SSM_CHUNK // SUBLANES),
        in_specs=[pl.BlockSpec((lt, None, SUBLANES, D_MODEL), lambda i, b: (i, b, 0, 0)),
                  pl.BlockSpec((1, D_MODEL), lambda i, b: (0, 0)),
                  pl.BlockSpec((SSM_WIDTH, D_MODEL), lambda i, b: (0, 0))],
        out_specs=pl.BlockSpec((SUBLANES, SSM_WIDTH, lt), lambda i, b: (b, 0, i)),
        out_shape=jax.ShapeDtypeStruct((SSM_CHUNK, SSM_WIDTH, nc), BF16),
        scratch_shapes=[pltpu.VMEM((D_MODEL // LANES, lt * SUBLANES, LANES), F32)],
        compiler_params=_params(("parallel", "parallel")),
        name="u_projection",
    )(xv, g, w_t)


def _gelu_tanh(x):
    return 0.5 * x * (1.0 + jnp.tanh(math.sqrt(2.0 / math.pi) * (x + 0.044715 * (x * x * x))))


def _ssm_kernel(a_ref, wt_ref, ws_ref, wc_ref, pw_ref, y_ref, *, chunks_per_seq, nsteps):
    nc = a_ref.shape[2]
    a = a_ref[...].reshape(TOEP, nc)
    y = jnp.dot(wt_ref[0], a, preferred_element_type=F32)
    s = jnp.dot(ws_ref[0], a, preferred_element_type=F32)
    pos = lax.broadcasted_iota(I32, (SSM_STATE, nc), 1) % chunks_per_seq
    carries = []
    for d in range(2):
        hr = s[2 * d * SSM_STATE:(2 * d + 1) * SSM_STATE]
        hi = s[(2 * d + 1) * SSM_STATE:(2 * d + 2) * SSM_STATE]
        for k in range(nsteps):
            sh = 1 << k
            pr = pw_ref[0, 2 * d * SSM_STATE:(2 * d + 1) * SSM_STATE, k:k + 1]
            pi = pw_ref[0, (2 * d + 1) * SSM_STATE:(2 * d + 2) * SSM_STATE, k:k + 1]
            if d == 0:
                ok = pos >= sh
                sr = pltpu.roll(hr, sh, 1)
                si = pltpu.roll(hi, sh, 1)
            else:
                ok = pos < chunks_per_seq - sh
                sr = pltpu.roll(hr, nc - sh, 1)
                si = pltpu.roll(hi, nc - sh, 1)
            hr, hi = (hr + jnp.where(ok, pr * sr - pi * si, 0.0),
                      hi + jnp.where(ok, pr * si + pi * sr, 0.0))
        if d == 0:
            ok = pos >= 1
            cr = pltpu.roll(hr, 1, 1)
            ci = pltpu.roll(hi, 1, 1)
        else:
            ok = pos < chunks_per_seq - 1
            cr = pltpu.roll(hr, nc - 1, 1)
            ci = pltpu.roll(hi, nc - 1, 1)
        carries += [jnp.where(ok, cr, 0.0), jnp.where(ok, ci, 0.0)]
    carry = jnp.concatenate(carries, axis=0).astype(BF16)
    y = y + jnp.dot(wc_ref[0], carry, preferred_element_type=F32)
    y_ref[...] = _gelu_tanh(y).reshape(SSM_CHUNK, SSM_GROUP, nc)


def _ssm(ut, w_toep, w_state, w_carry, pw, chunks_per_seq):
    nc = ut.shape[2]
    nsteps = max(1, (chunks_per_seq - 1).bit_length())
    g3 = lambda g: (g, 0, 0)
    return pl.pallas_call(
        functools.partial(_ssm_kernel, chunks_per_seq=chunks_per_seq, nsteps=nsteps),
        grid=(SSM_N_GROUPS,),
        in_specs=[pl.BlockSpec((SSM_CHUNK, SSM_GROUP, nc), lambda g: (0, g, 0)),
                  pl.BlockSpec((1, TOEP, TOEP), g3),
                  pl.BlockSpec((1, 4 * SSM_STATE, TOEP), g3),
                  pl.BlockSpec((1, TOEP, 4 * SSM_STATE), g3),
                  pl.BlockSpec((1, 4 * SSM_STATE, pw.shape[2]), g3)],
        out_specs=pl.BlockSpec((SSM_CHUNK, SSM_GROUP, nc), lambda g: (0, g, 0)),
        out_shape=jax.ShapeDtypeStruct((SSM_CHUNK, SSM_WIDTH, nc), F32),
        compiler_params=_params(("parallel",)),
        name="s5_core",
    )(ut, w_toep, w_state, w_carry, pw)


def _glu_kernel(y_ref, w_ref, b_ref, g_ref, o_ref, stage_ref):
    lt = o_ref.shape[0]
    for bl in range(SUBLANES):
        y = y_ref[bl]
        z = jnp.dot(w_ref[...], y.astype(BF16), preferred_element_type=F32) + b_ref[...]
        s = y * jax.nn.sigmoid(z)
        ms = jnp.mean(s * s, axis=0, keepdims=True)
        sn = (s * lax.rsqrt(ms + EPS) * g_ref[...]).T
        for j in range(SSM_WIDTH // LANES):
            stage_ref[j, pl.ds(bl, lt, stride=SUBLANES), :] = sn[:, j * LANES:(j + 1) * LANES]
    for j in range(SSM_WIDTH // LANES):
        o_ref[:, :, j * LANES:(j + 1) * LANES] = stage_ref[j].reshape(lt, SUBLANES, LANES)


def _glu(yt, w_t, b_col, g_col, lt):
    nc = yt.shape[2]
    out = pl.pallas_call(
        _glu_kernel,
        grid=(nc // lt, SSM_CHUNK // SUBLANES),
        in_specs=[pl.BlockSpec((SUBLANES, SSM_WIDTH, lt), lambda i, b: (b, 0, i)),
                  pl.BlockSpec((SSM_WIDTH, SSM_WIDTH), lambda i, b: (0, 0)),
                  pl.BlockSpec((SSM_WIDTH, 1), lambda i, b: (0, 0)),
                  pl.BlockSpec((SSM_WIDTH, 1), lambda i, b: (0, 0))],
        out_specs=pl.BlockSpec((lt, None, SUBLANES, SSM_WIDTH), lambda i, b: (i, b, 0, 0)),
        out_shape=jax.ShapeDtypeStruct((nc, SSM_CHUNK // SUBLANES, SUBLANES, SSM_WIDTH), F32),
        scratch_shapes=[pltpu.VMEM((SSM_WIDTH // LANES, lt * SUBLANES, LANES), F32)],
        compiler_params=_params(("parallel", "parallel")),
        name="glu_norm",
    )(yt, w_t, b_col, g_col)
    return out.reshape(nc * SSM_CHUNK, SSM_WIDTH)


def _outproj_kernel(x_ref, a_ref, s_ref, wa_ref, ws_ref, g_ref, wr_ref, br_ref, tri_ref,
                    x2_ref, h2_ref, rt_ref, gt_ref, cnt_ref, run_ref):
    i = pl.program_id(0)

    @pl.when(i == 0)
    def _():
        run_ref[...] = jnp.zeros_like(run_ref)

    x2 = (x_ref[...] + jnp.dot(a_ref[...], wa_ref[...], preferred_element_type=F32)
          + jnp.dot(s_ref[...].astype(BF16), ws_ref[...], preferred_element_type=F32))
    x2_ref[...] = x2
    h2 = _rms(x2, g_ref[...])
    _tile_rows_store(h2_ref, h2)
    logits = lax.dot_general(wr_ref[...], h2.astype(BF16), (((1,), (1,)), ((), ())),
                             preferred_element_type=F32) + br_ref[...]
    tm = logits.shape[1]
    sub = lax.broadcasted_iota(I32, (N_EXPERTS, tm), 0)
    sub_f = sub.astype(F32)
    work = logits
    sel = jnp.zeros((N_EXPERTS, tm), F32)
    top_v, top_i = [], []
    for _ in range(TOP_K):
        m = jnp.max(work, axis=0, keepdims=True)
        idx = jnp.min(jnp.where(work == m, sub_f, float(N_EXPERTS)), axis=0, keepdims=True).astype(I32)
        hit = sub == idx
        sel = jnp.where(hit, 1.0, sel)
        work = jnp.where(hit, -jnp.inf, work)
        top_v.append(m)
        top_i.append(idx)
    ex = [jnp.exp(v - top_v[0]) for v in top_v]
    den = ex[0] + ex[1] + ex[2] + ex[3]
    before = jnp.dot(sel.astype(BF16), tri_ref[...], preferred_element_type=F32) + run_ref[:, 0:1]
    ranks = [jnp.sum(jnp.where(sub == top_i[k], before, 0.0), axis=0, keepdims=True).astype(I32) for k in range(TOP_K)]
    rt_ref[...] = jnp.concatenate(top_i + ranks, axis=0)
    gt_ref[...] = jnp.concatenate([e / den for e in ex] + [jnp.zeros((SUBLANES - TOP_K, tm), F32)], axis=0)
    run = run_ref[...] + jnp.sum(sel, axis=1, keepdims=True)
    run_ref[...] = run
    cnt_ref[...] = run


def _outproj(x2d, attn_n, ssm_n, w_a, w_s, g, w_r, b_r, tri):
    t = x2d.shape[0]
    tm = ROUTE_TM
    row = lambda i: (i, 0)
    full = lambda i: (0, 0)
    return pl.pallas_call(
        _outproj_kernel,
        grid=(t // tm,),
        in_specs=[pl.BlockSpec((tm, D_MODEL), row), pl.BlockSpec((tm, ATTN_WIDTH), row),
                  pl.BlockSpec((tm, SSM_WIDTH), row),
                  pl.BlockSpec((ATTN_WIDTH, D_MODEL), full), pl.BlockSpec((SSM_WIDTH, D_MODEL), full),
                  pl.BlockSpec((1, D_MODEL), full),
                  pl.BlockSpec((N_EXPERTS, D_MODEL), full), pl.BlockSpec((N_EXPERTS, 1), full),
                  pl.BlockSpec((tm, tm), full)],
        out_specs=[pl.BlockSpec((tm, D_MODEL), row), pl.BlockSpec((tm * SUBLANES, LANES), row),
                   pl.BlockSpec((2 * TOP_K, tm), lambda i: (0, i)), pl.BlockSpec((SUBLANES, tm), lambda i: (0, i)),
                   pl.BlockSpec((N_EXPERTS, LANES), full)],
        out_shape=[jax.ShapeDtypeStruct((t, D_MODEL), F32), jax.ShapeDtypeStruct((t * SUBLANES, LANES), F32),
                   jax.ShapeDtypeStruct((2 * TOP_K, t), I32), jax.ShapeDtypeStruct((SUBLANES, t), F32),
                   jax.ShapeDtypeStruct((N_EXPERTS, LANES), F32)],
        scratch_shapes=[pltpu.VMEM((N_EXPERTS, LANES), F32)],
        compiler_params=_params(("arbitrary",)),
        name="out_projection_router",
    )(x2d, attn_n, ssm_n, w_a, w_s, g, w_r, b_r, tri)


DMA_GROUP = 4


def _rows_per_step(rows, tokens):
    nt = rows.shape[1] // tokens
    return rows.reshape(TOP_K, nt, tokens).transpose(1, 0, 2).reshape(nt, 1, TOP_K * tokens)


def _scatter_kernel(pad_start_ref, pad_cnt_ref, rows_ref, h_ref, xb_hbm, zero_ref, stage_ref, sems, zsem, *, tm):
    i = pl.program_id(0)
    nt = pl.num_programs(0)
    n_dma = tm * TOP_K

    def tile_wait(slot):
        whole = xb_hbm.at[pl.ds(0, n_dma * SUBLANES)]
        pltpu.make_async_copy(whole, whole, sems.at[slot]).wait()

    @pl.when(i == 0)
    def _():
        zero_ref[...] = jnp.zeros_like(zero_ref)

        def fill(row, n_rows):
            return pltpu.make_async_copy(zero_ref.at[pl.ds(0, n_rows * SUBLANES)],
                                         xb_hbm.at[pl.ds(pl.multiple_of(row * SUBLANES, SUBLANES), n_rows * SUBLANES)],
                                         zsem)

        for phase in range(2):
            def per_expert(e, c):
                cnt = pad_cnt_ref[e]
                row = pad_start_ref[e]
                run = MOE_ROWS // 2
                while run >= 1:
                    below = cnt & ~(2 * run - 1)

                    @pl.when((cnt & run) != 0)
                    def _(run=run, below=below):
                        cp = fill(row + below, run)
                        cp.start() if phase == 0 else cp.wait()
                    run //= 2
                return c
            lax.fori_loop(0, N_EXPERTS, per_expert, 0)

            def per_tail_block(b, c):
                cp = fill(pad_start_ref[N_EXPERTS] + b * MOE_ROWS, MOE_ROWS)
                cp.start() if phase == 0 else cp.wait()
                return c
            lax.fori_loop(0, pad_cnt_ref[N_EXPERTS] // MOE_ROWS, per_tail_block, 0)

    for slot in range(2):
        @pl.when(i > 0)
        def _(slot=slot):
            tile_wait(slot)

        stage_ref[slot] = h_ref[pl.ds(slot * tm * SUBLANES, tm * SUBLANES), :]

        def issue(grp, c, slot=slot):
            r0 = grp * DMA_GROUP
            dst = [rows_ref[0, 0, k * (2 * tm) + slot * tm + r0 + r] for r in range(DMA_GROUP) for k in range(TOP_K)]
            for r in range(DMA_GROUP):
                src = _tile_row(stage_ref.at[slot], r0 + r)
                for k in range(TOP_K):
                    pltpu.make_async_copy(src, _tile_row(xb_hbm, dst[r * TOP_K + k]),
                                          sems.at[slot]).start(priority=k % 2)
            return c
        lax.fori_loop(0, tm // DMA_GROUP, issue, 0)

    @pl.when(i == nt - 1)
    def _():
        tile_wait(0)
        tile_wait(1)


def _scatter_rows(pad_start, pad_cnt, rows, h2, n_pad):
    tm = ROUTE_TM
    nt = h2.shape[0] // SUBLANES // (2 * tm)
    grid_spec = pltpu.PrefetchScalarGridSpec(
        num_scalar_prefetch=2,
        grid=(nt,),
        in_specs=[pl.BlockSpec((1, 1, 2 * tm * TOP_K), lambda i, a, b: (i, 0, 0), memory_space=pltpu.SMEM),
                  pl.BlockSpec((2 * tm * SUBLANES, LANES), lambda i, a, b: (i, 0))],
        out_specs=pl.BlockSpec(memory_space=pl.ANY),
        scratch_shapes=[pltpu.VMEM((MOE_ROWS * SUBLANES, LANES), F32),
                        pltpu.VMEM((2, tm * SUBLANES, LANES), F32),
                        pltpu.SemaphoreType.DMA((2,)), pltpu.SemaphoreType.DMA(())],
    )
    return pl.pallas_call(
        functools.partial(_scatter_kernel, tm=tm),
        grid_spec=grid_spec,
        out_shape=jax.ShapeDtypeStruct((n_pad * SUBLANES, LANES), F32),
        compiler_params=_params(("arbitrary",)),
        name="scatter_rows",
    )(pad_start, pad_cnt, _rows_per_step(rows, 2 * tm), h2)


def _expert_kernel(be_ref, grp_ref, ia_ref, ib_ref, nv_ref, xa_ref, xb_ref, wgu_ref, bgu_ref, wd_ref, bd_ref,
                   o_ref, x_scr, wgu_scr, wd_scr):
    i = pl.program_id(0)
    live = i < nv_ref[0]

    @pl.when(live & ((i == 0) | (be_ref[i] != be_ref[jnp.maximum(i - 1, 0)])))
    def _():
        wgu_scr[...] = wgu_ref[0].astype(BF16)
        wd_scr[...] = wd_ref[0].astype(BF16)

    @pl.when(live & (grp_ref[i] == 0))
    def _():
        x_scr[...] = _tile_rows_load(xa_ref, MOE_ROWS).astype(BF16)

    @pl.when(live & (grp_ref[i] != 0))
    def _():
        x_scr[...] = _tile_rows_load(xb_ref, MOE_ROWS).astype(BF16)

    @pl.when(live)
    def _():
        gu = jnp.dot(x_scr[...], wgu_scr[...], preferred_element_type=F32) + bgu_ref[0]
        gate = jnp.minimum(gu[:, :D_MODEL], SWIGLU_LIMIT)
        up = jnp.clip(gu[:, D_MODEL:], -SWIGLU_LIMIT, SWIGLU_LIMIT)
        act = (up + 1.0) * (gate * jax.nn.sigmoid(SWIGLU_ALPHA * gate))
        _tile_rows_store(o_ref, jnp.dot(act.astype(BF16), wd_scr[...], preferred_element_type=F32) + bd_ref[0])

    @pl.when(i >= nv_ref[0])
    def _():
        o_ref[...] = jnp.zeros_like(o_ref)


def _experts(block_e, block_grp, blk_a, blk_b, n_valid, xa, xb, wgu, bgu, wd, bd):
    nb = block_e.shape[0]
    e3 = lambda i, be, gr, ia, ib, nv: (be[i], 0, 0)
    grid_spec = pltpu.PrefetchScalarGridSpec(
        num_scalar_prefetch=5,
        grid=(nb,),
        in_specs=[pl.BlockSpec((MOE_ROWS * SUBLANES, LANES), lambda i, be, gr, ia, ib, nv: (ia[i], 0)),
                  pl.BlockSpec((MOE_ROWS * SUBLANES, LANES), lambda i, be, gr, ia, ib, nv: (ib[i], 0)),
                  pl.BlockSpec((1, D_MODEL, 2 * D_MODEL), e3), pl.BlockSpec((1, 1, 2 * D_MODEL), e3),
                  pl.BlockSpec((1, D_MODEL, D_MODEL), e3), pl.BlockSpec((1, 1, D_MODEL), e3)],
        out_specs=pl.BlockSpec((MOE_ROWS * SUBLANES, LANES), lambda i, be, gr, ia, ib, nv: (i, 0)),
        scratch_shapes=[pltpu.VMEM((MOE_ROWS, D_MODEL), BF16), pltpu.VMEM((D_MODEL, 2 * D_MODEL), BF16),
                        pltpu.VMEM((D_MODEL, D_MODEL), BF16)],
    )
    return pl.pallas_call(
        _expert_kernel,
        grid_spec=grid_spec,
        out_shape=jax.ShapeDtypeStruct((nb * MOE_ROWS * SUBLANES, LANES), F32),
        compiler_params=_params(("arbitrary",), vmem=EXPERT_VMEM_LIMIT),
        name="routed_experts",
    )(block_e, block_grp, blk_a, blk_b, n_valid, xa, xb, wgu, bgu, wd, bd)


GATHER_PITCH = 12


def _combine_kernel(rows_cur, rows_nxt, x_ref, gt_ref, g_ref, y_hbm, o_ref, buf, sems, *, tm):
    i = pl.program_id(0)
    nt = pl.num_programs(0)

    def issue(rows_ref, half, slot):
        def body(grp, c):
            r0 = grp * DMA_GROUP
            src = [rows_ref[0, 0, k * (2 * tm) + half * tm + r0 + r] for r in range(DMA_GROUP) for k in range(TOP_K)]
            for r in range(DMA_GROUP):
                for k in range(TOP_K):
                    dst = buf.at[slot, k].at[pl.ds(pl.multiple_of((r0 + r) * GATHER_PITCH, 4), SUBLANES)]
                    pltpu.make_async_copy(_tile_row(y_hbm, src[r * TOP_K + k]), dst,
                                          sems.at[slot]).start(priority=k % 2)
            return c
        lax.fori_loop(0, tm // DMA_GROUP, body, 0)

    def finish(half):
        slot = half
        whole = y_hbm.at[pl.ds(0, TOP_K * tm * SUBLANES)]
        pltpu.make_async_copy(whole, whole, sems.at[slot]).wait()
        acc = x_ref[half * tm:(half + 1) * tm, :]
        g8 = gt_ref[:, half * tm:(half + 1) * tm]
        gt = jnp.concatenate([g8, jnp.zeros((tm - SUBLANES, tm), F32)], axis=0).T
        for k in range(TOP_K):
            yk = jnp.concatenate([buf[slot, k, pl.ds(j, tm, stride=GATHER_PITCH), :] for j in range(SUBLANES)], axis=1)
            acc = acc + yk * gt[:, k:k + 1]
        o_ref[half * tm:(half + 1) * tm, :] = _rms(acc, g_ref[...])

    @pl.when(i == 0)
    def _():
        issue(rows_cur, 0, 0)

    issue(rows_cur, 1, 1)
    finish(0)

    @pl.when(i + 1 < nt)
    def _():
        issue(rows_nxt, 0, 0)

    finish(1)


def _combine(rows, x2, gates, g, yb):
    t = x2.shape[0]
    tm = COMBINE_TM
    nt = t // (2 * tm)
    rows3 = _rows_per_step(rows, 2 * tm)
    row = lambda i: (i, 0)
    return pl.pallas_call(
        functools.partial(_combine_kernel, tm=tm),
        grid=(nt,),
        in_specs=[pl.BlockSpec((1, 1, 2 * tm * TOP_K), lambda i: (i, 0, 0), memory_space=pltpu.SMEM),
                  pl.BlockSpec((1, 1, 2 * tm * TOP_K), lambda i: (jnp.minimum(i + 1, nt - 1), 0, 0),
                               memory_space=pltpu.SMEM),
                  pl.BlockSpec((2 * tm, D_MODEL), row), pl.BlockSpec((SUBLANES, 2 * tm), lambda i: (0, i)),
                  pl.BlockSpec((1, D_MODEL), lambda i: (0, 0)),
                  pl.BlockSpec(memory_space=pl.ANY)],
        out_specs=pl.BlockSpec((2 * tm, D_MODEL), row),
        out_shape=jax.ShapeDtypeStruct((t, D_MODEL), F32),
        scratch_shapes=[pltpu.VMEM((2, TOP_K, tm * GATHER_PITCH, LANES), F32), pltpu.SemaphoreType.DMA((2,))],
        compiler_params=_params(("arbitrary",)),
        name="combine_final_norm",
    )(rows3, rows3, x2, gates, g, yb)


def _rope_tables(seq_len):
    inv_freq = ROPE_THETA ** (-jnp.arange(ROPE_HALF, dtype=F32) * 2.0 / ROPE_DIM)
    ang = jnp.arange(seq_len, dtype=F32)[:, None] * inv_freq[None, :]
    cos, sin = jnp.cos(ang), jnp.sin(ang)
    pad = HEAD_DIM - ROPE_DIM
    ones = jnp.ones((seq_len, pad), F32)
    zer_h = jnp.zeros((seq_len, ROPE_HALF), F32)
    zer_p = jnp.zeros((seq_len, pad), F32)
    c = jnp.concatenate([cos, cos, ones], axis=1)
    s1 = jnp.concatenate([zer_h, sin, zer_p], axis=1)
    s2 = jnp.concatenate([-sin, zer_h, zer_p], axis=1)
    rep = LANES // HEAD_DIM
    return jnp.tile(c, (1, rep)), jnp.tile(s1, (1, rep)), jnp.tile(s2, (1, rep))


def _ssm_weights(a_re, a_im, log_dt, b_re, b_im, c_re, c_im, ssm_d, nsteps):
    r = SSM_CHUNK
    dt = jnp.exp(log_dt)[..., None]
    lr, li = a_re * dt, a_im * dt

    def cpow(n):
        n = jnp.asarray(n, F32)[..., None, None, None]
        mag = jnp.exp(n * lr)
        return mag * jnp.cos(n * li), mag * jnp.sin(n * li)

    ab_re, ab_im = cpow(jnp.ones(()))
    den = a_re * a_re + a_im * a_im
    num_re, num_im = ab_re - 1.0, ab_im
    f_re = (num_re * a_re + num_im * a_im) / den
    f_im = (num_im * a_re - num_re * a_im) / den
    bb_re = f_re[..., None] * b_re - f_im[..., None] * b_im
    bb_im = f_re[..., None] * b_im + f_im[..., None] * b_re

    taus = jnp.arange(r + 1, dtype=F32)
    p_re, p_im = cpow(taus)
    m_re = p_re[..., None] * bb_re - p_im[..., None] * bb_im
    m_im = p_re[..., None] * bb_im + p_im[..., None] * bb_re
    kern = (jnp.einsum('dgcp,tdgpk->tdgck', c_re, m_re) - jnp.einsum('dgcp,tdgpk->tdgck', c_im, m_im))
    center = kern[0, 0] + kern[0, 1] + jnp.eye(SSM_GROUP, dtype=F32) * ssm_d[:, :, None]
    lags = jnp.concatenate([kern[r - 1:0:-1, 0], center[None], kern[1:r, 1]], axis=0)
    rows = jnp.stack([lags[r - 1 - b:2 * r - 1 - b] for b in range(r)], axis=0)
    w_toep = rows.transpose(2, 0, 3, 1, 4).reshape(SSM_N_GROUPS, TOEP, TOEP)

    st_f_re, st_f_im = m_re[r - 1::-1, 0], m_im[r - 1::-1, 0]
    st_b_re, st_b_im = m_re[:r, 1], m_im[:r, 1]
    w_state = jnp.stack([st_f_re, st_f_im, st_b_re, st_b_im], axis=0)
    w_state = w_state.transpose(2, 0, 3, 1, 4).reshape(SSM_N_GROUPS, 4 * SSM_STATE, TOEP)

    def c_times_pow(d, pr, pi):
        zr = c_re[d][None] * pr[:, :, None, :] - c_im[d][None] * pi[:, :, None, :]
        zi = c_re[d][None] * pi[:, :, None, :] + c_im[d][None] * pr[:, :, None, :]
        return zr, -zi
    cf_re, cf_im = c_times_pow(0, p_re[1:r + 1, 0], p_im[1:r + 1, 0])
    cb_re, cb_im = c_times_pow(1, p_re[r:0:-1, 1], p_im[r:0:-1, 1])
    w_carry = jnp.stack([cf_re, cf_im, cb_re, cb_im], axis=3)
    w_carry = w_carry.transpose(1, 0, 2, 3, 4).reshape(SSM_N_GROUPS, TOEP, 4 * SSM_STATE)

    qr, qi = p_re[r], p_im[r]
    cols = []
    for _ in range(nsteps):
        cols.append(jnp.stack([qr[0], qi[0], qr[1], qi[1]], axis=1))
        qr, qi = qr * qr - qi * qi, 2.0 * qr * qi
    pw = jnp.stack(cols, axis=-1).reshape(SSM_N_GROUPS, 4 * SSM_STATE, nsteps)
    return w_toep.astype(BF16), w_state.astype(BF16), w_carry.astype(BF16), pw


def _front(x, prm):
    n, seq_len, _ = x.shape
    t = n * seq_len
    x2d = x.reshape(t, D_MODEL)
    q, kv = _qkv(x2d, prm['norm1_g'], prm['w_qkv'], *_rope_tables(seq_len), seq_len)
    attn_n = _attention(q, kv, prm['sink'], prm['attn_out_g'], seq_len)

    chunks_per_seq = seq_len // SSM_CHUNK
    nsteps = max(1, (chunks_per_seq - 1).bit_length())
    nc = t // SSM_CHUNK
    lt = min(256, nc)
    ut = _uproj(x2d, prm['norm1_g'], prm['w_u_t'], lt)
    yt = _ssm(ut, prm['w_toep'], prm['w_state'], prm['w_carry'], prm['pw'][:, :, :nsteps], chunks_per_seq)
    ssm_n = _glu(yt, prm['glu_w_t'], prm['glu_b'], prm['ssm_out_g'], lt)
    return _outproj(x2d, attn_n, ssm_n, prm['w_out_a'], prm['w_out_s'], prm['norm2_g'],
                    prm['router_w'], prm['router_b'], prm['tri'])


def _cumsum_small(x):
    n = x.shape[0]
    keep = jnp.arange(n)[None, :] <= jnp.arange(n)[:, None]
    return jnp.sum(jnp.where(keep, x[None, :], 0), axis=1)


def _by_expert(idx, table):
    hit = idx[None] == jnp.arange(N_EXPERTS, dtype=I32)[:, None, None]
    return jnp.sum(jnp.where(hit, table[:, None, None], 0), axis=0)


def kernel(x_prompt, x_sample, norm1_g, w_in, attn_sink, ssm_a_re, ssm_a_im, ssm_log_dt, ssm_b_re, ssm_b_im, ssm_c_re, ssm_c_im, ssm_d, glu_w, glu_b, attn_out_g, ssm_out_g, w_out, norm2_g, router_w, router_b, w_gate_up, b_gate_up, w_down, b_down, final_g):
    assert norm1_g.shape[0] == 1, "single-layer problem"
    l = 0
    xs = [x_prompt, x_sample]
    max_chunks = max(x.shape[1] for x in xs) // SSM_CHUNK
    max_steps = max(1, (max_chunks - 1).bit_length())
    wq, wk, wv, wu = jnp.split(w_in[l], [ATTN_WIDTH, ATTN_WIDTH + KV_WIDTH, ATTN_WIDTH + 2 * KV_WIDTH], axis=1)
    dup = lambda w: jnp.concatenate([w[:, :HEAD_DIM], w[:, :HEAD_DIM], w[:, HEAD_DIM:], w[:, HEAD_DIM:]], axis=1)
    w_toep, w_state, w_carry, pw = _ssm_weights(ssm_a_re[l], ssm_a_im[l], ssm_log_dt[l], ssm_b_re[l], ssm_b_im[l],
                                                ssm_c_re[l], ssm_c_im[l], ssm_d[l], max_steps)
    tri_i = lax.broadcasted_iota(I32, (ROUTE_TM, ROUTE_TM), 0)
    tri_j = lax.broadcasted_iota(I32, (ROUTE_TM, ROUTE_TM), 1)
    prm = dict(
        norm1_g=norm1_g[l].reshape(1, D_MODEL),
        w_qkv=jnp.concatenate([wq, dup(wk), dup(wv)], axis=1).astype(BF16),
        w_u_t=wu.T.astype(BF16),
        sink=attn_sink[l].astype(F32),
        attn_out_g=attn_out_g[l].reshape(1, ATTN_WIDTH),
        w_toep=w_toep, w_state=w_state, w_carry=w_carry, pw=pw,
        glu_w_t=glu_w[l].T.astype(BF16),
        glu_b=glu_b[l].reshape(SSM_WIDTH, 1),
        ssm_out_g=ssm_out_g[l].reshape(SSM_WIDTH, 1),
        w_out_a=w_out[l][:ATTN_WIDTH].astype(BF16),
        w_out_s=w_out[l][ATTN_WIDTH:].astype(BF16),
        norm2_g=norm2_g[l].reshape(1, D_MODEL),
        router_w=router_w[l].T.astype(BF16),
        router_b=router_b[l].reshape(N_EXPERTS, 1),
        tri=(tri_i < tri_j).astype(BF16),
    )
    fronts = [_front(x, prm) for x in xs]

    cnts = [f[4][:, 0].astype(I32) for f in fronts]
    padded = [(c + MOE_ROWS - 1) // MOE_ROWS * MOE_ROWS for c in cnts]
    pends = [_cumsum_small(p) for p in padded]
    pstarts = [pe - p for pe, p in zip(pends, padded)]
    nbs = [f[0].shape[0] * TOP_K // MOE_ROWS + N_EXPERTS for f in fronts]
    seg_blocks = jnp.stack([p // MOE_ROWS for p in padded], axis=1).reshape(-1)
    seg_end = _cumsum_small(seg_blocks)
    seg_start = seg_end - seg_blocks
    nb = sum(nbs)
    bi = jnp.arange(nb, dtype=I32)
    seg = jnp.minimum(jnp.sum(seg_end[None, :] <= bi[:, None], axis=1), 2 * N_EXPERTS - 1).astype(I32)
    block_e = seg // 2
    block_grp = seg % 2
    n_valid = seg_end[-1].astype(I32).reshape(1)
    live = bi < n_valid[0]
    src_start = jnp.stack([ps // MOE_ROWS for ps in pstarts], axis=1).reshape(-1)
    in_seg = seg[:, None] == jnp.arange(2 * N_EXPERTS, dtype=I32)[None, :]
    src_blk = jnp.sum(jnp.where(in_seg, (src_start - seg_start)[None, :], 0), axis=1) + bi
    not_after = bi[None, :] <= bi[:, None]
    blk = [jnp.max(jnp.where(not_after & (live & (block_grp == g))[None, :], src_blk[None, :], 0), axis=1).astype(I32)
           for g in range(2)]

    xbufs, yrows = [], []
    for g, f in enumerate(fronts):
        e_idx = f[2][:TOP_K]
        rank = f[2][TOP_K:]
        n_pad = nbs[g] * MOE_ROWS
        pad_start = jnp.concatenate([pstarts[g] + cnts[g], pends[g][-1:]]).astype(I32)
        pad_cnt = jnp.concatenate([padded[g] - cnts[g], n_pad - pends[g][-1:]]).astype(I32)
        xrows = (_by_expert(e_idx, pstarts[g]) + rank).astype(I32)
        xbufs.append(_scatter_rows(pad_start, pad_cnt, xrows, f[1], n_pad))
        yrows.append((_by_expert(e_idx, seg_start[g::2] * MOE_ROWS) + rank).astype(I32))
    yb = _experts(block_e, block_grp, blk[0], blk[1], n_valid, xbufs[0], xbufs[1],
                  w_gate_up.reshape(N_EXPERTS, D_MODEL, 2 * D_MODEL), b_gate_up[l][:, None, :],
                  w_down.reshape(N_EXPERTS, D_MODEL, D_MODEL), b_down[l][:, None, :])
    gfin = final_g.reshape(1, D_MODEL)
    outs = [_combine(r, f[0], f[3], gfin, yb).reshape(x.shape) for x, f, r in zip(xs, fronts, yrows)]
    return tuple(outs)
```

```python
import functools
import math

import jax
import jax.numpy as jnp
import numpy as np
from jax import lax
from jax.experimental import pallas as pl
from jax.experimental.pallas import tpu as pltpu

F32 = jnp.float32
BF16 = jnp.bfloat16
I32 = jnp.int32

D_MODEL = 1024
HEAD_DIM = 64
N_Q_HEADS = 8
N_KV_HEADS = 2
Q_PER_KV = N_Q_HEADS // N_KV_HEADS
ATTN_WIDTH = N_Q_HEADS * HEAD_DIM
KV_WIDTH = N_KV_HEADS * HEAD_DIM
WINDOW = 128
ATT_BLOCK = 128
ROPE_THETA = 500000.0
ROPE_DIM = HEAD_DIM // 4
ROPE_HALF = ROPE_DIM // 2
SSM_WIDTH = 512
SSM_GROUP = 16
SSM_N_GROUPS = SSM_WIDTH // SSM_GROUP
SSM_STATE = 64
N_EXPERTS = 32
TOP_K = 4
SWIGLU_LIMIT = 7.0
SWIGLU_ALPHA = 1.702
EPS = 1e-5

LANES = 128
SUBLANES = 8
SSM_CHUNK = 32
TOEP = SSM_CHUNK * SSM_GROUP
MOE_ROWS = 512
ROUTE_TM = 512
COMBINE_TM = 128
VMEM_LIMIT = 52 * 1024 * 1024
EXPERT_VMEM_LIMIT = 60 * 1024 * 1024
NEG_BIG = -1e30


def _params(sem, vmem=VMEM_LIMIT):
    return pltpu.CompilerParams(dimension_semantics=sem, vmem_limit_bytes=vmem)


def _rms(x, g):
    ms = jnp.mean(x * x, axis=-1, keepdims=True)
    return x * lax.rsqrt(ms + EPS) * g


def _tile_rows_load(ref, rows):
    return jnp.concatenate([ref[pl.ds(j, rows, stride=SUBLANES), :] for j in range(SUBLANES)], axis=1)


def _tile_rows_store(ref, val):
    rows = val.shape[0]
    for j in range(SUBLANES):
        ref[pl.ds(j, rows, stride=SUBLANES), :] = val[:, j * LANES:(j + 1) * LANES]


def _tile_at(ref, start):
    return ref.at[pl.ds(pl.multiple_of(start, SUBLANES), SUBLANES)]


def _tile_row(ref, row):
    return _tile_at(ref, row * SUBLANES)


KV_COLS = 4 * LANES


def _qkv_kernel(x_ref, g_ref, w_ref, c_ref, s1_ref, s2_ref, q_ref, kv_ref):
    h = _rms(x_ref[...], g_ref[...]).astype(BF16)
    p = jnp.dot(h, w_ref[...], preferred_element_type=F32)
    c = c_ref[...]
    s1 = s1_ref[...]
    s2 = s2_ref[...]

    def rot(t):
        return t * c + pltpu.roll(t, ROPE_HALF, 1) * s1 + pltpu.roll(t, LANES - ROPE_HALF, 1) * s2

    for j in range(ATTN_WIDTH // LANES):
        q_ref[:, j * LANES:(j + 1) * LANES] = (rot(p[:, j * LANES:(j + 1) * LANES]) * (HEAD_DIM ** -0.5)).astype(BF16)
    for j in range(N_KV_HEADS):
        col = ATTN_WIDTH + j * LANES
        kv_ref[:, j * LANES:(j + 1) * LANES] = rot(p[:, col:col + LANES]).astype(BF16)
    kv_ref[:, N_KV_HEADS * LANES:] = p[:, ATTN_WIDTH + N_KV_HEADS * LANES:].astype(BF16)


def _qkv(x2d, g, w, c, s1, s2, seq_len, tm=512):
    t = x2d.shape[0]
    nlb = seq_len // tm
    row = lambda i: (i, 0)
    tab = lambda i: (i % nlb, 0)
    full = lambda i: (0, 0)
    return pl.pallas_call(
        _qkv_kernel,
        grid=(t // tm,),
        in_specs=[pl.BlockSpec((tm, D_MODEL), row), pl.BlockSpec((1, D_MODEL), full),
                  pl.BlockSpec((D_MODEL, ATTN_WIDTH + KV_COLS), full),
                  pl.BlockSpec((tm, LANES), tab), pl.BlockSpec((tm, LANES), tab), pl.BlockSpec((tm, LANES), tab)],
        out_specs=[pl.BlockSpec((tm, ATTN_WIDTH), row), pl.BlockSpec((tm, KV_COLS), row)],
        out_shape=[jax.ShapeDtypeStruct((t, ATTN_WIDTH), BF16), jax.ShapeDtypeStruct((t, KV_COLS), BF16)],
        compiler_params=_params(("parallel",)),
        name="qkv_rotary",
    )(x2d, g, w, c, s1, s2)


def _attn_kernel(sink_ref, q_ref, kvp, kvc, kvn, g_ref, o_ref, *, bps):
    i = pl.program_id(0)
    first = (i % bps) == 0
    last = (i % bps) == bps - 1
    qi = lax.broadcasted_iota(I32, (ATT_BLOCK, 3 * ATT_BLOCK), 0)
    kj = lax.broadcasted_iota(I32, (ATT_BLOCK, 3 * ATT_BLOCK), 1)
    rel = kj - ATT_BLOCK - qi
    valid = (jnp.abs(rel) <= WINDOW)
    valid = valid & ((kj >= ATT_BLOCK) | jnp.logical_not(first))
    valid = valid & ((kj < 2 * ATT_BLOCK) | jnp.logical_not(last))
    kv = jnp.concatenate([kvp[...], kvc[...], kvn[...]], axis=0)
    ks = [kv[:, h * LANES:(h + 1) * LANES] for h in range(N_KV_HEADS)]
    vs = [kv[:, (N_KV_HEADS + h) * LANES:(N_KV_HEADS + h + 1) * LANES] for h in range(N_KV_HEADS)]
    lo = lax.broadcasted_iota(I32, (ATT_BLOCK, LANES), 1) < HEAD_DIM
    zero = jnp.zeros((ATT_BLOCK, LANES), BF16)
    heads = [(j, par) for j in range(ATTN_WIDTH // LANES) for par in range(2)]
    nt_dims = (((1,), (1,)), ((), ()))
    scores = []
    for j, par in heads:
        qt = q_ref[:, j * LANES:(j + 1) * LANES]
        qm = jnp.where(lo if par == 0 else jnp.logical_not(lo), qt, zero)
        scores.append(lax.dot_general(qm, ks[j // 2], nt_dims, preferred_element_type=F32))
    scores = [jnp.where(valid, s, NEG_BIG) for s in scores]
    sinks = [sink_ref[2 * j + par] for j, par in heads]
    maxes = [jnp.maximum(jnp.max(s, axis=-1, keepdims=True), sk) for s, sk in zip(scores, sinks)]
    probs = [jnp.exp(s - m) for s, m in zip(scores, maxes)]
    dens = [jnp.sum(p, axis=-1, keepdims=True) + jnp.exp(sk - m) for p, m, sk in zip(probs, maxes, sinks)]
    outs = [jnp.dot(p.astype(BF16), vs[j // 2], preferred_element_type=F32) for p, (j, par) in zip(probs, heads)]
    outs = [o / d for o, d in zip(outs, dens)]
    tiles = [jnp.where(lo, outs[2 * j], outs[2 * j + 1]) for j in range(ATTN_WIDTH // LANES)]
    o = jnp.concatenate(tiles, axis=1)
    o_ref[...] = _rms(o, g_ref[...]).astype(BF16)


def _attention(q, kv, sink, g, seq_len):
    t = q.shape[0]
    nblk = t // ATT_BLOCK
    bps = seq_len // ATT_BLOCK
    cur = lambda i, s: (i, 0)
    prv = lambda i, s: (jnp.maximum(i - 1, 0), 0)
    nxt = lambda i, s: (jnp.minimum(i + 1, nblk - 1), 0)
    grid_spec = pltpu.PrefetchScalarGridSpec(
        num_scalar_prefetch=1,
        grid=(nblk,),
        in_specs=[pl.BlockSpec((ATT_BLOCK, ATTN_WIDTH), cur),
                  pl.BlockSpec((ATT_BLOCK, KV_COLS), prv), pl.BlockSpec((ATT_BLOCK, KV_COLS), cur),
                  pl.BlockSpec((ATT_BLOCK, KV_COLS), nxt),
                  pl.BlockSpec((1, ATTN_WIDTH), lambda i, s: (0, 0))],
        out_specs=pl.BlockSpec((ATT_BLOCK, ATTN_WIDTH), cur),
    )
    return pl.pallas_call(
        functools.partial(_attn_kernel, bps=bps),
        grid_spec=grid_spec,
        out_shape=jax.ShapeDtypeStruct((t, ATTN_WIDTH), BF16),
        compiler_params=_params(("parallel",)),
        name="banded_attention",
    )(sink, q, kv, kv, kv, g)


def _uproj_kernel(x_ref, g_ref, w_ref, o_ref, stage_ref):
    lt = x_ref.shape[0]
    h = _rms(x_ref[...].reshape(lt * SUBLANES, D_MODEL), g_ref[...])
    for j in range(D_MODEL // LANES):
        stage_ref[j] = h[:, j * LANES:(j + 1) * LANES]
    for bl in range(SUBLANES):
        hb = jnp.concatenate([stage_ref[j, pl.ds(bl, lt, stride=SUBLANES), :] for j in range(D_MODEL // LANES)],
                             axis=1).astype(BF16)
        ut = lax.dot_general(w_ref[...], hb, (((1,), (1,)), ((), ())), preferred_element_type=F32)
        o_ref[bl] = ut.astype(BF16)


def _uproj(x2d, g, w_t, lt):
    t = x2d.shape[0]
    nc = t // SSM_CHUNK
    xv = x2d.reshape(nc, SSM_CHUNK // SUBLANES, SUBLANES, D_MODEL)
    return pl.pallas_call(
        _uproj_kernel,
        grid=(nc // lt, SSM_CHUNK // SUBLANES),
        in_specs=[pl.BlockSpec((lt, None, SUBLANES, D_MODEL), lambda i, b: (i, b, 0, 0)),
                  pl.BlockSpec((1, D_MODEL), lambda i, b: (0, 0)),
                  pl.BlockSpec((SSM_WIDTH, D_MODEL), lambda i, b: (0, 0))],
        out_specs=pl.BlockSpec((SUBLANES, SSM_WIDTH, lt), lambda i, b: (b, 0, i)),
        out_shape=jax.ShapeDtypeStruct((SSM_CHUNK, SSM_WIDTH, nc), BF16),
        scratch_shapes=[pltpu.VMEM((D_MODEL // LANES, lt * SUBLANES, LANES), F32)],
        compiler_params=_params(("parallel", "parallel")),
        name="u_projection",
    )(xv, g, w_t)


def _gelu_tanh(x):
    return 0.5 * x * (1.0 + jnp.tanh(math.sqrt(2.0 / math.pi) * (x + 0.044715 * (x * x * x))))


LAG_STRIP = 8 * LANES


def _toeplitz_from_strip(strip):
    per_tile = LANES // SSM_GROUP
    rolled = [strip if q == 0 else pltpu.roll(strip, LAG_STRIP - q * SSM_GROUP, 1) for q in range(per_tile)]
    blocks = []
    for b in range(SSM_CHUNK):
        m, q = divmod(SSM_CHUNK - 1 - b, per_tile)
        blocks.append(rolled[q][:, m * LANES:m * LANES + TOEP])
    return jnp.concatenate(blocks, axis=0)


def _ssm_kernel(a_ref, wt_ref, ws_ref, wc_ref, pw_ref, y_ref, *, chunks_per_seq, nsteps):
    nc = a_ref.shape[2]
    a = a_ref[...].reshape(TOEP, nc)
    w_toep = _toeplitz_from_strip(wt_ref[0]).astype(BF16)
    y = jnp.dot(w_toep, a, preferred_element_type=F32)
    s = jnp.dot(ws_ref[0], a, preferred_element_type=F32)
    pos = lax.broadcasted_iota(I32, (SSM_STATE, nc), 1) % chunks_per_seq
    carries = []
    for d in range(2):
        hr = s[2 * d * SSM_STATE:(2 * d + 1) * SSM_STATE]
        hi = s[(2 * d + 1) * SSM_STATE:(2 * d + 2) * SSM_STATE]
        for k in range(nsteps):
            sh = 1 << k
            pr = pw_ref[0, 2 * d * SSM_STATE:(2 * d + 1) * SSM_STATE, k:k + 1]
            pi = pw_ref[0, (2 * d + 1) * SSM_STATE:(2 * d + 2) * SSM_STATE, k:k + 1]
            if d == 0:
                ok = pos >= sh
                sr = pltpu.roll(hr, sh, 1)
                si = pltpu.roll(hi, sh, 1)
            else:
                ok = pos < chunks_per_seq - sh
                sr = pltpu.roll(hr, nc - sh, 1)
                si = pltpu.roll(hi, nc - sh, 1)
            hr, hi = (hr + jnp.where(ok, pr * sr - pi * si, 0.0),
                      hi + jnp.where(ok, pr * si + pi * sr, 0.0))
        if d == 0:
            ok = pos >= 1
            cr = pltpu.roll(hr, 1, 1)
            ci = pltpu.roll(hi, 1, 1)
        else:
            ok = pos < chunks_per_seq - 1
            cr = pltpu.roll(hr, nc - 1, 1)
            ci = pltpu.roll(hi, nc - 1, 1)
        carries += [jnp.where(ok, cr, 0.0), jnp.where(ok, ci, 0.0)]
    carry = jnp.concatenate(carries, axis=0).astype(BF16)
    y = y + jnp.dot(wc_ref[0], carry, preferred_element_type=F32)
    y_ref[...] = _gelu_tanh(y).reshape(SSM_CHUNK, SSM_GROUP, nc)


def _ssm(ut, w_toep, w_state, w_carry, pw, chunks_per_seq):
    nc = ut.shape[2]
    nsteps = max(1, (chunks_per_seq - 1).bit_length())
    g3 = lambda g: (g, 0, 0)
    return pl.pallas_call(
        functools.partial(_ssm_kernel, chunks_per_seq=chunks_per_seq, nsteps=nsteps),
        grid=(SSM_N_GROUPS,),
        in_specs=[pl.BlockSpec((SSM_CHUNK, SSM_GROUP, nc), lambda g: (0, g, 0)),
                  pl.BlockSpec((1, SSM_GROUP, LAG_STRIP), g3),
                  pl.BlockSpec((1, 4 * SSM_STATE, TOEP), g3),
                  pl.BlockSpec((1, TOEP, 4 * SSM_STATE), g3),
                  pl.BlockSpec((1, 4 * SSM_STATE, pw.shape[2]), g3)],
        out_specs=pl.BlockSpec((SSM_CHUNK, SSM_GROUP, nc), lambda g: (0, g, 0)),
        out_shape=jax.ShapeDtypeStruct((SSM_CHUNK, SSM_WIDTH, nc), F32),
        compiler_params=_params(("parallel",)),
        name="s5_core",
    )(ut, w_toep, w_state, w_carry, pw)


def _glu_kernel(y_ref, w_ref, b_ref, g_ref, o_ref, stage_ref):
    lt = o_ref.shape[0]
    for bl in range(SUBLANES):
        y = y_ref[bl]
        z = jnp.dot(w_ref[...], y.astype(BF16), preferred_element_type=F32) + b_ref[...]
        s = y * jax.nn.sigmoid(z)
        ms = jnp.mean(s * s, axis=0, keepdims=True)
        sn = (s * lax.rsqrt(ms + EPS) * g_ref[...]).T
        for j in range(SSM_WIDTH // LANES):
            stage_ref[j, pl.ds(bl, lt, stride=SUBLANES), :] = sn[:, j * LANES:(j + 1) * LANES]
    for j in range(SSM_WIDTH // LANES):
        o_ref[:, :, j * LANES:(j + 1) * LANES] = stage_ref[j].reshape(lt, SUBLANES, LANES)


def _glu(yt, w_t, b_col, g_col, lt):
    nc = yt.shape[2]
    out = pl.pallas_call(
        _glu_kernel,
        grid=(nc // lt, SSM_CHUNK // SUBLANES),
        in_specs=[pl.BlockSpec((SUBLANES, SSM_WIDTH, lt), lambda i, b: (b, 0, i)),
                  pl.BlockSpec((SSM_WIDTH, SSM_WIDTH), lambda i, b: (0, 0)),
                  pl.BlockSpec((SSM_WIDTH, 1), lambda i, b: (0, 0)),
                  pl.BlockSpec((SSM_WIDTH, 1), lambda i, b: (0, 0))],
        out_specs=pl.BlockSpec((lt, None, SUBLANES, SSM_WIDTH), lambda i, b: (i, b, 0, 0)),
        out_shape=jax.ShapeDtypeStruct((nc, SSM_CHUNK // SUBLANES, SUBLANES, SSM_WIDTH), F32),
        scratch_shapes=[pltpu.VMEM((SSM_WIDTH // LANES, lt * SUBLANES, LANES), F32)],
        compiler_params=_params(("parallel", "parallel")),
        name="glu_norm",
    )(yt, w_t, b_col, g_col)
    return out.reshape(nc * SSM_CHUNK, SSM_WIDTH)


def _outproj_kernel(x_ref, a_ref, s_ref, wa_ref, ws_ref, g_ref, wr_ref, br_ref, tri_ref,
                    x2_ref, h2_ref, rt_ref, gt_ref, cnt_ref, run_ref):
    i = pl.program_id(0)

    @pl.when(i == 0)
    def _():
        run_ref[...] = jnp.zeros_like(run_ref)

    x2 = (x_ref[...] + jnp.dot(a_ref[...], wa_ref[...], preferred_element_type=F32)
          + jnp.dot(s_ref[...].astype(BF16), ws_ref[...], preferred_element_type=F32))
    x2_ref[...] = x2
    h2 = _rms(x2, g_ref[...])
    _tile_rows_store(h2_ref, h2)
    logits = lax.dot_general(wr_ref[...], h2.astype(BF16), (((1,), (1,)), ((), ())),
                             preferred_element_type=F32) + br_ref[...]
    tm = logits.shape[1]
    sub = lax.broadcasted_iota(I32, (N_EXPERTS, tm), 0)
    sub_f = sub.astype(F32)
    work = logits
    sel = jnp.zeros((N_EXPERTS, tm), F32)
    top_v, top_i = [], []
    for _ in range(TOP_K):
        m = jnp.max(work, axis=0, keepdims=True)
        idx = jnp.min(jnp.where(work == m, sub_f, float(N_EXPERTS)), axis=0, keepdims=True).astype(I32)
        hit = sub == idx
        sel = jnp.where(hit, 1.0, sel)
        work = jnp.where(hit, -jnp.inf, work)
        top_v.append(m)
        top_i.append(idx)
    ex = [jnp.exp(v - top_v[0]) for v in top_v]
    den = ex[0] + ex[1] + ex[2] + ex[3]
    before = jnp.dot(sel.astype(BF16), tri_ref[...], preferred_element_type=F32) + run_ref[:, 0:1]
    ranks = [jnp.sum(jnp.where(sub == top_i[k], before, 0.0), axis=0, keepdims=True).astype(I32) for k in range(TOP_K)]
    rt_ref[...] = jnp.concatenate(top_i + ranks, axis=0)
    gt_ref[...] = jnp.concatenate([e / den for e in ex] + [jnp.zeros((SUBLANES - TOP_K, tm), F32)], axis=0)
    run = run_ref[...] + jnp.sum(sel, axis=1, keepdims=True)
    run_ref[...] = run
    cnt_ref[...] = run


def _outproj(x2d, attn_n, ssm_n, w_a, w_s, g, w_r, b_r, tri):
    t = x2d.shape[0]
    tm = ROUTE_TM
    row = lambda i: (i, 0)
    full = lambda i: (0, 0)
    return pl.pallas_call(
        _outproj_kernel,
        grid=(t // tm,),
        in_specs=[pl.BlockSpec((tm, D_MODEL), row), pl.BlockSpec((tm, ATTN_WIDTH), row),
                  pl.BlockSpec((tm, SSM_WIDTH), row),
                  pl.BlockSpec((ATTN_WIDTH, D_MODEL), full), pl.BlockSpec((SSM_WIDTH, D_MODEL), full),
                  pl.BlockSpec((1, D_MODEL), full),
                  pl.BlockSpec((N_EXPERTS, D_MODEL), full), pl.BlockSpec((N_EXPERTS, 1), full),
                  pl.BlockSpec((tm, tm), full)],
        out_specs=[pl.BlockSpec((tm, D_MODEL), row), pl.BlockSpec((tm * SUBLANES, LANES), row),
                   pl.BlockSpec((2 * TOP_K, tm), lambda i: (0, i)), pl.BlockSpec((SUBLANES, tm), lambda i: (0, i)),
                   pl.BlockSpec((N_EXPERTS, LANES), full)],
        out_shape=[jax.ShapeDtypeStruct((t, D_MODEL), F32), jax.ShapeDtypeStruct((t * SUBLANES, LANES), F32),
                   jax.ShapeDtypeStruct((2 * TOP_K, t), I32), jax.ShapeDtypeStruct((SUBLANES, t), F32),
                   jax.ShapeDtypeStruct((N_EXPERTS, LANES), F32)],
        scratch_shapes=[pltpu.VMEM((N_EXPERTS, LANES), F32)],
        compiler_params=_params(("arbitrary",)),
        name="out_projection_router",
    )(x2d, attn_n, ssm_n, w_a, w_s, g, w_r, b_r, tri)


DMA_GROUP = 4


def _rows_per_step(rows, tokens):
    nt = rows.shape[1] // tokens
    return (rows * SUBLANES).reshape(TOP_K, nt, tokens).transpose(1, 0, 2).reshape(nt, 1, TOP_K * tokens)


def _scatter_kernel(pad_start_ref, pad_cnt_ref, rows_ref, h_ref, xb_hbm, zero_ref, stage_ref, sems, zsem, *, tm):
    i = pl.program_id(0)
    nt = pl.num_programs(0)
    n_dma = tm * TOP_K

    def tile_wait(slot):
        whole = xb_hbm.at[pl.ds(0, n_dma * SUBLANES)]
        pltpu.make_async_copy(whole, whole, sems.at[slot]).wait()

    @pl.when(i == 0)
    def _():
        zero_ref[...] = jnp.zeros_like(zero_ref)

        def fill(row, n_rows):
            return pltpu.make_async_copy(zero_ref.at[pl.ds(0, n_rows * SUBLANES)],
                                         xb_hbm.at[pl.ds(pl.multiple_of(row * SUBLANES, SUBLANES), n_rows * SUBLANES)],
                                         zsem)

        for phase in range(2):
            def per_expert(e, c):
                cnt = pad_cnt_ref[e]
                row = pad_start_ref[e]
                run = MOE_ROWS // 2
                while run >= 1:
                    below = cnt & ~(2 * run - 1)

                    @pl.when((cnt & run) != 0)
                    def _(run=run, below=below):
                        cp = fill(row + below, run)
                        cp.start() if phase == 0 else cp.wait()
                    run //= 2
                return c
            lax.fori_loop(0, N_EXPERTS, per_expert, 0)

            def per_tail_block(b, c):
                cp = fill(pad_start_ref[N_EXPERTS] + b * MOE_ROWS, MOE_ROWS)
                cp.start() if phase == 0 else cp.wait()
                return c
            lax.fori_loop(0, pad_cnt_ref[N_EXPERTS] // MOE_ROWS, per_tail_block, 0)

    for slot in range(2):
        @pl.when(i > 0)
        def _(slot=slot):
            tile_wait(slot)

        stage_ref[slot] = h_ref[pl.ds(slot * tm * SUBLANES, tm * SUBLANES), :]

        def issue(grp, c, slot=slot):
            r0 = grp * DMA_GROUP
            dst = [rows_ref[0, 0, k * (2 * tm) + slot * tm + r0 + r] for r in range(DMA_GROUP) for k in range(TOP_K)]
            for r in range(DMA_GROUP):
                src = _tile_row(stage_ref.at[slot], r0 + r)
                for k in range(TOP_K):
                    pltpu.make_async_copy(src, _tile_at(xb_hbm, dst[r * TOP_K + k]),
                                          sems.at[slot]).start(priority=k % 2)
            return c
        lax.fori_loop(0, tm // DMA_GROUP, issue, 0)

    @pl.when(i == nt - 1)
    def _():
        tile_wait(0)
        tile_wait(1)


def _scatter_rows(pad_start, pad_cnt, rows, h2, n_pad):
    tm = ROUTE_TM
    nt = h2.shape[0] // SUBLANES // (2 * tm)
    grid_spec = pltpu.PrefetchScalarGridSpec(
        num_scalar_prefetch=2,
        grid=(nt,),
        in_specs=[pl.BlockSpec((1, 1, 2 * tm * TOP_K), lambda i, a, b: (i, 0, 0), memory_space=pltpu.SMEM),
                  pl.BlockSpec((2 * tm * SUBLANES, LANES), lambda i, a, b: (i, 0))],
        out_specs=pl.BlockSpec(memory_space=pl.ANY),
        scratch_shapes=[pltpu.VMEM((MOE_ROWS * SUBLANES, LANES), F32),
                        pltpu.VMEM((2, tm * SUBLANES, LANES), F32),
                        pltpu.SemaphoreType.DMA((2,)), pltpu.SemaphoreType.DMA(())],
    )
    return pl.pallas_call(
        functools.partial(_scatter_kernel, tm=tm),
        grid_spec=grid_spec,
        out_shape=jax.ShapeDtypeStruct((n_pad * SUBLANES, LANES), F32),
        compiler_params=_params(("arbitrary",)),
        name="scatter_rows",
    )(pad_start, pad_cnt, _rows_per_step(rows, 2 * tm), h2)


def _expert_kernel(be_ref, grp_ref, ia_ref, ib_ref, nv_ref, xa_ref, xb_ref, wgu_ref, bgu_ref, wd_ref, bd_ref,
                   o_ref, x_scr, wgu_scr, wd_scr):
    i = pl.program_id(0)
    live = i < nv_ref[0]

    @pl.when(live & ((i == 0) | (be_ref[i] != be_ref[jnp.maximum(i - 1, 0)])))
    def _():
        wgu_scr[...] = wgu_ref[0].astype(BF16)
        wd_scr[...] = wd_ref[0].astype(BF16)

    @pl.when(live & (grp_ref[i] == 0))
    def _():
        x_scr[...] = _tile_rows_load(xa_ref, MOE_ROWS).astype(BF16)

    @pl.when(live & (grp_ref[i] != 0))
    def _():
        x_scr[...] = _tile_rows_load(xb_ref, MOE_ROWS).astype(BF16)

    @pl.when(live)
    def _():
        gu = jnp.dot(x_scr[...], wgu_scr[...], preferred_element_type=F32) + bgu_ref[0]
        gate = jnp.minimum(gu[:, :D_MODEL], SWIGLU_LIMIT)
        up = jnp.clip(gu[:, D_MODEL:], -SWIGLU_LIMIT, SWIGLU_LIMIT)
        act = (up + 1.0) * (gate * jax.nn.sigmoid(SWIGLU_ALPHA * gate))
        _tile_rows_store(o_ref, jnp.dot(act.astype(BF16), wd_scr[...], preferred_element_type=F32) + bd_ref[0])

    @pl.when(i >= nv_ref[0])
    def _():
        o_ref[...] = jnp.zeros_like(o_ref)


def _experts(block_e, block_grp, blk_a, blk_b, n_valid, xa, xb, wgu, bgu, wd, bd):
    nb = block_e.shape[0]
    e3 = lambda i, be, gr, ia, ib, nv: (be[i], 0, 0)
    grid_spec = pltpu.PrefetchScalarGridSpec(
        num_scalar_prefetch=5,
        grid=(nb,),
        in_specs=[pl.BlockSpec((MOE_ROWS * SUBLANES, LANES), lambda i, be, gr, ia, ib, nv: (ia[i], 0)),
                  pl.BlockSpec((MOE_ROWS * SUBLANES, LANES), lambda i, be, gr, ia, ib, nv: (ib[i], 0)),
                  pl.BlockSpec((1, D_MODEL, 2 * D_MODEL), e3), pl.BlockSpec((1, 1, 2 * D_MODEL), e3),
                  pl.BlockSpec((1, D_MODEL, D_MODEL), e3), pl.BlockSpec((1, 1, D_MODEL), e3)],
        out_specs=pl.BlockSpec((MOE_ROWS * SUBLANES, LANES), lambda i, be, gr, ia, ib, nv: (i, 0)),
        scratch_shapes=[pltpu.VMEM((MOE_ROWS, D_MODEL), BF16), pltpu.VMEM((D_MODEL, 2 * D_MODEL), BF16),
                        pltpu.VMEM((D_MODEL, D_MODEL), BF16)],
    )
    return pl.pallas_call(
        _expert_kernel,
        grid_spec=grid_spec,
        out_shape=jax.ShapeDtypeStruct((nb * MOE_ROWS * SUBLANES, LANES), F32),
        compiler_params=_params(("arbitrary",), vmem=EXPERT_VMEM_LIMIT),
        name="routed_experts",
    )(block_e, block_grp, blk_a, blk_b, n_valid, xa, xb, wgu, bgu, wd, bd)


GATHER_PITCH = 12


def _combine_kernel(rows_cur, rows_nxt, x_ref, gt_ref, g_ref, y_hbm, o_ref, buf, sems, *, tm):
    i = pl.program_id(0)
    nt = pl.num_programs(0)

    def issue(rows_ref, half, slot):
        def body(grp, c):
            r0 = grp * DMA_GROUP
            src = [rows_ref[0, 0, k * (2 * tm) + half * tm + r0 + r] for r in range(DMA_GROUP) for k in range(TOP_K)]
            for r in range(DMA_GROUP):
                for k in range(TOP_K):
                    dst = buf.at[slot, k].at[pl.ds(pl.multiple_of((r0 + r) * GATHER_PITCH, 4), SUBLANES)]
                    pltpu.make_async_copy(_tile_at(y_hbm, src[r * TOP_K + k]), dst,
                                          sems.at[slot]).start(priority=k % 2)
            return c
        lax.fori_loop(0, tm // DMA_GROUP, body, 0)

    def finish(half):
        slot = half
        whole = y_hbm.at[pl.ds(0, TOP_K * tm * SUBLANES)]
        pltpu.make_async_copy(whole, whole, sems.at[slot]).wait()
        acc = x_ref[half * tm:(half + 1) * tm, :]
        g8 = gt_ref[:, half * tm:(half + 1) * tm]
        gt = jnp.concatenate([g8, jnp.zeros((tm - SUBLANES, tm), F32)], axis=0).T
        for k in range(TOP_K):
            yk = jnp.concatenate([buf[slot, k, pl.ds(j, tm, stride=GATHER_PITCH), :] for j in range(SUBLANES)], axis=1)
            acc = acc + yk * gt[:, k:k + 1]
        o_ref[half * tm:(half + 1) * tm, :] = _rms(acc, g_ref[...])

    @pl.when(i == 0)
    def _():
        issue(rows_cur, 0, 0)

    issue(rows_cur, 1, 1)
    finish(0)

    @pl.when(i + 1 < nt)
    def _():
        issue(rows_nxt, 0, 0)

    finish(1)


def _combine(rows, x2, gates, g, yb):
    t = x2.shape[0]
    tm = COMBINE_TM
    nt = t // (2 * tm)
    rows3 = _rows_per_step(rows, 2 * tm)
    row = lambda i: (i, 0)
    return pl.pallas_call(
        functools.partial(_combine_kernel, tm=tm),
        grid=(nt,),
        in_specs=[pl.BlockSpec((1, 1, 2 * tm * TOP_K), lambda i: (i, 0, 0), memory_space=pltpu.SMEM),
                  pl.BlockSpec((1, 1, 2 * tm * TOP_K), lambda i: (jnp.minimum(i + 1, nt - 1), 0, 0),
                               memory_space=pltpu.SMEM),
                  pl.BlockSpec((2 * tm, D_MODEL), row), pl.BlockSpec((SUBLANES, 2 * tm), lambda i: (0, i)),
                  pl.BlockSpec((1, D_MODEL), lambda i: (0, 0)),
                  pl.BlockSpec(memory_space=pl.ANY)],
        out_specs=pl.BlockSpec((2 * tm, D_MODEL), row),
        out_shape=jax.ShapeDtypeStruct((t, D_MODEL), F32),
        scratch_shapes=[pltpu.VMEM((2, TOP_K, tm * GATHER_PITCH, LANES), F32), pltpu.SemaphoreType.DMA((2,))],
        compiler_params=_params(("arbitrary",)),
        name="combine_final_norm",
    )(rows3, rows3, x2, gates, g, yb)


def _rope_tables(seq_len):
    inv_freq = ROPE_THETA ** (-np.arange(ROPE_HALF, dtype=np.float64) * 2.0 / ROPE_DIM)
    ang = np.arange(seq_len, dtype=np.float64)[:, None] * inv_freq[None, :]
    cos = np.cos(ang).astype(np.float32)
    sin = np.sin(ang).astype(np.float32)
    pad = HEAD_DIM - ROPE_DIM
    ones = np.ones((seq_len, pad), np.float32)
    zer_h = np.zeros((seq_len, ROPE_HALF), np.float32)
    zer_p = np.zeros((seq_len, pad), np.float32)
    c = np.concatenate([cos, cos, ones], axis=1)
    s1 = np.concatenate([zer_h, sin, zer_p], axis=1)
    s2 = np.concatenate([-sin, zer_h, zer_p], axis=1)
    rep = LANES // HEAD_DIM
    return tuple(jnp.asarray(np.tile(t, (1, rep))) for t in (c, s1, s2))


def _ssm_weights(a_re, a_im, log_dt, b_re, b_im, c_re, c_im, ssm_d, nsteps):
    r = SSM_CHUNK
    dt = jnp.exp(log_dt)[..., None]
    lr, li = a_re * dt, a_im * dt

    def cpow(n):
        n = jnp.asarray(n, F32)[..., None, None, None]
        mag = jnp.exp(n * lr)
        return mag * jnp.cos(n * li), mag * jnp.sin(n * li)

    ab_re, ab_im = cpow(jnp.ones(()))
    den = a_re * a_re + a_im * a_im
    num_re, num_im = ab_re - 1.0, ab_im
    f_re = (num_re * a_re + num_im * a_im) / den
    f_im = (num_im * a_re - num_re * a_im) / den
    bb_re = f_re[..., None] * b_re - f_im[..., None] * b_im
    bb_im = f_re[..., None] * b_im + f_im[..., None] * b_re

    taus = jnp.arange(r + 1, dtype=F32)
    p_re, p_im = cpow(taus)
    m_re = p_re[..., None] * bb_re - p_im[..., None] * bb_im
    m_im = p_re[..., None] * bb_im + p_im[..., None] * bb_re
    kern = (jnp.einsum('dgcp,tdgpk->tdgck', c_re, m_re) - jnp.einsum('dgcp,tdgpk->tdgck', c_im, m_im))
    center = kern[0, 0] + kern[0, 1] + jnp.eye(SSM_GROUP, dtype=F32) * ssm_d[:, :, None]
    lags = jnp.concatenate([kern[r - 1:0:-1, 0], center[None], kern[1:r, 1]], axis=0)
    strip = lags.transpose(1, 2, 0, 3).reshape(SSM_N_GROUPS, SSM_GROUP, (2 * r - 1) * SSM_GROUP)
    w_toep = jnp.pad(strip, ((0, 0), (0, 0), (0, LAG_STRIP - (2 * r - 1) * SSM_GROUP)))

    st_f_re, st_f_im = m_re[r - 1::-1, 0], m_im[r - 1::-1, 0]
    st_b_re, st_b_im = m_re[:r, 1], m_im[:r, 1]
    w_state = jnp.stack([st_f_re, st_f_im, st_b_re, st_b_im], axis=0)
    w_state = w_state.transpose(2, 0, 3, 1, 4).reshape(SSM_N_GROUPS, 4 * SSM_STATE, TOEP)

    def c_times_pow(d, pr, pi):
        zr = c_re[d][None] * pr[:, :, None, :] - c_im[d][None] * pi[:, :, None, :]
        zi = c_re[d][None] * pi[:, :, None, :] + c_im[d][None] * pr[:, :, None, :]
        return zr, -zi
    cf_re, cf_im = c_times_pow(0, p_re[1:r + 1, 0], p_im[1:r + 1, 0])
    cb_re, cb_im = c_times_pow(1, p_re[r:0:-1, 1], p_im[r:0:-1, 1])
    w_carry = jnp.stack([cf_re, cf_im, cb_re, cb_im], axis=3)
    w_carry = w_carry.transpose(1, 0, 2, 3, 4).reshape(SSM_N_GROUPS, TOEP, 4 * SSM_STATE)

    qr, qi = p_re[r], p_im[r]
    cols = []
    for _ in range(nsteps):
        cols.append(jnp.stack([qr[0], qi[0], qr[1], qi[1]], axis=1))
        qr, qi = qr * qr - qi * qi, 2.0 * qr * qi
    pw = jnp.stack(cols, axis=-1).reshape(SSM_N_GROUPS, 4 * SSM_STATE, nsteps)
    return w_toep, w_state.astype(BF16), w_carry.astype(BF16), pw


def _front(x, prm):
    n, seq_len, _ = x.shape
    t = n * seq_len
    x2d = x.reshape(t, D_MODEL)
    q, kv = _qkv(x2d, prm['norm1_g'], prm['w_qkv'], *_rope_tables(seq_len), seq_len)
    attn_n = _attention(q, kv, prm['sink'], prm['attn_out_g'], seq_len)

    chunks_per_seq = seq_len // SSM_CHUNK
    nsteps = max(1, (chunks_per_seq - 1).bit_length())
    nc = t // SSM_CHUNK
    lt = min(256, nc)
    ut = _uproj(x2d, prm['norm1_g'], prm['w_u_t'], lt)
    yt = _ssm(ut, prm['w_toep'], prm['w_state'], prm['w_carry'], prm['pw'][:, :, :nsteps], chunks_per_seq)
    ssm_n = _glu(yt, prm['glu_w_t'], prm['glu_b'], prm['ssm_out_g'], lt)
    return _outproj(x2d, attn_n, ssm_n, prm['w_out_a'], prm['w_out_s'], prm['norm2_g'],
                    prm['router_w'], prm['router_b'], prm['tri'])


def _cumsum_small(x):
    n = x.shape[0]
    keep = jnp.arange(n)[None, :] <= jnp.arange(n)[:, None]
    return jnp.sum(jnp.where(keep, x[None, :], 0), axis=1)


def _by_expert(idx, table):
    hit = idx[None] == jnp.arange(N_EXPERTS, dtype=I32)[:, None, None]
    return jnp.sum(jnp.where(hit, table[:, None, None], 0), axis=0)


def kernel(x_prompt, x_sample, norm1_g, w_in, attn_sink, ssm_a_re, ssm_a_im, ssm_log_dt, ssm_b_re, ssm_b_im, ssm_c_re, ssm_c_im, ssm_d, glu_w, glu_b, attn_out_g, ssm_out_g, w_out, norm2_g, router_w, router_b, w_gate_up, b_gate_up, w_down, b_down, final_g):
    assert norm1_g.shape[0] == 1, "single-layer problem"
    l = 0
    xs = [x_prompt, x_sample]
    max_chunks = max(x.shape[1] for x in xs) // SSM_CHUNK
    max_steps = max(1, (max_chunks - 1).bit_length())
    wq, wk, wv, wu = jnp.split(w_in[l], [ATTN_WIDTH, ATTN_WIDTH + KV_WIDTH, ATTN_WIDTH + 2 * KV_WIDTH], axis=1)
    dup = lambda w: jnp.concatenate([w[:, :HEAD_DIM], w[:, :HEAD_DIM], w[:, HEAD_DIM:], w[:, HEAD_DIM:]], axis=1)
    w_toep, w_state, w_carry, pw = _ssm_weights(ssm_a_re[l], ssm_a_im[l], ssm_log_dt[l], ssm_b_re[l], ssm_b_im[l],
                                                ssm_c_re[l], ssm_c_im[l], ssm_d[l], max_steps)
    tri_i = lax.broadcasted_iota(I32, (ROUTE_TM, ROUTE_TM), 0)
    tri_j = lax.broadcasted_iota(I32, (ROUTE_TM, ROUTE_TM), 1)
    prm = dict(
        norm1_g=norm1_g[l].reshape(1, D_MODEL),
        w_qkv=jnp.concatenate([wq, dup(wk), dup(wv)], axis=1).astype(BF16),
        w_u_t=wu.T.astype(BF16),
        sink=attn_sink[l].astype(F32),
        attn_out_g=attn_out_g[l].reshape(1, ATTN_WIDTH),
        w_toep=w_toep, w_state=w_state, w_carry=w_carry, pw=pw,
        glu_w_t=glu_w[l].T.astype(BF16),
        glu_b=glu_b[l].reshape(SSM_WIDTH, 1),
        ssm_out_g=ssm_out_g[l].reshape(SSM_WIDTH, 1),
        w_out_a=w_out[l][:ATTN_WIDTH].astype(BF16),
        w_out_s=w_out[l][ATTN_WIDTH:].astype(BF16),
        norm2_g=norm2_g[l].reshape(1, D_MODEL),
        router_w=router_w[l].T.astype(BF16),
        router_b=router_b[l].reshape(N_EXPERTS, 1),
        tri=(tri_i < tri_j).astype(BF16),
    )
    fronts = [_front(x, prm) for x in xs]

    cnts = [f[4][:, 0].astype(I32) for f in fronts]
    padded = [(c + MOE_ROWS - 1) // MOE_ROWS * MOE_ROWS for c in cnts]
    pends = [_cumsum_small(p) for p in padded]
    pstarts = [pe - p for pe, p in zip(pends, padded)]
    nbs = [f[0].shape[0] * TOP_K // MOE_ROWS + N_EXPERTS for f in fronts]
    seg_blocks = jnp.stack([p // MOE_ROWS for p in padded], axis=1).reshape(-1)
    seg_end = _cumsum_small(seg_blocks)
    seg_start = seg_end - seg_blocks
    nb = sum(nbs)
    bi = jnp.arange(nb, dtype=I32)
    seg = jnp.minimum(jnp.sum(seg_end[None, :] <= bi[:, None], axis=1), 2 * N_EXPERTS - 1).astype(I32)
    block_e = seg // 2
    block_grp = seg % 2
    n_valid = seg_end[-1].astype(I32).reshape(1)
    live = bi < n_valid[0]
    src_start = jnp.stack([ps // MOE_ROWS for ps in pstarts], axis=1).reshape(-1)
    in_seg = seg[:, None] == jnp.arange(2 * N_EXPERTS, dtype=I32)[None, :]
    src_blk = jnp.sum(jnp.where(in_seg, (src_start - seg_start)[None, :], 0), axis=1) + bi
    not_after = bi[None, :] <= bi[:, None]
    blk = [jnp.max(jnp.where(not_after & (live & (block_grp == g))[None, :], src_blk[None, :], 0), axis=1).astype(I32)
           for g in range(2)]

    xbufs, yrows = [], []
    for g, f in enumerate(fronts):
        e_idx = f[2][:TOP_K]
        rank = f[2][TOP_K:]
        n_pad = nbs[g] * MOE_ROWS
        pad_start = jnp.concatenate([pstarts[g] + cnts[g], pends[g][-1:]]).astype(I32)
        pad_cnt = jnp.concatenate([padded[g] - cnts[g], n_pad - pends[g][-1:]]).astype(I32)
        xrows = (_by_expert(e_idx, pstarts[g]) + rank).astype(I32)
        xbufs.append(_scatter_rows(pad_start, pad_cnt, xrows, f[1], n_pad))
        yrows.append((_by_expert(e_idx, seg_start[g::2] * MOE_ROWS) + rank).astype(I32))
    yb = _experts(block_e, block_grp, blk[0], blk[1], n_valid, xbufs[0], xbufs[1],
                  w_gate_up.reshape(N_EXPERTS, D_MODEL, 2 * D_MODEL), b_gate_up[l][:, None, :],
                  w_down.reshape(N_EXPERTS, D_MODEL, D_MODEL), b_down[l][:, None, :])
    gfin = final_g.reshape(1, D_MODEL)
    outs = [_combine(r, f[0], f[3], gfin, yb).reshape(x.shape) for x, f, r in zip(xs, fronts, yrows)]
    return tuple(outs)
```

```python
import functools
import math

import jax
import jax.numpy as jnp
import numpy as np
from jax import lax
from jax.experimental import pallas as pl
from jax.experimental.pallas import tpu as pltpu

F32 = jnp.float32
BF16 = jnp.bfloat16
I32 = jnp.int32

D_MODEL = 1024
HEAD_DIM = 64
N_Q_HEADS = 8
N_KV_HEADS = 2
Q_PER_KV = N_Q_HEADS // N_KV_HEADS
ATTN_WIDTH = N_Q_HEADS * HEAD_DIM
KV_WIDTH = N_KV_HEADS * HEAD_DIM
WINDOW = 128
ATT_BLOCK = 128
ROPE_THETA = 500000.0
ROPE_DIM = HEAD_DIM // 4
ROPE_HALF = ROPE_DIM // 2
SSM_WIDTH = 512
SSM_GROUP = 16
SSM_N_GROUPS = SSM_WIDTH // SSM_GROUP
SSM_STATE = 64
N_EXPERTS = 32
TOP_K = 4
SWIGLU_LIMIT = 7.0
SWIGLU_ALPHA = 1.702
EPS = 1e-5

LANES = 128
SUBLANES = 8
SSM_CHUNK = 32
TOEP = SSM_CHUNK * SSM_GROUP
MOE_ROWS = 512
ROUTE_TM = 512
COMBINE_TM = 128
VMEM_LIMIT = 52 * 1024 * 1024
EXPERT_VMEM_LIMIT = 60 * 1024 * 1024
NEG_BIG = -1e30


def _params(sem, vmem=VMEM_LIMIT):
    return pltpu.CompilerParams(dimension_semantics=sem, vmem_limit_bytes=vmem)


def _rms(x, g):
    ms = jnp.mean(x * x, axis=-1, keepdims=True)
    return x * lax.rsqrt(ms + EPS) * g


def _tile_rows_load(ref, rows):
    return jnp.concatenate([ref[pl.ds(j, rows, stride=SUBLANES), :] for j in range(SUBLANES)], axis=1)


def _tile_rows_store(ref, val):
    rows = val.shape[0]
    for j in range(SUBLANES):
        ref[pl.ds(j, rows, stride=SUBLANES), :] = val[:, j * LANES:(j + 1) * LANES]


def _tile_at(ref, start):
    return ref.at[pl.ds(pl.multiple_of(start, SUBLANES), SUBLANES)]


def _tile_row(ref, row):
    return _tile_at(ref, row * SUBLANES)


KV_COLS = 4 * LANES


def _qkv_kernel(x_ref, g_ref, w_ref, c_ref, s1_ref, s2_ref, q_ref, kv_ref):
    h = _rms(x_ref[...], g_ref[...]).astype(BF16)
    p = jnp.dot(h, w_ref[...], preferred_element_type=F32)
    c = c_ref[...]
    s1 = s1_ref[...]
    s2 = s2_ref[...]

    def rot(t):
        return t * c + pltpu.roll(t, ROPE_HALF, 1) * s1 + pltpu.roll(t, LANES - ROPE_HALF, 1) * s2

    for j in range(ATTN_WIDTH // LANES):
        q_ref[:, j * LANES:(j + 1) * LANES] = (rot(p[:, j * LANES:(j + 1) * LANES]) * (HEAD_DIM ** -0.5)).astype(BF16)
    for j in range(N_KV_HEADS):
        col = ATTN_WIDTH + j * LANES
        kv_ref[:, j * LANES:(j + 1) * LANES] = rot(p[:, col:col + LANES]).astype(BF16)
    kv_ref[:, N_KV_HEADS * LANES:] = p[:, ATTN_WIDTH + N_KV_HEADS * LANES:].astype(BF16)


def _qkv(x2d, g, w, c, s1, s2, seq_len, tm=512):
    t = x2d.shape[0]
    nlb = seq_len // tm
    row = lambda i: (i, 0)
    tab = lambda i: (i % nlb, 0)
    full = lambda i: (0, 0)
    return pl.pallas_call(
        _qkv_kernel,
        grid=(t // tm,),
        in_specs=[pl.BlockSpec((tm, D_MODEL), row), pl.BlockSpec((1, D_MODEL), full),
                  pl.BlockSpec((D_MODEL, ATTN_WIDTH + KV_COLS), full),
                  pl.BlockSpec((tm, LANES), tab), pl.BlockSpec((tm, LANES), tab), pl.BlockSpec((tm, LANES), tab)],
        out_specs=[pl.BlockSpec((tm, ATTN_WIDTH), row), pl.BlockSpec((tm, KV_COLS), row)],
        out_shape=[jax.ShapeDtypeStruct((t, ATTN_WIDTH), BF16), jax.ShapeDtypeStruct((t, KV_COLS), BF16)],
        compiler_params=_params(("parallel",)),
        name="qkv_rotary",
    )(x2d, g, w, c, s1, s2)


def _attn_kernel(sink_ref, q_ref, kvp, kvc, kvn, g_ref, o_ref, *, bps):
    i = pl.program_id(0)
    first = (i % bps) == 0
    last = (i % bps) == bps - 1
    qi = lax.broadcasted_iota(I32, (ATT_BLOCK, 3 * ATT_BLOCK), 0)
    kj = lax.broadcasted_iota(I32, (ATT_BLOCK, 3 * ATT_BLOCK), 1)
    rel = kj - ATT_BLOCK - qi
    valid = (jnp.abs(rel) <= WINDOW)
    valid = valid & ((kj >= ATT_BLOCK) | jnp.logical_not(first))
    valid = valid & ((kj < 2 * ATT_BLOCK) | jnp.logical_not(last))
    kv = jnp.concatenate([kvp[...], kvc[...], kvn[...]], axis=0)
    ks = [kv[:, h * LANES:(h + 1) * LANES] for h in range(N_KV_HEADS)]
    vs = [kv[:, (N_KV_HEADS + h) * LANES:(N_KV_HEADS + h + 1) * LANES] for h in range(N_KV_HEADS)]
    lo = lax.broadcasted_iota(I32, (ATT_BLOCK, LANES), 1) < HEAD_DIM
    zero = jnp.zeros((ATT_BLOCK, LANES), BF16)
    heads = [(j, par) for j in range(ATTN_WIDTH // LANES) for par in range(2)]
    nt_dims = (((1,), (1,)), ((), ()))
    scores = []
    for j, par in heads:
        qt = q_ref[:, j * LANES:(j + 1) * LANES]
        qm = jnp.where(lo if par == 0 else jnp.logical_not(lo), qt, zero)
        scores.append(lax.dot_general(qm, ks[j // 2], nt_dims, preferred_element_type=F32))
    scores = [jnp.where(valid, s, NEG_BIG) for s in scores]
    sinks = [sink_ref[2 * j + par] for j, par in heads]
    maxes = [jnp.maximum(jnp.max(s, axis=-1, keepdims=True), sk) for s, sk in zip(scores, sinks)]
    probs = [jnp.exp(s - m) for s, m in zip(scores, maxes)]
    dens = [jnp.sum(p, axis=-1, keepdims=True) + jnp.exp(sk - m) for p, m, sk in zip(probs, maxes, sinks)]
    outs = [jnp.dot(p.astype(BF16), vs[j // 2], preferred_element_type=F32) for p, (j, par) in zip(probs, heads)]
    outs = [o / d for o, d in zip(outs, dens)]
    tiles = [jnp.where(lo, outs[2 * j], outs[2 * j + 1]) for j in range(ATTN_WIDTH // LANES)]
    o = jnp.concatenate(tiles, axis=1)
    o_ref[...] = _rms(o, g_ref[...]).astype(BF16)


def _attention(q, kv, sink, g, seq_len):
    t = q.shape[0]
    nblk = t // ATT_BLOCK
    bps = seq_len // ATT_BLOCK
    cur = lambda i, s: (i, 0)
    prv = lambda i, s: (jnp.maximum(i - 1, 0), 0)
    nxt = lambda i, s: (jnp.minimum(i + 1, nblk - 1), 0)
    grid_spec = pltpu.PrefetchScalarGridSpec(
        num_scalar_prefetch=1,
        grid=(nblk,),
        in_specs=[pl.BlockSpec((ATT_BLOCK, ATTN_WIDTH), cur),
                  pl.BlockSpec((ATT_BLOCK, KV_COLS), prv), pl.BlockSpec((ATT_BLOCK, KV_COLS), cur),
                  pl.BlockSpec((ATT_BLOCK, KV_COLS), nxt),
                  pl.BlockSpec((1, ATTN_WIDTH), lambda i, s: (0, 0))],
        out_specs=pl.BlockSpec((ATT_BLOCK, ATTN_WIDTH), cur),
    )
    return pl.pallas_call(
        functools.partial(_attn_kernel, bps=bps),
        grid_spec=grid_spec,
        out_shape=jax.ShapeDtypeStruct((t, ATTN_WIDTH), BF16),
        compiler_params=_params(("parallel",)),
        name="banded_attention",
    )(sink, q, kv, kv, kv, g)


def _uproj_kernel(x_ref, g_ref, w_ref, o_ref, stage_ref):
    lt = x_ref.shape[0]
    h = _rms(x_ref[...].reshape(lt * SUBLANES, D_MODEL), g_ref[...])
    for j in range(D_MODEL // LANES):
        stage_ref[j] = h[:, j * LANES:(j + 1) * LANES]
    for bl in range(SUBLANES):
        hb = jnp.concatenate([stage_ref[j, pl.ds(bl, lt, stride=SUBLANES), :] for j in range(D_MODEL // LANES)],
                             axis=1).astype(BF16)
        ut = lax.dot_general(w_ref[...], hb, (((1,), (1,)), ((), ())), preferred_element_type=F32)
        o_ref[bl] = ut.astype(BF16)


def _uproj(x2d, g, w_t, lt):
    t = x2d.shape[0]
    nc = t // SSM_CHUNK
    xv = x2d.reshape(nc, SSM_CHUNK // SUBLANES, SUBLANES, D_MODEL)
    return pl.pallas_call(
        _uproj_kernel,
        grid=(nc // lt, SSM_CHUNK // SUBLANES),
        in_specs=[pl.BlockSpec((lt, None, SUBLANES, D_MODEL), lambda i, b: (i, b, 0, 0)),
                  pl.BlockSpec((1, D_MODEL), lambda i, b: (0, 0)),
                  pl.BlockSpec((SSM_WIDTH, D_MODEL), lambda i, b: (0, 0))],
        out_specs=pl.BlockSpec((SUBLANES, SSM_WIDTH, lt), lambda i, b: (b, 0, i)),
        out_shape=jax.ShapeDtypeStruct((SSM_CHUNK, SSM_WIDTH, nc), BF16),
        scratch_shapes=[pltpu.VMEM((D_MODEL // LANES, lt * SUBLANES, LANES), F32)],
        compiler_params=_params(("parallel", "parallel")),
        name="u_projection",
    )(xv, g, w_t)


def _gelu_tanh(x):
    return 0.5 * x * (1.0 + jnp.tanh(math.sqrt(2.0 / math.pi) * (x + 0.044715 * (x * x * x))))


LAG_STRIP = 8 * LANES


def _toeplitz_from_strip(strip):
    per_tile = LANES // SSM_GROUP
    rolled = [strip if q == 0 else pltpu.roll(strip, LAG_STRIP - q * SSM_GROUP, 1) for q in range(per_tile)]
    blocks = []
    for b in range(SSM_CHUNK):
        m, q = divmod(SSM_CHUNK - 1 - b, per_tile)
        blocks.append(rolled[q][:, m * LANES:m * LANES + TOEP])
    return jnp.concatenate(blocks, axis=0)


def _ssm_kernel(a_ref, wt_ref, ws_ref, wc_ref, pw_ref, y_ref, *, chunks_per_seq, nsteps):
    nc = a_ref.shape[2]
    a = a_ref[...].reshape(TOEP, nc)
    w_toep = _toeplitz_from_strip(wt_ref[0]).astype(BF16)
    y = jnp.dot(w_toep, a, preferred_element_type=F32)
    s = jnp.dot(ws_ref[0], a, preferred_element_type=F32)
    pos = lax.broadcasted_iota(I32, (SSM_STATE, nc), 1) % chunks_per_seq
    carries = []
    for d in range(2):
        hr = s[2 * d * SSM_STATE:(2 * d + 1) * SSM_STATE]
        hi = s[(2 * d + 1) * SSM_STATE:(2 * d + 2) * SSM_STATE]
        for k in range(nsteps):
            sh = 1 << k
            pr = pw_ref[0, 2 * d * SSM_STATE:(2 * d + 1) * SSM_STATE, k:k + 1]
            pi = pw_ref[0, (2 * d + 1) * SSM_STATE:(2 * d + 2) * SSM_STATE, k:k + 1]
            if d == 0:
                ok = pos >= sh
                sr = pltpu.roll(hr, sh, 1)
                si = pltpu.roll(hi, sh, 1)
            else:
                ok = pos < chunks_per_seq - sh
                sr = pltpu.roll(hr, nc - sh, 1)
                si = pltpu.roll(hi, nc - sh, 1)
            hr, hi = (hr + jnp.where(ok, pr * sr - pi * si, 0.0),
                      hi + jnp.where(ok, pr * si + pi * sr, 0.0))
        if d == 0:
            ok = pos >= 1
            cr = pltpu.roll(hr, 1, 1)
            ci = pltpu.roll(hi, 1, 1)
        else:
            ok = pos < chunks_per_seq - 1
            cr = pltpu.roll(hr, nc - 1, 1)
            ci = pltpu.roll(hi, nc - 1, 1)
        carries += [jnp.where(ok, cr, 0.0), jnp.where(ok, ci, 0.0)]
    carry = jnp.concatenate(carries, axis=0).astype(BF16)
    y = y + jnp.dot(wc_ref[0], carry, preferred_element_type=F32)
    y_ref[...] = _gelu_tanh(y).reshape(SSM_CHUNK, SSM_GROUP, nc)


def _ssm(ut, w_toep, w_state, w_carry, pw, chunks_per_seq):
    nc = ut.shape[2]
    nsteps = max(1, (chunks_per_seq - 1).bit_length())
    g3 = lambda g: (g, 0, 0)
    return pl.pallas_call(
        functools.partial(_ssm_kernel, chunks_per_seq=chunks_per_seq, nsteps=nsteps),
        grid=(SSM_N_GROUPS,),
        in_specs=[pl.BlockSpec((SSM_CHUNK, SSM_GROUP, nc), lambda g: (0, g, 0)),
                  pl.BlockSpec((1, SSM_GROUP, LAG_STRIP), g3),
                  pl.BlockSpec((1, 4 * SSM_STATE, TOEP), g3),
                  pl.BlockSpec((1, TOEP, 4 * SSM_STATE), g3),
                  pl.BlockSpec((1, 4 * SSM_STATE, pw.shape[2]), g3)],
        out_specs=pl.BlockSpec((SSM_CHUNK, SSM_GROUP, nc), lambda g: (0, g, 0)),
        out_shape=jax.ShapeDtypeStruct((SSM_CHUNK, SSM_WIDTH, nc), F32),
        compiler_params=_params(("parallel",)),
        name="s5_core",
    )(ut, w_toep, w_state, w_carry, pw)


def _glu_kernel(y_ref, w_ref, b_ref, g_ref, o_ref, stage_ref):
    lt = o_ref.shape[0]
    for bl in range(SUBLANES):
        y = y_ref[bl]
        z = jnp.dot(w_ref[...], y.astype(BF16), preferred_element_type=F32) + b_ref[...]
        s = y * jax.nn.sigmoid(z)
        ms = jnp.mean(s * s, axis=0, keepdims=True)
        sn = (s * lax.rsqrt(ms + EPS) * g_ref[...]).T
        for j in range(SSM_WIDTH // LANES):
            stage_ref[j, pl.ds(bl, lt, stride=SUBLANES), :] = sn[:, j * LANES:(j + 1) * LANES]
    for j in range(SSM_WIDTH // LANES):
        o_ref[:, :, j * LANES:(j + 1) * LANES] = stage_ref[j].reshape(lt, SUBLANES, LANES)


def _glu(yt, w_t, b_col, g_col, lt):
    nc = yt.shape[2]
    out = pl.pallas_call(
        _glu_kernel,
        grid=(nc // lt, SSM_CHUNK // SUBLANES),
        in_specs=[pl.BlockSpec((SUBLANES, SSM_WIDTH, lt), lambda i, b: (b, 0, i)),
                  pl.BlockSpec((SSM_WIDTH, SSM_WIDTH), lambda i, b: (0, 0)),
                  pl.BlockSpec((SSM_WIDTH, 1), lambda i, b: (0, 0)),
                  pl.BlockSpec((SSM_WIDTH, 1), lambda i, b: (0, 0))],
        out_specs=pl.BlockSpec((lt, None, SUBLANES, SSM_WIDTH), lambda i, b: (i, b, 0, 0)),
        out_shape=jax.ShapeDtypeStruct((nc, SSM_CHUNK // SUBLANES, SUBLANES, SSM_WIDTH), F32),
        scratch_shapes=[pltpu.VMEM((SSM_WIDTH // LANES, lt * SUBLANES, LANES), F32)],
        compiler_params=_params(("parallel", "parallel")),
        name="glu_norm",
    )(yt, w_t, b_col, g_col)
    return out.reshape(nc * SSM_CHUNK, SSM_WIDTH)


def _outproj_kernel(x_ref, a_ref, s_ref, wa_ref, ws_ref, g_ref, wr_ref, br_ref, tri_ref,
                    x2_ref, h2_ref, rt_ref, gt_ref, cnt_ref, run_ref):
    i = pl.program_id(0)

    @pl.when(i == 0)
    def _():
        run_ref[...] = jnp.zeros_like(run_ref)

    x2 = (x_ref[...] + jnp.dot(a_ref[...], wa_ref[...], preferred_element_type=F32)
          + jnp.dot(s_ref[...].astype(BF16), ws_ref[...], preferred_element_type=F32))
    x2_ref[...] = x2
    h2 = _rms(x2, g_ref[...])
    _tile_rows_store(h2_ref, h2)
    logits = lax.dot_general(wr_ref[...], h2.astype(BF16), (((1,), (1,)), ((), ())),
                             preferred_element_type=F32) + br_ref[...]
    tm = logits.shape[1]
    sub = lax.broadcasted_iota(I32, (N_EXPERTS, tm), 0)
    sub_f = sub.astype(F32)
    work = logits
    sel = jnp.zeros((N_EXPERTS, tm), F32)
    top_v, top_i = [], []
    for _ in range(TOP_K):
        m = jnp.max(work, axis=0, keepdims=True)
        idx = jnp.min(jnp.where(work == m, sub_f, float(N_EXPERTS)), axis=0, keepdims=True).astype(I32)
        hit = sub == idx
        sel = jnp.where(hit, 1.0, sel)
        work = jnp.where(hit, -jnp.inf, work)
        top_v.append(m)
        top_i.append(idx)
    ex = [jnp.exp(v - top_v[0]) for v in top_v]
    den = ex[0] + ex[1] + ex[2] + ex[3]
    before = jnp.dot(sel.astype(BF16), tri_ref[...], preferred_element_type=F32) + run_ref[:, 0:1]
    ranks = [jnp.sum(jnp.where(sub == top_i[k], before, 0.0), axis=0, keepdims=True).astype(I32) for k in range(TOP_K)]
    rt_ref[...] = jnp.concatenate(top_i + ranks, axis=0)
    gt_ref[...] = jnp.concatenate([e / den for e in ex] + [jnp.zeros((SUBLANES - TOP_K, tm), F32)], axis=0)
    run = run_ref[...] + jnp.sum(sel, axis=1, keepdims=True)
    run_ref[...] = run
    cnt_ref[...] = run


def _outproj(x2d, attn_n, ssm_n, w_a, w_s, g, w_r, b_r, tri):
    t = x2d.shape[0]
    tm = ROUTE_TM
    row = lambda i: (i, 0)
    full = lambda i: (0, 0)
    return pl.pallas_call(
        _outproj_kernel,
        grid=(t // tm,),
        in_specs=[pl.BlockSpec((tm, D_MODEL), row), pl.BlockSpec((tm, ATTN_WIDTH), row),
                  pl.BlockSpec((tm, SSM_WIDTH), row),
                  pl.BlockSpec((ATTN_WIDTH, D_MODEL), full), pl.BlockSpec((SSM_WIDTH, D_MODEL), full),
                  pl.BlockSpec((1, D_MODEL), full),
                  pl.BlockSpec((N_EXPERTS, D_MODEL), full), pl.BlockSpec((N_EXPERTS, 1), full),
                  pl.BlockSpec((tm, tm), full)],
        out_specs=[pl.BlockSpec((tm, D_MODEL), row), pl.BlockSpec((tm * SUBLANES, LANES), row),
                   pl.BlockSpec((2 * TOP_K, tm), lambda i: (0, i)), pl.BlockSpec((SUBLANES, tm), lambda i: (0, i)),
                   pl.BlockSpec((N_EXPERTS, LANES), full)],
        out_shape=[jax.ShapeDtypeStruct((t, D_MODEL), F32), jax.ShapeDtypeStruct((t * SUBLANES, LANES), F32),
                   jax.ShapeDtypeStruct((2 * TOP_K, t), I32), jax.ShapeDtypeStruct((SUBLANES, t), F32),
                   jax.ShapeDtypeStruct((N_EXPERTS, LANES), F32)],
        scratch_shapes=[pltpu.VMEM((N_EXPERTS, LANES), F32)],
        compiler_params=_params(("arbitrary",)),
        name="out_projection_router",
    )(x2d, attn_n, ssm_n, w_a, w_s, g, w_r, b_r, tri)


DMA_GROUP = 4


def _rows_per_step(rows, tokens):
    nt = rows.shape[1] // tokens
    return (rows * SUBLANES).reshape(TOP_K, nt, tokens).transpose(1, 0, 2).reshape(nt, 1, TOP_K * tokens)


def _scatter_kernel(pad_start_ref, pad_cnt_ref, rows_ref, h_ref, xb_hbm, zero_ref, stage_ref, sems, zsem, *, tm):
    i = pl.program_id(0)
    nt = pl.num_programs(0)
    n_dma = tm * TOP_K

    def tile_wait(slot):
        whole = xb_hbm.at[pl.ds(0, n_dma * SUBLANES)]
        pltpu.make_async_copy(whole, whole, sems.at[slot]).wait()

    @pl.when(i == 0)
    def _():
        zero_ref[...] = jnp.zeros_like(zero_ref)

        def fill(row, n_rows):
            return pltpu.make_async_copy(zero_ref.at[pl.ds(0, n_rows * SUBLANES)],
                                         xb_hbm.at[pl.ds(pl.multiple_of(row * SUBLANES, SUBLANES), n_rows * SUBLANES)],
                                         zsem)

        for phase in range(2):
            def per_expert(e, c):
                cnt = pad_cnt_ref[e]
                row = pad_start_ref[e]
                run = MOE_ROWS // 2
                while run >= 1:
                    below = cnt & ~(2 * run - 1)

                    @pl.when((cnt & run) != 0)
                    def _(run=run, below=below):
                        cp = fill(row + below, run)
                        cp.start() if phase == 0 else cp.wait()
                    run //= 2
                return c
            lax.fori_loop(0, N_EXPERTS, per_expert, 0)

            def per_tail_block(b, c):
                cp = fill(pad_start_ref[N_EXPERTS] + b * MOE_ROWS, MOE_ROWS)
                cp.start() if phase == 0 else cp.wait()
                return c
            lax.fori_loop(0, pad_cnt_ref[N_EXPERTS] // MOE_ROWS, per_tail_block, 0)

    for slot in range(2):
        @pl.when(i > 0)
        def _(slot=slot):
            tile_wait(slot)

        stage_ref[slot] = h_ref[pl.ds(slot * tm * SUBLANES, tm * SUBLANES), :]

        def issue(grp, c, slot=slot):
            r0 = grp * DMA_GROUP
            dst = [rows_ref[0, 0, k * (2 * tm) + slot * tm + r0 + r] for r in range(DMA_GROUP) for k in range(TOP_K)]
            for r in range(DMA_GROUP):
                src = _tile_row(stage_ref.at[slot], r0 + r)
                for k in range(TOP_K):
                    pltpu.make_async_copy(src, _tile_at(xb_hbm, dst[r * TOP_K + k]),
                                          sems.at[slot]).start(priority=k % 2)
            return c
        lax.fori_loop(0, tm // DMA_GROUP, issue, 0)

    @pl.when(i == nt - 1)
    def _():
        tile_wait(0)
        tile_wait(1)


def _scatter_rows(pad_start, pad_cnt, rows, h2, n_pad):
    tm = ROUTE_TM
    nt = h2.shape[0] // SUBLANES // (2 * tm)
    grid_spec = pltpu.PrefetchScalarGridSpec(
        num_scalar_prefetch=2,
        grid=(nt,),
        in_specs=[pl.BlockSpec((1, 1, 2 * tm * TOP_K), lambda i, a, b: (i, 0, 0), memory_space=pltpu.SMEM),
                  pl.BlockSpec((2 * tm * SUBLANES, LANES), lambda i, a, b: (i, 0))],
        out_specs=pl.BlockSpec(memory_space=pl.ANY),
        scratch_shapes=[pltpu.VMEM((MOE_ROWS * SUBLANES, LANES), F32),
                        pltpu.VMEM((2, tm * SUBLANES, LANES), F32),
                        pltpu.SemaphoreType.DMA((2,)), pltpu.SemaphoreType.DMA(())],
    )
    return pl.pallas_call(
        functools.partial(_scatter_kernel, tm=tm),
        grid_spec=grid_spec,
        out_shape=jax.ShapeDtypeStruct((n_pad * SUBLANES, LANES), F32),
        compiler_params=_params(("arbitrary",)),
        name="scatter_rows",
    )(pad_start, pad_cnt, _rows_per_step(rows, 2 * tm), h2)


def _expert_kernel(be_ref, grp_ref, ia_ref, ib_ref, nv_ref, xa_ref, xb_ref, wgu_ref, bgu_ref, wd_ref, bd_ref,
                   o_ref, x_scr, wgu_scr, wd_scr):
    i = pl.program_id(0)
    live = i < nv_ref[0]

    @pl.when(live & ((i == 0) | (be_ref[i] != be_ref[jnp.maximum(i - 1, 0)])))
    def _():
        wgu_scr[...] = wgu_ref[0].astype(BF16)
        wd_scr[...] = wd_ref[0].astype(BF16)

    @pl.when(live & (grp_ref[i] == 0))
    def _():
        x_scr[...] = _tile_rows_load(xa_ref, MOE_ROWS).astype(BF16)

    @pl.when(live & (grp_ref[i] != 0))
    def _():
        x_scr[...] = _tile_rows_load(xb_ref, MOE_ROWS).astype(BF16)

    @pl.when(live)
    def _():
        gu = jnp.dot(x_scr[...], wgu_scr[...], preferred_element_type=F32) + bgu_ref[0]
        gate = jnp.minimum(gu[:, :D_MODEL], SWIGLU_LIMIT)
        up = jnp.clip(gu[:, D_MODEL:], -SWIGLU_LIMIT, SWIGLU_LIMIT)
        act = (up + 1.0) * (gate * jax.nn.sigmoid(SWIGLU_ALPHA * gate))
        _tile_rows_store(o_ref, jnp.dot(act.astype(BF16), wd_scr[...], preferred_element_type=F32) + bd_ref[0])

    @pl.when(i >= nv_ref[0])
    def _():
        o_ref[...] = jnp.zeros_like(o_ref)


def _experts(block_e, block_grp, blk_a, blk_b, n_valid, xa, xb, wgu, bgu, wd, bd):
    nb = block_e.shape[0]
    e3 = lambda i, be, gr, ia, ib, nv: (be[i], 0, 0)
    grid_spec = pltpu.PrefetchScalarGridSpec(
        num_scalar_prefetch=5,
        grid=(nb,),
        in_specs=[pl.BlockSpec((MOE_ROWS * SUBLANES, LANES), lambda i, be, gr, ia, ib, nv: (ia[i], 0)),
                  pl.BlockSpec((MOE_ROWS * SUBLANES, LANES), lambda i, be, gr, ia, ib, nv: (ib[i], 0)),
                  pl.BlockSpec((1, D_MODEL, 2 * D_MODEL), e3), pl.BlockSpec((1, 1, 2 * D_MODEL), e3),
                  pl.BlockSpec((1, D_MODEL, D_MODEL), e3), pl.BlockSpec((1, 1, D_MODEL), e3)],
        out_specs=pl.BlockSpec((MOE_ROWS * SUBLANES, LANES), lambda i, be, gr, ia, ib, nv: (i, 0)),
        scratch_shapes=[pltpu.VMEM((MOE_ROWS, D_MODEL), BF16), pltpu.VMEM((D_MODEL, 2 * D_MODEL), BF16),
                        pltpu.VMEM((D_MODEL, D_MODEL), BF16)],
    )
    return pl.pallas_call(
        _expert_kernel,
        grid_spec=grid_spec,
        out_shape=jax.ShapeDtypeStruct((nb * MOE_ROWS * SUBLANES, LANES), F32),
        compiler_params=_params(("arbitrary",), vmem=EXPERT_VMEM_LIMIT),
        name="routed_experts",
    )(block_e, block_grp, blk_a, blk_b, n_valid, xa, xb, wgu, bgu, wd, bd)


GATHER_PITCH = 12


def _combine_kernel(rows_cur, rows_nxt, x_ref, gt_ref, g_ref, y_hbm, o_ref, buf, sems, *, tm):
    i = pl.program_id(0)
    nt = pl.num_programs(0)
    units = SUBLANES * TOP_K
    per_unit = tm // units

    def tile_wait(slot):
        whole = y_hbm.at[pl.ds(0, TOP_K * tm * SUBLANES)]
        pltpu.make_async_copy(whole, whole, sems.at[slot]).wait()

    def start_rows(rows_ref, half, slot, r_lo, r_hi):
        idx = [rows_ref[0, 0, k * (2 * tm) + half * tm + r] for r in range(r_lo, r_hi) for k in range(TOP_K)]
        for n, (r, k) in enumerate((r, k) for r in range(r_lo, r_hi) for k in range(TOP_K)):
            dst = buf.at[slot, k].at[pl.ds(r * GATHER_PITCH, SUBLANES)]
            pltpu.make_async_copy(_tile_at(y_hbm, idx[n]), dst, sems.at[slot]).start(priority=k % 2)

    def finish(half, rows_ref, next_half):
        slot = half
        tile_wait(slot)
        rows = pl.ds(half * tm, tm)
        g8 = gt_ref[:, half * tm:(half + 1) * tm]
        gt = jnp.concatenate([g8, jnp.zeros((tm - SUBLANES, tm), F32)], axis=0).T
        gk = [gt[:, k:k + 1] for k in range(TOP_K)]
        ss = jnp.zeros((tm, 1), F32)
        for j in range(SUBLANES):
            cols = slice(j * LANES, (j + 1) * LANES)
            a = x_ref[rows, cols]
            for k in range(TOP_K):
                u = j * TOP_K + k
                start_rows(rows_ref, next_half, 1 - slot, u * per_unit, (u + 1) * per_unit)
                a = a + buf[slot, k, pl.ds(j, tm, stride=GATHER_PITCH), :] * gk[k]
            ss = ss + jnp.sum(a * a, axis=-1, keepdims=True)
            o_ref[rows, cols] = a
        scale = lax.rsqrt(ss * (1.0 / D_MODEL) + EPS)
        o_ref[rows, :] = o_ref[rows, :] * scale * g_ref[...]

    @pl.when(i == 0)
    def _():
        start_rows(rows_cur, 0, 0, 0, tm)

    finish(0, rows_cur, 1)
    finish(1, rows_nxt, 0)

    @pl.when(i == nt - 1)
    def _():
        tile_wait(0)


def _combine(rows, x2, gates, g, yb):
    t = x2.shape[0]
    tm = COMBINE_TM
    nt = t // (2 * tm)
    rows3 = _rows_per_step(rows, 2 * tm)
    row = lambda i: (i, 0)
    return pl.pallas_call(
        functools.partial(_combine_kernel, tm=tm),
        grid=(nt,),
        in_specs=[pl.BlockSpec((1, 1, 2 * tm * TOP_K), lambda i: (i, 0, 0), memory_space=pltpu.SMEM),
                  pl.BlockSpec((1, 1, 2 * tm * TOP_K), lambda i: (jnp.minimum(i + 1, nt - 1), 0, 0),
                               memory_space=pltpu.SMEM),
                  pl.BlockSpec((2 * tm, D_MODEL), row), pl.BlockSpec((SUBLANES, 2 * tm), lambda i: (0, i)),
                  pl.BlockSpec((1, D_MODEL), lambda i: (0, 0)),
                  pl.BlockSpec(memory_space=pl.ANY)],
        out_specs=pl.BlockSpec((2 * tm, D_MODEL), row),
        out_shape=jax.ShapeDtypeStruct((t, D_MODEL), F32),
        scratch_shapes=[pltpu.VMEM((2, TOP_K, tm * GATHER_PITCH, LANES), F32), pltpu.SemaphoreType.DMA((2,))],
        compiler_params=_params(("arbitrary",)),
        name="combine_final_norm",
    )(rows3, rows3, x2, gates, g, yb)


def _rope_tables(seq_len):
    inv_freq = ROPE_THETA ** (-np.arange(ROPE_HALF, dtype=np.float64) * 2.0 / ROPE_DIM)
    ang = np.arange(seq_len, dtype=np.float64)[:, None] * inv_freq[None, :]
    cos = np.cos(ang).astype(np.float32)
    sin = np.sin(ang).astype(np.float32)
    pad = HEAD_DIM - ROPE_DIM
    ones = np.ones((seq_len, pad), np.float32)
    zer_h = np.zeros((seq_len, ROPE_HALF), np.float32)
    zer_p = np.zeros((seq_len, pad), np.float32)
    c = np.concatenate([cos, cos, ones], axis=1)
    s1 = np.concatenate([zer_h, sin, zer_p], axis=1)
    s2 = np.concatenate([-sin, zer_h, zer_p], axis=1)
    rep = LANES // HEAD_DIM
    return tuple(jnp.asarray(np.tile(t, (1, rep))) for t in (c, s1, s2))


def _ssm_weights(a_re, a_im, log_dt, b_re, b_im, c_re, c_im, ssm_d, nsteps):
    r = SSM_CHUNK
    dt = jnp.exp(log_dt)[..., None]
    lr, li = a_re * dt, a_im * dt

    def cpow(n):
        n = jnp.asarray(n, F32)[..., None, None, None]
        mag = jnp.exp(n * lr)
        return mag * jnp.cos(n * li), mag * jnp.sin(n * li)

    ab_re, ab_im = cpow(jnp.ones(()))
    den = a_re * a_re + a_im * a_im
    num_re, num_im = ab_re - 1.0, ab_im
    f_re = (num_re * a_re + num_im * a_im) / den
    f_im = (num_im * a_re - num_re * a_im) / den
    bb_re = f_re[..., None] * b_re - f_im[..., None] * b_im
    bb_im = f_re[..., None] * b_im + f_im[..., None] * b_re

    taus = jnp.arange(r + 1, dtype=F32)
    p_re, p_im = cpow(taus)
    m_re = p_re[..., None] * bb_re - p_im[..., None] * bb_im
    m_im = p_re[..., None] * bb_im + p_im[..., None] * bb_re
    kern = (jnp.einsum('dgcp,tdgpk->tdgck', c_re, m_re) - jnp.einsum('dgcp,tdgpk->tdgck', c_im, m_im))
    center = kern[0, 0] + kern[0, 1] + jnp.eye(SSM_GROUP, dtype=F32) * ssm_d[:, :, None]
    lags = jnp.concatenate([kern[r - 1:0:-1, 0], center[None], kern[1:r, 1]], axis=0)
    strip = lags.transpose(1, 2, 0, 3).reshape(SSM_N_GROUPS, SSM_GROUP, (2 * r - 1) * SSM_GROUP)
    w_toep = jnp.pad(strip, ((0, 0), (0, 0), (0, LAG_STRIP - (2 * r - 1) * SSM_GROUP)))

    st_f_re, st_f_im = m_re[r - 1::-1, 0], m_im[r - 1::-1, 0]
    st_b_re, st_b_im = m_re[:r, 1], m_im[:r, 1]
    w_state = jnp.stack([st_f_re, st_f_im, st_b_re, st_b_im], axis=0)
    w_state = w_state.transpose(2, 0, 3, 1, 4).reshape(SSM_N_GROUPS, 4 * SSM_STATE, TOEP)

    def c_times_pow(d, pr, pi):
        zr = c_re[d][None] * pr[:, :, None, :] - c_im[d][None] * pi[:, :, None, :]
        zi = c_re[d][None] * pi[:, :, None, :] + c_im[d][None] * pr[:, :, None, :]
        return zr, -zi
    cf_re, cf_im = c_times_pow(0, p_re[1:r + 1, 0], p_im[1:r + 1, 0])
    cb_re, cb_im = c_times_pow(1, p_re[r:0:-1, 1], p_im[r:0:-1, 1])
    w_carry = jnp.stack([cf_re, cf_im, cb_re, cb_im], axis=3)
    w_carry = w_carry.transpose(1, 0, 2, 3, 4).reshape(SSM_N_GROUPS, TOEP, 4 * SSM_STATE)

    qr, qi = p_re[r], p_im[r]
    cols = []
    for _ in range(nsteps):
        cols.append(jnp.stack([qr[0], qi[0], qr[1], qi[1]], axis=1))
        qr, qi = qr * qr - qi * qi, 2.0 * qr * qi
    pw = jnp.stack(cols, axis=-1).reshape(SSM_N_GROUPS, 4 * SSM_STATE, nsteps)
    return w_toep, w_state.astype(BF16), w_carry.astype(BF16), pw


def _front(x, prm):
    n, seq_len, _ = x.shape
    t = n * seq_len
    x2d = x.reshape(t, D_MODEL)
    q, kv = _qkv(x2d, prm['norm1_g'], prm['w_qkv'], *_rope_tables(seq_len), seq_len)
    attn_n = _attention(q, kv, prm['sink'], prm['attn_out_g'], seq_len)

    chunks_per_seq = seq_len // SSM_CHUNK
    nsteps = max(1, (chunks_per_seq - 1).bit_length())
    nc = t // SSM_CHUNK
    lt = min(256, nc)
    ut = _uproj(x2d, prm['norm1_g'], prm['w_u_t'], lt)
    yt = _ssm(ut, prm['w_toep'], prm['w_state'], prm['w_carry'], prm['pw'][:, :, :nsteps], chunks_per_seq)
    ssm_n = _glu(yt, prm['glu_w_t'], prm['glu_b'], prm['ssm_out_g'], lt)
    return _outproj(x2d, attn_n, ssm_n, prm['w_out_a'], prm['w_out_s'], prm['norm2_g'],
                    prm['router_w'], prm['router_b'], prm['tri'])


def _cumsum_small(x):
    n = x.shape[0]
    keep = jnp.arange(n)[None, :] <= jnp.arange(n)[:, None]
    return jnp.sum(jnp.where(keep, x[None, :], 0), axis=1)


def _by_expert(idx, table):
    hit = idx[None] == jnp.arange(N_EXPERTS, dtype=I32)[:, None, None]
    return jnp.sum(jnp.where(hit, table[:, None, None], 0), axis=0)


def kernel(x_prompt, x_sample, norm1_g, w_in, attn_sink, ssm_a_re, ssm_a_im, ssm_log_dt, ssm_b_re, ssm_b_im, ssm_c_re, ssm_c_im, ssm_d, glu_w, glu_b, attn_out_g, ssm_out_g, w_out, norm2_g, router_w, router_b, w_gate_up, b_gate_up, w_down, b_down, final_g):
    assert norm1_g.shape[0] == 1, "single-layer problem"
    l = 0
    xs = [x_prompt, x_sample]
    max_chunks = max(x.shape[1] for x in xs) // SSM_CHUNK
    max_steps = max(1, (max_chunks - 1).bit_length())
    wq, wk, wv, wu = jnp.split(w_in[l], [ATTN_WIDTH, ATTN_WIDTH + KV_WIDTH, ATTN_WIDTH + 2 * KV_WIDTH], axis=1)
    dup = lambda w: jnp.concatenate([w[:, :HEAD_DIM], w[:, :HEAD_DIM], w[:, HEAD_DIM:], w[:, HEAD_DIM:]], axis=1)
    w_toep, w_state, w_carry, pw = _ssm_weights(ssm_a_re[l], ssm_a_im[l], ssm_log_dt[l], ssm_b_re[l], ssm_b_im[l],
                                                ssm_c_re[l], ssm_c_im[l], ssm_d[l], max_steps)
    tri_i = lax.broadcasted_iota(I32, (ROUTE_TM, ROUTE_TM), 0)
    tri_j = lax.broadcasted_iota(I32, (ROUTE_TM, ROUTE_TM), 1)
    prm = dict(
        norm1_g=norm1_g[l].reshape(1, D_MODEL),
        w_qkv=jnp.concatenate([wq, dup(wk), dup(wv)], axis=1).astype(BF16),
        w_u_t=wu.T.astype(BF16),
        sink=attn_sink[l].astype(F32),
        attn_out_g=attn_out_g[l].reshape(1, ATTN_WIDTH),
        w_toep=w_toep, w_state=w_state, w_carry=w_carry, pw=pw,
        glu_w_t=glu_w[l].T.astype(BF16),
        glu_b=glu_b[l].reshape(SSM_WIDTH, 1),
        ssm_out_g=ssm_out_g[l].reshape(SSM_WIDTH, 1),
        w_out_a=w_out[l][:ATTN_WIDTH].astype(BF16),
        w_out_s=w_out[l][ATTN_WIDTH:].astype(BF16),
        norm2_g=norm2_g[l].reshape(1, D_MODEL),
        router_w=router_w[l].T.astype(BF16),
        router_b=router_b[l].reshape(N_EXPERTS, 1),
        tri=(tri_i < tri_j).astype(BF16),
    )
    fronts = [_front(x, prm) for x in xs]

    cnts = [f[4][:, 0].astype(I32) for f in fronts]
    padded = [(c + MOE_ROWS - 1) // MOE_ROWS * MOE_ROWS for c in cnts]
    pends = [_cumsum_small(p) for p in padded]
    pstarts = [pe - p for pe, p in zip(pends, padded)]
    nbs = [f[0].shape[0] * TOP_K // MOE_ROWS + N_EXPERTS for f in fronts]
    seg_blocks = jnp.stack([p // MOE_ROWS for p in padded], axis=1).reshape(-1)
    seg_end = _cumsum_small(seg_blocks)
    seg_start = seg_end - seg_blocks
    nb = sum(nbs)
    bi = jnp.arange(nb, dtype=I32)
    seg = jnp.minimum(jnp.sum(seg_end[None, :] <= bi[:, None], axis=1), 2 * N_EXPERTS - 1).astype(I32)
    block_e = seg // 2
    block_grp = seg % 2
    n_valid = seg_end[-1].astype(I32).reshape(1)
    live = bi < n_valid[0]
    src_start = jnp.stack([ps // MOE_ROWS for ps in pstarts], axis=1).reshape(-1)
    in_seg = seg[:, None] == jnp.arange(2 * N_EXPERTS, dtype=I32)[None, :]
    src_blk = jnp.sum(jnp.where(in_seg, (src_start - seg_start)[None, :], 0), axis=1) + bi
    not_after = bi[None, :] <= bi[:, None]
    blk = [jnp.max(jnp.where(not_after & (live & (block_grp == g))[None, :], src_blk[None, :], 0), axis=1).astype(I32)
           for g in range(2)]

    xbufs, yrows = [], []
    for g, f in enumerate(fronts):
        e_idx = f[2][:TOP_K]
        rank = f[2][TOP_K:]
        n_pad = nbs[g] * MOE_ROWS
        pad_start = jnp.concatenate([pstarts[g] + cnts[g], pends[g][-1:]]).astype(I32)
        pad_cnt = jnp.concatenate([padded[g] - cnts[g], n_pad - pends[g][-1:]]).astype(I32)
        xrows = (_by_expert(e_idx, pstarts[g]) + rank).astype(I32)
        xbufs.append(_scatter_rows(pad_start, pad_cnt, xrows, f[1], n_pad))
        yrows.append((_by_expert(e_idx, seg_start[g::2] * MOE_ROWS) + rank).astype(I32))
    yb = _experts(block_e, block_grp, blk[0], blk[1], n_valid, xbufs[0], xbufs[1],
                  w_gate_up.reshape(N_EXPERTS, D_MODEL, 2 * D_MODEL), b_gate_up[l][:, None, :],
                  w_down.reshape(N_EXPERTS, D_MODEL, D_MODEL), b_down[l][:, None, :])
    gfin = final_g.reshape(1, D_MODEL)
    outs = [_combine(r, f[0], f[3], gfin, yb).reshape(x.shape) for x, f, r in zip(xs, fronts, yrows)]
    return tuple(outs)
```

```python
import functools
import math

import jax
import jax.numpy as jnp
import numpy as np
from jax import lax
from jax.experimental import pallas as pl
from jax.experimental.pallas import tpu as pltpu

F32 = jnp.float32
BF16 = jnp.bfloat16
I32 = jnp.int32

D_MODEL = 1024
HEAD_DIM = 64
N_Q_HEADS = 8
N_KV_HEADS = 2
Q_PER_KV = N_Q_HEADS // N_KV_HEADS
ATTN_WIDTH = N_Q_HEADS * HEAD_DIM
KV_WIDTH = N_KV_HEADS * HEAD_DIM
WINDOW = 128
ATT_BLOCK = 128
ROPE_THETA = 500000.0
ROPE_DIM = HEAD_DIM // 4
ROPE_HALF = ROPE_DIM // 2
SSM_WIDTH = 512
SSM_GROUP = 16
SSM_N_GROUPS = SSM_WIDTH // SSM_GROUP
SSM_STATE = 64
N_EXPERTS = 32
TOP_K = 4
SWIGLU_LIMIT = 7.0
SWIGLU_ALPHA = 1.702
EPS = 1e-5

LANES = 128
SUBLANES = 8
SSM_CHUNK = 32
TOEP = SSM_CHUNK * SSM_GROUP
MOE_ROWS = 512
ROUTE_TM = 512
COMBINE_TM = 128
COMBINE_TILES = 4
VMEM_LIMIT = 52 * 1024 * 1024
EXPERT_VMEM_LIMIT = 60 * 1024 * 1024
NEG_BIG = -1e30


def _params(sem, vmem=VMEM_LIMIT):
    return pltpu.CompilerParams(dimension_semantics=sem, vmem_limit_bytes=vmem)


def _rms(x, g):
    ms = jnp.mean(x * x, axis=-1, keepdims=True)
    return x * lax.rsqrt(ms + EPS) * g


def _tile_rows_load(ref, rows):
    return jnp.concatenate([ref[pl.ds(j, rows, stride=SUBLANES), :] for j in range(SUBLANES)], axis=1)


def _tile_rows_store(ref, val):
    rows = val.shape[0]
    for j in range(SUBLANES):
        ref[pl.ds(j, rows, stride=SUBLANES), :] = val[:, j * LANES:(j + 1) * LANES]


def _tile_at(ref, start):
    return ref.at[pl.ds(pl.multiple_of(start, SUBLANES), SUBLANES)]


def _tile_row(ref, row):
    return _tile_at(ref, row * SUBLANES)


KV_COLS = 4 * LANES


def _qkv_kernel(x_ref, g_ref, w_ref, c_ref, s1_ref, s2_ref, q_ref, kv_ref):
    h = _rms(x_ref[...], g_ref[...]).astype(BF16)
    p = jnp.dot(h, w_ref[...], preferred_element_type=F32)
    c = c_ref[...]
    s1 = s1_ref[...]
    s2 = s2_ref[...]

    def rot(t):
        return t * c + pltpu.roll(t, ROPE_HALF, 1) * s1 + pltpu.roll(t, LANES - ROPE_HALF, 1) * s2

    for j in range(ATTN_WIDTH // LANES):
        q_ref[:, j * LANES:(j + 1) * LANES] = (rot(p[:, j * LANES:(j + 1) * LANES]) * (HEAD_DIM ** -0.5)).astype(BF16)
    for j in range(N_KV_HEADS):
        col = ATTN_WIDTH + j * LANES
        kv_ref[:, j * LANES:(j + 1) * LANES] = rot(p[:, col:col + LANES]).astype(BF16)
    kv_ref[:, N_KV_HEADS * LANES:] = p[:, ATTN_WIDTH + N_KV_HEADS * LANES:].astype(BF16)


def _qkv(x2d, g, w, c, s1, s2, seq_len, tm=512):
    t = x2d.shape[0]
    nlb = seq_len // tm
    row = lambda i: (i, 0)
    tab = lambda i: (i % nlb, 0)
    full = lambda i: (0, 0)
    return pl.pallas_call(
        _qkv_kernel,
        grid=(t // tm,),
        in_specs=[pl.BlockSpec((tm, D_MODEL), row), pl.BlockSpec((1, D_MODEL), full),
                  pl.BlockSpec((D_MODEL, ATTN_WIDTH + KV_COLS), full),
                  pl.BlockSpec((tm, LANES), tab), pl.BlockSpec((tm, LANES), tab), pl.BlockSpec((tm, LANES), tab)],
        out_specs=[pl.BlockSpec((tm, ATTN_WIDTH), row), pl.BlockSpec((tm, KV_COLS), row)],
        out_shape=[jax.ShapeDtypeStruct((t, ATTN_WIDTH), BF16), jax.ShapeDtypeStruct((t, KV_COLS), BF16)],
        compiler_params=_params(("parallel",)),
        name="qkv_rotary",
    )(x2d, g, w, c, s1, s2)


def _attn_kernel(sink_ref, q_ref, kvp, kvc, kvn, g_ref, o_ref, *, bps):
    i = pl.program_id(0)
    first = (i % bps) == 0
    last = (i % bps) == bps - 1
    qi = lax.broadcasted_iota(I32, (ATT_BLOCK, 3 * ATT_BLOCK), 0)
    kj = lax.broadcasted_iota(I32, (ATT_BLOCK, 3 * ATT_BLOCK), 1)
    rel = kj - ATT_BLOCK - qi
    valid = (jnp.abs(rel) <= WINDOW)
    valid = valid & ((kj >= ATT_BLOCK) | jnp.logical_not(first))
    valid = valid & ((kj < 2 * ATT_BLOCK) | jnp.logical_not(last))
    kv = jnp.concatenate([kvp[...], kvc[...], kvn[...]], axis=0)
    ks = [kv[:, h * LANES:(h + 1) * LANES] for h in range(N_KV_HEADS)]
    vs = [kv[:, (N_KV_HEADS + h) * LANES:(N_KV_HEADS + h + 1) * LANES] for h in range(N_KV_HEADS)]
    lo = lax.broadcasted_iota(I32, (ATT_BLOCK, LANES), 1) < HEAD_DIM
    zero = jnp.zeros((ATT_BLOCK, LANES), BF16)
    heads = [(j, par) for j in range(ATTN_WIDTH // LANES) for par in range(2)]
    nt_dims = (((1,), (1,)), ((), ()))
    scores = []
    for j, par in heads:
        qt = q_ref[:, j * LANES:(j + 1) * LANES]
        qm = jnp.where(lo if par == 0 else jnp.logical_not(lo), qt, zero)
        scores.append(lax.dot_general(qm, ks[j // 2], nt_dims, preferred_element_type=F32))
    scores = [jnp.where(valid, s, NEG_BIG) for s in scores]
    sinks = [sink_ref[2 * j + par] for j, par in heads]
    maxes = [jnp.maximum(jnp.max(s, axis=-1, keepdims=True), sk) for s, sk in zip(scores, sinks)]
    probs = [jnp.exp(s - m) for s, m in zip(scores, maxes)]
    dens = [jnp.sum(p, axis=-1, keepdims=True) + jnp.exp(sk - m) for p, m, sk in zip(probs, maxes, sinks)]
    outs = [jnp.dot(p.astype(BF16), vs[j // 2], preferred_element_type=F32) for p, (j, par) in zip(probs, heads)]
    outs = [o / d for o, d in zip(outs, dens)]
    tiles = [jnp.where(lo, outs[2 * j], outs[2 * j + 1]) for j in range(ATTN_WIDTH // LANES)]
    o = jnp.concatenate(tiles, axis=1)
    o_ref[...] = _rms(o, g_ref[...]).astype(BF16)


def _attention(q, kv, sink, g, seq_len):
    t = q.shape[0]
    nblk = t // ATT_BLOCK
    bps = seq_len // ATT_BLOCK
    cur = lambda i, s: (i, 0)
    prv = lambda i, s: (jnp.maximum(i - 1, 0), 0)
    nxt = lambda i, s: (jnp.minimum(i + 1, nblk - 1), 0)
    grid_spec = pltpu.PrefetchScalarGridSpec(
        num_scalar_prefetch=1,
        grid=(nblk,),
        in_specs=[pl.BlockSpec((ATT_BLOCK, ATTN_WIDTH), cur),
                  pl.BlockSpec((ATT_BLOCK, KV_COLS), prv), pl.BlockSpec((ATT_BLOCK, KV_COLS), cur),
                  pl.BlockSpec((ATT_BLOCK, KV_COLS), nxt),
                  pl.BlockSpec((1, ATTN_WIDTH), lambda i, s: (0, 0))],
        out_specs=pl.BlockSpec((ATT_BLOCK, ATTN_WIDTH), cur),
    )
    return pl.pallas_call(
        functools.partial(_attn_kernel, bps=bps),
        grid_spec=grid_spec,
        out_shape=jax.ShapeDtypeStruct((t, ATTN_WIDTH), BF16),
        compiler_params=_params(("parallel",)),
        name="banded_attention",
    )(sink, q, kv, kv, kv, g)


def _uproj_kernel(x_ref, g_ref, w_ref, o_ref, stage_ref):
    lt = x_ref.shape[0]
    h = _rms(x_ref[...].reshape(lt * SUBLANES, D_MODEL), g_ref[...])
    for j in range(D_MODEL // LANES):
        stage_ref[j] = h[:, j * LANES:(j + 1) * LANES]
    for bl in range(SUBLANES):
        hb = jnp.concatenate([stage_ref[j, pl.ds(bl, lt, stride=SUBLANES), :] for j in range(D_MODEL // LANES)],
                             axis=1).astype(BF16)
        ut = lax.dot_general(w_ref[...], hb, (((1,), (1,)), ((), ())), preferred_element_type=F32)
        o_ref[bl] = ut.astype(BF16)


def _uproj(x2d, g, w_t, lt):
    t = x2d.shape[0]
    nc = t // SSM_CHUNK
    xv = x2d.reshape(nc, SSM_CHUNK // SUBLANES, SUBLANES, D_MODEL)
    return pl.pallas_call(
        _uproj_kernel,
        grid=(nc // lt, SSM_CHUNK // SUBLANES),
        in_specs=[pl.BlockSpec((lt, None, SUBLANES, D_MODEL), lambda i, b: (i, b, 0, 0)),
                  pl.BlockSpec((1, D_MODEL), lambda i, b: (0, 0)),
                  pl.BlockSpec((SSM_WIDTH, D_MODEL), lambda i, b: (0, 0))],
        out_specs=pl.BlockSpec((SUBLANES, SSM_WIDTH, lt), lambda i, b: (b, 0, i)),
        out_shape=jax.ShapeDtypeStruct((SSM_CHUNK, SSM_WIDTH, nc), BF16),
        scratch_shapes=[pltpu.VMEM((D_MODEL // LANES, lt * SUBLANES, LANES), F32)],
        compiler_params=_params(("parallel", "parallel")),
        name="u_projection",
    )(xv, g, w_t)


def _gelu_tanh(x):
    return 0.5 * x * (1.0 + jnp.tanh(math.sqrt(2.0 / math.pi) * (x + 0.044715 * (x * x * x))))


LAG_STRIP = 8 * LANES


def _toeplitz_from_strip(strip):
    per_tile = LANES // SSM_GROUP
    rolled = [strip if q == 0 else pltpu.roll(strip, LAG_STRIP - q * SSM_GROUP, 1) for q in range(per_tile)]
    blocks = []
    for b in range(SSM_CHUNK):
        m, q = divmod(SSM_CHUNK - 1 - b, per_tile)
        blocks.append(rolled[q][:, m * LANES:m * LANES + TOEP])
    return jnp.concatenate(blocks, axis=0)


def _ssm_kernel(a_ref, wt_ref, ws_ref, wc_ref, pw_ref, y_ref, *, chunks_per_seq, nsteps):
    nc = a_ref.shape[2]
    a = a_ref[...].reshape(TOEP, nc)
    w_toep = _toeplitz_from_strip(wt_ref[0]).astype(BF16)
    y = jnp.dot(w_toep, a, preferred_element_type=F32)
    s = jnp.dot(ws_ref[0], a, preferred_element_type=F32)
    pos = lax.broadcasted_iota(I32, (SSM_STATE, nc), 1) % chunks_per_seq
    carries = []
    for d in range(2):
        hr = s[2 * d * SSM_STATE:(2 * d + 1) * SSM_STATE]
        hi = s[(2 * d + 1) * SSM_STATE:(2 * d + 2) * SSM_STATE]
        for k in range(nsteps):
            sh = 1 << k
            pr = pw_ref[0, 2 * d * SSM_STATE:(2 * d + 1) * SSM_STATE, k:k + 1]
            pi = pw_ref[0, (2 * d + 1) * SSM_STATE:(2 * d + 2) * SSM_STATE, k:k + 1]
            if d == 0:
                ok = pos >= sh
                sr = pltpu.roll(hr, sh, 1)
                si = pltpu.roll(hi, sh, 1)
            else:
                ok = pos < chunks_per_seq - sh
                sr = pltpu.roll(hr, nc - sh, 1)
                si = pltpu.roll(hi, nc - sh, 1)
            hr, hi = (hr + jnp.where(ok, pr * sr - pi * si, 0.0),
                      hi + jnp.where(ok, pr * si + pi * sr, 0.0))
        if d == 0:
            ok = pos >= 1
            cr = pltpu.roll(hr, 1, 1)
            ci = pltpu.roll(hi, 1, 1)
        else:
            ok = pos < chunks_per_seq - 1
            cr = pltpu.roll(hr, nc - 1, 1)
            ci = pltpu.roll(hi, nc - 1, 1)
        carries += [jnp.where(ok, cr, 0.0), jnp.where(ok, ci, 0.0)]
    carry = jnp.concatenate(carries, axis=0).astype(BF16)
    y = y + jnp.dot(wc_ref[0], carry, preferred_element_type=F32)
    y_ref[...] = _gelu_tanh(y).reshape(SSM_CHUNK, SSM_GROUP, nc)


def _ssm(ut, w_toep, w_state, w_carry, pw, chunks_per_seq):
    nc = ut.shape[2]
    nsteps = max(1, (chunks_per_seq - 1).bit_length())
    g3 = lambda g: (g, 0, 0)
    return pl.pallas_call(
        functools.partial(_ssm_kernel, chunks_per_seq=chunks_per_seq, nsteps=nsteps),
        grid=(SSM_N_GROUPS,),
        in_specs=[pl.BlockSpec((SSM_CHUNK, SSM_GROUP, nc), lambda g: (0, g, 0)),
                  pl.BlockSpec((1, SSM_GROUP, LAG_STRIP), g3),
                  pl.BlockSpec((1, 4 * SSM_STATE, TOEP), g3),
                  pl.BlockSpec((1, TOEP, 4 * SSM_STATE), g3),
                  pl.BlockSpec((1, 4 * SSM_STATE, pw.shape[2]), g3)],
        out_specs=pl.BlockSpec((SSM_CHUNK, SSM_GROUP, nc), lambda g: (0, g, 0)),
        out_shape=jax.ShapeDtypeStruct((SSM_CHUNK, SSM_WIDTH, nc), F32),
        compiler_params=_params(("parallel",)),
        name="s5_core",
    )(ut, w_toep, w_state, w_carry, pw)


def _glu_kernel(y_ref, w_ref, b_ref, g_ref, o_ref, stage_ref):
    lt = o_ref.shape[0]
    for bl in range(SUBLANES):
        y = y_ref[bl]
        z = jnp.dot(w_ref[...], y.astype(BF16), preferred_element_type=F32) + b_ref[...]
        s = y * jax.nn.sigmoid(z)
        ms = jnp.mean(s * s, axis=0, keepdims=True)
        sn = (s * lax.rsqrt(ms + EPS) * g_ref[...]).T
        for j in range(SSM_WIDTH // LANES):
            stage_ref[j, pl.ds(bl, lt, stride=SUBLANES), :] = sn[:, j * LANES:(j + 1) * LANES]
    for j in range(SSM_WIDTH // LANES):
        o_ref[:, :, j * LANES:(j + 1) * LANES] = stage_ref[j].reshape(lt, SUBLANES, LANES)


def _glu(yt, w_t, b_col, g_col, lt):
    nc = yt.shape[2]
    out = pl.pallas_call(
        _glu_kernel,
        grid=(nc // lt, SSM_CHUNK // SUBLANES),
        in_specs=[pl.BlockSpec((SUBLANES, SSM_WIDTH, lt), lambda i, b: (b, 0, i)),
                  pl.BlockSpec((SSM_WIDTH, SSM_WIDTH), lambda i, b: (0, 0)),
                  pl.BlockSpec((SSM_WIDTH, 1), lambda i, b: (0, 0)),
                  pl.BlockSpec((SSM_WIDTH, 1), lambda i, b: (0, 0))],
        out_specs=pl.BlockSpec((lt, None, SUBLANES, SSM_WIDTH), lambda i, b: (i, b, 0, 0)),
        out_shape=jax.ShapeDtypeStruct((nc, SSM_CHUNK // SUBLANES, SUBLANES, SSM_WIDTH), F32),
        scratch_shapes=[pltpu.VMEM((SSM_WIDTH // LANES, lt * SUBLANES, LANES), F32)],
        compiler_params=_params(("parallel", "parallel")),
        name="glu_norm",
    )(yt, w_t, b_col, g_col)
    return out.reshape(nc * SSM_CHUNK, SSM_WIDTH)


def _outproj_kernel(x_ref, a_ref, s_ref, wa_ref, ws_ref, g_ref, wr_ref, br_ref, tri_ref,
                    x2_ref, h2_ref, rt_ref, gt_ref, cnt_ref, run_ref):
    i = pl.program_id(0)

    @pl.when(i == 0)
    def _():
        run_ref[...] = jnp.zeros_like(run_ref)

    x2 = (x_ref[...] + jnp.dot(a_ref[...], wa_ref[...], preferred_element_type=F32)
          + jnp.dot(s_ref[...].astype(BF16), ws_ref[...], preferred_element_type=F32))
    x2_ref[...] = x2
    h2 = _rms(x2, g_ref[...])
    _tile_rows_store(h2_ref, h2)
    logits = lax.dot_general(wr_ref[...], h2.astype(BF16), (((1,), (1,)), ((), ())),
                             preferred_element_type=F32) + br_ref[...]
    tm = logits.shape[1]
    sub = lax.broadcasted_iota(I32, (N_EXPERTS, tm), 0)
    sub_f = sub.astype(F32)
    work = logits
    sel = jnp.zeros((N_EXPERTS, tm), F32)
    top_v, top_i = [], []
    for _ in range(TOP_K):
        m = jnp.max(work, axis=0, keepdims=True)
        idx = jnp.min(jnp.where(work == m, sub_f, float(N_EXPERTS)), axis=0, keepdims=True).astype(I32)
        hit = sub == idx
        sel = jnp.where(hit, 1.0, sel)
        work = jnp.where(hit, -jnp.inf, work)
        top_v.append(m)
        top_i.append(idx)
    ex = [jnp.exp(v - top_v[0]) for v in top_v]
    den = ex[0] + ex[1] + ex[2] + ex[3]
    before = jnp.dot(sel.astype(BF16), tri_ref[...], preferred_element_type=F32) + run_ref[:, 0:1]
    ranks = [jnp.sum(jnp.where(sub == top_i[k], before, 0.0), axis=0, keepdims=True).astype(I32) for k in range(TOP_K)]
    rt_ref[...] = jnp.concatenate(top_i + ranks, axis=0)
    gt_ref[...] = jnp.concatenate([e / den for e in ex] + [jnp.zeros((SUBLANES - TOP_K, tm), F32)], axis=0)
    run = run_ref[...] + jnp.sum(sel, axis=1, keepdims=True)
    run_ref[...] = run
    cnt_ref[...] = run


def _outproj(x2d, attn_n, ssm_n, w_a, w_s, g, w_r, b_r, tri):
    t = x2d.shape[0]
    tm = ROUTE_TM
    row = lambda i: (i, 0)
    full = lambda i: (0, 0)
    return pl.pallas_call(
        _outproj_kernel,
        grid=(t // tm,),
        in_specs=[pl.BlockSpec((tm, D_MODEL), row), pl.BlockSpec((tm, ATTN_WIDTH), row),
                  pl.BlockSpec((tm, SSM_WIDTH), row),
                  pl.BlockSpec((ATTN_WIDTH, D_MODEL), full), pl.BlockSpec((SSM_WIDTH, D_MODEL), full),
                  pl.BlockSpec((1, D_MODEL), full),
                  pl.BlockSpec((N_EXPERTS, D_MODEL), full), pl.BlockSpec((N_EXPERTS, 1), full),
                  pl.BlockSpec((tm, tm), full)],
        out_specs=[pl.BlockSpec((tm, D_MODEL), row), pl.BlockSpec((tm * SUBLANES, LANES), row),
                   pl.BlockSpec((2 * TOP_K, tm), lambda i: (0, i)), pl.BlockSpec((SUBLANES, tm), lambda i: (0, i)),
                   pl.BlockSpec((N_EXPERTS, LANES), full)],
        out_shape=[jax.ShapeDtypeStruct((t, D_MODEL), F32), jax.ShapeDtypeStruct((t * SUBLANES, LANES), F32),
                   jax.ShapeDtypeStruct((2 * TOP_K, t), I32), jax.ShapeDtypeStruct((SUBLANES, t), F32),
                   jax.ShapeDtypeStruct((N_EXPERTS, LANES), F32)],
        scratch_shapes=[pltpu.VMEM((N_EXPERTS, LANES), F32)],
        compiler_params=_params(("arbitrary",)),
        name="out_projection_router",
    )(x2d, attn_n, ssm_n, w_a, w_s, g, w_r, b_r, tri)


DMA_GROUP = 4


def _rows_per_step(rows, tokens):
    nt = rows.shape[1] // tokens
    return (rows * SUBLANES).reshape(TOP_K, nt, tokens).transpose(1, 0, 2).reshape(nt, 1, TOP_K * tokens)


def _scatter_kernel(pad_start_ref, pad_cnt_ref, rows_ref, h_ref, xb_hbm, zero_ref, stage_ref, sems, zsem, *, tm):
    i = pl.program_id(0)
    nt = pl.num_programs(0)
    n_dma = tm * TOP_K

    def tile_wait(slot):
        whole = xb_hbm.at[pl.ds(0, n_dma * SUBLANES)]
        pltpu.make_async_copy(whole, whole, sems.at[slot]).wait()

    @pl.when(i == 0)
    def _():
        zero_ref[...] = jnp.zeros_like(zero_ref)

        def fill(row, n_rows):
            return pltpu.make_async_copy(zero_ref.at[pl.ds(0, n_rows * SUBLANES)],
                                         xb_hbm.at[pl.ds(pl.multiple_of(row * SUBLANES, SUBLANES), n_rows * SUBLANES)],
                                         zsem)

        for phase in range(2):
            def per_expert(e, c):
                cnt = pad_cnt_ref[e]
                row = pad_start_ref[e]
                run = MOE_ROWS // 2
                while run >= 1:
                    below = cnt & ~(2 * run - 1)

                    @pl.when((cnt & run) != 0)
                    def _(run=run, below=below):
                        cp = fill(row + below, run)
                        cp.start() if phase == 0 else cp.wait()
                    run //= 2
                return c
            lax.fori_loop(0, N_EXPERTS, per_expert, 0)

            def per_tail_block(b, c):
                cp = fill(pad_start_ref[N_EXPERTS] + b * MOE_ROWS, MOE_ROWS)
                cp.start() if phase == 0 else cp.wait()
                return c
            lax.fori_loop(0, pad_cnt_ref[N_EXPERTS] // MOE_ROWS, per_tail_block, 0)

    for slot in range(2):
        @pl.when(i > 0)
        def _(slot=slot):
            tile_wait(slot)

        stage_ref[slot] = h_ref[pl.ds(slot * tm * SUBLANES, tm * SUBLANES), :]

        def issue(grp, c, slot=slot):
            r0 = grp * DMA_GROUP
            dst = [rows_ref[0, 0, k * (2 * tm) + slot * tm + r0 + r] for r in range(DMA_GROUP) for k in range(TOP_K)]
            for r in range(DMA_GROUP):
                src = _tile_row(stage_ref.at[slot], r0 + r)
                for k in range(TOP_K):
                    pltpu.make_async_copy(src, _tile_at(xb_hbm, dst[r * TOP_K + k]),
                                          sems.at[slot]).start(priority=k % 2)
            return c
        lax.fori_loop(0, tm // DMA_GROUP, issue, 0)

    @pl.when(i == nt - 1)
    def _():
        tile_wait(0)
        tile_wait(1)


def _scatter_rows(pad_start, pad_cnt, rows, h2, n_pad):
    tm = ROUTE_TM
    nt = h2.shape[0] // SUBLANES // (2 * tm)
    grid_spec = pltpu.PrefetchScalarGridSpec(
        num_scalar_prefetch=2,
        grid=(nt,),
        in_specs=[pl.BlockSpec((1, 1, 2 * tm * TOP_K), lambda i, a, b: (i, 0, 0), memory_space=pltpu.SMEM),
                  pl.BlockSpec((2 * tm * SUBLANES, LANES), lambda i, a, b: (i, 0))],
        out_specs=pl.BlockSpec(memory_space=pl.ANY),
        scratch_shapes=[pltpu.VMEM((MOE_ROWS * SUBLANES, LANES), F32),
                        pltpu.VMEM((2, tm * SUBLANES, LANES), F32),
                        pltpu.SemaphoreType.DMA((2,)), pltpu.SemaphoreType.DMA(())],
    )
    return pl.pallas_call(
        functools.partial(_scatter_kernel, tm=tm),
        grid_spec=grid_spec,
        out_shape=jax.ShapeDtypeStruct((n_pad * SUBLANES, LANES), F32),
        compiler_params=_params(("arbitrary",)),
        name="scatter_rows",
    )(pad_start, pad_cnt, _rows_per_step(rows, 2 * tm), h2)


def _expert_kernel(be_ref, grp_ref, ia_ref, ib_ref, nv_ref, xa_ref, xb_ref, wgu_ref, bgu_ref, wd_ref, bd_ref,
                   o_ref, x_scr, wgu_scr, wd_scr):
    i = pl.program_id(0)
    live = i < nv_ref[0]

    @pl.when(live & ((i == 0) | (be_ref[i] != be_ref[jnp.maximum(i - 1, 0)])))
    def _():
        wgu_scr[...] = wgu_ref[0].astype(BF16)
        wd_scr[...] = wd_ref[0].astype(BF16)

    @pl.when(live & (grp_ref[i] == 0))
    def _():
        x_scr[...] = _tile_rows_load(xa_ref, MOE_ROWS).astype(BF16)

    @pl.when(live & (grp_ref[i] != 0))
    def _():
        x_scr[...] = _tile_rows_load(xb_ref, MOE_ROWS).astype(BF16)

    @pl.when(live)
    def _():
        gu = jnp.dot(x_scr[...], wgu_scr[...], preferred_element_type=F32) + bgu_ref[0]
        gate = jnp.minimum(gu[:, :D_MODEL], SWIGLU_LIMIT)
        up = jnp.clip(gu[:, D_MODEL:], -SWIGLU_LIMIT, SWIGLU_LIMIT)
        act = (up + 1.0) * (gate * jax.nn.sigmoid(SWIGLU_ALPHA * gate))
        _tile_rows_store(o_ref, jnp.dot(act.astype(BF16), wd_scr[...], preferred_element_type=F32) + bd_ref[0])

    @pl.when(i >= nv_ref[0])
    def _():
        o_ref[...] = jnp.zeros_like(o_ref)


def _experts(block_e, block_grp, blk_a, blk_b, n_valid, xa, xb, wgu, bgu, wd, bd):
    nb = block_e.shape[0]
    e3 = lambda i, be, gr, ia, ib, nv: (be[i], 0, 0)
    grid_spec = pltpu.PrefetchScalarGridSpec(
        num_scalar_prefetch=5,
        grid=(nb,),
        in_specs=[pl.BlockSpec((MOE_ROWS * SUBLANES, LANES), lambda i, be, gr, ia, ib, nv: (ia[i], 0)),
                  pl.BlockSpec((MOE_ROWS * SUBLANES, LANES), lambda i, be, gr, ia, ib, nv: (ib[i], 0)),
                  pl.BlockSpec((1, D_MODEL, 2 * D_MODEL), e3), pl.BlockSpec((1, 1, 2 * D_MODEL), e3),
                  pl.BlockSpec((1, D_MODEL, D_MODEL), e3), pl.BlockSpec((1, 1, D_MODEL), e3)],
        out_specs=pl.BlockSpec((MOE_ROWS * SUBLANES, LANES), lambda i, be, gr, ia, ib, nv: (i, 0)),
        scratch_shapes=[pltpu.VMEM((MOE_ROWS, D_MODEL), BF16), pltpu.VMEM((D_MODEL, 2 * D_MODEL), BF16),
                        pltpu.VMEM((D_MODEL, D_MODEL), BF16)],
    )
    return pl.pallas_call(
        _expert_kernel,
        grid_spec=grid_spec,
        out_shape=jax.ShapeDtypeStruct((nb * MOE_ROWS * SUBLANES, LANES), F32),
        compiler_params=_params(("arbitrary",), vmem=EXPERT_VMEM_LIMIT),
        name="routed_experts",
    )(block_e, block_grp, blk_a, blk_b, n_valid, xa, xb, wgu, bgu, wd, bd)


GATHER_PITCH = 12


def _combine_kernel(rows_cur, rows_nxt, x_ref, gt_ref, g_ref, y_hbm, o_ref, buf, sems, *, tm):
    i = pl.program_id(0)
    nt = pl.num_programs(0)
    units = SUBLANES * TOP_K
    per_unit = tm // units
    ahead = COMBINE_TILES // 2

    def tile_wait(slot):
        whole = y_hbm.at[pl.ds(0, TOP_K * tm * SUBLANES)]
        pltpu.make_async_copy(whole, whole, sems.at[slot]).wait()

    def start_rows(rows_ref, tile, r_lo, r_hi):
        idx = [rows_ref[0, 0, k * (COMBINE_TILES * tm) + tile * tm + r] for r in range(r_lo, r_hi) for k in range(TOP_K)]
        for n, (r, k) in enumerate((r, k) for r in range(r_lo, r_hi) for k in range(TOP_K)):
            dst = buf.at[tile, k].at[pl.ds(r * GATHER_PITCH, SUBLANES)]
            pltpu.make_async_copy(_tile_at(y_hbm, idx[n]), dst, sems.at[tile]).start(priority=k % 2)

    def finish(tile):
        nxt = tile + ahead
        nxt_rows, nxt_tile = (rows_cur, nxt) if nxt < COMBINE_TILES else (rows_nxt, nxt - COMBINE_TILES)
        tile_wait(tile)
        rows = pl.ds(tile * tm, tm)
        g8 = gt_ref[:, tile * tm:(tile + 1) * tm]
        gt = jnp.concatenate([g8, jnp.zeros((tm - SUBLANES, tm), F32)], axis=0).T
        gk = [gt[:, k:k + 1] for k in range(TOP_K)]
        ss = jnp.zeros((tm, 1), F32)
        for j in range(SUBLANES):
            cols = slice(j * LANES, (j + 1) * LANES)
            a = x_ref[rows, cols]
            for k in range(TOP_K):
                u = j * TOP_K + k
                start_rows(nxt_rows, nxt_tile, u * per_unit, (u + 1) * per_unit)
                a = a + buf[tile, k, pl.ds(j, tm, stride=GATHER_PITCH), :] * gk[k]
            ss = ss + jnp.sum(a * a, axis=-1, keepdims=True)
            o_ref[rows, cols] = a
        scale = lax.rsqrt(ss * (1.0 / D_MODEL) + EPS)
        o_ref[rows, :] = o_ref[rows, :] * scale * g_ref[...]

    @pl.when(i == 0)
    def _():
        for tile in range(ahead):
            start_rows(rows_cur, tile, 0, tm)

    for tile in range(COMBINE_TILES):
        finish(tile)

    @pl.when(i == nt - 1)
    def _():
        for tile in range(ahead):
            tile_wait(tile)


def _combine(rows, x2, gates, g, yb):
    t = x2.shape[0]
    tm = COMBINE_TM
    step = COMBINE_TILES * tm
    nt = t // step
    rows3 = _rows_per_step(rows, step)
    row = lambda i: (i, 0)
    return pl.pallas_call(
        functools.partial(_combine_kernel, tm=tm),
        grid=(nt,),
        in_specs=[pl.BlockSpec((1, 1, step * TOP_K), lambda i: (i, 0, 0), memory_space=pltpu.SMEM),
                  pl.BlockSpec((1, 1, step * TOP_K), lambda i: (jnp.minimum(i + 1, nt - 1), 0, 0),
                               memory_space=pltpu.SMEM),
                  pl.BlockSpec((step, D_MODEL), row), pl.BlockSpec((SUBLANES, step), lambda i: (0, i)),
                  pl.BlockSpec((1, D_MODEL), lambda i: (0, 0)),
                  pl.BlockSpec(memory_space=pl.ANY)],
        out_specs=pl.BlockSpec((step, D_MODEL), row),
        out_shape=jax.ShapeDtypeStruct((t, D_MODEL), F32),
        scratch_shapes=[pltpu.VMEM((COMBINE_TILES, TOP_K, tm * GATHER_PITCH, LANES), F32),
                        pltpu.SemaphoreType.DMA((COMBINE_TILES,))],
        compiler_params=_params(("arbitrary",)),
        name="combine_final_norm",
    )(rows3, rows3, x2, gates, g, yb)


def _rope_tables(seq_len):
    inv_freq = ROPE_THETA ** (-np.arange(ROPE_HALF, dtype=np.float64) * 2.0 / ROPE_DIM)
    ang = np.arange(seq_len, dtype=np.float64)[:, None] * inv_freq[None, :]
    cos = np.cos(ang).astype(np.float32)
    sin = np.sin(ang).astype(np.float32)
    pad = HEAD_DIM - ROPE_DIM
    ones = np.ones((seq_len, pad), np.float32)
    zer_h = np.zeros((seq_len, ROPE_HALF), np.float32)
    zer_p = np.zeros((seq_len, pad), np.float32)
    c = np.concatenate([cos, cos, ones], axis=1)
    s1 = np.concatenate([zer_h, sin, zer_p], axis=1)
    s2 = np.concatenate([-sin, zer_h, zer_p], axis=1)
    rep = LANES // HEAD_DIM
    return tuple(jnp.asarray(np.tile(t, (1, rep))) for t in (c, s1, s2))


def _ssm_weights(a_re, a_im, log_dt, b_re, b_im, c_re, c_im, ssm_d, nsteps):
    r = SSM_CHUNK
    dt = jnp.exp(log_dt)[..., None]
    lr, li = a_re * dt, a_im * dt

    def cpow(n):
        n = jnp.asarray(n, F32)[..., None, None, None]
        mag = jnp.exp(n * lr)
        return mag * jnp.cos(n * li), mag * jnp.sin(n * li)

    ab_re, ab_im = cpow(jnp.ones(()))
    den = a_re * a_re + a_im * a_im
    num_re, num_im = ab_re - 1.0, ab_im
    f_re = (num_re * a_re + num_im * a_im) / den
    f_im = (num_im * a_re - num_re * a_im) / den
    bb_re = f_re[..., None] * b_re - f_im[..., None] * b_im
    bb_im = f_re[..., None] * b_im + f_im[..., None] * b_re

    taus = jnp.arange(r + 1, dtype=F32)
    p_re, p_im = cpow(taus)
    m_re = p_re[..., None] * bb_re - p_im[..., None] * bb_im
    m_im = p_re[..., None] * bb_im + p_im[..., None] * bb_re
    kern = (jnp.einsum('dgcp,tdgpk->tdgck', c_re, m_re) - jnp.einsum('dgcp,tdgpk->tdgck', c_im, m_im))
    center = kern[0, 0] + kern[0, 1] + jnp.eye(SSM_GROUP, dtype=F32) * ssm_d[:, :, None]
    lags = jnp.concatenate([kern[r - 1:0:-1, 0], center[None], kern[1:r, 1]], axis=0)
    strip = lags.transpose(1, 2, 0, 3).reshape(SSM_N_GROUPS, SSM_GROUP, (2 * r - 1) * SSM_GROUP)
    w_toep = jnp.pad(strip, ((0, 0), (0, 0), (0, LAG_STRIP - (2 * r - 1) * SSM_GROUP)))

    st_f_re, st_f_im = m_re[r - 1::-1, 0], m_im[r - 1::-1, 0]
    st_b_re, st_b_im = m_re[:r, 1], m_im[:r, 1]
    w_state = jnp.stack([st_f_re, st_f_im, st_b_re, st_b_im], axis=0)
    w_state = w_state.transpose(2, 0, 3, 1, 4).reshape(SSM_N_GROUPS, 4 * SSM_STATE, TOEP)

    def c_times_pow(d, pr, pi):
        zr = c_re[d][None] * pr[:, :, None, :] - c_im[d][None] * pi[:, :, None, :]
        zi = c_re[d][None] * pi[:, :, None, :] + c_im[d][None] * pr[:, :, None, :]
        return zr, -zi
    cf_re, cf_im = c_times_pow(0, p_re[1:r + 1, 0], p_im[1:r + 1, 0])
    cb_re, cb_im = c_times_pow(1, p_re[r:0:-1, 1], p_im[r:0:-1, 1])
    w_carry = jnp.stack([cf_re, cf_im, cb_re, cb_im], axis=3)
    w_carry = w_carry.transpose(1, 0, 2, 3, 4).reshape(SSM_N_GROUPS, TOEP, 4 * SSM_STATE)

    qr, qi = p_re[r], p_im[r]
    cols = []
    for _ in range(nsteps):
        cols.append(jnp.stack([qr[0], qi[0], qr[1], qi[1]], axis=1))
        qr, qi = qr * qr - qi * qi, 2.0 * qr * qi
    pw = jnp.stack(cols, axis=-1).reshape(SSM_N_GROUPS, 4 * SSM_STATE, nsteps)
    return w_toep, w_state.astype(BF16), w_carry.astype(BF16), pw


def _front(x, prm):
    n, seq_len, _ = x.shape
    t = n * seq_len
    x2d = x.reshape(t, D_MODEL)
    q, kv = _qkv(x2d, prm['norm1_g'], prm['w_qkv'], *_rope_tables(seq_len), seq_len)
    attn_n = _attention(q, kv, prm['sink'], prm['attn_out_g'], seq_len)

    chunks_per_seq = seq_len // SSM_CHUNK
    nsteps = max(1, (chunks_per_seq - 1).bit_length())
    nc = t // SSM_CHUNK
    lt = min(256, nc)
    ut = _uproj(x2d, prm['norm1_g'], prm['w_u_t'], lt)
    yt = _ssm(ut, prm['w_toep'], prm['w_state'], prm['w_carry'], prm['pw'][:, :, :nsteps], chunks_per_seq)
    ssm_n = _glu(yt, prm['glu_w_t'], prm['glu_b'], prm['ssm_out_g'], lt)
    return _outproj(x2d, attn_n, ssm_n, prm['w_out_a'], prm['w_out_s'], prm['norm2_g'],
                    prm['router_w'], prm['router_b'], prm['tri'])


def _cumsum_small(x):
    n = x.shape[0]
    keep = jnp.arange(n)[None, :] <= jnp.arange(n)[:, None]
    return jnp.sum(jnp.where(keep, x[None, :], 0), axis=1)


def _by_expert(idx, table):
    hit = idx[None] == jnp.arange(N_EXPERTS, dtype=I32)[:, None, None]
    return jnp.sum(jnp.where(hit, table[:, None, None], 0), axis=0)


def kernel(x_prompt, x_sample, norm1_g, w_in, attn_sink, ssm_a_re, ssm_a_im, ssm_log_dt, ssm_b_re, ssm_b_im, ssm_c_re, ssm_c_im, ssm_d, glu_w, glu_b, attn_out_g, ssm_out_g, w_out, norm2_g, router_w, router_b, w_gate_up, b_gate_up, w_down, b_down, final_g):
    assert norm1_g.shape[0] == 1, "single-layer problem"
    l = 0
    xs = [x_prompt, x_sample]
    max_chunks = max(x.shape[1] for x in xs) // SSM_CHUNK
    max_steps = max(1, (max_chunks - 1).bit_length())
    wq, wk, wv, wu = jnp.split(w_in[l], [ATTN_WIDTH, ATTN_WIDTH + KV_WIDTH, ATTN_WIDTH + 2 * KV_WIDTH], axis=1)
    dup = lambda w: jnp.concatenate([w[:, :HEAD_DIM], w[:, :HEAD_DIM], w[:, HEAD_DIM:], w[:, HEAD_DIM:]], axis=1)
    w_toep, w_state, w_carry, pw = _ssm_weights(ssm_a_re[l], ssm_a_im[l], ssm_log_dt[l], ssm_b_re[l], ssm_b_im[l],
                                                ssm_c_re[l], ssm_c_im[l], ssm_d[l], max_steps)
    tri_i = lax.broadcasted_iota(I32, (ROUTE_TM, ROUTE_TM), 0)
    tri_j = lax.broadcasted_iota(I32, (ROUTE_TM, ROUTE_TM), 1)
    prm = dict(
        norm1_g=norm1_g[l].reshape(1, D_MODEL),
        w_qkv=jnp.concatenate([wq, dup(wk), dup(wv)], axis=1).astype(BF16),
        w_u_t=wu.T.astype(BF16),
        sink=attn_sink[l].astype(F32),
        attn_out_g=attn_out_g[l].reshape(1, ATTN_WIDTH),
        w_toep=w_toep, w_state=w_state, w_carry=w_carry, pw=pw,
        glu_w_t=glu_w[l].T.astype(BF16),
        glu_b=glu_b[l].reshape(SSM_WIDTH, 1),
        ssm_out_g=ssm_out_g[l].reshape(SSM_WIDTH, 1),
        w_out_a=w_out[l][:ATTN_WIDTH].astype(BF16),
        w_out_s=w_out[l][ATTN_WIDTH:].astype(BF16),
        norm2_g=norm2_g[l].reshape(1, D_MODEL),
        router_w=router_w[l].T.astype(BF16),
        router_b=router_b[l].reshape(N_EXPERTS, 1),
        tri=(tri_i < tri_j).astype(BF16),
    )
    fronts = [_front(x, prm) for x in xs]

    cnts = [f[4][:, 0].astype(I32) for f in fronts]
    padded = [(c + MOE_ROWS - 1) // MOE_ROWS * MOE_ROWS for c in cnts]
    pends = [_cumsum_small(p) for p in padded]
    pstarts = [pe - p for pe, p in zip(pends, padded)]
    nbs = [f[0].shape[0] * TOP_K // MOE_ROWS + N_EXPERTS for f in fronts]
    seg_blocks = jnp.stack([p // MOE_ROWS for p in padded], axis=1).reshape(-1)
    seg_end = _cumsum_small(seg_blocks)
    seg_start = seg_end - seg_blocks
    nb = sum(nbs)
    bi = jnp.arange(nb, dtype=I32)
    seg = jnp.minimum(jnp.sum(seg_end[None, :] <= bi[:, None], axis=1), 2 * N_EXPERTS - 1).astype(I32)
    block_e = seg // 2
    block_grp = seg % 2
    n_valid = seg_end[-1].astype(I32).reshape(1)
    live = bi < n_valid[0]
    src_start = jnp.stack([ps // MOE_ROWS for ps in pstarts], axis=1).reshape(-1)
    in_seg = seg[:, None] == jnp.arange(2 * N_EXPERTS, dtype=I32)[None, :]
    src_blk = jnp.sum(jnp.where(in_seg, (src_start - seg_start)[None, :], 0), axis=1) + bi
    not_after = bi[None, :] <= bi[:, None]
    blk = [jnp.max(jnp.where(not_after & (live & (block_grp == g))[None, :], src_blk[None, :], 0), axis=1).astype(I32)
           for g in range(2)]

    xbufs, yrows = [], []
    for g, f in enumerate(fronts):
        e_idx = f[2][:TOP_K]
        rank = f[2][TOP_K:]
        n_pad = nbs[g] * MOE_ROWS
        pad_start = jnp.concatenate([pstarts[g] + cnts[g], pends[g][-1:]]).astype(I32)
        pad_cnt = jnp.concatenate([padded[g] - cnts[g], n_pad - pends[g][-1:]]).astype(I32)
        xrows = (_by_expert(e_idx, pstarts[g]) + rank).astype(I32)
        xbufs.append(_scatter_rows(pad_start, pad_cnt, xrows, f[1], n_pad))
        yrows.append((_by_expert(e_idx, seg_start[g::2] * MOE_ROWS) + rank).astype(I32))
    yb = _experts(block_e, block_grp, blk[0], blk[1], n_valid, xbufs[0], xbufs[1],
                  w_gate_up.reshape(N_EXPERTS, D_MODEL, 2 * D_MODEL), b_gate_up[l][:, None, :],
                  w_down.reshape(N_EXPERTS, D_MODEL, D_MODEL), b_down[l][:, None, :])
    gfin = final_g.reshape(1, D_MODEL)
    outs = [_combine(r, f[0], f[3], gfin, yb).reshape(x.shape) for x, f, r in zip(xs, fronts, yrows)]
    return tuple(outs)
```

```python
import functools
import math

import jax
import jax.numpy as jnp
import numpy as np
from jax import lax
from jax.experimental import pallas as pl
from jax.experimental.pallas import tpu as pltpu

F32 = jnp.float32
BF16 = jnp.bfloat16
I32 = jnp.int32

D_MODEL = 1024
HEAD_DIM = 64
N_Q_HEADS = 8
N_KV_HEADS = 2
Q_PER_KV = N_Q_HEADS // N_KV_HEADS
ATTN_WIDTH = N_Q_HEADS * HEAD_DIM
KV_WIDTH = N_KV_HEADS * HEAD_DIM
WINDOW = 128
ATT_BLOCK = 128
ROPE_THETA = 500000.0
ROPE_DIM = HEAD_DIM // 4
ROPE_HALF = ROPE_DIM // 2
SSM_WIDTH = 512
SSM_GROUP = 16
SSM_N_GROUPS = SSM_WIDTH // SSM_GROUP
SSM_STATE = 64
N_EXPERTS = 32
TOP_K = 4
SWIGLU_LIMIT = 7.0
SWIGLU_ALPHA = 1.702
EPS = 1e-5

LANES = 128
SUBLANES = 8
SSM_CHUNK = 32
TOEP = SSM_CHUNK * SSM_GROUP
MOE_ROWS = 512
ROUTE_TM = 512
COMBINE_TM = 128
COMBINE_TILES = 4
VMEM_LIMIT = 52 * 1024 * 1024
EXPERT_VMEM_LIMIT = 60 * 1024 * 1024
NEG_BIG = -1e30


def _params(sem, vmem=VMEM_LIMIT):
    return pltpu.CompilerParams(dimension_semantics=sem, vmem_limit_bytes=vmem)


def _rms(x, g):
    ms = jnp.mean(x * x, axis=-1, keepdims=True)
    return x * lax.rsqrt(ms + EPS) * g


def _tile_rows_load(ref, rows):
    return jnp.concatenate([ref[pl.ds(j, rows, stride=SUBLANES), :] for j in range(SUBLANES)], axis=1)


def _tile_rows_store(ref, val):
    rows = val.shape[0]
    for j in range(SUBLANES):
        ref[pl.ds(j, rows, stride=SUBLANES), :] = val[:, j * LANES:(j + 1) * LANES]


def _tile_at(ref, start):
    return ref.at[pl.ds(pl.multiple_of(start, SUBLANES), SUBLANES)]


def _tile_row(ref, row):
    return _tile_at(ref, row * SUBLANES)


KV_COLS = 4 * LANES


def _qkv_kernel(x_ref, g_ref, w_ref, c_ref, s1_ref, s2_ref, q_ref, kv_ref):
    h = _rms(x_ref[...], g_ref[...]).astype(BF16)
    p = jnp.dot(h, w_ref[...], preferred_element_type=F32)
    c = c_ref[...]
    s1 = s1_ref[...]
    s2 = s2_ref[...]

    def rot(t):
        return t * c + pltpu.roll(t, ROPE_HALF, 1) * s1 + pltpu.roll(t, LANES - ROPE_HALF, 1) * s2

    for j in range(ATTN_WIDTH // LANES):
        q_ref[:, j * LANES:(j + 1) * LANES] = (rot(p[:, j * LANES:(j + 1) * LANES]) * (HEAD_DIM ** -0.5)).astype(BF16)
    for j in range(N_KV_HEADS):
        col = ATTN_WIDTH + j * LANES
        kv_ref[:, j * LANES:(j + 1) * LANES] = rot(p[:, col:col + LANES]).astype(BF16)
    kv_ref[:, N_KV_HEADS * LANES:] = p[:, ATTN_WIDTH + N_KV_HEADS * LANES:].astype(BF16)


def _qkv(x2d, g, w, c, s1, s2, seq_len, tm=512):
    t = x2d.shape[0]
    nlb = seq_len // tm
    row = lambda i: (i, 0)
    tab = lambda i: (i % nlb, 0)
    full = lambda i: (0, 0)
    return pl.pallas_call(
        _qkv_kernel,
        grid=(t // tm,),
        in_specs=[pl.BlockSpec((tm, D_MODEL), row), pl.BlockSpec((1, D_MODEL), full),
                  pl.BlockSpec((D_MODEL, ATTN_WIDTH + KV_COLS), full),
                  pl.BlockSpec((tm, LANES), tab), pl.BlockSpec((tm, LANES), tab), pl.BlockSpec((tm, LANES), tab)],
        out_specs=[pl.BlockSpec((tm, ATTN_WIDTH), row), pl.BlockSpec((tm, KV_COLS), row)],
        out_shape=[jax.ShapeDtypeStruct((t, ATTN_WIDTH), BF16), jax.ShapeDtypeStruct((t, KV_COLS), BF16)],
        compiler_params=_params(("parallel",)),
        name="qkv_rotary",
    )(x2d, g, w, c, s1, s2)


def _attn_kernel(sink_ref, q_ref, kvp, kvc, kvn, g_ref, o_ref, *, bps):
    i = pl.program_id(0)
    first = (i % bps) == 0
    last = (i % bps) == bps - 1
    qi = lax.broadcasted_iota(I32, (ATT_BLOCK, 3 * ATT_BLOCK), 0)
    kj = lax.broadcasted_iota(I32, (ATT_BLOCK, 3 * ATT_BLOCK), 1)
    rel = kj - ATT_BLOCK - qi
    valid = (jnp.abs(rel) <= WINDOW)
    valid = valid & ((kj >= ATT_BLOCK) | jnp.logical_not(first))
    valid = valid & ((kj < 2 * ATT_BLOCK) | jnp.logical_not(last))
    kv = jnp.concatenate([kvp[...], kvc[...], kvn[...]], axis=0)
    ks = [kv[:, h * LANES:(h + 1) * LANES] for h in range(N_KV_HEADS)]
    vs = [kv[:, (N_KV_HEADS + h) * LANES:(N_KV_HEADS + h + 1) * LANES] for h in range(N_KV_HEADS)]
    lo = lax.broadcasted_iota(I32, (ATT_BLOCK, LANES), 1) < HEAD_DIM
    zero = jnp.zeros((ATT_BLOCK, LANES), BF16)
    heads = [(j, par) for j in range(ATTN_WIDTH // LANES) for par in range(2)]
    nt_dims = (((1,), (1,)), ((), ()))
    scores = []
    for j, par in heads:
        qt = q_ref[:, j * LANES:(j + 1) * LANES]
        qm = jnp.where(lo if par == 0 else jnp.logical_not(lo), qt, zero)
        scores.append(lax.dot_general(qm, ks[j // 2], nt_dims, preferred_element_type=F32))
    scores = [jnp.where(valid, s, NEG_BIG) for s in scores]
    sinks = [sink_ref[2 * j + par] for j, par in heads]
    maxes = [jnp.maximum(jnp.max(s, axis=-1, keepdims=True), sk) for s, sk in zip(scores, sinks)]
    probs = [jnp.exp(s - m) for s, m in zip(scores, maxes)]
    dens = [jnp.sum(p, axis=-1, keepdims=True) + jnp.exp(sk - m) for p, m, sk in zip(probs, maxes, sinks)]
    outs = [jnp.dot(p.astype(BF16), vs[j // 2], preferred_element_type=F32) for p, (j, par) in zip(probs, heads)]
    outs = [o / d for o, d in zip(outs, dens)]
    tiles = [jnp.where(lo, outs[2 * j], outs[2 * j + 1]) for j in range(ATTN_WIDTH // LANES)]
    o = jnp.concatenate(tiles, axis=1)
    o_ref[...] = _rms(o, g_ref[...]).astype(BF16)


def _attention(q, kv, sink, g, seq_len):
    t = q.shape[0]
    nblk = t // ATT_BLOCK
    bps = seq_len // ATT_BLOCK
    cur = lambda i, s: (i, 0)
    prv = lambda i, s: (jnp.maximum(i - 1, 0), 0)
    nxt = lambda i, s: (jnp.minimum(i + 1, nblk - 1), 0)
    grid_spec = pltpu.PrefetchScalarGridSpec(
        num_scalar_prefetch=1,
        grid=(nblk,),
        in_specs=[pl.BlockSpec((ATT_BLOCK, ATTN_WIDTH), cur),
                  pl.BlockSpec((ATT_BLOCK, KV_COLS), prv), pl.BlockSpec((ATT_BLOCK, KV_COLS), cur),
                  pl.BlockSpec((ATT_BLOCK, KV_COLS), nxt),
                  pl.BlockSpec((1, ATTN_WIDTH), lambda i, s: (0, 0))],
        out_specs=pl.BlockSpec((ATT_BLOCK, ATTN_WIDTH), cur),
    )
    return pl.pallas_call(
        functools.partial(_attn_kernel, bps=bps),
        grid_spec=grid_spec,
        out_shape=jax.ShapeDtypeStruct((t, ATTN_WIDTH), BF16),
        compiler_params=_params(("parallel",)),
        name="banded_attention",
    )(sink, q, kv, kv, kv, g)


def _uproj_kernel(x_ref, g_ref, w_ref, o_ref, stage_ref):
    lt = x_ref.shape[0]
    h = _rms(x_ref[...].reshape(lt * SUBLANES, D_MODEL), g_ref[...])
    for j in range(D_MODEL // LANES):
        stage_ref[j] = h[:, j * LANES:(j + 1) * LANES]
    for bl in range(SUBLANES):
        hb = jnp.concatenate([stage_ref[j, pl.ds(bl, lt, stride=SUBLANES), :] for j in range(D_MODEL // LANES)],
                             axis=1).astype(BF16)
        ut = lax.dot_general(w_ref[...], hb, (((1,), (1,)), ((), ())), preferred_element_type=F32)
        o_ref[bl] = ut.astype(BF16)


def _uproj(x2d, g, w_t, lt):
    t = x2d.shape[0]
    nc = t // SSM_CHUNK
    xv = x2d.reshape(nc, SSM_CHUNK // SUBLANES, SUBLANES, D_MODEL)
    return pl.pallas_call(
        _uproj_kernel,
        grid=(nc // lt, SSM_CHUNK // SUBLANES),
        in_specs=[pl.BlockSpec((lt, None, SUBLANES, D_MODEL), lambda i, b: (i, b, 0, 0)),
                  pl.BlockSpec((1, D_MODEL), lambda i, b: (0, 0)),
                  pl.BlockSpec((SSM_WIDTH, D_MODEL), lambda i, b: (0, 0))],
        out_specs=pl.BlockSpec((SUBLANES, SSM_WIDTH, lt), lambda i, b: (b, 0, i)),
        out_shape=jax.ShapeDtypeStruct((SSM_CHUNK, SSM_WIDTH, nc), BF16),
        scratch_shapes=[pltpu.VMEM((D_MODEL // LANES, lt * SUBLANES, LANES), F32)],
        compiler_params=_params(("parallel", "parallel")),
        name="u_projection",
    )(xv, g, w_t)


def _gelu_tanh(x):
    return 0.5 * x * (1.0 + jnp.tanh(math.sqrt(2.0 / math.pi) * (x + 0.044715 * (x * x * x))))


LAG_STRIP = 8 * LANES


def _toeplitz_from_strip(strip):
    per_tile = LANES // SSM_GROUP
    rolled = [strip if q == 0 else pltpu.roll(strip, LAG_STRIP - q * SSM_GROUP, 1) for q in range(per_tile)]
    blocks = []
    for b in range(SSM_CHUNK):
        m, q = divmod(SSM_CHUNK - 1 - b, per_tile)
        blocks.append(rolled[q][:, m * LANES:m * LANES + TOEP])
    return jnp.concatenate(blocks, axis=0)


def _ssm_kernel(a_ref, wt_ref, ws_ref, wc_ref, pw_ref, y_ref, *, chunks_per_seq, nsteps):
    nc = a_ref.shape[2]
    a = a_ref[...].reshape(TOEP, nc)
    w_toep = _toeplitz_from_strip(wt_ref[0]).astype(BF16)
    y = jnp.dot(w_toep, a, preferred_element_type=F32)
    s = jnp.dot(ws_ref[0], a, preferred_element_type=F32)
    pos = lax.broadcasted_iota(I32, (SSM_STATE, nc), 1) % chunks_per_seq
    carries = []
    for d in range(2):
        hr = s[2 * d * SSM_STATE:(2 * d + 1) * SSM_STATE]
        hi = s[(2 * d + 1) * SSM_STATE:(2 * d + 2) * SSM_STATE]
        for k in range(nsteps):
            sh = 1 << k
            pr = pw_ref[0, 2 * d * SSM_STATE:(2 * d + 1) * SSM_STATE, k:k + 1]
            pi = pw_ref[0, (2 * d + 1) * SSM_STATE:(2 * d + 2) * SSM_STATE, k:k + 1]
            if d == 0:
                ok = pos >= sh
                sr = pltpu.roll(hr, sh, 1)
                si = pltpu.roll(hi, sh, 1)
            else:
                ok = pos < chunks_per_seq - sh
                sr = pltpu.roll(hr, nc - sh, 1)
                si = pltpu.roll(hi, nc - sh, 1)
            hr, hi = (hr + jnp.where(ok, pr * sr - pi * si, 0.0),
                      hi + jnp.where(ok, pr * si + pi * sr, 0.0))
        if d == 0:
            ok = pos >= 1
            cr = pltpu.roll(hr, 1, 1)
            ci = pltpu.roll(hi, 1, 1)
        else:
            ok = pos < chunks_per_seq - 1
            cr = pltpu.roll(hr, nc - 1, 1)
            ci = pltpu.roll(hi, nc - 1, 1)
        carries += [jnp.where(ok, cr, 0.0), jnp.where(ok, ci, 0.0)]
    carry = jnp.concatenate(carries, axis=0).astype(BF16)
    y = y + jnp.dot(wc_ref[0], carry, preferred_element_type=F32)
    y_ref[...] = _gelu_tanh(y).reshape(SSM_CHUNK, SSM_GROUP, nc)


def _ssm(ut, w_toep, w_state, w_carry, pw, chunks_per_seq):
    nc = ut.shape[2]
    nsteps = max(1, (chunks_per_seq - 1).bit_length())
    g3 = lambda g: (g, 0, 0)
    return pl.pallas_call(
        functools.partial(_ssm_kernel, chunks_per_seq=chunks_per_seq, nsteps=nsteps),
        grid=(SSM_N_GROUPS,),
        in_specs=[pl.BlockSpec((SSM_CHUNK, SSM_GROUP, nc), lambda g: (0, g, 0)),
                  pl.BlockSpec((1, SSM_GROUP, LAG_STRIP), g3),
                  pl.BlockSpec((1, 4 * SSM_STATE, TOEP), g3),
                  pl.BlockSpec((1, TOEP, 4 * SSM_STATE), g3),
                  pl.BlockSpec((1, 4 * SSM_STATE, pw.shape[2]), g3)],
        out_specs=pl.BlockSpec((SSM_CHUNK, SSM_GROUP, nc), lambda g: (0, g, 0)),
        out_shape=jax.ShapeDtypeStruct((SSM_CHUNK, SSM_WIDTH, nc), F32),
        compiler_params=_params(("parallel",)),
        name="s5_core",
    )(ut, w_toep, w_state, w_carry, pw)


def _glu_kernel(y_ref, w_ref, b_ref, g_ref, o_ref, stage_ref):
    lt = o_ref.shape[0]
    for bl in range(SUBLANES):
        y = y_ref[bl]
        z = jnp.dot(w_ref[...], y.astype(BF16), preferred_element_type=F32) + b_ref[...]
        s = y * jax.nn.sigmoid(z)
        ms = jnp.mean(s * s, axis=0, keepdims=True)
        sn = (s * lax.rsqrt(ms + EPS) * g_ref[...]).T
        for j in range(SSM_WIDTH // LANES):
            stage_ref[j, pl.ds(bl, lt, stride=SUBLANES), :] = sn[:, j * LANES:(j + 1) * LANES]
    for j in range(SSM_WIDTH // LANES):
        o_ref[:, :, j * LANES:(j + 1) * LANES] = stage_ref[j].reshape(lt, SUBLANES, LANES)


def _glu(yt, w_t, b_col, g_col, lt):
    nc = yt.shape[2]
    out = pl.pallas_call(
        _glu_kernel,
        grid=(nc // lt, SSM_CHUNK // SUBLANES),
        in_specs=[pl.BlockSpec((SUBLANES, SSM_WIDTH, lt), lambda i, b: (b, 0, i)),
                  pl.BlockSpec((SSM_WIDTH, SSM_WIDTH), lambda i, b: (0, 0)),
                  pl.BlockSpec((SSM_WIDTH, 1), lambda i, b: (0, 0)),
                  pl.BlockSpec((SSM_WIDTH, 1), lambda i, b: (0, 0))],
        out_specs=pl.BlockSpec((lt, None, SUBLANES, SSM_WIDTH), lambda i, b: (i, b, 0, 0)),
        out_shape=jax.ShapeDtypeStruct((nc, SSM_CHUNK // SUBLANES, SUBLANES, SSM_WIDTH), F32),
        scratch_shapes=[pltpu.VMEM((SSM_WIDTH // LANES, lt * SUBLANES, LANES), F32)],
        compiler_params=_params(("parallel", "parallel")),
        name="glu_norm",
    )(yt, w_t, b_col, g_col)
    return out.reshape(nc * SSM_CHUNK, SSM_WIDTH)


def _outproj_kernel(x_ref, a_ref, s_ref, wa_ref, ws_ref, g_ref, wr_ref, br_ref, tri_ref,
                    x2_ref, h2_ref, rt_ref, gt_ref, cnt_ref, run_ref):
    i = pl.program_id(0)

    @pl.when(i == 0)
    def _():
        run_ref[...] = jnp.zeros_like(run_ref)

    x2 = (x_ref[...] + jnp.dot(a_ref[...], wa_ref[...], preferred_element_type=F32)
          + jnp.dot(s_ref[...].astype(BF16), ws_ref[...], preferred_element_type=F32))
    x2_ref[...] = x2
    h2 = _rms(x2, g_ref[...])
    _tile_rows_store(h2_ref, h2)
    logits = lax.dot_general(wr_ref[...], h2.astype(BF16), (((1,), (1,)), ((), ())),
                             preferred_element_type=F32) + br_ref[...]
    tm = logits.shape[1]
    sub = lax.broadcasted_iota(I32, (N_EXPERTS, tm), 0)
    sub_f = sub.astype(F32)
    work = logits
    sel = jnp.zeros((N_EXPERTS, tm), F32)
    top_v, top_i = [], []
    for _ in range(TOP_K):
        m = jnp.max(work, axis=0, keepdims=True)
        idx = jnp.min(jnp.where(work == m, sub_f, float(N_EXPERTS)), axis=0, keepdims=True).astype(I32)
        hit = sub == idx
        sel = jnp.where(hit, 1.0, sel)
        work = jnp.where(hit, -jnp.inf, work)
        top_v.append(m)
        top_i.append(idx)
    ex = [jnp.exp(v - top_v[0]) for v in top_v]
    den = ex[0] + ex[1] + ex[2] + ex[3]
    before = jnp.dot(sel.astype(BF16), tri_ref[...], preferred_element_type=F32) + run_ref[:, 0:1]
    ranks = [jnp.sum(jnp.where(sub == top_i[k], before, 0.0), axis=0, keepdims=True).astype(I32) for k in range(TOP_K)]
    rt_ref[...] = jnp.concatenate(top_i + ranks, axis=0)
    gt_ref[...] = jnp.concatenate([e / den for e in ex] + [jnp.zeros((SUBLANES - TOP_K, tm), F32)], axis=0)
    run = run_ref[...] + jnp.sum(sel, axis=1, keepdims=True)
    run_ref[...] = run
    cnt_ref[...] = run


def _outproj(x2d, attn_n, ssm_n, w_a, w_s, g, w_r, b_r, tri):
    t = x2d.shape[0]
    tm = ROUTE_TM
    row = lambda i: (i, 0)
    full = lambda i: (0, 0)
    return pl.pallas_call(
        _outproj_kernel,
        grid=(t // tm,),
        in_specs=[pl.BlockSpec((tm, D_MODEL), row), pl.BlockSpec((tm, ATTN_WIDTH), row),
                  pl.BlockSpec((tm, SSM_WIDTH), row),
                  pl.BlockSpec((ATTN_WIDTH, D_MODEL), full), pl.BlockSpec((SSM_WIDTH, D_MODEL), full),
                  pl.BlockSpec((1, D_MODEL), full),
                  pl.BlockSpec((N_EXPERTS, D_MODEL), full), pl.BlockSpec((N_EXPERTS, 1), full),
                  pl.BlockSpec((tm, tm), full)],
        out_specs=[pl.BlockSpec((tm, D_MODEL), row), pl.BlockSpec((tm * SUBLANES, LANES), row),
                   pl.BlockSpec((2 * TOP_K, tm), lambda i: (0, i)), pl.BlockSpec((SUBLANES, tm), lambda i: (0, i)),
                   pl.BlockSpec((N_EXPERTS, LANES), full)],
        out_shape=[jax.ShapeDtypeStruct((t, D_MODEL), F32), jax.ShapeDtypeStruct((t * SUBLANES, LANES), F32),
                   jax.ShapeDtypeStruct((2 * TOP_K, t), I32), jax.ShapeDtypeStruct((SUBLANES, t), F32),
                   jax.ShapeDtypeStruct((N_EXPERTS, LANES), F32)],
        scratch_shapes=[pltpu.VMEM((N_EXPERTS, LANES), F32)],
        compiler_params=_params(("arbitrary",)),
        name="out_projection_router",
    )(x2d, attn_n, ssm_n, w_a, w_s, g, w_r, b_r, tri)


DMA_GROUP = 4


def _scatter_kernel(pad_start_ref, pad_cnt_ref, rows_ref, h_ref, xb_hbm, zero_ref, stage_ref, sems, zsem, *, tm):
    i = pl.program_id(0)
    nt = pl.num_programs(0)
    n_dma = tm * TOP_K

    def tile_wait(slot):
        whole = xb_hbm.at[pl.ds(0, n_dma * SUBLANES)]
        pltpu.make_async_copy(whole, whole, sems.at[slot]).wait()

    @pl.when(i == 0)
    def _():
        zero_ref[...] = jnp.zeros_like(zero_ref)

        def fill(row, n_rows):
            return pltpu.make_async_copy(zero_ref.at[pl.ds(0, n_rows * SUBLANES)],
                                         xb_hbm.at[pl.ds(pl.multiple_of(row * SUBLANES, SUBLANES), n_rows * SUBLANES)],
                                         zsem)

        for phase in range(2):
            def per_expert(e, c):
                cnt = pad_cnt_ref[e]
                row = pad_start_ref[e]
                run = MOE_ROWS // 2
                while run >= 1:
                    below = cnt & ~(2 * run - 1)

                    @pl.when((cnt & run) != 0)
                    def _(run=run, below=below):
                        cp = fill(row + below, run)
                        cp.start() if phase == 0 else cp.wait()
                    run //= 2
                return c
            lax.fori_loop(0, N_EXPERTS, per_expert, 0)

            def per_tail_block(b, c):
                cp = fill(pad_start_ref[N_EXPERTS] + b * MOE_ROWS, MOE_ROWS)
                cp.start() if phase == 0 else cp.wait()
                return c
            lax.fori_loop(0, pad_cnt_ref[N_EXPERTS] // MOE_ROWS, per_tail_block, 0)

    for slot in range(2):
        @pl.when(i > 0)
        def _(slot=slot):
            tile_wait(slot)

        stage_ref[slot] = h_ref[pl.ds(slot * tm * SUBLANES, tm * SUBLANES), :]

        def issue(grp, c, slot=slot):
            r0 = grp * DMA_GROUP
            dst = [rows_ref[k, slot * tm + r0 + r] for r in range(DMA_GROUP) for k in range(TOP_K)]
            for r in range(DMA_GROUP):
                src = _tile_row(stage_ref.at[slot], r0 + r)
                for k in range(TOP_K):
                    pltpu.make_async_copy(src, _tile_at(xb_hbm, dst[r * TOP_K + k]),
                                          sems.at[slot]).start(priority=k % 2)
            return c
        lax.fori_loop(0, tm // DMA_GROUP, issue, 0)

    @pl.when(i == nt - 1)
    def _():
        tile_wait(0)
        tile_wait(1)


def _scatter_rows(pad_start, pad_cnt, rows, h2, n_pad):
    tm = ROUTE_TM
    nt = h2.shape[0] // SUBLANES // (2 * tm)
    grid_spec = pltpu.PrefetchScalarGridSpec(
        num_scalar_prefetch=2,
        grid=(nt,),
        in_specs=[pl.BlockSpec((TOP_K, 2 * tm), lambda i, a, b: (0, i), memory_space=pltpu.SMEM),
                  pl.BlockSpec((2 * tm * SUBLANES, LANES), lambda i, a, b: (i, 0))],
        out_specs=pl.BlockSpec(memory_space=pl.ANY),
        scratch_shapes=[pltpu.VMEM((MOE_ROWS * SUBLANES, LANES), F32),
                        pltpu.VMEM((2, tm * SUBLANES, LANES), F32),
                        pltpu.SemaphoreType.DMA((2,)), pltpu.SemaphoreType.DMA(())],
    )
    return pl.pallas_call(
        functools.partial(_scatter_kernel, tm=tm),
        grid_spec=grid_spec,
        out_shape=jax.ShapeDtypeStruct((n_pad * SUBLANES, LANES), F32),
        compiler_params=_params(("arbitrary",)),
        name="scatter_rows",
    )(pad_start, pad_cnt, rows * SUBLANES, h2)


def _expert_kernel(be_ref, grp_ref, ia_ref, ib_ref, nv_ref, xa_ref, xb_ref, wgu_ref, bgu_ref, wd_ref, bd_ref,
                   o_ref, x_scr, wgu_scr, wd_scr):
    i = pl.program_id(0)
    live = i < nv_ref[0]

    @pl.when(live & ((i == 0) | (be_ref[i] != be_ref[jnp.maximum(i - 1, 0)])))
    def _():
        wgu_scr[...] = wgu_ref[0].astype(BF16)
        wd_scr[...] = wd_ref[0].astype(BF16)

    @pl.when(live & (grp_ref[i] == 0))
    def _():
        x_scr[...] = _tile_rows_load(xa_ref, MOE_ROWS).astype(BF16)

    @pl.when(live & (grp_ref[i] != 0))
    def _():
        x_scr[...] = _tile_rows_load(xb_ref, MOE_ROWS).astype(BF16)

    @pl.when(live)
    def _():
        gu = jnp.dot(x_scr[...], wgu_scr[...], preferred_element_type=F32) + bgu_ref[0]
        gate = jnp.minimum(gu[:, :D_MODEL], SWIGLU_LIMIT)
        up = jnp.clip(gu[:, D_MODEL:], -SWIGLU_LIMIT, SWIGLU_LIMIT)
        act = (up + 1.0) * (gate * jax.nn.sigmoid(SWIGLU_ALPHA * gate))
        _tile_rows_store(o_ref, jnp.dot(act.astype(BF16), wd_scr[...], preferred_element_type=F32) + bd_ref[0])

    @pl.when(i >= nv_ref[0])
    def _():
        o_ref[...] = jnp.zeros_like(o_ref)


def _experts(block_e, block_grp, blk_a, blk_b, n_valid, xa, xb, wgu, bgu, wd, bd):
    nb = block_e.shape[0]
    e3 = lambda i, be, gr, ia, ib, nv: (be[i], 0, 0)
    grid_spec = pltpu.PrefetchScalarGridSpec(
        num_scalar_prefetch=5,
        grid=(nb,),
        in_specs=[pl.BlockSpec((MOE_ROWS * SUBLANES, LANES), lambda i, be, gr, ia, ib, nv: (ia[i], 0)),
                  pl.BlockSpec((MOE_ROWS * SUBLANES, LANES), lambda i, be, gr, ia, ib, nv: (ib[i], 0)),
                  pl.BlockSpec((1, D_MODEL, 2 * D_MODEL), e3), pl.BlockSpec((1, 1, 2 * D_MODEL), e3),
                  pl.BlockSpec((1, D_MODEL, D_MODEL), e3), pl.BlockSpec((1, 1, D_MODEL), e3)],
        out_specs=pl.BlockSpec((MOE_ROWS * SUBLANES, LANES), lambda i, be, gr, ia, ib, nv: (i, 0)),
        scratch_shapes=[pltpu.VMEM((MOE_ROWS, D_MODEL), BF16), pltpu.VMEM((D_MODEL, 2 * D_MODEL), BF16),
                        pltpu.VMEM((D_MODEL, D_MODEL), BF16)],
    )
    return pl.pallas_call(
        _expert_kernel,
        grid_spec=grid_spec,
        out_shape=jax.ShapeDtypeStruct((nb * MOE_ROWS * SUBLANES, LANES), F32),
        compiler_params=_params(("arbitrary",), vmem=EXPERT_VMEM_LIMIT),
        name="routed_experts",
    )(block_e, block_grp, blk_a, blk_b, n_valid, xa, xb, wgu, bgu, wd, bd)


GATHER_PITCH = 12


def _combine_kernel(rows_cur, rows_nxt, x_ref, gt_ref, g_ref, y_hbm, o_ref, buf, sems, *, tm):
    i = pl.program_id(0)
    nt = pl.num_programs(0)
    units = SUBLANES * TOP_K
    per_unit = tm // units
    ahead = COMBINE_TILES // 2

    def tile_wait(slot):
        whole = y_hbm.at[pl.ds(0, TOP_K * tm * SUBLANES)]
        pltpu.make_async_copy(whole, whole, sems.at[slot]).wait()

    def start_rows(rows_ref, tile, r_lo, r_hi):
        idx = [rows_ref[k, tile * tm + r] for r in range(r_lo, r_hi) for k in range(TOP_K)]
        for n, (r, k) in enumerate((r, k) for r in range(r_lo, r_hi) for k in range(TOP_K)):
            dst = buf.at[tile, k].at[pl.ds(r * GATHER_PITCH, SUBLANES)]
            pltpu.make_async_copy(_tile_at(y_hbm, idx[n]), dst, sems.at[tile]).start(priority=k % 2)

    def finish(tile):
        nxt = tile + ahead
        nxt_rows, nxt_tile = (rows_cur, nxt) if nxt < COMBINE_TILES else (rows_nxt, nxt - COMBINE_TILES)
        tile_wait(tile)
        rows = pl.ds(tile * tm, tm)
        g8 = gt_ref[:, tile * tm:(tile + 1) * tm]
        gt = jnp.concatenate([g8, jnp.zeros((tm - SUBLANES, tm), F32)], axis=0).T
        gk = [gt[:, k:k + 1] for k in range(TOP_K)]
        ss = jnp.zeros((tm, 1), F32)
        for j in range(SUBLANES):
            cols = slice(j * LANES, (j + 1) * LANES)
            a = x_ref[rows, cols]
            for k in range(TOP_K):
                u = j * TOP_K + k
                start_rows(nxt_rows, nxt_tile, u * per_unit, (u + 1) * per_unit)
                a = a + buf[tile, k, pl.ds(j, tm, stride=GATHER_PITCH), :] * gk[k]
            ss = ss + jnp.sum(a * a, axis=-1, keepdims=True)
            o_ref[rows, cols] = a
        scale = lax.rsqrt(ss * (1.0 / D_MODEL) + EPS)
        o_ref[rows, :] = o_ref[rows, :] * scale * g_ref[...]

    @pl.when(i == 0)
    def _():
        for tile in range(ahead):
            start_rows(rows_cur, tile, 0, tm)

    for tile in range(COMBINE_TILES):
        finish(tile)

    @pl.when(i == nt - 1)
    def _():
        for tile in range(ahead):
            tile_wait(tile)


def _combine(rows, x2, gates, g, yb):
    t = x2.shape[0]
    tm = COMBINE_TM
    step = COMBINE_TILES * tm
    nt = t // step
    rows3 = rows * SUBLANES
    row = lambda i: (i, 0)
    return pl.pallas_call(
        functools.partial(_combine_kernel, tm=tm),
        grid=(nt,),
        in_specs=[pl.BlockSpec((TOP_K, step), lambda i: (0, i), memory_space=pltpu.SMEM),
                  pl.BlockSpec((TOP_K, step), lambda i: (0, jnp.minimum(i + 1, nt - 1)), memory_space=pltpu.SMEM),
                  pl.BlockSpec((step, D_MODEL), row), pl.BlockSpec((SUBLANES, step), lambda i: (0, i)),
                  pl.BlockSpec((1, D_MODEL), lambda i: (0, 0)),
                  pl.BlockSpec(memory_space=pl.ANY)],
        out_specs=pl.BlockSpec((step, D_MODEL), row),
        out_shape=jax.ShapeDtypeStruct((t, D_MODEL), F32),
        scratch_shapes=[pltpu.VMEM((COMBINE_TILES, TOP_K, tm * GATHER_PITCH, LANES), F32),
                        pltpu.SemaphoreType.DMA((COMBINE_TILES,))],
        compiler_params=_params(("arbitrary",)),
        name="combine_final_norm",
    )(rows3, rows3, x2, gates, g, yb)


def _rope_tables(seq_len):
    inv_freq = ROPE_THETA ** (-np.arange(ROPE_HALF, dtype=np.float64) * 2.0 / ROPE_DIM)
    ang = np.arange(seq_len, dtype=np.float64)[:, None] * inv_freq[None, :]
    cos = np.cos(ang).astype(np.float32)
    sin = np.sin(ang).astype(np.float32)
    pad = HEAD_DIM - ROPE_DIM
    ones = np.ones((seq_len, pad), np.float32)
    zer_h = np.zeros((seq_len, ROPE_HALF), np.float32)
    zer_p = np.zeros((seq_len, pad), np.float32)
    c = np.concatenate([cos, cos, ones], axis=1)
    s1 = np.concatenate([zer_h, sin, zer_p], axis=1)
    s2 = np.concatenate([-sin, zer_h, zer_p], axis=1)
    rep = LANES // HEAD_DIM
    return tuple(jnp.asarray(np.tile(t, (1, rep))) for t in (c, s1, s2))


def _ssm_weights(a_re, a_im, log_dt, b_re, b_im, c_re, c_im, ssm_d, nsteps):
    r = SSM_CHUNK
    dt = jnp.exp(log_dt)[..., None]
    lr, li = a_re * dt, a_im * dt

    def cpow(n):
        n = jnp.asarray(n, F32)[..., None, None, None]
        mag = jnp.exp(n * lr)
        return mag * jnp.cos(n * li), mag * jnp.sin(n * li)

    ab_re, ab_im = cpow(jnp.ones(()))
    den = a_re * a_re + a_im * a_im
    num_re, num_im = ab_re - 1.0, ab_im
    f_re = (num_re * a_re + num_im * a_im) / den
    f_im = (num_im * a_re - num_re * a_im) / den
    bb_re = f_re[..., None] * b_re - f_im[..., None] * b_im
    bb_im = f_re[..., None] * b_im + f_im[..., None] * b_re

    taus = jnp.arange(r + 1, dtype=F32)
    p_re, p_im = cpow(taus)
    m_re = p_re[..., None] * bb_re - p_im[..., None] * bb_im
    m_im = p_re[..., None] * bb_im + p_im[..., None] * bb_re
    kern = (jnp.einsum('dgcp,tdgpk->tdgck', c_re, m_re) - jnp.einsum('dgcp,tdgpk->tdgck', c_im, m_im))
    center = kern[0, 0] + kern[0, 1] + jnp.eye(SSM_GROUP, dtype=F32) * ssm_d[:, :, None]
    lags = jnp.concatenate([kern[r - 1:0:-1, 0], center[None], kern[1:r, 1]], axis=0)
    strip = lags.transpose(1, 2, 0, 3).reshape(SSM_N_GROUPS, SSM_GROUP, (2 * r - 1) * SSM_GROUP)
    w_toep = jnp.pad(strip, ((0, 0), (0, 0), (0, LAG_STRIP - (2 * r - 1) * SSM_GROUP)))

    st_f_re, st_f_im = m_re[r - 1::-1, 0], m_im[r - 1::-1, 0]
    st_b_re, st_b_im = m_re[:r, 1], m_im[:r, 1]
    w_state = jnp.stack([st_f_re, st_f_im, st_b_re, st_b_im], axis=0)
    w_state = w_state.transpose(2, 0, 3, 1, 4).reshape(SSM_N_GROUPS, 4 * SSM_STATE, TOEP)

    def c_times_pow(d, pr, pi):
        zr = c_re[d][None] * pr[:, :, None, :] - c_im[d][None] * pi[:, :, None, :]
        zi = c_re[d][None] * pi[:, :, None, :] + c_im[d][None] * pr[:, :, None, :]
        return zr, -zi
    cf_re, cf_im = c_times_pow(0, p_re[1:r + 1, 0], p_im[1:r + 1, 0])
    cb_re, cb_im = c_times_pow(1, p_re[r:0:-1, 1], p_im[r:0:-1, 1])
    w_carry = jnp.stack([cf_re, cf_im, cb_re, cb_im], axis=3)
    w_carry = w_carry.transpose(1, 0, 2, 3, 4).reshape(SSM_N_GROUPS, TOEP, 4 * SSM_STATE)

    qr, qi = p_re[r], p_im[r]
    cols = []
    for _ in range(nsteps):
        cols.append(jnp.stack([qr[0], qi[0], qr[1], qi[1]], axis=1))
        qr, qi = qr * qr - qi * qi, 2.0 * qr * qi
    pw = jnp.stack(cols, axis=-1).reshape(SSM_N_GROUPS, 4 * SSM_STATE, nsteps)
    return w_toep, w_state.astype(BF16), w_carry.astype(BF16), pw


def _front(x, prm):
    n, seq_len, _ = x.shape
    t = n * seq_len
    x2d = x.reshape(t, D_MODEL)
    q, kv = _qkv(x2d, prm['norm1_g'], prm['w_qkv'], *_rope_tables(seq_len), seq_len)
    attn_n = _attention(q, kv, prm['sink'], prm['attn_out_g'], seq_len)

    chunks_per_seq = seq_len // SSM_CHUNK
    nsteps = max(1, (chunks_per_seq - 1).bit_length())
    nc = t // SSM_CHUNK
    lt = min(256, nc)
    ut = _uproj(x2d, prm['norm1_g'], prm['w_u_t'], lt)
    yt = _ssm(ut, prm['w_toep'], prm['w_state'], prm['w_carry'], prm['pw'][:, :, :nsteps], chunks_per_seq)
    ssm_n = _glu(yt, prm['glu_w_t'], prm['glu_b'], prm['ssm_out_g'], lt)
    return _outproj(x2d, attn_n, ssm_n, prm['w_out_a'], prm['w_out_s'], prm['norm2_g'],
                    prm['router_w'], prm['router_b'], prm['tri'])


def _cumsum_small(x):
    n = x.shape[0]
    keep = jnp.arange(n)[None, :] <= jnp.arange(n)[:, None]
    return jnp.sum(jnp.where(keep, x[None, :], 0), axis=1)


def _by_expert(idx, table):
    hit = idx[None] == jnp.arange(N_EXPERTS, dtype=I32)[:, None, None]
    return jnp.sum(jnp.where(hit, table[:, None, None], 0), axis=0)


def kernel(x_prompt, x_sample, norm1_g, w_in, attn_sink, ssm_a_re, ssm_a_im, ssm_log_dt, ssm_b_re, ssm_b_im, ssm_c_re, ssm_c_im, ssm_d, glu_w, glu_b, attn_out_g, ssm_out_g, w_out, norm2_g, router_w, router_b, w_gate_up, b_gate_up, w_down, b_down, final_g):
    assert norm1_g.shape[0] == 1, "single-layer problem"
    l = 0
    xs = [x_prompt, x_sample]
    max_chunks = max(x.shape[1] for x in xs) // SSM_CHUNK
    max_steps = max(1, (max_chunks - 1).bit_length())
    wq, wk, wv, wu = jnp.split(w_in[l], [ATTN_WIDTH, ATTN_WIDTH + KV_WIDTH, ATTN_WIDTH + 2 * KV_WIDTH], axis=1)
    dup = lambda w: jnp.concatenate([w[:, :HEAD_DIM], w[:, :HEAD_DIM], w[:, HEAD_DIM:], w[:, HEAD_DIM:]], axis=1)
    w_toep, w_state, w_carry, pw = _ssm_weights(ssm_a_re[l], ssm_a_im[l], ssm_log_dt[l], ssm_b_re[l], ssm_b_im[l],
                                                ssm_c_re[l], ssm_c_im[l], ssm_d[l], max_steps)
    tri_i = lax.broadcasted_iota(I32, (ROUTE_TM, ROUTE_TM), 0)
    tri_j = lax.broadcasted_iota(I32, (ROUTE_TM, ROUTE_TM), 1)
    prm = dict(
        norm1_g=norm1_g[l].reshape(1, D_MODEL),
        w_qkv=jnp.concatenate([wq, dup(wk), dup(wv)], axis=1).astype(BF16),
        w_u_t=wu.T.astype(BF16),
        sink=attn_sink[l].astype(F32),
        attn_out_g=attn_out_g[l].reshape(1, ATTN_WIDTH),
        w_toep=w_toep, w_state=w_state, w_carry=w_carry, pw=pw,
        glu_w_t=glu_w[l].T.astype(BF16),
        glu_b=glu_b[l].reshape(SSM_WIDTH, 1),
        ssm_out_g=ssm_out_g[l].reshape(SSM_WIDTH, 1),
        w_out_a=w_out[l][:ATTN_WIDTH].astype(BF16),
        w_out_s=w_out[l][ATTN_WIDTH:].astype(BF16),
        norm2_g=norm2_g[l].reshape(1, D_MODEL),
        router_w=router_w[l].T.astype(BF16),
        router_b=router_b[l].reshape(N_EXPERTS, 1),
        tri=(tri_i < tri_j).astype(BF16),
    )
    fronts = [_front(x, prm) for x in xs]

    cnts = [f[4][:, 0].astype(I32) for f in fronts]
    padded = [(c + MOE_ROWS - 1) // MOE_ROWS * MOE_ROWS for c in cnts]
    pends = [_cumsum_small(p) for p in padded]
    pstarts = [pe - p for pe, p in zip(pends, padded)]
    nbs = [f[0].shape[0] * TOP_K // MOE_ROWS + N_EXPERTS for f in fronts]
    seg_blocks = jnp.stack([p // MOE_ROWS for p in padded], axis=1).reshape(-1)
    seg_end = _cumsum_small(seg_blocks)
    seg_start = seg_end - seg_blocks
    nb = sum(nbs)
    bi = jnp.arange(nb, dtype=I32)
    seg = jnp.minimum(jnp.sum(seg_end[None, :] <= bi[:, None], axis=1), 2 * N_EXPERTS - 1).astype(I32)
    block_e = seg // 2
    block_grp = seg % 2
    n_valid = seg_end[-1].astype(I32).reshape(1)
    seg_ids = jnp.arange(2 * N_EXPERTS, dtype=I32)
    in_seg = seg[:, None] == seg_ids[None, :]
    pick = lambda table: jnp.sum(jnp.where(in_seg, table[None, :], 0), axis=1)
    within = bi - pick(seg_start) + 1
    blk = []
    for g in range(2):
        mine = jnp.where(seg_ids % 2 == g, seg_blocks, 0)
        before = _cumsum_small(mine) - mine
        seen = pick(before) + jnp.where(block_grp == g, within, 0)
        blk.append(jnp.maximum(jnp.minimum(seen, jnp.sum(mine)) - 1, 0).astype(I32))

    xbufs, yrows = [], []
    for g, f in enumerate(fronts):
        e_idx = f[2][:TOP_K]
        rank = f[2][TOP_K:]
        n_pad = nbs[g] * MOE_ROWS
        pad_start = jnp.concatenate([pstarts[g] + cnts[g], pends[g][-1:]]).astype(I32)
        pad_cnt = jnp.concatenate([padded[g] - cnts[g], n_pad - pends[g][-1:]]).astype(I32)
        xrows = (_by_expert(e_idx, pstarts[g]) + rank).astype(I32)
        xbufs.append(_scatter_rows(pad_start, pad_cnt, xrows, f[1], n_pad))
        yrows.append((_by_expert(e_idx, seg_start[g::2] * MOE_ROWS) + rank).astype(I32))
    yb = _experts(block_e, block_grp, blk[0], blk[1], n_valid, xbufs[0], xbufs[1],
                  w_gate_up.reshape(N_EXPERTS, D_MODEL, 2 * D_MODEL), b_gate_up[l][:, None, :],
                  w_down.reshape(N_EXPERTS, D_MODEL, D_MODEL), b_down[l][:, None, :])
    gfin = final_g.reshape(1, D_MODEL)
    outs = [_combine(r, f[0], f[3], gfin, yb).reshape(x.shape) for x, f, r in zip(xs, fronts, yrows)]
    return tuple(outs)
```

```python
import functools
import math

import jax
import jax.numpy as jnp
import numpy as np
from jax import lax
from jax.experimental import pallas as pl
from jax.experimental.pallas import tpu as pltpu

F32 = jnp.float32
BF16 = jnp.bfloat16
I32 = jnp.int32

D_MODEL = 1024
HEAD_DIM = 64
N_Q_HEADS = 8
N_KV_HEADS = 2
ATTN_WIDTH = N_Q_HEADS * HEAD_DIM
KV_WIDTH = N_KV_HEADS * HEAD_DIM
WINDOW = 128
ATT_BLOCK = 128
ROPE_THETA = 500000.0
ROPE_DIM = HEAD_DIM // 4
ROPE_HALF = ROPE_DIM // 2
SSM_WIDTH = 512
SSM_GROUP = 16
SSM_N_GROUPS = SSM_WIDTH // SSM_GROUP
SSM_STATE = 64
N_EXPERTS = 32
TOP_K = 4
SWIGLU_LIMIT = 7.0
SWIGLU_ALPHA = 1.702
EPS = 1e-5

LANES = 128
SUBLANES = 8
SSM_CHUNK = 32
TOEP = SSM_CHUNK * SSM_GROUP
MOE_ROWS = 512
MXU_WIDTH = 256
ROUTE_TM = 512
COMBINE_TM = 128
COMBINE_TILES = 4
VMEM_LIMIT = 52 * 1024 * 1024
EXPERT_VMEM_LIMIT = 60 * 1024 * 1024
NEG_BIG = -1e30


def _params(sem, vmem=VMEM_LIMIT):
    return pltpu.CompilerParams(dimension_semantics=sem, vmem_limit_bytes=vmem)


def _rms(x, g):
    ms = jnp.mean(x * x, axis=-1, keepdims=True)
    return x * lax.rsqrt(ms + EPS) * g


def _tile_rows_load(ref, rows):
    return jnp.concatenate([ref[pl.ds(j, rows, stride=SUBLANES), :] for j in range(SUBLANES)], axis=1)


def _tile_rows_store(ref, val):
    rows = val.shape[0]
    for j in range(SUBLANES):
        ref[pl.ds(j, rows, stride=SUBLANES), :] = val[:, j * LANES:(j + 1) * LANES]


def _tile_at(ref, start):
    return ref.at[pl.ds(pl.multiple_of(start, SUBLANES), SUBLANES)]


def _tile_row(ref, row):
    return _tile_at(ref, row * SUBLANES)


KV_COLS = 4 * LANES


def _qkv_kernel(x_ref, g_ref, w_ref, c_ref, s1_ref, s2_ref, q_ref, kv_ref):
    h = _rms(x_ref[...], g_ref[...]).astype(BF16)
    p = jnp.dot(h, w_ref[...], preferred_element_type=F32)
    c = c_ref[...]
    s1 = s1_ref[...]
    s2 = s2_ref[...]

    def rot(t):
        return t * c + pltpu.roll(t, ROPE_HALF, 1) * s1 + pltpu.roll(t, LANES - ROPE_HALF, 1) * s2

    for j in range(ATTN_WIDTH // LANES):
        q_ref[:, j * LANES:(j + 1) * LANES] = (rot(p[:, j * LANES:(j + 1) * LANES]) * (HEAD_DIM ** -0.5)).astype(BF16)
    for j in range(N_KV_HEADS):
        col = ATTN_WIDTH + j * LANES
        kv_ref[:, j * LANES:(j + 1) * LANES] = rot(p[:, col:col + LANES]).astype(BF16)
    kv_ref[:, N_KV_HEADS * LANES:] = p[:, ATTN_WIDTH + N_KV_HEADS * LANES:].astype(BF16)


def _qkv(x2d, g, w, c, s1, s2, seq_len):
    t = x2d.shape[0]
    tm = ROUTE_TM
    nlb = seq_len // tm
    row = lambda i: (i, 0)
    tab = lambda i: (i % nlb, 0)
    full = lambda i: (0, 0)
    return pl.pallas_call(
        _qkv_kernel,
        grid=(t // tm,),
        in_specs=[pl.BlockSpec((tm, D_MODEL), row), pl.BlockSpec((1, D_MODEL), full),
                  pl.BlockSpec((D_MODEL, ATTN_WIDTH + KV_COLS), full),
                  pl.BlockSpec((tm, LANES), tab), pl.BlockSpec((tm, LANES), tab), pl.BlockSpec((tm, LANES), tab)],
        out_specs=[pl.BlockSpec((tm, ATTN_WIDTH), row), pl.BlockSpec((tm, KV_COLS), row)],
        out_shape=[jax.ShapeDtypeStruct((t, ATTN_WIDTH), BF16), jax.ShapeDtypeStruct((t, KV_COLS), BF16)],
        compiler_params=_params(("parallel",)),
        name="qkv_rotary",
    )(x2d, g, w, c, s1, s2)


def _attn_kernel(sink_ref, q_ref, kvp, kvc, kvn, g_ref, o_ref, *, bps):
    i = pl.program_id(0)
    first = (i % bps) == 0
    last = (i % bps) == bps - 1
    qi = lax.broadcasted_iota(I32, (ATT_BLOCK, 3 * ATT_BLOCK), 0)
    kj = lax.broadcasted_iota(I32, (ATT_BLOCK, 3 * ATT_BLOCK), 1)
    rel = kj - ATT_BLOCK - qi
    valid = (jnp.abs(rel) <= WINDOW)
    valid = valid & ((kj >= ATT_BLOCK) | jnp.logical_not(first))
    valid = valid & ((kj < 2 * ATT_BLOCK) | jnp.logical_not(last))
    kv = jnp.concatenate([kvp[...], kvc[...], kvn[...]], axis=0)
    ks = [kv[:, h * LANES:(h + 1) * LANES] for h in range(N_KV_HEADS)]
    vs = [kv[:, (N_KV_HEADS + h) * LANES:(N_KV_HEADS + h + 1) * LANES] for h in range(N_KV_HEADS)]
    lo = lax.broadcasted_iota(I32, (ATT_BLOCK, LANES), 1) < HEAD_DIM
    zero = jnp.zeros((ATT_BLOCK, LANES), BF16)
    heads = [(j, par) for j in range(ATTN_WIDTH // LANES) for par in range(2)]
    nt_dims = (((1,), (1,)), ((), ()))
    scores = []
    for j, par in heads:
        qt = q_ref[:, j * LANES:(j + 1) * LANES]
        qm = jnp.where(lo if par == 0 else jnp.logical_not(lo), qt, zero)
        scores.append(lax.dot_general(qm, ks[j // 2], nt_dims, preferred_element_type=F32))
    scores = [jnp.where(valid, s, NEG_BIG) for s in scores]
    sinks = [sink_ref[2 * j + par] for j, par in heads]
    maxes = [jnp.maximum(jnp.max(s, axis=-1, keepdims=True), sk) for s, sk in zip(scores, sinks)]
    probs = [jnp.exp(s - m) for s, m in zip(scores, maxes)]
    dens = [jnp.sum(p, axis=-1, keepdims=True) + jnp.exp(sk - m) for p, m, sk in zip(probs, maxes, sinks)]
    outs = [jnp.dot(p.astype(BF16), vs[j // 2], preferred_element_type=F32) for p, (j, par) in zip(probs, heads)]
    outs = [o / d for o, d in zip(outs, dens)]
    tiles = [jnp.where(lo, outs[2 * j], outs[2 * j + 1]) for j in range(ATTN_WIDTH // LANES)]
    o = jnp.concatenate(tiles, axis=1)
    o_ref[...] = _rms(o, g_ref[...]).astype(BF16)


def _attention(q, kv, sink, g, seq_len):
    t = q.shape[0]
    nblk = t // ATT_BLOCK
    bps = seq_len // ATT_BLOCK
    cur = lambda i, s: (i, 0)
    prv = lambda i, s: (jnp.maximum(i - 1, 0), 0)
    nxt = lambda i, s: (jnp.minimum(i + 1, nblk - 1), 0)
    grid_spec = pltpu.PrefetchScalarGridSpec(
        num_scalar_prefetch=1,
        grid=(nblk,),
        in_specs=[pl.BlockSpec((ATT_BLOCK, ATTN_WIDTH), cur),
                  pl.BlockSpec((ATT_BLOCK, KV_COLS), prv), pl.BlockSpec((ATT_BLOCK, KV_COLS), cur),
                  pl.BlockSpec((ATT_BLOCK, KV_COLS), nxt),
                  pl.BlockSpec((1, ATTN_WIDTH), lambda i, s: (0, 0))],
        out_specs=pl.BlockSpec((ATT_BLOCK, ATTN_WIDTH), cur),
    )
    return pl.pallas_call(
        functools.partial(_attn_kernel, bps=bps),
        grid_spec=grid_spec,
        out_shape=jax.ShapeDtypeStruct((t, ATTN_WIDTH), BF16),
        compiler_params=_params(("parallel",)),
        name="banded_attention",
    )(sink, q, kv, kv, kv, g)


def _uproj_kernel(x_ref, g_ref, w_ref, o_ref, stage_ref):
    lt = x_ref.shape[0]
    h = _rms(x_ref[...].reshape(lt * SUBLANES, D_MODEL), g_ref[...])
    for j in range(D_MODEL // LANES):
        stage_ref[j] = h[:, j * LANES:(j + 1) * LANES]
    for bl in range(SUBLANES):
        hb = jnp.concatenate([stage_ref[j, pl.ds(bl, lt, stride=SUBLANES), :] for j in range(D_MODEL // LANES)],
                             axis=1).astype(BF16)
        ut = lax.dot_general(w_ref[...], hb, (((1,), (1,)), ((), ())), preferred_element_type=F32)
        o_ref[bl] = ut.astype(BF16)


def _uproj(x2d, g, w_t, lt):
    t = x2d.shape[0]
    nc = t // SSM_CHUNK
    xv = x2d.reshape(nc, SSM_CHUNK, D_MODEL)
    return pl.pallas_call(
        _uproj_kernel,
        grid=(nc // lt, SSM_CHUNK // SUBLANES),
        in_specs=[pl.BlockSpec((lt, SUBLANES, D_MODEL), lambda i, b: (i, b, 0)),
                  pl.BlockSpec((1, D_MODEL), lambda i, b: (0, 0)),
                  pl.BlockSpec((SSM_WIDTH, D_MODEL), lambda i, b: (0, 0))],
        out_specs=pl.BlockSpec((SUBLANES, SSM_WIDTH, lt), lambda i, b: (b, 0, i)),
        out_shape=jax.ShapeDtypeStruct((SSM_CHUNK, SSM_WIDTH, nc), BF16),
        scratch_shapes=[pltpu.VMEM((D_MODEL // LANES, lt * SUBLANES, LANES), F32)],
        compiler_params=_params(("parallel", "parallel")),
        name="u_projection",
    )(xv, g, w_t)


def _gelu_tanh(x):
    return 0.5 * x * (1.0 + jnp.tanh(math.sqrt(2.0 / math.pi) * (x + 0.044715 * (x * x * x))))


LAG_STRIP = 8 * LANES


def _toeplitz_from_strip(strip):
    per_tile = LANES // SSM_GROUP
    rolled = [strip if q == 0 else pltpu.roll(strip, LAG_STRIP - q * SSM_GROUP, 1) for q in range(per_tile)]
    blocks = []
    for b in range(SSM_CHUNK):
        m, q = divmod(SSM_CHUNK - 1 - b, per_tile)
        blocks.append(rolled[q][:, m * LANES:m * LANES + TOEP])
    return jnp.concatenate(blocks, axis=0)


def _ssm_kernel(a_ref, wt_ref, ws_ref, wc_ref, pw_ref, y_ref, *, chunks_per_seq, nsteps):
    nc = a_ref.shape[2]
    a = a_ref[...].reshape(TOEP, nc)
    w_toep = _toeplitz_from_strip(wt_ref[0]).astype(BF16)
    y = jnp.dot(w_toep, a, preferred_element_type=F32)
    s = jnp.dot(ws_ref[0], a, preferred_element_type=F32)
    pos = lax.broadcasted_iota(I32, (SSM_STATE, nc), 1) % chunks_per_seq
    carries = []
    for d in range(2):
        hr = s[2 * d * SSM_STATE:(2 * d + 1) * SSM_STATE]
        hi = s[(2 * d + 1) * SSM_STATE:(2 * d + 2) * SSM_STATE]
        for k in range(nsteps):
            sh = 1 << k
            pr = pw_ref[0, 2 * d * SSM_STATE:(2 * d + 1) * SSM_STATE, k:k + 1]
            pi = pw_ref[0, (2 * d + 1) * SSM_STATE:(2 * d + 2) * SSM_STATE, k:k + 1]
            if d == 0:
                ok = pos >= sh
                sr = pltpu.roll(hr, sh, 1)
                si = pltpu.roll(hi, sh, 1)
            else:
                ok = pos < chunks_per_seq - sh
                sr = pltpu.roll(hr, nc - sh, 1)
                si = pltpu.roll(hi, nc - sh, 1)
            hr, hi = (hr + jnp.where(ok, pr * sr - pi * si, 0.0),
                      hi + jnp.where(ok, pr * si + pi * sr, 0.0))
        if d == 0:
            ok = pos >= 1
            cr = pltpu.roll(hr, 1, 1)
            ci = pltpu.roll(hi, 1, 1)
        else:
            ok = pos < chunks_per_seq - 1
            cr = pltpu.roll(hr, nc - 1, 1)
            ci = pltpu.roll(hi, nc - 1, 1)
        carries += [jnp.where(ok, cr, 0.0), jnp.where(ok, ci, 0.0)]
    carry = jnp.concatenate(carries, axis=0).astype(BF16)
    y = y + jnp.dot(wc_ref[0], carry, preferred_element_type=F32)
    y_ref[...] = _gelu_tanh(y).reshape(SSM_CHUNK, SSM_GROUP, nc)


def _ssm(ut, w_toep, w_state, w_carry, pw, chunks_per_seq):
    nc = ut.shape[2]
    nsteps = max(1, (chunks_per_seq - 1).bit_length())
    g3 = lambda g: (g, 0, 0)
    return pl.pallas_call(
        functools.partial(_ssm_kernel, chunks_per_seq=chunks_per_seq, nsteps=nsteps),
        grid=(SSM_N_GROUPS,),
        in_specs=[pl.BlockSpec((SSM_CHUNK, SSM_GROUP, nc), lambda g: (0, g, 0)),
                  pl.BlockSpec((1, SSM_GROUP, LAG_STRIP), g3),
                  pl.BlockSpec((1, 4 * SSM_STATE, TOEP), g3),
                  pl.BlockSpec((1, TOEP, 4 * SSM_STATE), g3),
                  pl.BlockSpec((1, 4 * SSM_STATE, pw.shape[2]), g3)],
        out_specs=pl.BlockSpec((SSM_CHUNK, SSM_GROUP, nc), lambda g: (0, g, 0)),
        out_shape=jax.ShapeDtypeStruct((SSM_CHUNK, SSM_WIDTH, nc), F32),
        compiler_params=_params(("parallel",)),
        name="s5_core",
    )(ut, w_toep, w_state, w_carry, pw)


def _glu_kernel(y_ref, w_ref, b_ref, g_ref, o_ref, stage_ref):
    lt = o_ref.shape[0]
    for bl in range(SUBLANES):
        y = y_ref[bl]
        z = jnp.dot(w_ref[...], y.astype(BF16), preferred_element_type=F32) + b_ref[...]
        s = y * jax.nn.sigmoid(z)
        ms = jnp.mean(s * s, axis=0, keepdims=True)
        sn = (s * lax.rsqrt(ms + EPS) * g_ref[...]).T
        for j in range(SSM_WIDTH // LANES):
            stage_ref[j, pl.ds(bl, lt, stride=SUBLANES), :] = sn[:, j * LANES:(j + 1) * LANES]
    for j in range(SSM_WIDTH // LANES):
        o_ref[:, :, j * LANES:(j + 1) * LANES] = stage_ref[j].reshape(lt, SUBLANES, LANES)


def _glu(yt, w_t, b_col, g_col, lt):
    nc = yt.shape[2]
    out = pl.pallas_call(
        _glu_kernel,
        grid=(nc // lt, SSM_CHUNK // SUBLANES),
        in_specs=[pl.BlockSpec((SUBLANES, SSM_WIDTH, lt), lambda i, b: (b, 0, i)),
                  pl.BlockSpec((SSM_WIDTH, SSM_WIDTH), lambda i, b: (0, 0)),
                  pl.BlockSpec((SSM_WIDTH, 1), lambda i, b: (0, 0)),
                  pl.BlockSpec((SSM_WIDTH, 1), lambda i, b: (0, 0))],
        out_specs=pl.BlockSpec((lt, SUBLANES, SSM_WIDTH), lambda i, b: (i, b, 0)),
        out_shape=jax.ShapeDtypeStruct((nc, SSM_CHUNK, SSM_WIDTH), F32),
        scratch_shapes=[pltpu.VMEM((SSM_WIDTH // LANES, lt * SUBLANES, LANES), F32)],
        compiler_params=_params(("parallel", "parallel")),
        name="glu_norm",
    )(yt, w_t, b_col, g_col)
    return out.reshape(nc * SSM_CHUNK, SSM_WIDTH)


def _outproj_kernel(x_ref, a_ref, s_ref, wa_ref, ws_ref, g_ref, wr_ref, br_ref, tri_ref,
                    x2_ref, h2_ref, rt_ref, gt_ref, cnt_ref, run_ref):
    i = pl.program_id(0)

    @pl.when(i == 0)
    def _():
        run_ref[...] = jnp.zeros_like(run_ref)

    x2 = (x_ref[...] + jnp.dot(a_ref[...], wa_ref[...], preferred_element_type=F32)
          + jnp.dot(s_ref[...].astype(BF16), ws_ref[...], preferred_element_type=F32))
    x2_ref[...] = x2
    h2 = _rms(x2, g_ref[...])
    _tile_rows_store(h2_ref, h2)
    logits = lax.dot_general(wr_ref[...], h2.astype(BF16), (((1,), (1,)), ((), ())),
                             preferred_element_type=F32) + br_ref[...]
    tm = logits.shape[1]
    sub = lax.broadcasted_iota(I32, (N_EXPERTS, tm), 0)
    sub_f = sub.astype(F32)
    work = logits
    sel = jnp.zeros((N_EXPERTS, tm), F32)
    top_v, top_i = [], []
    for _ in range(TOP_K):
        m = jnp.max(work, axis=0, keepdims=True)
        idx = jnp.min(jnp.where(work == m, sub_f, float(N_EXPERTS)), axis=0, keepdims=True).astype(I32)
        hit = sub == idx
        sel = jnp.where(hit, 1.0, sel)
        work = jnp.where(hit, -jnp.inf, work)
        top_v.append(m)
        top_i.append(idx)
    ex = [jnp.exp(v - top_v[0]) for v in top_v]
    den = ex[0] + ex[1] + ex[2] + ex[3]
    before = jnp.dot(sel.astype(BF16), tri_ref[...], preferred_element_type=F32) + run_ref[:, 0:1]
    ranks = [jnp.sum(jnp.where(sub == top_i[k], before, 0.0), axis=0, keepdims=True).astype(I32) for k in range(TOP_K)]
    rt_ref[...] = jnp.concatenate(top_i + ranks, axis=0)
    gt_ref[...] = jnp.concatenate([e / den for e in ex] + [jnp.zeros((SUBLANES - TOP_K, tm), F32)], axis=0)
    run = run_ref[...] + jnp.sum(sel, axis=1, keepdims=True)
    run_ref[...] = run
    cnt_ref[...] = run


def _outproj(x2d, attn_n, ssm_n, w_a, w_s, g, w_r, b_r, tri):
    t = x2d.shape[0]
    tm = ROUTE_TM
    row = lambda i: (i, 0)
    full = lambda i: (0, 0)
    return pl.pallas_call(
        _outproj_kernel,
        grid=(t // tm,),
        in_specs=[pl.BlockSpec((tm, D_MODEL), row), pl.BlockSpec((tm, ATTN_WIDTH), row),
                  pl.BlockSpec((tm, SSM_WIDTH), row),
                  pl.BlockSpec((ATTN_WIDTH, D_MODEL), full), pl.BlockSpec((SSM_WIDTH, D_MODEL), full),
                  pl.BlockSpec((1, D_MODEL), full),
                  pl.BlockSpec((N_EXPERTS, D_MODEL), full), pl.BlockSpec((N_EXPERTS, 1), full),
                  pl.BlockSpec((tm, tm), full)],
        out_specs=[pl.BlockSpec((tm, D_MODEL), row), pl.BlockSpec((tm * SUBLANES, LANES), row),
                   pl.BlockSpec((2 * TOP_K, tm), lambda i: (0, i)), pl.BlockSpec((SUBLANES, tm), lambda i: (0, i)),
                   pl.BlockSpec((N_EXPERTS, LANES), full)],
        out_shape=[jax.ShapeDtypeStruct((t, D_MODEL), F32), jax.ShapeDtypeStruct((t * SUBLANES, LANES), F32),
                   jax.ShapeDtypeStruct((2 * TOP_K, t), I32), jax.ShapeDtypeStruct((SUBLANES, t), F32),
                   jax.ShapeDtypeStruct((N_EXPERTS, LANES), F32)],
        scratch_shapes=[pltpu.VMEM((N_EXPERTS, LANES), F32)],
        compiler_params=_params(("arbitrary",)),
        name="out_projection_router",
    )(x2d, attn_n, ssm_n, w_a, w_s, g, w_r, b_r, tri)


DMA_GROUP = 4


def _rows_per_step(rows, tokens):
    nt = rows.shape[1] // tokens
    return (rows * SUBLANES).reshape(TOP_K, nt, tokens).transpose(1, 0, 2).reshape(nt, 1, TOP_K * tokens)


def _scatter_kernel(pad_start_ref, pad_cnt_ref, rows_ref, h_ref, xb_hbm, zero_ref, stage_ref, sems, zsem, *, tm):
    i = pl.program_id(0)
    nt = pl.num_programs(0)
    n_dma = tm * TOP_K

    def tile_wait(slot):
        whole = xb_hbm.at[pl.ds(0, n_dma * SUBLANES)]
        pltpu.make_async_copy(whole, whole, sems.at[slot]).wait()

    @pl.when(i == 0)
    def _():
        zero_ref[...] = jnp.zeros_like(zero_ref)

        def fill(row, n_rows):
            return pltpu.make_async_copy(zero_ref.at[pl.ds(0, n_rows * SUBLANES)],
                                         xb_hbm.at[pl.ds(pl.multiple_of(row * SUBLANES, SUBLANES), n_rows * SUBLANES)],
                                         zsem)

        for phase in range(2):
            def per_expert(e, c):
                cnt = pad_cnt_ref[e]
                row = pad_start_ref[e]
                run = MOE_ROWS // 2
                while run >= 1:
                    below = cnt & ~(2 * run - 1)

                    @pl.when((cnt & run) != 0)
                    def _(run=run, below=below):
                        cp = fill(row + below, run)
                        cp.start() if phase == 0 else cp.wait()
                    run //= 2
                return c
            lax.fori_loop(0, N_EXPERTS, per_expert, 0)

            def per_tail_block(b, c):
                cp = fill(pad_start_ref[N_EXPERTS] + b * MOE_ROWS, MOE_ROWS)
                cp.start() if phase == 0 else cp.wait()
                return c
            lax.fori_loop(0, pad_cnt_ref[N_EXPERTS] // MOE_ROWS, per_tail_block, 0)

    for slot in range(2):
        @pl.when(i > 0)
        def _(slot=slot):
            tile_wait(slot)

        stage_ref[slot] = h_ref[pl.ds(slot * tm * SUBLANES, tm * SUBLANES), :]

        def issue(grp, c, slot=slot):
            r0 = grp * DMA_GROUP
            dst = [rows_ref[0, 0, k * (2 * tm) + slot * tm + r0 + r] for r in range(DMA_GROUP) for k in range(TOP_K)]
            for r in range(DMA_GROUP):
                src = _tile_row(stage_ref.at[slot], r0 + r)
                for k in range(TOP_K):
                    pltpu.make_async_copy(src, _tile_at(xb_hbm, dst[r * TOP_K + k]),
                                          sems.at[slot]).start(priority=k % 2)
            return c
        lax.fori_loop(0, tm // DMA_GROUP, issue, 0)

    @pl.when(i == nt - 1)
    def _():
        tile_wait(0)
        tile_wait(1)


def _scatter_rows(pad_start, pad_cnt, rows, h2, n_pad):
    tm = ROUTE_TM
    nt = h2.shape[0] // SUBLANES // (2 * tm)
    grid_spec = pltpu.PrefetchScalarGridSpec(
        num_scalar_prefetch=2,
        grid=(nt,),
        in_specs=[pl.BlockSpec((1, 1, 2 * tm * TOP_K), lambda i, a, b: (i, 0, 0), memory_space=pltpu.SMEM),
                  pl.BlockSpec((2 * tm * SUBLANES, LANES), lambda i, a, b: (i, 0))],
        out_specs=pl.BlockSpec(memory_space=pl.ANY),
        scratch_shapes=[pltpu.VMEM((MOE_ROWS * SUBLANES, LANES), F32),
                        pltpu.VMEM((2, tm * SUBLANES, LANES), F32),
                        pltpu.SemaphoreType.DMA((2,)), pltpu.SemaphoreType.DMA(())],
    )
    return pl.pallas_call(
        functools.partial(_scatter_kernel, tm=tm),
        grid_spec=grid_spec,
        out_shape=jax.ShapeDtypeStruct((n_pad * SUBLANES, LANES), F32),
        compiler_params=_params(("arbitrary",)),
        name="scatter_rows",
    )(pad_start, pad_cnt, _rows_per_step(rows, 2 * tm), h2)


def _expert_kernel(be_ref, grp_ref, ia_ref, ib_ref, nv_ref, xa_ref, xb_ref, wgu_ref, bgu_ref, wd_ref, bd_ref,
                   o_ref, x_scr, wgu_scr, wd_scr):
    i = pl.program_id(0)
    live = i < nv_ref[0]

    @pl.when(live & ((i == 0) | (be_ref[i] != be_ref[jnp.maximum(i - 1, 0)])))
    def _():
        wgu_scr[...] = wgu_ref[0].astype(BF16)
        wd_scr[...] = wd_ref[0].astype(BF16)

    @pl.when(live & (grp_ref[i] == 0))
    def _():
        x_scr[...] = _tile_rows_load(xa_ref, MOE_ROWS).astype(BF16)

    @pl.when(live & (grp_ref[i] != 0))
    def _():
        x_scr[...] = _tile_rows_load(xb_ref, MOE_ROWS).astype(BF16)

    @pl.when(live)
    def _():
        gu = jnp.dot(x_scr[...], wgu_scr[...], preferred_element_type=F32) + bgu_ref[0]
        gate = jnp.minimum(gu[:, :D_MODEL], SWIGLU_LIMIT)
        up = jnp.clip(gu[:, D_MODEL:], -SWIGLU_LIMIT, SWIGLU_LIMIT)
        act = (up + 1.0) * (gate * jax.nn.sigmoid(SWIGLU_ALPHA * gate))
        _tile_rows_store(o_ref, jnp.dot(act.astype(BF16), wd_scr[...], preferred_element_type=F32) + bd_ref[0])

    @pl.when(i >= nv_ref[0])
    def _():
        o_ref[...] = jnp.zeros_like(o_ref)


def _experts(block_e, block_grp, blk_a, blk_b, n_valid, xa, xb, wgu, bgu, wd, bd):
    nb = block_e.shape[0]
    e3 = lambda i, be, gr, ia, ib, nv: (be[i], 0, 0)
    grid_spec = pltpu.PrefetchScalarGridSpec(
        num_scalar_prefetch=5,
        grid=(nb,),
        in_specs=[pl.BlockSpec((MOE_ROWS * SUBLANES, LANES), lambda i, be, gr, ia, ib, nv: (ia[i], 0)),
                  pl.BlockSpec((MOE_ROWS * SUBLANES, LANES), lambda i, be, gr, ia, ib, nv: (ib[i], 0)),
                  pl.BlockSpec((1, D_MODEL, 2 * D_MODEL), e3), pl.BlockSpec((1, 1, 2 * D_MODEL), e3),
                  pl.BlockSpec((1, D_MODEL, D_MODEL), e3), pl.BlockSpec((1, 1, D_MODEL), e3)],
        out_specs=pl.BlockSpec((MOE_ROWS * SUBLANES, LANES), lambda i, be, gr, ia, ib, nv: (i, 0)),
        scratch_shapes=[pltpu.VMEM((MOE_ROWS, D_MODEL), BF16), pltpu.VMEM((D_MODEL, 2 * D_MODEL), BF16),
                        pltpu.VMEM((D_MODEL, D_MODEL), BF16)],
    )
    return pl.pallas_call(
        _expert_kernel,
        grid_spec=grid_spec,
        out_shape=jax.ShapeDtypeStruct((nb * MOE_ROWS * SUBLANES, LANES), F32),
        compiler_params=_params(("arbitrary",), vmem=EXPERT_VMEM_LIMIT),
        name="routed_experts",
    )(block_e, block_grp, blk_a, blk_b, n_valid, xa, xb, wgu, bgu, wd, bd)


GATHER_PITCH = 12


def _combine_kernel(rows_cur, rows_nxt, x_ref, gt_ref, g_ref, y_hbm, o_ref, buf, sems, *, tm):
    i = pl.program_id(0)
    nt = pl.num_programs(0)
    units = SUBLANES * TOP_K
    per_unit = tm // units
    ahead = COMBINE_TILES // 2

    def tile_wait(slot):
        whole = y_hbm.at[pl.ds(0, TOP_K * tm * SUBLANES)]
        pltpu.make_async_copy(whole, whole, sems.at[slot]).wait()

    def start_rows(rows_ref, tile, r_lo, r_hi):
        idx = [rows_ref[0, 0, k * (COMBINE_TILES * tm) + tile * tm + r] for r in range(r_lo, r_hi) for k in range(TOP_K)]
        for n, (r, k) in enumerate((r, k) for r in range(r_lo, r_hi) for k in range(TOP_K)):
            dst = buf.at[tile, k].at[pl.ds(r * GATHER_PITCH, SUBLANES)]
            pltpu.make_async_copy(_tile_at(y_hbm, idx[n]), dst, sems.at[tile]).start(priority=k % 2)

    def finish(tile):
        nxt = tile + ahead
        nxt_rows, nxt_tile = (rows_cur, nxt) if nxt < COMBINE_TILES else (rows_nxt, nxt - COMBINE_TILES)
        tile_wait(tile)
        rows = pl.ds(tile * tm, tm)
        g8 = gt_ref[:, tile * tm:(tile + 1) * tm]
        gt = jnp.concatenate([g8, jnp.zeros((tm - SUBLANES, tm), F32)], axis=0).T
        gk = [gt[:, k:k + 1] for k in range(TOP_K)]
        ss = jnp.zeros((tm, 1), F32)
        for j in range(SUBLANES):
            cols = slice(j * LANES, (j + 1) * LANES)
            a = x_ref[rows, cols]
            for k in range(TOP_K):
                u = j * TOP_K + k
                start_rows(nxt_rows, nxt_tile, u * per_unit, (u + 1) * per_unit)
                a = a + buf[tile, k, pl.ds(j, tm, stride=GATHER_PITCH), :] * gk[k]
            ss = ss + jnp.sum(a * a, axis=-1, keepdims=True)
            o_ref[rows, cols] = a
        scale = lax.rsqrt(ss * (1.0 / D_MODEL) + EPS)
        o_ref[rows, :] = o_ref[rows, :] * scale * g_ref[...]

    @pl.when(i == 0)
    def _():
        for tile in range(ahead):
            start_rows(rows_cur, tile, 0, tm)

    for tile in range(COMBINE_TILES):
        finish(tile)

    @pl.when(i == nt - 1)
    def _():
        for tile in range(ahead):
            tile_wait(tile)


def _combine(rows, x2, gates, g, yb):
    t = x2.shape[0]
    tm = COMBINE_TM
    step = COMBINE_TILES * tm
    nt = t // step
    rows3 = _rows_per_step(rows, step)
    row = lambda i: (i, 0)
    return pl.pallas_call(
        functools.partial(_combine_kernel, tm=tm),
        grid=(nt,),
        in_specs=[pl.BlockSpec((1, 1, step * TOP_K), lambda i: (i, 0, 0), memory_space=pltpu.SMEM),
                  pl.BlockSpec((1, 1, step * TOP_K), lambda i: (jnp.minimum(i + 1, nt - 1), 0, 0),
                               memory_space=pltpu.SMEM),
                  pl.BlockSpec((step, D_MODEL), row), pl.BlockSpec((SUBLANES, step), lambda i: (0, i)),
                  pl.BlockSpec((1, D_MODEL), lambda i: (0, 0)),
                  pl.BlockSpec(memory_space=pl.ANY)],
        out_specs=pl.BlockSpec((step, D_MODEL), row),
        out_shape=jax.ShapeDtypeStruct((t, D_MODEL), F32),
        scratch_shapes=[pltpu.VMEM((COMBINE_TILES, TOP_K, tm * GATHER_PITCH, LANES), F32),
                        pltpu.SemaphoreType.DMA((COMBINE_TILES,))],
        compiler_params=_params(("arbitrary",)),
        name="combine_final_norm",
    )(rows3, rows3, x2, gates, g, yb)


def _rope_tables(seq_len):
    inv_freq = ROPE_THETA ** (-np.arange(ROPE_HALF, dtype=np.float64) * 2.0 / ROPE_DIM)
    ang = np.arange(seq_len, dtype=np.float64)[:, None] * inv_freq[None, :]
    cos = np.cos(ang).astype(np.float32)
    sin = np.sin(ang).astype(np.float32)
    pad = HEAD_DIM - ROPE_DIM
    ones = np.ones((seq_len, pad), np.float32)
    zer_h = np.zeros((seq_len, ROPE_HALF), np.float32)
    zer_p = np.zeros((seq_len, pad), np.float32)
    c = np.concatenate([cos, cos, ones], axis=1)
    s1 = np.concatenate([zer_h, sin, zer_p], axis=1)
    s2 = np.concatenate([-sin, zer_h, zer_p], axis=1)
    rep = LANES // HEAD_DIM
    return tuple(jnp.asarray(np.tile(t, (1, rep))) for t in (c, s1, s2))


def _ssm_weights(a_re, a_im, log_dt, b_re, b_im, c_re, c_im, ssm_d, nsteps):
    r = SSM_CHUNK
    dt = jnp.exp(log_dt)[..., None]
    lr, li = a_re * dt, a_im * dt

    def cpow(n):
        n = jnp.asarray(n, F32)[..., None, None, None]
        mag = jnp.exp(n * lr)
        return mag * jnp.cos(n * li), mag * jnp.sin(n * li)

    ab_re, ab_im = cpow(jnp.ones(()))
    den = a_re * a_re + a_im * a_im
    num_re, num_im = ab_re - 1.0, ab_im
    f_re = (num_re * a_re + num_im * a_im) / den
    f_im = (num_im * a_re - num_re * a_im) / den
    bb_re = f_re[..., None] * b_re - f_im[..., None] * b_im
    bb_im = f_re[..., None] * b_im + f_im[..., None] * b_re

    taus = jnp.arange(r + 1, dtype=F32)
    p_re, p_im = cpow(taus)
    m_re = p_re[..., None] * bb_re - p_im[..., None] * bb_im
    m_im = p_re[..., None] * bb_im + p_im[..., None] * bb_re
    kern = (jnp.einsum('dgcp,tdgpk->tdgck', c_re, m_re) - jnp.einsum('dgcp,tdgpk->tdgck', c_im, m_im))
    center = kern[0, 0] + kern[0, 1] + jnp.eye(SSM_GROUP, dtype=F32) * ssm_d[:, :, None]
    lags = jnp.concatenate([kern[r - 1:0:-1, 0], center[None], kern[1:r, 1]], axis=0)
    strip = lags.transpose(1, 2, 0, 3).reshape(SSM_N_GROUPS, SSM_GROUP, (2 * r - 1) * SSM_GROUP)
    w_toep = jnp.pad(strip, ((0, 0), (0, 0), (0, LAG_STRIP - (2 * r - 1) * SSM_GROUP)))

    st_f_re, st_f_im = m_re[r - 1::-1, 0], m_im[r - 1::-1, 0]
    st_b_re, st_b_im = m_re[:r, 1], m_im[:r, 1]
    w_state = jnp.stack([st_f_re, st_f_im, st_b_re, st_b_im], axis=0)
    w_state = w_state.transpose(2, 0, 3, 1, 4).reshape(SSM_N_GROUPS, 4 * SSM_STATE, TOEP)

    def c_times_pow(d, pr, pi):
        zr = c_re[d][None] * pr[:, :, None, :] - c_im[d][None] * pi[:, :, None, :]
        zi = c_re[d][None] * pi[:, :, None, :] + c_im[d][None] * pr[:, :, None, :]
        return zr, -zi
    cf_re, cf_im = c_times_pow(0, p_re[1:r + 1, 0], p_im[1:r + 1, 0])
    cb_re, cb_im = c_times_pow(1, p_re[r:0:-1, 1], p_im[r:0:-1, 1])
    w_carry = jnp.stack([cf_re, cf_im, cb_re, cb_im], axis=3)
    w_carry = w_carry.transpose(1, 0, 2, 3, 4).reshape(SSM_N_GROUPS, TOEP, 4 * SSM_STATE)

    qr, qi = p_re[r], p_im[r]
    cols = []
    for _ in range(nsteps):
        cols.append(jnp.stack([qr[0], qi[0], qr[1], qi[1]], axis=1))
        qr, qi = qr * qr - qi * qi, 2.0 * qr * qi
    pw = jnp.stack(cols, axis=-1).reshape(SSM_N_GROUPS, 4 * SSM_STATE, nsteps)
    return w_toep, w_state.astype(BF16), w_carry.astype(BF16), pw


def _front(x, prm):
    n, seq_len, _ = x.shape
    t = n * seq_len
    x2d = x.reshape(t, D_MODEL)
    q, kv = _qkv(x2d, prm['norm1_g'], prm['w_qkv'], *_rope_tables(seq_len), seq_len)
    attn_n = _attention(q, kv, prm['sink'], prm['attn_out_g'], seq_len)

    chunks_per_seq = seq_len // SSM_CHUNK
    nsteps = max(1, (chunks_per_seq - 1).bit_length())
    nc = t // SSM_CHUNK
    lt = min(MXU_WIDTH, nc)
    ut = _uproj(x2d, prm['norm1_g'], prm['w_u_t'], lt)
    yt = _ssm(ut, prm['w_toep'], prm['w_state'], prm['w_carry'], prm['pw'][:, :, :nsteps], chunks_per_seq)
    ssm_n = _glu(yt, prm['glu_w_t'], prm['glu_b'], prm['ssm_out_g'], lt)
    return _outproj(x2d, attn_n, ssm_n, prm['w_out_a'], prm['w_out_s'], prm['norm2_g'],
                    prm['router_w'], prm['router_b'], prm['tri'])


def _cumsum_small(x):
    n = x.shape[0]
    keep = jnp.arange(n)[None, :] <= jnp.arange(n)[:, None]
    return jnp.sum(jnp.where(keep, x[None, :], 0), axis=1)


def _by_expert(idx, table):
    hit = idx[None] == jnp.arange(N_EXPERTS, dtype=I32)[:, None, None]
    return jnp.sum(jnp.where(hit, table[:, None, None], 0), axis=0)


def kernel(x_prompt, x_sample, norm1_g, w_in, attn_sink, ssm_a_re, ssm_a_im, ssm_log_dt, ssm_b_re, ssm_b_im, ssm_c_re, ssm_c_im, ssm_d, glu_w, glu_b, attn_out_g, ssm_out_g, w_out, norm2_g, router_w, router_b, w_gate_up, b_gate_up, w_down, b_down, final_g):
    assert norm1_g.shape[0] == 1, "single-layer problem"
    l = 0
    xs = [x_prompt, x_sample]
    max_chunks = max(x.shape[1] for x in xs) // SSM_CHUNK
    max_steps = max(1, (max_chunks - 1).bit_length())
    wq, wk, wv, wu = jnp.split(w_in[l], [ATTN_WIDTH, ATTN_WIDTH + KV_WIDTH, ATTN_WIDTH + 2 * KV_WIDTH], axis=1)
    dup = lambda w: jnp.concatenate([w[:, :HEAD_DIM], w[:, :HEAD_DIM], w[:, HEAD_DIM:], w[:, HEAD_DIM:]], axis=1)
    w_toep, w_state, w_carry, pw = _ssm_weights(ssm_a_re[l], ssm_a_im[l], ssm_log_dt[l], ssm_b_re[l], ssm_b_im[l],
                                                ssm_c_re[l], ssm_c_im[l], ssm_d[l], max_steps)
    tri_i = lax.broadcasted_iota(I32, (ROUTE_TM, ROUTE_TM), 0)
    tri_j = lax.broadcasted_iota(I32, (ROUTE_TM, ROUTE_TM), 1)
    prm = dict(
        norm1_g=norm1_g[l].reshape(1, D_MODEL),
        w_qkv=jnp.concatenate([wq, dup(wk), dup(wv)], axis=1).astype(BF16),
        w_u_t=wu.T.astype(BF16),
        sink=attn_sink[l].astype(F32),
        attn_out_g=attn_out_g[l].reshape(1, ATTN_WIDTH),
        w_toep=w_toep, w_state=w_state, w_carry=w_carry, pw=pw,
        glu_w_t=glu_w[l].T.astype(BF16),
        glu_b=glu_b[l].reshape(SSM_WIDTH, 1),
        ssm_out_g=ssm_out_g[l].reshape(SSM_WIDTH, 1),
        w_out_a=w_out[l][:ATTN_WIDTH].astype(BF16),
        w_out_s=w_out[l][ATTN_WIDTH:].astype(BF16),
        norm2_g=norm2_g[l].reshape(1, D_MODEL),
        router_w=router_w[l].T.astype(BF16),
        router_b=router_b[l].reshape(N_EXPERTS, 1),
        tri=(tri_i < tri_j).astype(BF16),
    )
    fronts = [_front(x, prm) for x in xs]

    cnts = [f[4][:, 0].astype(I32) for f in fronts]
    padded = [(c + MOE_ROWS - 1) // MOE_ROWS * MOE_ROWS for c in cnts]
    pends = [_cumsum_small(p) for p in padded]
    pstarts = [pe - p for pe, p in zip(pends, padded)]
    nbs = [f[0].shape[0] * TOP_K // MOE_ROWS + N_EXPERTS for f in fronts]
    seg_blocks = jnp.stack([p // MOE_ROWS for p in padded], axis=1).reshape(-1)
    seg_end = _cumsum_small(seg_blocks)
    seg_start = seg_end - seg_blocks
    nb = sum(nbs)
    bi = jnp.arange(nb, dtype=I32)
    seg = jnp.minimum(jnp.sum(seg_end[None, :] <= bi[:, None], axis=1), 2 * N_EXPERTS - 1).astype(I32)
    block_e = seg // 2
    block_grp = seg % 2
    n_valid = seg_end[-1].astype(I32).reshape(1)
    seg_ids = jnp.arange(2 * N_EXPERTS, dtype=I32)
    in_seg = seg[:, None] == seg_ids[None, :]
    pick = lambda table: jnp.sum(jnp.where(in_seg, table[None, :], 0), axis=1)
    within = bi - pick(seg_start) + 1
    blk = []
    for g in range(2):
        mine = jnp.where(seg_ids % 2 == g, seg_blocks, 0)
        before = _cumsum_small(mine) - mine
        seen = pick(before) + jnp.where(block_grp == g, within, 0)
        blk.append(jnp.maximum(jnp.minimum(seen, jnp.sum(mine)) - 1, 0).astype(I32))

    xbufs, yrows = [], []
    for g, f in enumerate(fronts):
        e_idx = f[2][:TOP_K]
        rank = f[2][TOP_K:]
        n_pad = nbs[g] * MOE_ROWS
        pad_start = jnp.concatenate([pstarts[g] + cnts[g], pends[g][-1:]]).astype(I32)
        pad_cnt = jnp.concatenate([padded[g] - cnts[g], n_pad - pends[g][-1:]]).astype(I32)
        xrows = (_by_expert(e_idx, pstarts[g]) + rank).astype(I32)
        xbufs.append(_scatter_rows(pad_start, pad_cnt, xrows, f[1], n_pad))
        yrows.append((_by_expert(e_idx, seg_start[g::2] * MOE_ROWS) + rank).astype(I32))
    yb = _experts(block_e, block_grp, blk[0], blk[1], n_valid, xbufs[0], xbufs[1],
                  w_gate_up.reshape(N_EXPERTS, D_MODEL, 2 * D_MODEL), b_gate_up[l][:, None, :],
                  w_down.reshape(N_EXPERTS, D_MODEL, D_MODEL), b_down[l][:, None, :])
    gfin = final_g.reshape(1, D_MODEL)
    outs = [_combine(r, f[0], f[3], gfin, yb).reshape(x.shape) for x, f, r in zip(xs, fronts, yrows)]
    return tuple(outs)
```

```python
import functools
import math

import jax
import jax.numpy as jnp
import numpy as np
from jax import lax
from jax.experimental import pallas as pl
from jax.experimental.pallas import tpu as pltpu

F32 = jnp.float32
BF16 = jnp.bfloat16
I32 = jnp.int32

D_MODEL = 1024
HEAD_DIM = 64
N_Q_HEADS = 8
N_KV_HEADS = 2
ATTN_WIDTH = N_Q_HEADS * HEAD_DIM
KV_WIDTH = N_KV_HEADS * HEAD_DIM
WINDOW = 128
ATT_BLOCK = 128
ROPE_THETA = 500000.0
ROPE_DIM = HEAD_DIM // 4
ROPE_HALF = ROPE_DIM // 2
SSM_WIDTH = 512
SSM_GROUP = 16
SSM_N_GROUPS = SSM_WIDTH // SSM_GROUP
SSM_STATE = 64
N_EXPERTS = 32
TOP_K = 4
SWIGLU_LIMIT = 7.0
SWIGLU_ALPHA = 1.702
EPS = 1e-5

LANES = 128
SUBLANES = 8
SSM_CHUNK = 32
TOEP = SSM_CHUNK * SSM_GROUP
MOE_ROWS = 512
MXU_WIDTH = 256
ROUTE_TM = 512
COMBINE_TM = 128
COMBINE_TILES = 4
VMEM_LIMIT = 52 * 1024 * 1024
EXPERT_VMEM_LIMIT = 60 * 1024 * 1024
NEG_BIG = -1e30


def _params(sem, vmem=VMEM_LIMIT):
    return pltpu.CompilerParams(dimension_semantics=sem, vmem_limit_bytes=vmem)


def _rms(x, g):
    ms = jnp.mean(x * x, axis=-1, keepdims=True)
    return x * lax.rsqrt(ms + EPS) * g


def _tile_rows_load(ref, rows):
    return jnp.concatenate([ref[pl.ds(j, rows, stride=SUBLANES), :] for j in range(SUBLANES)], axis=1)


def _tile_rows_store(ref, val):
    rows = val.shape[0]
    for j in range(SUBLANES):
        ref[pl.ds(j, rows, stride=SUBLANES), :] = val[:, j * LANES:(j + 1) * LANES]


def _tile_at(ref, start):
    return ref.at[pl.ds(pl.multiple_of(start, SUBLANES), SUBLANES)]


def _tile_row(ref, row):
    return _tile_at(ref, row * SUBLANES)


KV_COLS = 4 * LANES


def _qkv_kernel(x_ref, g_ref, w_ref, c_ref, s1_ref, s2_ref, q_ref, kv_ref):
    h = _rms(x_ref[...], g_ref[...]).astype(BF16)
    p = jnp.dot(h, w_ref[...], preferred_element_type=F32)
    c = c_ref[...]
    s1 = s1_ref[...]
    s2 = s2_ref[...]

    def rot(t):
        return t * c + pltpu.roll(t, ROPE_HALF, 1) * s1 + pltpu.roll(t, LANES - ROPE_HALF, 1) * s2

    for j in range(ATTN_WIDTH // LANES):
        q_ref[:, j * LANES:(j + 1) * LANES] = (rot(p[:, j * LANES:(j + 1) * LANES]) * (HEAD_DIM ** -0.5)).astype(BF16)
    for j in range(N_KV_HEADS):
        col = ATTN_WIDTH + j * LANES
        kv_ref[:, j * LANES:(j + 1) * LANES] = rot(p[:, col:col + LANES]).astype(BF16)
    kv_ref[:, N_KV_HEADS * LANES:] = p[:, ATTN_WIDTH + N_KV_HEADS * LANES:].astype(BF16)


def _qkv(x2d, g, w, c, s1, s2, seq_len):
    t = x2d.shape[0]
    tm = ROUTE_TM
    nlb = seq_len // tm
    row = lambda i: (i, 0)
    tab = lambda i: (i % nlb, 0)
    full = lambda i: (0, 0)
    return pl.pallas_call(
        _qkv_kernel,
        grid=(t // tm,),
        in_specs=[pl.BlockSpec((tm, D_MODEL), row), pl.BlockSpec((1, D_MODEL), full),
                  pl.BlockSpec((D_MODEL, ATTN_WIDTH + KV_COLS), full),
                  pl.BlockSpec((tm, LANES), tab), pl.BlockSpec((tm, LANES), tab), pl.BlockSpec((tm, LANES), tab)],
        out_specs=[pl.BlockSpec((tm, ATTN_WIDTH), row), pl.BlockSpec((tm, KV_COLS), row)],
        out_shape=[jax.ShapeDtypeStruct((t, ATTN_WIDTH), BF16), jax.ShapeDtypeStruct((t, KV_COLS), BF16)],
        compiler_params=_params(("parallel",)),
        name="qkv_rotary",
    )(x2d, g, w, c, s1, s2)


def _attn_kernel(sink_ref, q_ref, kvp, kvc, kvn, g_ref, o_ref, *, bps):
    i = pl.program_id(0)
    first = (i % bps) == 0
    last = (i % bps) == bps - 1
    qi = lax.broadcasted_iota(I32, (ATT_BLOCK, 3 * ATT_BLOCK), 0)
    kj = lax.broadcasted_iota(I32, (ATT_BLOCK, 3 * ATT_BLOCK), 1)
    rel = kj - ATT_BLOCK - qi
    valid = (jnp.abs(rel) <= WINDOW)
    valid = valid & ((kj >= ATT_BLOCK) | jnp.logical_not(first))
    valid = valid & ((kj < 2 * ATT_BLOCK) | jnp.logical_not(last))
    kv = jnp.concatenate([kvp[...], kvc[...], kvn[...]], axis=0)
    ks = [kv[:, h * LANES:(h + 1) * LANES] for h in range(N_KV_HEADS)]
    vs = [kv[:, (N_KV_HEADS + h) * LANES:(N_KV_HEADS + h + 1) * LANES] for h in range(N_KV_HEADS)]
    lo = lax.broadcasted_iota(I32, (ATT_BLOCK, LANES), 1) < HEAD_DIM
    zero = jnp.zeros((ATT_BLOCK, LANES), BF16)
    heads = [(j, par) for j in range(ATTN_WIDTH // LANES) for par in range(2)]
    nt_dims = (((1,), (1,)), ((), ()))
    scores = []
    for j, par in heads:
        qt = q_ref[:, j * LANES:(j + 1) * LANES]
        qm = jnp.where(lo if par == 0 else jnp.logical_not(lo), qt, zero)
        scores.append(lax.dot_general(qm, ks[j // 2], nt_dims, preferred_element_type=F32))
    scores = [jnp.where(valid, s, NEG_BIG) for s in scores]
    sinks = [sink_ref[2 * j + par] for j, par in heads]
    maxes = [jnp.maximum(jnp.max(s, axis=-1, keepdims=True), sk) for s, sk in zip(scores, sinks)]
    probs = [jnp.exp(s - m) for s, m in zip(scores, maxes)]
    dens = [jnp.sum(p, axis=-1, keepdims=True) + jnp.exp(sk - m) for p, m, sk in zip(probs, maxes, sinks)]
    outs = [jnp.dot(p.astype(BF16), vs[j // 2], preferred_element_type=F32) for p, (j, par) in zip(probs, heads)]
    outs = [o / d for o, d in zip(outs, dens)]
    tiles = [jnp.where(lo, outs[2 * j], outs[2 * j + 1]) for j in range(ATTN_WIDTH // LANES)]
    o = jnp.concatenate(tiles, axis=1)
    o_ref[...] = _rms(o, g_ref[...]).astype(BF16)


def _attention(q, kv, sink, g, seq_len):
    t = q.shape[0]
    nblk = t // ATT_BLOCK
    bps = seq_len // ATT_BLOCK
    cur = lambda i, s: (i, 0)
    prv = lambda i, s: (jnp.maximum(i - 1, 0), 0)
    nxt = lambda i, s: (jnp.minimum(i + 1, nblk - 1), 0)
    grid_spec = pltpu.PrefetchScalarGridSpec(
        num_scalar_prefetch=1,
        grid=(nblk,),
        in_specs=[pl.BlockSpec((ATT_BLOCK, ATTN_WIDTH), cur),
                  pl.BlockSpec((ATT_BLOCK, KV_COLS), prv), pl.BlockSpec((ATT_BLOCK, KV_COLS), cur),
                  pl.BlockSpec((ATT_BLOCK, KV_COLS), nxt),
                  pl.BlockSpec((1, ATTN_WIDTH), lambda i, s: (0, 0))],
        out_specs=pl.BlockSpec((ATT_BLOCK, ATTN_WIDTH), cur),
    )
    return pl.pallas_call(
        functools.partial(_attn_kernel, bps=bps),
        grid_spec=grid_spec,
        out_shape=jax.ShapeDtypeStruct((t, ATTN_WIDTH), BF16),
        compiler_params=_params(("parallel",)),
        name="banded_attention",
    )(sink, q, kv, kv, kv, g)


def _uproj_kernel(x_ref, g_ref, w_ref, o_ref, stage_ref):
    lt = x_ref.shape[0]
    h = _rms(x_ref[...].reshape(lt * SUBLANES, D_MODEL), g_ref[...])
    for j in range(D_MODEL // LANES):
        stage_ref[j] = h[:, j * LANES:(j + 1) * LANES]
    for bl in range(SUBLANES):
        hb = jnp.concatenate([stage_ref[j, pl.ds(bl, lt, stride=SUBLANES), :] for j in range(D_MODEL // LANES)],
                             axis=1).astype(BF16)
        ut = lax.dot_general(w_ref[...], hb, (((1,), (1,)), ((), ())), preferred_element_type=F32)
        o_ref[bl] = ut.astype(BF16)


def _uproj(x2d, g, w_t, lt):
    t = x2d.shape[0]
    nc = t // SSM_CHUNK
    xv = x2d.reshape(nc, SSM_CHUNK, D_MODEL)
    return pl.pallas_call(
        _uproj_kernel,
        grid=(nc // lt, SSM_CHUNK // SUBLANES),
        in_specs=[pl.BlockSpec((lt, SUBLANES, D_MODEL), lambda i, b: (i, b, 0)),
                  pl.BlockSpec((1, D_MODEL), lambda i, b: (0, 0)),
                  pl.BlockSpec((SSM_WIDTH, D_MODEL), lambda i, b: (0, 0))],
        out_specs=pl.BlockSpec((SUBLANES, SSM_WIDTH, lt), lambda i, b: (b, 0, i)),
        out_shape=jax.ShapeDtypeStruct((SSM_CHUNK, SSM_WIDTH, nc), BF16),
        scratch_shapes=[pltpu.VMEM((D_MODEL // LANES, lt * SUBLANES, LANES), F32)],
        compiler_params=_params(("parallel", "parallel")),
        name="u_projection",
    )(xv, g, w_t)


def _gelu_tanh(x):
    return 0.5 * x * (1.0 + jnp.tanh(math.sqrt(2.0 / math.pi) * (x + 0.044715 * (x * x * x))))


LAG_STRIP = 8 * LANES


def _toeplitz_from_strip(strip):
    per_tile = LANES // SSM_GROUP
    rolled = [strip if q == 0 else pltpu.roll(strip, LAG_STRIP - q * SSM_GROUP, 1) for q in range(per_tile)]
    blocks = []
    for b in range(SSM_CHUNK):
        m, q = divmod(SSM_CHUNK - 1 - b, per_tile)
        blocks.append(rolled[q][:, m * LANES:m * LANES + TOEP])
    return jnp.concatenate(blocks, axis=0)


def _ssm_kernel(a_ref, wt_ref, ws_ref, wc_ref, pw_ref, y_ref, *, chunks_per_seq, nsteps):
    nc = a_ref.shape[2]
    a = a_ref[...].reshape(TOEP, nc)
    w_toep = _toeplitz_from_strip(wt_ref[0]).astype(BF16)
    y = jnp.dot(w_toep, a, preferred_element_type=F32)
    s = jnp.dot(ws_ref[0], a, preferred_element_type=F32)
    pos = lax.broadcasted_iota(I32, (SSM_STATE, nc), 1) % chunks_per_seq
    carries = []
    for d in range(2):
        hr = s[2 * d * SSM_STATE:(2 * d + 1) * SSM_STATE]
        hi = s[(2 * d + 1) * SSM_STATE:(2 * d + 2) * SSM_STATE]
        for k in range(nsteps):
            sh = 1 << k
            pr = pw_ref[0, 2 * d * SSM_STATE:(2 * d + 1) * SSM_STATE, k:k + 1]
            pi = pw_ref[0, (2 * d + 1) * SSM_STATE:(2 * d + 2) * SSM_STATE, k:k + 1]
            if d == 0:
                ok = pos >= sh
                sr = pltpu.roll(hr, sh, 1)
                si = pltpu.roll(hi, sh, 1)
            else:
                ok = pos < chunks_per_seq - sh
                sr = pltpu.roll(hr, nc - sh, 1)
                si = pltpu.roll(hi, nc - sh, 1)
            hr, hi = (hr + jnp.where(ok, pr * sr - pi * si, 0.0),
                      hi + jnp.where(ok, pr * si + pi * sr, 0.0))
        if d == 0:
            ok = pos >= 1
            cr = pltpu.roll(hr, 1, 1)
            ci = pltpu.roll(hi, 1, 1)
        else:
            ok = pos < chunks_per_seq - 1
            cr = pltpu.roll(hr, nc - 1, 1)
            ci = pltpu.roll(hi, nc - 1, 1)
        carries += [jnp.where(ok, cr, 0.0), jnp.where(ok, ci, 0.0)]
    carry = jnp.concatenate(carries, axis=0).astype(BF16)
    y = y + jnp.dot(wc_ref[0], carry, preferred_element_type=F32)
    y_ref[...] = _gelu_tanh(y).reshape(SSM_CHUNK, SSM_GROUP, nc)


def _ssm(ut, w_toep, w_state, w_carry, pw, chunks_per_seq):
    nc = ut.shape[2]
    nsteps = max(1, (chunks_per_seq - 1).bit_length())
    g3 = lambda g: (g, 0, 0)
    return pl.pallas_call(
        functools.partial(_ssm_kernel, chunks_per_seq=chunks_per_seq, nsteps=nsteps),
        grid=(SSM_N_GROUPS,),
        in_specs=[pl.BlockSpec((SSM_CHUNK, SSM_GROUP, nc), lambda g: (0, g, 0)),
                  pl.BlockSpec((1, SSM_GROUP, LAG_STRIP), g3),
                  pl.BlockSpec((1, 4 * SSM_STATE, TOEP), g3),
                  pl.BlockSpec((1, TOEP, 4 * SSM_STATE), g3),
                  pl.BlockSpec((1, 4 * SSM_STATE, pw.shape[2]), g3)],
        out_specs=pl.BlockSpec((SSM_CHUNK, SSM_GROUP, nc), lambda g: (0, g, 0)),
        out_shape=jax.ShapeDtypeStruct((SSM_CHUNK, SSM_WIDTH, nc), F32),
        compiler_params=_params(("parallel",)),
        name="s5_core",
    )(ut, w_toep, w_state, w_carry, pw)


def _glu_kernel(y_ref, w_ref, b_ref, g_ref, o_ref, stage_ref):
    lt = o_ref.shape[0]
    for bl in range(SUBLANES):
        y = y_ref[bl]
        z = jnp.dot(w_ref[...], y.astype(BF16), preferred_element_type=F32) + b_ref[...]
        s = y * jax.nn.sigmoid(z)
        ms = jnp.mean(s * s, axis=0, keepdims=True)
        sn = (s * lax.rsqrt(ms + EPS) * g_ref[...]).T
        for j in range(SSM_WIDTH // LANES):
            stage_ref[j, pl.ds(bl, lt, stride=SUBLANES), :] = sn[:, j * LANES:(j + 1) * LANES]
    for j in range(SSM_WIDTH // LANES):
        o_ref[:, :, j * LANES:(j + 1) * LANES] = stage_ref[j].reshape(lt, SUBLANES, LANES)


def _glu(yt, w_t, b_col, g_col, lt):
    nc = yt.shape[2]
    out = pl.pallas_call(
        _glu_kernel,
        grid=(nc // lt, SSM_CHUNK // SUBLANES),
        in_specs=[pl.BlockSpec((SUBLANES, SSM_WIDTH, lt), lambda i, b: (b, 0, i)),
                  pl.BlockSpec((SSM_WIDTH, SSM_WIDTH), lambda i, b: (0, 0)),
                  pl.BlockSpec((SSM_WIDTH, 1), lambda i, b: (0, 0)),
                  pl.BlockSpec((SSM_WIDTH, 1), lambda i, b: (0, 0))],
        out_specs=pl.BlockSpec((lt, SUBLANES, SSM_WIDTH), lambda i, b: (i, b, 0)),
        out_shape=jax.ShapeDtypeStruct((nc, SSM_CHUNK, SSM_WIDTH), F32),
        scratch_shapes=[pltpu.VMEM((SSM_WIDTH // LANES, lt * SUBLANES, LANES), F32)],
        compiler_params=_params(("parallel", "parallel")),
        name="glu_norm",
    )(yt, w_t, b_col, g_col)
    return out.reshape(nc * SSM_CHUNK, SSM_WIDTH)


def _outproj_kernel(x_ref, a_ref, s_ref, wa_ref, ws_ref, g_ref, wr_ref, br_ref, tri_ref,
                    x2_ref, h2_ref, rt_ref, gt_ref, cnt_ref, run_ref):
    i = pl.program_id(0)

    @pl.when(i == 0)
    def _():
        run_ref[...] = jnp.zeros_like(run_ref)

    x2 = (x_ref[...] + jnp.dot(a_ref[...], wa_ref[...], preferred_element_type=F32)
          + jnp.dot(s_ref[...].astype(BF16), ws_ref[...], preferred_element_type=F32))
    x2_ref[...] = x2
    h2 = _rms(x2, g_ref[...])
    _tile_rows_store(h2_ref, h2)
    logits = lax.dot_general(wr_ref[...], h2.astype(BF16), (((1,), (1,)), ((), ())),
                             preferred_element_type=F32) + br_ref[...]
    tm = logits.shape[1]
    sub = lax.broadcasted_iota(I32, (N_EXPERTS, tm), 0)
    sub_f = sub.astype(F32)
    work = logits
    sel = jnp.zeros((N_EXPERTS, tm), F32)
    top_v, top_i = [], []
    for _ in range(TOP_K):
        m = jnp.max(work, axis=0, keepdims=True)
        idx = jnp.min(jnp.where(work == m, sub_f, float(N_EXPERTS)), axis=0, keepdims=True).astype(I32)
        hit = sub == idx
        sel = jnp.where(hit, 1.0, sel)
        work = jnp.where(hit, -jnp.inf, work)
        top_v.append(m)
        top_i.append(idx)
    ex = [jnp.exp(v - top_v[0]) for v in top_v]
    den = ex[0] + ex[1] + ex[2] + ex[3]
    before = jnp.dot(sel.astype(BF16), tri_ref[...], preferred_element_type=F32) + run_ref[:, 0:1]
    ranks = [jnp.sum(jnp.where(sub == top_i[k], before, 0.0), axis=0, keepdims=True).astype(I32) for k in range(TOP_K)]
    rt_ref[...] = jnp.concatenate(top_i + ranks, axis=0)
    gt_ref[...] = jnp.concatenate([e / den for e in ex] + [jnp.zeros((SUBLANES - TOP_K, tm), F32)], axis=0)
    run = run_ref[...] + jnp.sum(sel, axis=1, keepdims=True)
    run_ref[...] = run
    cnt_ref[...] = run


def _outproj(x2d, attn_n, ssm_n, w_a, w_s, g, w_r, b_r, tri):
    t = x2d.shape[0]
    tm = ROUTE_TM
    row = lambda i: (i, 0)
    full = lambda i: (0, 0)
    return pl.pallas_call(
        _outproj_kernel,
        grid=(t // tm,),
        in_specs=[pl.BlockSpec((tm, D_MODEL), row), pl.BlockSpec((tm, ATTN_WIDTH), row),
                  pl.BlockSpec((tm, SSM_WIDTH), row),
                  pl.BlockSpec((ATTN_WIDTH, D_MODEL), full), pl.BlockSpec((SSM_WIDTH, D_MODEL), full),
                  pl.BlockSpec((1, D_MODEL), full),
                  pl.BlockSpec((N_EXPERTS, D_MODEL), full), pl.BlockSpec((N_EXPERTS, 1), full),
                  pl.BlockSpec((tm, tm), full)],
        out_specs=[pl.BlockSpec((tm, D_MODEL), row), pl.BlockSpec((tm * SUBLANES, LANES), row),
                   pl.BlockSpec((2 * TOP_K, tm), lambda i: (0, i)), pl.BlockSpec((SUBLANES, tm), lambda i: (0, i)),
                   pl.BlockSpec((N_EXPERTS, LANES), full)],
        out_shape=[jax.ShapeDtypeStruct((t, D_MODEL), F32), jax.ShapeDtypeStruct((t * SUBLANES, LANES), F32),
                   jax.ShapeDtypeStruct((2 * TOP_K, t), I32), jax.ShapeDtypeStruct((SUBLANES, t), F32),
                   jax.ShapeDtypeStruct((N_EXPERTS, LANES), F32)],
        scratch_shapes=[pltpu.VMEM((N_EXPERTS, LANES), F32)],
        compiler_params=_params(("arbitrary",)),
        name="out_projection_router",
    )(x2d, attn_n, ssm_n, w_a, w_s, g, w_r, b_r, tri)


DMA_GROUP = 4


def _rows_per_step(rows, tokens):
    nt = rows.shape[1] // tokens
    return (rows * SUBLANES).reshape(TOP_K, nt, tokens).transpose(1, 0, 2).reshape(nt, 1, TOP_K * tokens)


def _scatter_kernel(pad_start_ref, pad_cnt_ref, rows_ref, h_ref, xb_hbm, zero_ref, stage_ref, sems, zsem, *, tm):
    i = pl.program_id(0)
    nt = pl.num_programs(0)
    n_dma = tm * TOP_K

    def tile_wait(slot):
        whole = xb_hbm.at[pl.ds(0, n_dma * SUBLANES)]
        pltpu.make_async_copy(whole, whole, sems.at[slot]).wait()

    @pl.when(i == 0)
    def _():
        zero_ref[...] = jnp.zeros_like(zero_ref)

        def fill(row, n_rows):
            return pltpu.make_async_copy(zero_ref.at[pl.ds(0, n_rows * SUBLANES)],
                                         xb_hbm.at[pl.ds(pl.multiple_of(row * SUBLANES, SUBLANES), n_rows * SUBLANES)],
                                         zsem)

        for phase in range(2):
            def per_expert(e, c):
                cnt = pad_cnt_ref[e]
                row = pad_start_ref[e]
                run = MOE_ROWS // 2
                while run >= 1:
                    below = cnt & ~(2 * run - 1)

                    @pl.when((cnt & run) != 0)
                    def _(run=run, below=below):
                        cp = fill(row + below, run)
                        cp.start() if phase == 0 else cp.wait()
                    run //= 2
                return c
            lax.fori_loop(0, N_EXPERTS, per_expert, 0)

            def per_tail_block(b, c):
                cp = fill(pad_start_ref[N_EXPERTS] + b * MOE_ROWS, MOE_ROWS)
                cp.start() if phase == 0 else cp.wait()
                return c
            lax.fori_loop(0, pad_cnt_ref[N_EXPERTS] // MOE_ROWS, per_tail_block, 0)

    for slot in range(2):
        @pl.when(i > 0)
        def _(slot=slot):
            tile_wait(slot)

        stage_ref[slot] = h_ref[pl.ds(slot * tm * SUBLANES, tm * SUBLANES), :]

        def issue(grp, c, slot=slot):
            r0 = grp * DMA_GROUP
            dst = [rows_ref[0, 0, k * (2 * tm) + slot * tm + r0 + r] for r in range(DMA_GROUP) for k in range(TOP_K)]
            for r in range(DMA_GROUP):
                src = _tile_row(stage_ref.at[slot], r0 + r)
                for k in range(TOP_K):
                    pltpu.make_async_copy(src, _tile_at(xb_hbm, dst[r * TOP_K + k]),
                                          sems.at[slot]).start(priority=k % 2)
            return c
        lax.fori_loop(0, tm // DMA_GROUP, issue, 0)

    @pl.when(i == nt - 1)
    def _():
        tile_wait(0)
        tile_wait(1)


def _scatter_rows(pad_start, pad_cnt, rows, h2, n_pad):
    tm = ROUTE_TM
    nt = h2.shape[0] // SUBLANES // (2 * tm)
    grid_spec = pltpu.PrefetchScalarGridSpec(
        num_scalar_prefetch=2,
        grid=(nt,),
        in_specs=[pl.BlockSpec((1, 1, 2 * tm * TOP_K), lambda i, a, b: (i, 0, 0), memory_space=pltpu.SMEM),
                  pl.BlockSpec((2 * tm * SUBLANES, LANES), lambda i, a, b: (i, 0))],
        out_specs=pl.BlockSpec(memory_space=pl.ANY),
        scratch_shapes=[pltpu.VMEM((MOE_ROWS * SUBLANES, LANES), F32),
                        pltpu.VMEM((2, tm * SUBLANES, LANES), F32),
                        pltpu.SemaphoreType.DMA((2,)), pltpu.SemaphoreType.DMA(())],
    )
    return pl.pallas_call(
        functools.partial(_scatter_kernel, tm=tm),
        grid_spec=grid_spec,
        out_shape=jax.ShapeDtypeStruct((n_pad * SUBLANES, LANES), F32),
        compiler_params=_params(("arbitrary",)),
        name="scatter_rows",
    )(pad_start, pad_cnt, _rows_per_step(rows, 2 * tm), h2)


def _expert_kernel(be_ref, grp_ref, ia_ref, ib_ref, nv_ref, xa_ref, xb_ref, wgu_ref, bgu_ref, wd_ref, bd_ref,
                   o_ref, x_scr, wgu_scr, wd_scr):
    i = pl.program_id(0)
    live = i < nv_ref[0]

    @pl.when(live & ((i == 0) | (be_ref[i] != be_ref[jnp.maximum(i - 1, 0)])))
    def _():
        wgu_scr[...] = wgu_ref[0].astype(BF16)
        wd_scr[...] = wd_ref[0].astype(BF16)

    @pl.when(live & (grp_ref[i] == 0))
    def _():
        x_scr[...] = _tile_rows_load(xa_ref, MOE_ROWS).astype(BF16)

    @pl.when(live & (grp_ref[i] != 0))
    def _():
        x_scr[...] = _tile_rows_load(xb_ref, MOE_ROWS).astype(BF16)

    @pl.when(live)
    def _():
        gu = jnp.dot(x_scr[...], wgu_scr[...], preferred_element_type=F32) + bgu_ref[0]
        gate = jnp.minimum(gu[:, :D_MODEL], SWIGLU_LIMIT)
        up = jnp.clip(gu[:, D_MODEL:], -SWIGLU_LIMIT, SWIGLU_LIMIT)
        act = (up + 1.0) * (gate * jax.nn.sigmoid(SWIGLU_ALPHA * gate))
        _tile_rows_store(o_ref, jnp.dot(act.astype(BF16), wd_scr[...], preferred_element_type=F32) + bd_ref[0])

    @pl.when(i >= nv_ref[0])
    def _():
        o_ref[...] = jnp.zeros_like(o_ref)


def _experts(block_e, block_grp, blk_a, blk_b, n_valid, xa, xb, wgu, bgu, wd, bd):
    nb = block_e.shape[0]
    e3 = lambda i, be, gr, ia, ib, nv: (be[i], 0, 0)
    grid_spec = pltpu.PrefetchScalarGridSpec(
        num_scalar_prefetch=5,
        grid=(nb,),
        in_specs=[pl.BlockSpec((MOE_ROWS * SUBLANES, LANES), lambda i, be, gr, ia, ib, nv: (ia[i], 0)),
                  pl.BlockSpec((MOE_ROWS * SUBLANES, LANES), lambda i, be, gr, ia, ib, nv: (ib[i], 0)),
                  pl.BlockSpec((1, D_MODEL, 2 * D_MODEL), e3), pl.BlockSpec((1, 1, 2 * D_MODEL), e3),
                  pl.BlockSpec((1, D_MODEL, D_MODEL), e3), pl.BlockSpec((1, 1, D_MODEL), e3)],
        out_specs=pl.BlockSpec((MOE_ROWS * SUBLANES, LANES), lambda i, be, gr, ia, ib, nv: (i, 0)),
        scratch_shapes=[pltpu.VMEM((MOE_ROWS, D_MODEL), BF16), pltpu.VMEM((D_MODEL, 2 * D_MODEL), BF16),
                        pltpu.VMEM((D_MODEL, D_MODEL), BF16)],
    )
    return pl.pallas_call(
        _expert_kernel,
        grid_spec=grid_spec,
        out_shape=jax.ShapeDtypeStruct((nb * MOE_ROWS * SUBLANES, LANES), F32),
        compiler_params=_params(("arbitrary",), vmem=EXPERT_VMEM_LIMIT),
        name="routed_experts",
    )(block_e, block_grp, blk_a, blk_b, n_valid, xa, xb, wgu, bgu, wd, bd)


GATHER_PITCH = 12


def _combine_kernel(rows_cur, rows_nxt, x_ref, gt_ref, g_ref, y_hbm, o_ref, buf, sems, *, tm):
    i = pl.program_id(0)
    nt = pl.num_programs(0)
    units = SUBLANES * TOP_K
    per_unit = tm // units
    ahead = COMBINE_TILES // 2

    def tile_wait(slot):
        whole = y_hbm.at[pl.ds(0, TOP_K * tm * SUBLANES)]
        pltpu.make_async_copy(whole, whole, sems.at[slot]).wait()

    def start_rows(rows_ref, tile, r_lo, r_hi):
        idx = [rows_ref[0, 0, k * (COMBINE_TILES * tm) + tile * tm + r] for r in range(r_lo, r_hi) for k in range(TOP_K)]
        for n, (r, k) in enumerate((r, k) for r in range(r_lo, r_hi) for k in range(TOP_K)):
            dst = buf.at[tile, k].at[pl.ds(r * GATHER_PITCH, SUBLANES)]
            pltpu.make_async_copy(_tile_at(y_hbm, idx[n]), dst, sems.at[tile]).start(priority=k % 2)

    def finish(tile):
        nxt = tile + ahead
        nxt_rows, nxt_tile = (rows_cur, nxt) if nxt < COMBINE_TILES else (rows_nxt, nxt - COMBINE_TILES)
        tile_wait(tile)
        rows = pl.ds(tile * tm, tm)
        g8 = gt_ref[:, tile * tm:(tile + 1) * tm]
        gt = jnp.concatenate([g8, jnp.zeros((tm - SUBLANES, tm), F32)], axis=0).T
        gk = [gt[:, k:k + 1] for k in range(TOP_K)]
        ss = jnp.zeros((tm, 1), F32)
        for j in range(SUBLANES):
            cols = slice(j * LANES, (j + 1) * LANES)
            a = x_ref[rows, cols]
            for k in range(TOP_K):
                u = j * TOP_K + k
                start_rows(nxt_rows, nxt_tile, u * per_unit, (u + 1) * per_unit)
                a = a + buf[tile, k, pl.ds(j, tm, stride=GATHER_PITCH), :] * gk[k]
            ss = ss + jnp.sum(a * a, axis=-1, keepdims=True)
            o_ref[rows, cols] = a
        scale = lax.rsqrt(ss * (1.0 / D_MODEL) + EPS)
        o_ref[rows, :] = o_ref[rows, :] * scale * g_ref[...]

    @pl.when(i == 0)
    def _():
        for tile in range(ahead):
            start_rows(rows_cur, tile, 0, tm)

    for tile in range(COMBINE_TILES):
        finish(tile)

    @pl.when(i == nt - 1)
    def _():
        for tile in range(ahead):
            tile_wait(tile)


def _combine(rows, x2, gates, g, yb):
    t = x2.shape[0]
    tm = COMBINE_TM
    step = COMBINE_TILES * tm
    nt = t // step
    rows3 = _rows_per_step(rows, step)
    row = lambda i: (i, 0)
    return pl.pallas_call(
        functools.partial(_combine_kernel, tm=tm),
        grid=(nt,),
        in_specs=[pl.BlockSpec((1, 1, step * TOP_K), lambda i: (i, 0, 0), memory_space=pltpu.SMEM),
                  pl.BlockSpec((1, 1, step * TOP_K), lambda i: (jnp.minimum(i + 1, nt - 1), 0, 0),
                               memory_space=pltpu.SMEM),
                  pl.BlockSpec((step, D_MODEL), row), pl.BlockSpec((SUBLANES, step), lambda i: (0, i)),
                  pl.BlockSpec((1, D_MODEL), lambda i: (0, 0)),
                  pl.BlockSpec(memory_space=pl.ANY)],
        out_specs=pl.BlockSpec((step, D_MODEL), row),
        out_shape=jax.ShapeDtypeStruct((t, D_MODEL), F32),
        scratch_shapes=[pltpu.VMEM((COMBINE_TILES, TOP_K, tm * GATHER_PITCH, LANES), F32),
                        pltpu.SemaphoreType.DMA((COMBINE_TILES,))],
        compiler_params=_params(("arbitrary",)),
        name="combine_final_norm",
    )(rows3, rows3, x2, gates, g, yb)


def _rope_tables(seq_len):
    inv_freq = ROPE_THETA ** (-np.arange(ROPE_HALF, dtype=np.float64) * 2.0 / ROPE_DIM)
    ang = np.arange(seq_len, dtype=np.float64)[:, None] * inv_freq[None, :]
    cos = np.cos(ang).astype(np.float32)
    sin = np.sin(ang).astype(np.float32)
    pad = HEAD_DIM - ROPE_DIM
    ones = np.ones((seq_len, pad), np.float32)
    zer_h = np.zeros((seq_len, ROPE_HALF), np.float32)
    zer_p = np.zeros((seq_len, pad), np.float32)
    c = np.concatenate([cos, cos, ones], axis=1)
    s1 = np.concatenate([zer_h, sin, zer_p], axis=1)
    s2 = np.concatenate([-sin, zer_h, zer_p], axis=1)
    rep = LANES // HEAD_DIM
    return tuple(jnp.asarray(np.tile(t, (1, rep))) for t in (c, s1, s2))


def _ssm_weights(a_re, a_im, log_dt, b_re, b_im, c_re, c_im, ssm_d, nsteps):
    r = SSM_CHUNK
    dt = jnp.exp(log_dt)[..., None]
    lr, li = a_re * dt, a_im * dt

    taus = jnp.arange(r + 1, dtype=F32)
    mag = jnp.exp(lr[..., None] * taus)
    p_re, p_im = mag * jnp.cos(li[..., None] * taus), mag * jnp.sin(li[..., None] * taus)

    ab_re, ab_im = p_re[..., 1], p_im[..., 1]
    den = a_re * a_re + a_im * a_im
    num_re, num_im = ab_re - 1.0, ab_im
    f_re = (num_re * a_re + num_im * a_im) / den
    f_im = (num_im * a_re - num_re * a_im) / den
    bb_re = f_re[..., None] * b_re - f_im[..., None] * b_im
    bb_im = f_re[..., None] * b_im + f_im[..., None] * b_re

    m_re = p_re[..., None] * bb_re[:, :, :, None, :] - p_im[..., None] * bb_im[:, :, :, None, :]
    m_im = p_re[..., None] * bb_im[:, :, :, None, :] + p_im[..., None] * bb_re[:, :, :, None, :]
    kern = (jnp.einsum('dgcp,dgptk->dgctk', c_re, m_re) - jnp.einsum('dgcp,dgptk->dgctk', c_im, m_im))
    center = kern[0, :, :, 0] + kern[1, :, :, 0] + jnp.eye(SSM_GROUP, dtype=F32) * ssm_d[:, :, None]
    lags = jnp.concatenate([kern[0, :, :, r - 1:0:-1], center[:, :, None], kern[1, :, :, 1:r]], axis=2)
    strip = lags.reshape(SSM_N_GROUPS, SSM_GROUP, (2 * r - 1) * SSM_GROUP)
    w_toep = jnp.pad(strip, ((0, 0), (0, 0), (0, LAG_STRIP - (2 * r - 1) * SSM_GROUP)))

    flat = lambda a: a.reshape(SSM_N_GROUPS, SSM_STATE, TOEP)
    w_state = jnp.concatenate([flat(m_re[0, :, :, r - 1::-1]), flat(m_im[0, :, :, r - 1::-1]),
                               flat(m_re[1, :, :, :r]), flat(m_im[1, :, :, :r])], axis=1)

    def c_times_pow(d, pr, pi):
        pr, pi = pr.transpose(0, 2, 1)[:, :, None, :], pi.transpose(0, 2, 1)[:, :, None, :]
        zr = c_re[d][:, None] * pr - c_im[d][:, None] * pi
        zi = c_re[d][:, None] * pi + c_im[d][:, None] * pr
        return zr, -zi
    cf_re, cf_im = c_times_pow(0, p_re[0, :, :, 1:r + 1], p_im[0, :, :, 1:r + 1])
    cb_re, cb_im = c_times_pow(1, p_re[1, :, :, r:0:-1], p_im[1, :, :, r:0:-1])
    w_carry = jnp.concatenate([cf_re, cf_im, cb_re, cb_im], axis=3).reshape(SSM_N_GROUPS, TOEP, 4 * SSM_STATE)

    qr, qi = p_re[..., r], p_im[..., r]
    cols = []
    for _ in range(nsteps):
        cols.append(jnp.stack([qr[0], qi[0], qr[1], qi[1]], axis=1))
        qr, qi = qr * qr - qi * qi, 2.0 * qr * qi
    pw = jnp.stack(cols, axis=-1).reshape(SSM_N_GROUPS, 4 * SSM_STATE, nsteps)
    return w_toep, w_state.astype(BF16), w_carry.astype(BF16), pw


def _front(x, prm):
    n, seq_len, _ = x.shape
    t = n * seq_len
    x2d = x.reshape(t, D_MODEL)
    q, kv = _qkv(x2d, prm['norm1_g'], prm['w_qkv'], *_rope_tables(seq_len), seq_len)
    attn_n = _attention(q, kv, prm['sink'], prm['attn_out_g'], seq_len)

    chunks_per_seq = seq_len // SSM_CHUNK
    nsteps = max(1, (chunks_per_seq - 1).bit_length())
    nc = t // SSM_CHUNK
    lt = min(MXU_WIDTH, nc)
    ut = _uproj(x2d, prm['norm1_g'], prm['w_u_t'], lt)
    yt = _ssm(ut, prm['w_toep'], prm['w_state'], prm['w_carry'], prm['pw'][:, :, :nsteps], chunks_per_seq)
    ssm_n = _glu(yt, prm['glu_w_t'], prm['glu_b'], prm['ssm_out_g'], lt)
    return _outproj(x2d, attn_n, ssm_n, prm['w_out_a'], prm['w_out_s'], prm['norm2_g'],
                    prm['router_w'], prm['router_b'], prm['tri'])


def _cumsum_small(x):
    n = x.shape[0]
    keep = jnp.arange(n)[None, :] <= jnp.arange(n)[:, None]
    return jnp.sum(jnp.where(keep, x[None, :], 0), axis=1)


def _by_expert(idx, table):
    hit = idx[None] == jnp.arange(N_EXPERTS, dtype=I32)[:, None, None]
    return jnp.sum(jnp.where(hit, table[:, None, None], 0), axis=0)


def kernel(x_prompt, x_sample, norm1_g, w_in, attn_sink, ssm_a_re, ssm_a_im, ssm_log_dt, ssm_b_re, ssm_b_im, ssm_c_re, ssm_c_im, ssm_d, glu_w, glu_b, attn_out_g, ssm_out_g, w_out, norm2_g, router_w, router_b, w_gate_up, b_gate_up, w_down, b_down, final_g):
    assert norm1_g.shape[0] == 1, "single-layer problem"
    l = 0
    xs = [x_prompt, x_sample]
    max_chunks = max(x.shape[1] for x in xs) // SSM_CHUNK
    max_steps = max(1, (max_chunks - 1).bit_length())
    wq, wk, wv, wu = jnp.split(w_in[l], [ATTN_WIDTH, ATTN_WIDTH + KV_WIDTH, ATTN_WIDTH + 2 * KV_WIDTH], axis=1)
    dup = lambda w: jnp.concatenate([w[:, :HEAD_DIM], w[:, :HEAD_DIM], w[:, HEAD_DIM:], w[:, HEAD_DIM:]], axis=1)
    w_toep, w_state, w_carry, pw = _ssm_weights(ssm_a_re[l], ssm_a_im[l], ssm_log_dt[l], ssm_b_re[l], ssm_b_im[l],
                                                ssm_c_re[l], ssm_c_im[l], ssm_d[l], max_steps)
    tri_i = lax.broadcasted_iota(I32, (ROUTE_TM, ROUTE_TM), 0)
    tri_j = lax.broadcasted_iota(I32, (ROUTE_TM, ROUTE_TM), 1)
    prm = dict(
        norm1_g=norm1_g[l].reshape(1, D_MODEL),
        w_qkv=jnp.concatenate([wq, dup(wk), dup(wv)], axis=1).astype(BF16),
        w_u_t=wu.T.astype(BF16),
        sink=attn_sink[l].astype(F32),
        attn_out_g=attn_out_g[l].reshape(1, ATTN_WIDTH),
        w_toep=w_toep, w_state=w_state, w_carry=w_carry, pw=pw,
        glu_w_t=glu_w[l].T.astype(BF16),
        glu_b=glu_b[l].reshape(SSM_WIDTH, 1),
        ssm_out_g=ssm_out_g[l].reshape(SSM_WIDTH, 1),
        w_out_a=w_out[l][:ATTN_WIDTH].astype(BF16),
        w_out_s=w_out[l][ATTN_WIDTH:].astype(BF16),
        norm2_g=norm2_g[l].reshape(1, D_MODEL),
        router_w=router_w[l].T.astype(BF16),
        router_b=router_b[l].reshape(N_EXPERTS, 1),
        tri=(tri_i < tri_j).astype(BF16),
    )
    fronts = [_front(x, prm) for x in xs]

    cnts = [f[4][:, 0].astype(I32) for f in fronts]
    padded = [(c + MOE_ROWS - 1) // MOE_ROWS * MOE_ROWS for c in cnts]
    pends = [_cumsum_small(p) for p in padded]
    pstarts = [pe - p for pe, p in zip(pends, padded)]
    nbs = [f[0].shape[0] * TOP_K // MOE_ROWS + N_EXPERTS for f in fronts]
    seg_blocks = jnp.stack([p // MOE_ROWS for p in padded], axis=1).reshape(-1)
    seg_end = _cumsum_small(seg_blocks)
    seg_start = seg_end - seg_blocks
    nb = sum(nbs)
    bi = jnp.arange(nb, dtype=I32)
    seg = jnp.minimum(jnp.sum(seg_end[None, :] <= bi[:, None], axis=1), 2 * N_EXPERTS - 1).astype(I32)
    block_e = seg // 2
    block_grp = seg % 2
    n_valid = seg_end[-1].astype(I32).reshape(1)
    seg_ids = jnp.arange(2 * N_EXPERTS, dtype=I32)
    in_seg = seg[:, None] == seg_ids[None, :]
    pick = lambda table: jnp.sum(jnp.where(in_seg, table[None, :], 0), axis=1)
    within = bi - pick(seg_start) + 1
    blk = []
    for g in range(2):
        mine = jnp.where(seg_ids % 2 == g, seg_blocks, 0)
        before = _cumsum_small(mine) - mine
        seen = pick(before) + jnp.where(block_grp == g, within, 0)
        blk.append(jnp.maximum(jnp.minimum(seen, jnp.sum(mine)) - 1, 0).astype(I32))

    xbufs, yrows = [], []
    for g, f in enumerate(fronts):
        e_idx = f[2][:TOP_K]
        rank = f[2][TOP_K:]
        n_pad = nbs[g] * MOE_ROWS
        pad_start = jnp.concatenate([pstarts[g] + cnts[g], pends[g][-1:]]).astype(I32)
        pad_cnt = jnp.concatenate([padded[g] - cnts[g], n_pad - pends[g][-1:]]).astype(I32)
        xrows = (_by_expert(e_idx, pstarts[g]) + rank).astype(I32)
        xbufs.append(_scatter_rows(pad_start, pad_cnt, xrows, f[1], n_pad))
        yrows.append((_by_expert(e_idx, seg_start[g::2] * MOE_ROWS) + rank).astype(I32))
    yb = _experts(block_e, block_grp, blk[0], blk[1], n_valid, xbufs[0], xbufs[1],
                  w_gate_up.reshape(N_EXPERTS, D_MODEL, 2 * D_MODEL), b_gate_up[l][:, None, :],
                  w_down.reshape(N_EXPERTS, D_MODEL, D_MODEL), b_down[l][:, None, :])
    gfin = final_g.reshape(1, D_MODEL)
    outs = [_combine(r, f[0], f[3], gfin, yb).reshape(x.shape) for x, f, r in zip(xs, fronts, yrows)]
    return tuple(outs)
```

```python
import functools
import math

import jax
import jax.numpy as jnp
import numpy as np
from jax import lax
from jax.experimental import pallas as pl
from jax.experimental.pallas import tpu as pltpu

F32 = jnp.float32
BF16 = jnp.bfloat16
I32 = jnp.int32

D_MODEL = 1024
HEAD_DIM = 64
N_Q_HEADS = 8
N_KV_HEADS = 2
ATTN_WIDTH = N_Q_HEADS * HEAD_DIM
KV_WIDTH = N_KV_HEADS * HEAD_DIM
WINDOW = 128
ATT_BLOCK = 128
ROPE_THETA = 500000.0
ROPE_DIM = HEAD_DIM // 4
ROPE_HALF = ROPE_DIM // 2
SSM_WIDTH = 512
SSM_GROUP = 16
SSM_N_GROUPS = SSM_WIDTH // SSM_GROUP
SSM_STATE = 64
N_EXPERTS = 32
TOP_K = 4
SWIGLU_LIMIT = 7.0
SWIGLU_ALPHA = 1.702
EPS = 1e-5

LANES = 128
SUBLANES = 8
SSM_CHUNK = 32
TOEP = SSM_CHUNK * SSM_GROUP
MOE_ROWS = 512
MXU_WIDTH = 256
ROUTE_TM = 512
COMBINE_TM = 128
COMBINE_TILES = 4
VMEM_LIMIT = 52 * 1024 * 1024
EXPERT_VMEM_LIMIT = 60 * 1024 * 1024
NEG_BIG = -1e30


def _params(sem, vmem=VMEM_LIMIT):
    return pltpu.CompilerParams(dimension_semantics=sem, vmem_limit_bytes=vmem)


def _rms(x, g):
    ms = jnp.mean(x * x, axis=-1, keepdims=True)
    return x * lax.rsqrt(ms + EPS) * g


def _tile_rows_load(ref, rows):
    return jnp.concatenate([ref[pl.ds(j, rows, stride=SUBLANES), :] for j in range(SUBLANES)], axis=1)


def _tile_rows_store(ref, val):
    rows = val.shape[0]
    for j in range(SUBLANES):
        ref[pl.ds(j, rows, stride=SUBLANES), :] = val[:, j * LANES:(j + 1) * LANES]


def _tile_at(ref, start):
    return ref.at[pl.ds(pl.multiple_of(start, SUBLANES), SUBLANES)]


def _tile_row(ref, row):
    return _tile_at(ref, row * SUBLANES)


KV_COLS = 4 * LANES


def _qkv_kernel(x_ref, g_ref, w_ref, c_ref, s1_ref, s2_ref, q_ref, kv_ref):
    h = _rms(x_ref[...], g_ref[...]).astype(BF16)
    p = jnp.dot(h, w_ref[...], preferred_element_type=F32)
    c = c_ref[...]
    s1 = s1_ref[...]
    s2 = s2_ref[...]

    def rot(t):
        return t * c + pltpu.roll(t, ROPE_HALF, 1) * s1 + pltpu.roll(t, LANES - ROPE_HALF, 1) * s2

    for j in range(ATTN_WIDTH // LANES):
        q_ref[:, j * LANES:(j + 1) * LANES] = (rot(p[:, j * LANES:(j + 1) * LANES]) * (HEAD_DIM ** -0.5)).astype(BF16)
    for j in range(N_KV_HEADS):
        col = ATTN_WIDTH + j * LANES
        kv_ref[:, j * LANES:(j + 1) * LANES] = rot(p[:, col:col + LANES]).astype(BF16)
    kv_ref[:, N_KV_HEADS * LANES:] = p[:, ATTN_WIDTH + N_KV_HEADS * LANES:].astype(BF16)


def _qkv(x2d, g, w, c, s1, s2, seq_len):
    t = x2d.shape[0]
    tm = ROUTE_TM
    nlb = seq_len // tm
    row = lambda i: (i, 0)
    tab = lambda i: (i % nlb, 0)
    full = lambda i: (0, 0)
    return pl.pallas_call(
        _qkv_kernel,
        grid=(t // tm,),
        in_specs=[pl.BlockSpec((tm, D_MODEL), row), pl.BlockSpec((1, D_MODEL), full),
                  pl.BlockSpec((D_MODEL, ATTN_WIDTH + KV_COLS), full),
                  pl.BlockSpec((tm, LANES), tab), pl.BlockSpec((tm, LANES), tab), pl.BlockSpec((tm, LANES), tab)],
        out_specs=[pl.BlockSpec((tm, ATTN_WIDTH), row), pl.BlockSpec((tm, KV_COLS), row)],
        out_shape=[jax.ShapeDtypeStruct((t, ATTN_WIDTH), BF16), jax.ShapeDtypeStruct((t, KV_COLS), BF16)],
        compiler_params=_params(("parallel",)),
        name="qkv_rotary",
    )(x2d, g, w, c, s1, s2)


def _attention_block(sink_ref, q_ref, kvp, kvc, kvn, g_ref, o_ref, i, bps, between=lambda: None):
    first = (i % bps) == 0
    last = (i % bps) == bps - 1
    qi = lax.broadcasted_iota(I32, (ATT_BLOCK, 3 * ATT_BLOCK), 0)
    kj = lax.broadcasted_iota(I32, (ATT_BLOCK, 3 * ATT_BLOCK), 1)
    rel = kj - ATT_BLOCK - qi
    valid = (jnp.abs(rel) <= WINDOW)
    valid = valid & ((kj >= ATT_BLOCK) | jnp.logical_not(first))
    valid = valid & ((kj < 2 * ATT_BLOCK) | jnp.logical_not(last))
    kv = jnp.concatenate([kvp[...], kvc[...], kvn[...]], axis=0)
    ks = [kv[:, h * LANES:(h + 1) * LANES] for h in range(N_KV_HEADS)]
    vs = [kv[:, (N_KV_HEADS + h) * LANES:(N_KV_HEADS + h + 1) * LANES] for h in range(N_KV_HEADS)]
    lo = lax.broadcasted_iota(I32, (ATT_BLOCK, LANES), 1) < HEAD_DIM
    zero = jnp.zeros((ATT_BLOCK, LANES), BF16)
    heads = [(j, par) for j in range(ATTN_WIDTH // LANES) for par in range(2)]
    nt_dims = (((1,), (1,)), ((), ()))

    def staged(fn, *lists):
        out = []
        for args in zip(*lists):
            between()
            out.append(fn(*args))
        return out

    def score(head):
        j, par = head
        qt = q_ref[:, j * LANES:(j + 1) * LANES]
        qm = jnp.where(lo if par == 0 else jnp.logical_not(lo), qt, zero)
        return lax.dot_general(qm, ks[j // 2], nt_dims, preferred_element_type=F32)

    scores = staged(score, heads)
    scores = staged(lambda s: jnp.where(valid, s, NEG_BIG), scores)
    sinks = [sink_ref[2 * j + par] for j, par in heads]
    maxes = staged(lambda s, sk: jnp.maximum(jnp.max(s, axis=-1, keepdims=True), sk), scores, sinks)
    probs = staged(lambda s, m: jnp.exp(s - m), scores, maxes)
    dens = staged(lambda p, m, sk: jnp.sum(p, axis=-1, keepdims=True) + jnp.exp(sk - m), probs, maxes, sinks)
    outs = staged(lambda p, head: jnp.dot(p.astype(BF16), vs[head[0] // 2], preferred_element_type=F32), probs, heads)
    outs = staged(lambda o, d: o / d, outs, dens)
    tiles = [jnp.where(lo, outs[2 * j], outs[2 * j + 1]) for j in range(ATTN_WIDTH // LANES)]
    o = jnp.concatenate(tiles, axis=1)
    o_ref[...] = _rms(o, g_ref[...]).astype(BF16)


ATTN_STAGES = 7


def _attn_kernel(sink_ref, q_ref, kvp, kvc, kvn, g_ref, o_ref, *, bps):
    _attention_block(sink_ref, q_ref, kvp, kvc, kvn, g_ref, o_ref, pl.program_id(0), bps)


def _attn_scatter_kernel(sink_ref, pad_start_ref, pad_cnt_ref, q_ref, kvp, kvc, kvn, g_ref, rows_ref, h_ref,
                         o_ref, xb_hbm, zero_ref, stage_ref, sems, zsem, *, bps, tm):
    i = pl.program_id(0)
    nt = pl.num_programs(0)
    slot = i % 2
    n_dma = tm * TOP_K

    def tile_wait(s):
        whole = xb_hbm.at[pl.ds(0, n_dma * SUBLANES)]
        pltpu.make_async_copy(whole, whole, sems.at[s]).wait()

    @pl.when(i == 0)
    def _():
        _zero_fill(pad_start_ref, pad_cnt_ref, xb_hbm, zero_ref, zsem)

    @pl.when(i >= 2)
    def _():
        tile_wait(slot)

    stage_ref[slot] = h_ref[...]
    points = ATTN_STAGES * N_Q_HEADS
    per_point = -(-tm // points)
    started = [0]

    def between():
        lo, hi = started[0], min(tm, started[0] + per_point)
        started[0] = hi
        if hi <= lo:
            return
        dst = [rows_ref[0, 0, k * tm + r] for r in range(lo, hi) for k in range(TOP_K)]
        for n, (r, k) in enumerate((r, k) for r in range(lo, hi) for k in range(TOP_K)):
            src = stage_ref.at[slot].at[pl.ds(r * SUBLANES, SUBLANES)]
            pltpu.make_async_copy(src, _tile_at(xb_hbm, dst[n]), sems.at[slot]).start(priority=k % 2)

    _attention_block(sink_ref, q_ref, kvp, kvc, kvn, g_ref, o_ref, i, bps, between)
    while started[0] < tm:
        between()

    @pl.when(i == nt - 1)
    def _():
        @pl.when(nt > 1)
        def _():
            tile_wait(1 - slot)
        tile_wait(slot)


def _attention(q, kv, sink, g, seq_len):
    t = q.shape[0]
    nblk = t // ATT_BLOCK
    bps = seq_len // ATT_BLOCK
    cur = lambda i, s: (i, 0)
    prv = lambda i, s: (jnp.maximum(i - 1, 0), 0)
    nxt = lambda i, s: (jnp.minimum(i + 1, nblk - 1), 0)
    grid_spec = pltpu.PrefetchScalarGridSpec(
        num_scalar_prefetch=1,
        grid=(nblk,),
        in_specs=[pl.BlockSpec((ATT_BLOCK, ATTN_WIDTH), cur),
                  pl.BlockSpec((ATT_BLOCK, KV_COLS), prv), pl.BlockSpec((ATT_BLOCK, KV_COLS), cur),
                  pl.BlockSpec((ATT_BLOCK, KV_COLS), nxt),
                  pl.BlockSpec((1, ATTN_WIDTH), lambda i, s: (0, 0))],
        out_specs=pl.BlockSpec((ATT_BLOCK, ATTN_WIDTH), cur),
    )
    return pl.pallas_call(
        functools.partial(_attn_kernel, bps=bps),
        grid_spec=grid_spec,
        out_shape=jax.ShapeDtypeStruct((t, ATTN_WIDTH), BF16),
        compiler_params=_params(("parallel",)),
        name="banded_attention",
    )(sink, q, kv, kv, kv, g)


def _attention_scatter(q, kv, sink, g, seq_len, pad_start, pad_cnt, rows, h2, n_pad):
    t = q.shape[0]
    nblk = t // ATT_BLOCK
    bps = seq_len // ATT_BLOCK
    t_other = h2.shape[0] // SUBLANES
    assert t_other % nblk == 0 and nblk >= 2
    tm = t_other // nblk
    cur = lambda i, s, a, b: (i, 0)
    prv = lambda i, s, a, b: (jnp.maximum(i - 1, 0), 0)
    nxt = lambda i, s, a, b: (jnp.minimum(i + 1, nblk - 1), 0)
    grid_spec = pltpu.PrefetchScalarGridSpec(
        num_scalar_prefetch=3,
        grid=(nblk,),
        in_specs=[pl.BlockSpec((ATT_BLOCK, ATTN_WIDTH), cur),
                  pl.BlockSpec((ATT_BLOCK, KV_COLS), prv), pl.BlockSpec((ATT_BLOCK, KV_COLS), cur),
                  pl.BlockSpec((ATT_BLOCK, KV_COLS), nxt),
                  pl.BlockSpec((1, ATTN_WIDTH), lambda i, s, a, b: (0, 0)),
                  pl.BlockSpec((1, 1, TOP_K * tm), lambda i, s, a, b: (i, 0, 0), memory_space=pltpu.SMEM),
                  pl.BlockSpec((tm * SUBLANES, LANES), lambda i, s, a, b: (i, 0))],
        out_specs=[pl.BlockSpec((ATT_BLOCK, ATTN_WIDTH), cur), pl.BlockSpec(memory_space=pl.ANY)],
        scratch_shapes=[pltpu.VMEM((MOE_ROWS * SUBLANES, LANES), F32),
                        pltpu.VMEM((2, tm * SUBLANES, LANES), F32),
                        pltpu.SemaphoreType.DMA((2,)), pltpu.SemaphoreType.DMA(())],
    )
    return pl.pallas_call(
        functools.partial(_attn_scatter_kernel, bps=bps, tm=tm),
        grid_spec=grid_spec,
        out_shape=[jax.ShapeDtypeStruct((t, ATTN_WIDTH), BF16),
                   jax.ShapeDtypeStruct((n_pad * SUBLANES, LANES), F32)],
        compiler_params=_params(("arbitrary",)),
        name="attention_scatter",
    )(sink, pad_start, pad_cnt, q, kv, kv, kv, g, _rows_per_step(rows, tm), h2)


def _uproj_kernel(x_ref, g_ref, w_ref, o_ref, stage_ref):
    lt = x_ref.shape[0]
    h = _rms(x_ref[...].reshape(lt * SUBLANES, D_MODEL), g_ref[...])
    for j in range(D_MODEL // LANES):
        stage_ref[j] = h[:, j * LANES:(j + 1) * LANES]
    for bl in range(SUBLANES):
        hb = jnp.concatenate([stage_ref[j, pl.ds(bl, lt, stride=SUBLANES), :] for j in range(D_MODEL // LANES)],
                             axis=1).astype(BF16)
        ut = lax.dot_general(w_ref[...], hb, (((1,), (1,)), ((), ())), preferred_element_type=F32)
        o_ref[bl] = ut.astype(BF16)


def _uproj(x2d, g, w_t, lt):
    t = x2d.shape[0]
    nc = t // SSM_CHUNK
    xv = x2d.reshape(nc, SSM_CHUNK, D_MODEL)
    return pl.pallas_call(
        _uproj_kernel,
        grid=(nc // lt, SSM_CHUNK // SUBLANES),
        in_specs=[pl.BlockSpec((lt, SUBLANES, D_MODEL), lambda i, b: (i, b, 0)),
                  pl.BlockSpec((1, D_MODEL), lambda i, b: (0, 0)),
                  pl.BlockSpec((SSM_WIDTH, D_MODEL), lambda i, b: (0, 0))],
        out_specs=pl.BlockSpec((SUBLANES, SSM_WIDTH, lt), lambda i, b: (b, 0, i)),
        out_shape=jax.ShapeDtypeStruct((SSM_CHUNK, SSM_WIDTH, nc), BF16),
        scratch_shapes=[pltpu.VMEM((D_MODEL // LANES, lt * SUBLANES, LANES), F32)],
        compiler_params=_params(("parallel", "parallel")),
        name="u_projection",
    )(xv, g, w_t)


def _gelu_tanh(x):
    return 0.5 * x * (1.0 + jnp.tanh(math.sqrt(2.0 / math.pi) * (x + 0.044715 * (x * x * x))))


LAG_STRIP = 8 * LANES


def _toeplitz_from_strip(strip):
    per_tile = LANES // SSM_GROUP
    rolled = [strip if q == 0 else pltpu.roll(strip, LAG_STRIP - q * SSM_GROUP, 1) for q in range(per_tile)]
    blocks = []
    for b in range(SSM_CHUNK):
        m, q = divmod(SSM_CHUNK - 1 - b, per_tile)
        blocks.append(rolled[q][:, m * LANES:m * LANES + TOEP])
    return jnp.concatenate(blocks, axis=0)


def _ssm_kernel(a_ref, wt_ref, ws_ref, wc_ref, pw_ref, y_ref, *, chunks_per_seq, nsteps):
    nc = a_ref.shape[2]
    a = a_ref[...].reshape(TOEP, nc)
    w_toep = _toeplitz_from_strip(wt_ref[0]).astype(BF16)
    y = jnp.dot(w_toep, a, preferred_element_type=F32)
    s = jnp.dot(ws_ref[0], a, preferred_element_type=F32)
    pos = lax.broadcasted_iota(I32, (SSM_STATE, nc), 1) % chunks_per_seq
    carries = []
    for d in range(2):
        hr = s[2 * d * SSM_STATE:(2 * d + 1) * SSM_STATE]
        hi = s[(2 * d + 1) * SSM_STATE:(2 * d + 2) * SSM_STATE]
        for k in range(nsteps):
            sh = 1 << k
            pr = pw_ref[0, 2 * d * SSM_STATE:(2 * d + 1) * SSM_STATE, k:k + 1]
            pi = pw_ref[0, (2 * d + 1) * SSM_STATE:(2 * d + 2) * SSM_STATE, k:k + 1]
            if d == 0:
                ok = pos >= sh
                sr = pltpu.roll(hr, sh, 1)
                si = pltpu.roll(hi, sh, 1)
            else:
                ok = pos < chunks_per_seq - sh
                sr = pltpu.roll(hr, nc - sh, 1)
                si = pltpu.roll(hi, nc - sh, 1)
            hr, hi = (hr + jnp.where(ok, pr * sr - pi * si, 0.0),
                      hi + jnp.where(ok, pr * si + pi * sr, 0.0))
        if d == 0:
            ok = pos >= 1
            cr = pltpu.roll(hr, 1, 1)
            ci = pltpu.roll(hi, 1, 1)
        else:
            ok = pos < chunks_per_seq - 1
            cr = pltpu.roll(hr, nc - 1, 1)
            ci = pltpu.roll(hi, nc - 1, 1)
        carries += [jnp.where(ok, cr, 0.0), jnp.where(ok, ci, 0.0)]
    carry = jnp.concatenate(carries, axis=0).astype(BF16)
    y = y + jnp.dot(wc_ref[0], carry, preferred_element_type=F32)
    y_ref[...] = _gelu_tanh(y).reshape(SSM_CHUNK, SSM_GROUP, nc)


def _ssm(ut, w_toep, w_state, w_carry, pw, chunks_per_seq):
    nc = ut.shape[2]
    nsteps = max(1, (chunks_per_seq - 1).bit_length())
    g3 = lambda g: (g, 0, 0)
    return pl.pallas_call(
        functools.partial(_ssm_kernel, chunks_per_seq=chunks_per_seq, nsteps=nsteps),
        grid=(SSM_N_GROUPS,),
        in_specs=[pl.BlockSpec((SSM_CHUNK, SSM_GROUP, nc), lambda g: (0, g, 0)),
                  pl.BlockSpec((1, SSM_GROUP, LAG_STRIP), g3),
                  pl.BlockSpec((1, 4 * SSM_STATE, TOEP), g3),
                  pl.BlockSpec((1, TOEP, 4 * SSM_STATE), g3),
                  pl.BlockSpec((1, 4 * SSM_STATE, pw.shape[2]), g3)],
        out_specs=pl.BlockSpec((SSM_CHUNK, SSM_GROUP, nc), lambda g: (0, g, 0)),
        out_shape=jax.ShapeDtypeStruct((SSM_CHUNK, SSM_WIDTH, nc), F32),
        compiler_params=_params(("parallel",)),
        name="s5_core",
    )(ut, w_toep, w_state, w_carry, pw)


def _glu_kernel(y_ref, w_ref, b_ref, g_ref, o_ref, stage_ref):
    lt = o_ref.shape[0]
    for bl in range(SUBLANES):
        y = y_ref[bl]
        z = jnp.dot(w_ref[...], y.astype(BF16), preferred_element_type=F32) + b_ref[...]
        s = y * jax.nn.sigmoid(z)
        ms = jnp.mean(s * s, axis=0, keepdims=True)
        sn = (s * lax.rsqrt(ms + EPS) * g_ref[...]).T
        for j in range(SSM_WIDTH // LANES):
            stage_ref[j, pl.ds(bl, lt, stride=SUBLANES), :] = sn[:, j * LANES:(j + 1) * LANES]
    for j in range(SSM_WIDTH // LANES):
        o_ref[:, :, j * LANES:(j + 1) * LANES] = stage_ref[j].reshape(lt, SUBLANES, LANES)


def _glu(yt, w_t, b_col, g_col, lt):
    nc = yt.shape[2]
    out = pl.pallas_call(
        _glu_kernel,
        grid=(nc // lt, SSM_CHUNK // SUBLANES),
        in_specs=[pl.BlockSpec((SUBLANES, SSM_WIDTH, lt), lambda i, b: (b, 0, i)),
                  pl.BlockSpec((SSM_WIDTH, SSM_WIDTH), lambda i, b: (0, 0)),
                  pl.BlockSpec((SSM_WIDTH, 1), lambda i, b: (0, 0)),
                  pl.BlockSpec((SSM_WIDTH, 1), lambda i, b: (0, 0))],
        out_specs=pl.BlockSpec((lt, SUBLANES, SSM_WIDTH), lambda i, b: (i, b, 0)),
        out_shape=jax.ShapeDtypeStruct((nc, SSM_CHUNK, SSM_WIDTH), F32),
        scratch_shapes=[pltpu.VMEM((SSM_WIDTH // LANES, lt * SUBLANES, LANES), F32)],
        compiler_params=_params(("parallel", "parallel")),
        name="glu_norm",
    )(yt, w_t, b_col, g_col)
    return out.reshape(nc * SSM_CHUNK, SSM_WIDTH)


def _outproj_kernel(x_ref, a_ref, s_ref, wa_ref, ws_ref, g_ref, wr_ref, br_ref, tri_ref,
                    x2_ref, h2_ref, rt_ref, gt_ref, cnt_ref, run_ref):
    i = pl.program_id(0)

    @pl.when(i == 0)
    def _():
        run_ref[...] = jnp.zeros_like(run_ref)

    x2 = (x_ref[...] + jnp.dot(a_ref[...], wa_ref[...], preferred_element_type=F32)
          + jnp.dot(s_ref[...].astype(BF16), ws_ref[...], preferred_element_type=F32))
    x2_ref[...] = x2
    h2 = _rms(x2, g_ref[...])
    _tile_rows_store(h2_ref, h2)
    logits = lax.dot_general(wr_ref[...], h2.astype(BF16), (((1,), (1,)), ((), ())),
                             preferred_element_type=F32) + br_ref[...]
    tm = logits.shape[1]
    sub = lax.broadcasted_iota(I32, (N_EXPERTS, tm), 0)
    sub_f = sub.astype(F32)
    work = logits
    sel = jnp.zeros((N_EXPERTS, tm), F32)
    top_v, top_i = [], []
    for _ in range(TOP_K):
        m = jnp.max(work, axis=0, keepdims=True)
        idx = jnp.min(jnp.where(work == m, sub_f, float(N_EXPERTS)), axis=0, keepdims=True).astype(I32)
        hit = sub == idx
        sel = jnp.where(hit, 1.0, sel)
        work = jnp.where(hit, -jnp.inf, work)
        top_v.append(m)
        top_i.append(idx)
    ex = [jnp.exp(v - top_v[0]) for v in top_v]
    den = ex[0] + ex[1] + ex[2] + ex[3]
    before = jnp.dot(sel.astype(BF16), tri_ref[...], preferred_element_type=F32) + run_ref[:, 0:1]
    ranks = [jnp.sum(jnp.where(sub == top_i[k], before, 0.0), axis=0, keepdims=True).astype(I32) for k in range(TOP_K)]
    rt_ref[...] = jnp.concatenate(top_i + ranks, axis=0)
    gt_ref[...] = jnp.concatenate([e / den for e in ex] + [jnp.zeros((SUBLANES - TOP_K, tm), F32)], axis=0)
    run = run_ref[...] + jnp.sum(sel, axis=1, keepdims=True)
    run_ref[...] = run
    cnt_ref[...] = run


def _outproj(x2d, attn_n, ssm_n, w_a, w_s, g, w_r, b_r, tri):
    t = x2d.shape[0]
    tm = ROUTE_TM
    row = lambda i: (i, 0)
    full = lambda i: (0, 0)
    return pl.pallas_call(
        _outproj_kernel,
        grid=(t // tm,),
        in_specs=[pl.BlockSpec((tm, D_MODEL), row), pl.BlockSpec((tm, ATTN_WIDTH), row),
                  pl.BlockSpec((tm, SSM_WIDTH), row),
                  pl.BlockSpec((ATTN_WIDTH, D_MODEL), full), pl.BlockSpec((SSM_WIDTH, D_MODEL), full),
                  pl.BlockSpec((1, D_MODEL), full),
                  pl.BlockSpec((N_EXPERTS, D_MODEL), full), pl.BlockSpec((N_EXPERTS, 1), full),
                  pl.BlockSpec((tm, tm), full)],
        out_specs=[pl.BlockSpec((tm, D_MODEL), row), pl.BlockSpec((tm * SUBLANES, LANES), row),
                   pl.BlockSpec((2 * TOP_K, tm), lambda i: (0, i)), pl.BlockSpec((SUBLANES, tm), lambda i: (0, i)),
                   pl.BlockSpec((N_EXPERTS, LANES), full)],
        out_shape=[jax.ShapeDtypeStruct((t, D_MODEL), F32), jax.ShapeDtypeStruct((t * SUBLANES, LANES), F32),
                   jax.ShapeDtypeStruct((2 * TOP_K, t), I32), jax.ShapeDtypeStruct((SUBLANES, t), F32),
                   jax.ShapeDtypeStruct((N_EXPERTS, LANES), F32)],
        scratch_shapes=[pltpu.VMEM((N_EXPERTS, LANES), F32)],
        compiler_params=_params(("arbitrary",)),
        name="out_projection_router",
    )(x2d, attn_n, ssm_n, w_a, w_s, g, w_r, b_r, tri)


DMA_GROUP = 4


def _rows_per_step(rows, tokens):
    nt = rows.shape[1] // tokens
    return (rows * SUBLANES).reshape(TOP_K, nt, tokens).transpose(1, 0, 2).reshape(nt, 1, TOP_K * tokens)


def _zero_fill(pad_start_ref, pad_cnt_ref, xb_hbm, zero_ref, zsem):
    zero_ref[...] = jnp.zeros_like(zero_ref)

    def fill(row, n_rows):
        return pltpu.make_async_copy(zero_ref.at[pl.ds(0, n_rows * SUBLANES)],
                                     xb_hbm.at[pl.ds(pl.multiple_of(row * SUBLANES, SUBLANES), n_rows * SUBLANES)],
                                     zsem)

    for phase in range(2):
        def per_expert(e, c):
            cnt = pad_cnt_ref[e]
            row = pad_start_ref[e]
            run = MOE_ROWS // 2
            while run >= 1:
                below = cnt & ~(2 * run - 1)

                @pl.when((cnt & run) != 0)
                def _(run=run, below=below):
                    cp = fill(row + below, run)
                    cp.start() if phase == 0 else cp.wait()
                run //= 2
            return c
        lax.fori_loop(0, N_EXPERTS, per_expert, 0)

        def per_tail_block(b, c):
            cp = fill(pad_start_ref[N_EXPERTS] + b * MOE_ROWS, MOE_ROWS)
            cp.start() if phase == 0 else cp.wait()
            return c
        lax.fori_loop(0, pad_cnt_ref[N_EXPERTS] // MOE_ROWS, per_tail_block, 0)


def _scatter_kernel(pad_start_ref, pad_cnt_ref, rows_ref, h_ref, xb_hbm, zero_ref, stage_ref, sems, zsem, *, tm):
    i = pl.program_id(0)
    nt = pl.num_programs(0)
    n_dma = tm * TOP_K

    def tile_wait(slot):
        whole = xb_hbm.at[pl.ds(0, n_dma * SUBLANES)]
        pltpu.make_async_copy(whole, whole, sems.at[slot]).wait()

    @pl.when(i == 0)
    def _():
        _zero_fill(pad_start_ref, pad_cnt_ref, xb_hbm, zero_ref, zsem)

    for slot in range(2):
        @pl.when(i > 0)
        def _(slot=slot):
            tile_wait(slot)

        stage_ref[slot] = h_ref[pl.ds(slot * tm * SUBLANES, tm * SUBLANES), :]

        def issue(grp, c, slot=slot):
            r0 = grp * DMA_GROUP
            dst = [rows_ref[0, 0, k * (2 * tm) + slot * tm + r0 + r] for r in range(DMA_GROUP) for k in range(TOP_K)]
            for r in range(DMA_GROUP):
                src = _tile_row(stage_ref.at[slot], r0 + r)
                for k in range(TOP_K):
                    pltpu.make_async_copy(src, _tile_at(xb_hbm, dst[r * TOP_K + k]),
                                          sems.at[slot]).start(priority=k % 2)
            return c
        lax.fori_loop(0, tm // DMA_GROUP, issue, 0)

    @pl.when(i == nt - 1)
    def _():
        tile_wait(0)
        tile_wait(1)


def _scatter_rows(pad_start, pad_cnt, rows, h2, n_pad):
    tm = ROUTE_TM
    nt = h2.shape[0] // SUBLANES // (2 * tm)
    grid_spec = pltpu.PrefetchScalarGridSpec(
        num_scalar_prefetch=2,
        grid=(nt,),
        in_specs=[pl.BlockSpec((1, 1, 2 * tm * TOP_K), lambda i, a, b: (i, 0, 0), memory_space=pltpu.SMEM),
                  pl.BlockSpec((2 * tm * SUBLANES, LANES), lambda i, a, b: (i, 0))],
        out_specs=pl.BlockSpec(memory_space=pl.ANY),
        scratch_shapes=[pltpu.VMEM((MOE_ROWS * SUBLANES, LANES), F32),
                        pltpu.VMEM((2, tm * SUBLANES, LANES), F32),
                        pltpu.SemaphoreType.DMA((2,)), pltpu.SemaphoreType.DMA(())],
    )
    return pl.pallas_call(
        functools.partial(_scatter_kernel, tm=tm),
        grid_spec=grid_spec,
        out_shape=jax.ShapeDtypeStruct((n_pad * SUBLANES, LANES), F32),
        compiler_params=_params(("arbitrary",)),
        name="scatter_rows",
    )(pad_start, pad_cnt, _rows_per_step(rows, 2 * tm), h2)


def _expert_kernel(be_ref, grp_ref, ia_ref, ib_ref, nv_ref, xa_ref, xb_ref, wgu_ref, bgu_ref, wd_ref, bd_ref,
                   o_ref, x_scr, wgu_scr, wd_scr):
    i = pl.program_id(0)
    live = i < nv_ref[0]

    @pl.when(live & ((i == 0) | (be_ref[i] != be_ref[jnp.maximum(i - 1, 0)])))
    def _():
        wgu_scr[...] = wgu_ref[0].astype(BF16)
        wd_scr[...] = wd_ref[0].astype(BF16)

    @pl.when(live & (grp_ref[i] == 0))
    def _():
        x_scr[...] = _tile_rows_load(xa_ref, MOE_ROWS).astype(BF16)

    @pl.when(live & (grp_ref[i] != 0))
    def _():
        x_scr[...] = _tile_rows_load(xb_ref, MOE_ROWS).astype(BF16)

    @pl.when(live)
    def _():
        gu = jnp.dot(x_scr[...], wgu_scr[...], preferred_element_type=F32) + bgu_ref[0]
        gate = jnp.minimum(gu[:, :D_MODEL], SWIGLU_LIMIT)
        up = jnp.clip(gu[:, D_MODEL:], -SWIGLU_LIMIT, SWIGLU_LIMIT)
        act = (up + 1.0) * (gate * jax.nn.sigmoid(SWIGLU_ALPHA * gate))
        _tile_rows_store(o_ref, jnp.dot(act.astype(BF16), wd_scr[...], preferred_element_type=F32) + bd_ref[0])

    @pl.when(i >= nv_ref[0])
    def _():
        o_ref[...] = jnp.zeros_like(o_ref)


def _experts(block_e, block_grp, blk_a, blk_b, n_valid, xa, xb, wgu, bgu, wd, bd):
    nb = block_e.shape[0]
    e3 = lambda i, be, gr, ia, ib, nv: (be[i], 0, 0)
    grid_spec = pltpu.PrefetchScalarGridSpec(
        num_scalar_prefetch=5,
        grid=(nb,),
        in_specs=[pl.BlockSpec((MOE_ROWS * SUBLANES, LANES), lambda i, be, gr, ia, ib, nv: (ia[i], 0)),
                  pl.BlockSpec((MOE_ROWS * SUBLANES, LANES), lambda i, be, gr, ia, ib, nv: (ib[i], 0)),
                  pl.BlockSpec((1, D_MODEL, 2 * D_MODEL), e3), pl.BlockSpec((1, 1, 2 * D_MODEL), e3),
                  pl.BlockSpec((1, D_MODEL, D_MODEL), e3), pl.BlockSpec((1, 1, D_MODEL), e3)],
        out_specs=pl.BlockSpec((MOE_ROWS * SUBLANES, LANES), lambda i, be, gr, ia, ib, nv: (i, 0)),
        scratch_shapes=[pltpu.VMEM((MOE_ROWS, D_MODEL), BF16), pltpu.VMEM((D_MODEL, 2 * D_MODEL), BF16),
                        pltpu.VMEM((D_MODEL, D_MODEL), BF16)],
    )
    return pl.pallas_call(
        _expert_kernel,
        grid_spec=grid_spec,
        out_shape=jax.ShapeDtypeStruct((nb * MOE_ROWS * SUBLANES, LANES), F32),
        compiler_params=_params(("arbitrary",), vmem=EXPERT_VMEM_LIMIT),
        name="routed_experts",
    )(block_e, block_grp, blk_a, blk_b, n_valid, xa, xb, wgu, bgu, wd, bd)


GATHER_PITCH = 12


def _combine_kernel(rows_cur, rows_nxt, x_ref, gt_ref, g_ref, y_hbm, o_ref, buf, sems, *, tm):
    i = pl.program_id(0)
    nt = pl.num_programs(0)
    units = SUBLANES * TOP_K
    per_unit = tm // units
    ahead = COMBINE_TILES // 2

    def tile_wait(slot):
        whole = y_hbm.at[pl.ds(0, TOP_K * tm * SUBLANES)]
        pltpu.make_async_copy(whole, whole, sems.at[slot]).wait()

    def start_rows(rows_ref, tile, r_lo, r_hi):
        idx = [rows_ref[0, 0, k * (COMBINE_TILES * tm) + tile * tm + r] for r in range(r_lo, r_hi) for k in range(TOP_K)]
        for n, (r, k) in enumerate((r, k) for r in range(r_lo, r_hi) for k in range(TOP_K)):
            dst = buf.at[tile, k].at[pl.ds(r * GATHER_PITCH, SUBLANES)]
            pltpu.make_async_copy(_tile_at(y_hbm, idx[n]), dst, sems.at[tile]).start(priority=k % 2)

    def finish(tile):
        nxt = tile + ahead
        nxt_rows, nxt_tile = (rows_cur, nxt) if nxt < COMBINE_TILES else (rows_nxt, nxt - COMBINE_TILES)
        tile_wait(tile)
        rows = pl.ds(tile * tm, tm)
        g8 = gt_ref[:, tile * tm:(tile + 1) * tm]
        gt = jnp.concatenate([g8, jnp.zeros((tm - SUBLANES, tm), F32)], axis=0).T
        gk = [gt[:, k:k + 1] for k in range(TOP_K)]
        ss = jnp.zeros((tm, 1), F32)
        for j in range(SUBLANES):
            cols = slice(j * LANES, (j + 1) * LANES)
            a = x_ref[rows, cols]
            for k in range(TOP_K):
                u = j * TOP_K + k
                start_rows(nxt_rows, nxt_tile, u * per_unit, (u + 1) * per_unit)
                a = a + buf[tile, k, pl.ds(j, tm, stride=GATHER_PITCH), :] * gk[k]
            ss = ss + jnp.sum(a * a, axis=-1, keepdims=True)
            o_ref[rows, cols] = a
        scale = lax.rsqrt(ss * (1.0 / D_MODEL) + EPS)
        o_ref[rows, :] = o_ref[rows, :] * scale * g_ref[...]

    @pl.when(i == 0)
    def _():
        for tile in range(ahead):
            start_rows(rows_cur, tile, 0, tm)

    for tile in range(COMBINE_TILES):
        finish(tile)

    @pl.when(i == nt - 1)
    def _():
        for tile in range(ahead):
            tile_wait(tile)


def _combine(rows, x2, gates, g, yb):
    t = x2.shape[0]
    tm = COMBINE_TM
    step = COMBINE_TILES * tm
    nt = t // step
    rows3 = _rows_per_step(rows, step)
    row = lambda i: (i, 0)
    return pl.pallas_call(
        functools.partial(_combine_kernel, tm=tm),
        grid=(nt,),
        in_specs=[pl.BlockSpec((1, 1, step * TOP_K), lambda i: (i, 0, 0), memory_space=pltpu.SMEM),
                  pl.BlockSpec((1, 1, step * TOP_K), lambda i: (jnp.minimum(i + 1, nt - 1), 0, 0),
                               memory_space=pltpu.SMEM),
                  pl.BlockSpec((step, D_MODEL), row), pl.BlockSpec((SUBLANES, step), lambda i: (0, i)),
                  pl.BlockSpec((1, D_MODEL), lambda i: (0, 0)),
                  pl.BlockSpec(memory_space=pl.ANY)],
        out_specs=pl.BlockSpec((step, D_MODEL), row),
        out_shape=jax.ShapeDtypeStruct((t, D_MODEL), F32),
        scratch_shapes=[pltpu.VMEM((COMBINE_TILES, TOP_K, tm * GATHER_PITCH, LANES), F32),
                        pltpu.SemaphoreType.DMA((COMBINE_TILES,))],
        compiler_params=_params(("arbitrary",)),
        name="combine_final_norm",
    )(rows3, rows3, x2, gates, g, yb)


def _rope_tables(seq_len):
    inv_freq = ROPE_THETA ** (-np.arange(ROPE_HALF, dtype=np.float64) * 2.0 / ROPE_DIM)
    ang = np.arange(seq_len, dtype=np.float64)[:, None] * inv_freq[None, :]
    cos = np.cos(ang).astype(np.float32)
    sin = np.sin(ang).astype(np.float32)
    pad = HEAD_DIM - ROPE_DIM
    ones = np.ones((seq_len, pad), np.float32)
    zer_h = np.zeros((seq_len, ROPE_HALF), np.float32)
    zer_p = np.zeros((seq_len, pad), np.float32)
    c = np.concatenate([cos, cos, ones], axis=1)
    s1 = np.concatenate([zer_h, sin, zer_p], axis=1)
    s2 = np.concatenate([-sin, zer_h, zer_p], axis=1)
    rep = LANES // HEAD_DIM
    return tuple(jnp.asarray(np.tile(t, (1, rep))) for t in (c, s1, s2))


def _ssm_weights(a_re, a_im, log_dt, b_re, b_im, c_re, c_im, ssm_d, nsteps):
    r = SSM_CHUNK
    dt = jnp.exp(log_dt)[..., None]
    lr, li = a_re * dt, a_im * dt

    taus = jnp.arange(r + 1, dtype=F32)
    mag = jnp.exp(lr[..., None] * taus)
    p_re, p_im = mag * jnp.cos(li[..., None] * taus), mag * jnp.sin(li[..., None] * taus)

    ab_re, ab_im = p_re[..., 1], p_im[..., 1]
    den = a_re * a_re + a_im * a_im
    num_re, num_im = ab_re - 1.0, ab_im
    f_re = (num_re * a_re + num_im * a_im) / den
    f_im = (num_im * a_re - num_re * a_im) / den
    bb_re = f_re[..., None] * b_re - f_im[..., None] * b_im
    bb_im = f_re[..., None] * b_im + f_im[..., None] * b_re

    m_re = p_re[..., None] * bb_re[:, :, :, None, :] - p_im[..., None] * bb_im[:, :, :, None, :]
    m_im = p_re[..., None] * bb_im[:, :, :, None, :] + p_im[..., None] * bb_re[:, :, :, None, :]
    kern = (jnp.einsum('dgcp,dgptk->dgctk', c_re, m_re) - jnp.einsum('dgcp,dgptk->dgctk', c_im, m_im))
    center = kern[0, :, :, 0] + kern[1, :, :, 0] + jnp.eye(SSM_GROUP, dtype=F32) * ssm_d[:, :, None]
    lags = jnp.concatenate([kern[0, :, :, r - 1:0:-1], center[:, :, None], kern[1, :, :, 1:r]], axis=2)
    strip = lags.reshape(SSM_N_GROUPS, SSM_GROUP, (2 * r - 1) * SSM_GROUP)
    w_toep = jnp.pad(strip, ((0, 0), (0, 0), (0, LAG_STRIP - (2 * r - 1) * SSM_GROUP)))

    flat = lambda a: a.reshape(SSM_N_GROUPS, SSM_STATE, TOEP)
    w_state = jnp.concatenate([flat(m_re[0, :, :, r - 1::-1]), flat(m_im[0, :, :, r - 1::-1]),
                               flat(m_re[1, :, :, :r]), flat(m_im[1, :, :, :r])], axis=1)

    def c_times_pow(d, pr, pi):
        pr, pi = pr.transpose(0, 2, 1)[:, :, None, :], pi.transpose(0, 2, 1)[:, :, None, :]
        zr = c_re[d][:, None] * pr - c_im[d][:, None] * pi
        zi = c_re[d][:, None] * pi + c_im[d][:, None] * pr
        return zr, -zi
    cf_re, cf_im = c_times_pow(0, p_re[0, :, :, 1:r + 1], p_im[0, :, :, 1:r + 1])
    cb_re, cb_im = c_times_pow(1, p_re[1, :, :, r:0:-1], p_im[1, :, :, r:0:-1])
    w_carry = jnp.concatenate([cf_re, cf_im, cb_re, cb_im], axis=3).reshape(SSM_N_GROUPS, TOEP, 4 * SSM_STATE)

    qr, qi = p_re[..., r], p_im[..., r]
    cols = []
    for _ in range(nsteps):
        cols.append(jnp.stack([qr[0], qi[0], qr[1], qi[1]], axis=1))
        qr, qi = qr * qr - qi * qi, 2.0 * qr * qi
    pw = jnp.stack(cols, axis=-1).reshape(SSM_N_GROUPS, 4 * SSM_STATE, nsteps)
    return w_toep, w_state.astype(BF16), w_carry.astype(BF16), pw


def _front(x, prm, other_scatter=None):
    n, seq_len, _ = x.shape
    t = n * seq_len
    x2d = x.reshape(t, D_MODEL)
    q, kv = _qkv(x2d, prm['norm1_g'], prm['w_qkv'], *_rope_tables(seq_len), seq_len)
    if other_scatter is None:
        attn_n, other_buf = _attention(q, kv, prm['sink'], prm['attn_out_g'], seq_len), None
    else:
        attn_n, other_buf = _attention_scatter(q, kv, prm['sink'], prm['attn_out_g'], seq_len, *other_scatter)

    chunks_per_seq = seq_len // SSM_CHUNK
    nsteps = max(1, (chunks_per_seq - 1).bit_length())
    nc = t // SSM_CHUNK
    lt = min(MXU_WIDTH, nc)
    ut = _uproj(x2d, prm['norm1_g'], prm['w_u_t'], lt)
    yt = _ssm(ut, prm['w_toep'], prm['w_state'], prm['w_carry'], prm['pw'][:, :, :nsteps], chunks_per_seq)
    ssm_n = _glu(yt, prm['glu_w_t'], prm['glu_b'], prm['ssm_out_g'], lt)
    outs = _outproj(x2d, attn_n, ssm_n, prm['w_out_a'], prm['w_out_s'], prm['norm2_g'],
                    prm['router_w'], prm['router_b'], prm['tri'])
    return outs, other_buf


def _cumsum_small(x):
    n = x.shape[0]
    keep = jnp.arange(n)[None, :] <= jnp.arange(n)[:, None]
    return jnp.sum(jnp.where(keep, x[None, :], 0), axis=1)


def _by_expert(idx, table):
    hit = idx[None] == jnp.arange(N_EXPERTS, dtype=I32)[:, None, None]
    return jnp.sum(jnp.where(hit, table[:, None, None], 0), axis=0)


def kernel(x_prompt, x_sample, norm1_g, w_in, attn_sink, ssm_a_re, ssm_a_im, ssm_log_dt, ssm_b_re, ssm_b_im, ssm_c_re, ssm_c_im, ssm_d, glu_w, glu_b, attn_out_g, ssm_out_g, w_out, norm2_g, router_w, router_b, w_gate_up, b_gate_up, w_down, b_down, final_g):
    assert norm1_g.shape[0] == 1, "single-layer problem"
    l = 0
    xs = [x_prompt, x_sample]
    max_chunks = max(x.shape[1] for x in xs) // SSM_CHUNK
    max_steps = max(1, (max_chunks - 1).bit_length())
    wq, wk, wv, wu = jnp.split(w_in[l], [ATTN_WIDTH, ATTN_WIDTH + KV_WIDTH, ATTN_WIDTH + 2 * KV_WIDTH], axis=1)
    dup = lambda w: jnp.concatenate([w[:, :HEAD_DIM], w[:, :HEAD_DIM], w[:, HEAD_DIM:], w[:, HEAD_DIM:]], axis=1)
    w_toep, w_state, w_carry, pw = _ssm_weights(ssm_a_re[l], ssm_a_im[l], ssm_log_dt[l], ssm_b_re[l], ssm_b_im[l],
                                                ssm_c_re[l], ssm_c_im[l], ssm_d[l], max_steps)
    tri_i = lax.broadcasted_iota(I32, (ROUTE_TM, ROUTE_TM), 0)
    tri_j = lax.broadcasted_iota(I32, (ROUTE_TM, ROUTE_TM), 1)
    prm = dict(
        norm1_g=norm1_g[l].reshape(1, D_MODEL),
        w_qkv=jnp.concatenate([wq, dup(wk), dup(wv)], axis=1).astype(BF16),
        w_u_t=wu.T.astype(BF16),
        sink=attn_sink[l].astype(F32),
        attn_out_g=attn_out_g[l].reshape(1, ATTN_WIDTH),
        w_toep=w_toep, w_state=w_state, w_carry=w_carry, pw=pw,
        glu_w_t=glu_w[l].T.astype(BF16),
        glu_b=glu_b[l].reshape(SSM_WIDTH, 1),
        ssm_out_g=ssm_out_g[l].reshape(SSM_WIDTH, 1),
        w_out_a=w_out[l][:ATTN_WIDTH].astype(BF16),
        w_out_s=w_out[l][ATTN_WIDTH:].astype(BF16),
        norm2_g=norm2_g[l].reshape(1, D_MODEL),
        router_w=router_w[l].T.astype(BF16),
        router_b=router_b[l].reshape(N_EXPERTS, 1),
        tri=(tri_i < tri_j).astype(BF16),
    )
    def scatter_plan(f):
        cnt = f[4][:, 0].astype(I32)
        pad = (cnt + MOE_ROWS - 1) // MOE_ROWS * MOE_ROWS
        pend = _cumsum_small(pad)
        pstart = pend - pad
        nb_g = f[0].shape[0] * TOP_K // MOE_ROWS + N_EXPERTS
        n_pad = nb_g * MOE_ROWS
        pad_start = jnp.concatenate([pstart + cnt, pend[-1:]]).astype(I32)
        pad_cnt = jnp.concatenate([pad - cnt, n_pad - pend[-1:]]).astype(I32)
        xrows = (_by_expert(f[2][:TOP_K], pstart) + f[2][TOP_K:]).astype(I32)
        return dict(padded=pad, pstart=pstart, nb=nb_g, job=(pad_start, pad_cnt, xrows, f[1], n_pad))

    f_a, _ = _front(xs[0], prm)
    plan_a = scatter_plan(f_a)
    f_b, xbuf_a = _front(xs[1], prm, other_scatter=plan_a['job'])
    plan_b = scatter_plan(f_b)
    xbufs = [xbuf_a, _scatter_rows(*plan_b['job'])]
    fronts, plans = [f_a, f_b], [plan_a, plan_b]
    padded = [p['padded'] for p in plans]
    nbs = [p['nb'] for p in plans]
    seg_blocks = jnp.stack([p // MOE_ROWS for p in padded], axis=1).reshape(-1)
    seg_end = _cumsum_small(seg_blocks)
    seg_start = seg_end - seg_blocks
    nb = sum(nbs)
    bi = jnp.arange(nb, dtype=I32)
    seg = jnp.minimum(jnp.sum(seg_end[None, :] <= bi[:, None], axis=1), 2 * N_EXPERTS - 1).astype(I32)
    block_e = seg // 2
    block_grp = seg % 2
    n_valid = seg_end[-1].astype(I32).reshape(1)
    seg_ids = jnp.arange(2 * N_EXPERTS, dtype=I32)
    in_seg = seg[:, None] == seg_ids[None, :]
    pick = lambda table: jnp.sum(jnp.where(in_seg, table[None, :], 0), axis=1)
    within = bi - pick(seg_start) + 1
    blk = []
    for g in range(2):
        mine = jnp.where(seg_ids % 2 == g, seg_blocks, 0)
        before = _cumsum_small(mine) - mine
        seen = pick(before) + jnp.where(block_grp == g, within, 0)
        blk.append(jnp.maximum(jnp.minimum(seen, jnp.sum(mine)) - 1, 0).astype(I32))

    yrows = [(_by_expert(f[2][:TOP_K], seg_start[g::2] * MOE_ROWS) + f[2][TOP_K:]).astype(I32)
             for g, f in enumerate(fronts)]
    yb = _experts(block_e, block_grp, blk[0], blk[1], n_valid, xbufs[0], xbufs[1],
                  w_gate_up.reshape(N_EXPERTS, D_MODEL, 2 * D_MODEL), b_gate_up[l][:, None, :],
                  w_down.reshape(N_EXPERTS, D_MODEL, D_MODEL), b_down[l][:, None, :])
    gfin = final_g.reshape(1, D_MODEL)
    outs = [_combine(r, f[0], f[3], gfin, yb).reshape(x.shape) for x, f, r in zip(xs, fronts, yrows)]
    return tuple(outs)
```

```python
import functools
import math

import jax
import jax.numpy as jnp
import numpy as np
from jax import lax
from jax.experimental import pallas as pl
from jax.experimental.pallas import tpu as pltpu

F32 = jnp.float32
BF16 = jnp.bfloat16
I32 = jnp.int32

D_MODEL = 1024
HEAD_DIM = 64
N_Q_HEADS = 8
N_KV_HEADS = 2
ATTN_WIDTH = N_Q_HEADS * HEAD_DIM
KV_WIDTH = N_KV_HEADS * HEAD_DIM
WINDOW = 128
ATT_BLOCK = 128
ROPE_THETA = 500000.0
ROPE_DIM = HEAD_DIM // 4
ROPE_HALF = ROPE_DIM // 2
SSM_WIDTH = 512
SSM_GROUP = 16
SSM_N_GROUPS = SSM_WIDTH // SSM_GROUP
SSM_STATE = 64
N_EXPERTS = 32
TOP_K = 4
SWIGLU_LIMIT = 7.0
SWIGLU_ALPHA = 1.702
EPS = 1e-5

LANES = 128
SUBLANES = 8
SSM_CHUNK = 32
TOEP = SSM_CHUNK * SSM_GROUP
MOE_ROWS = 512
MXU_WIDTH = 256
ROUTE_TM = 512
COMBINE_TM = 128
COMBINE_TILES = 8
VMEM_LIMIT = 52 * 1024 * 1024
EXPERT_VMEM_LIMIT = 60 * 1024 * 1024
NEG_BIG = -1e30


def _params(sem, vmem=VMEM_LIMIT):
    return pltpu.CompilerParams(dimension_semantics=sem, vmem_limit_bytes=vmem)


def _rms(x, g):
    ms = jnp.mean(x * x, axis=-1, keepdims=True)
    return x * lax.rsqrt(ms + EPS) * g


def _tile_rows_load(ref, rows):
    return jnp.concatenate([ref[pl.ds(j, rows, stride=SUBLANES), :] for j in range(SUBLANES)], axis=1)


def _tile_rows_store(ref, val):
    rows = val.shape[0]
    for j in range(SUBLANES):
        ref[pl.ds(j, rows, stride=SUBLANES), :] = val[:, j * LANES:(j + 1) * LANES]


def _tile_at(ref, start):
    return ref.at[pl.ds(pl.multiple_of(start, SUBLANES), SUBLANES)]


def _tile_row(ref, row):
    return _tile_at(ref, row * SUBLANES)


KV_COLS = 4 * LANES


def _qkv_kernel(x_ref, g_ref, w_ref, c_ref, s1_ref, s2_ref, q_ref, kv_ref):
    h = _rms(x_ref[...], g_ref[...]).astype(BF16)
    p = jnp.dot(h, w_ref[...], preferred_element_type=F32)
    c = c_ref[...]
    s1 = s1_ref[...]
    s2 = s2_ref[...]

    def rot(t):
        return t * c + pltpu.roll(t, ROPE_HALF, 1) * s1 + pltpu.roll(t, LANES - ROPE_HALF, 1) * s2

    for j in range(ATTN_WIDTH // LANES):
        q_ref[:, j * LANES:(j + 1) * LANES] = (rot(p[:, j * LANES:(j + 1) * LANES]) * (HEAD_DIM ** -0.5)).astype(BF16)
    for j in range(N_KV_HEADS):
        col = ATTN_WIDTH + j * LANES
        kv_ref[:, j * LANES:(j + 1) * LANES] = rot(p[:, col:col + LANES]).astype(BF16)
    kv_ref[:, N_KV_HEADS * LANES:] = p[:, ATTN_WIDTH + N_KV_HEADS * LANES:].astype(BF16)


def _qkv(x2d, g, w, c, s1, s2, seq_len):
    t = x2d.shape[0]
    tm = ROUTE_TM
    nlb = seq_len // tm
    row = lambda i: (i, 0)
    tab = lambda i: (i % nlb, 0)
    full = lambda i: (0, 0)
    return pl.pallas_call(
        _qkv_kernel,
        grid=(t // tm,),
        in_specs=[pl.BlockSpec((tm, D_MODEL), row), pl.BlockSpec((1, D_MODEL), full),
                  pl.BlockSpec((D_MODEL, ATTN_WIDTH + KV_COLS), full),
                  pl.BlockSpec((tm, LANES), tab), pl.BlockSpec((tm, LANES), tab), pl.BlockSpec((tm, LANES), tab)],
        out_specs=[pl.BlockSpec((tm, ATTN_WIDTH), row), pl.BlockSpec((tm, KV_COLS), row)],
        out_shape=[jax.ShapeDtypeStruct((t, ATTN_WIDTH), BF16), jax.ShapeDtypeStruct((t, KV_COLS), BF16)],
        compiler_params=_params(("parallel",)),
        name="qkv_rotary",
    )(x2d, g, w, c, s1, s2)


def _attention_block(sink_ref, q_ref, kvp, kvc, kvn, g_ref, o_ref, i, bps, between=lambda: None):
    first = (i % bps) == 0
    last = (i % bps) == bps - 1
    qi = lax.broadcasted_iota(I32, (ATT_BLOCK, 3 * ATT_BLOCK), 0)
    kj = lax.broadcasted_iota(I32, (ATT_BLOCK, 3 * ATT_BLOCK), 1)
    rel = kj - ATT_BLOCK - qi
    valid = (jnp.abs(rel) <= WINDOW)
    valid = valid & ((kj >= ATT_BLOCK) | jnp.logical_not(first))
    valid = valid & ((kj < 2 * ATT_BLOCK) | jnp.logical_not(last))
    kv = jnp.concatenate([kvp[...], kvc[...], kvn[...]], axis=0)
    ks = [kv[:, h * LANES:(h + 1) * LANES] for h in range(N_KV_HEADS)]
    vs = [kv[:, (N_KV_HEADS + h) * LANES:(N_KV_HEADS + h + 1) * LANES] for h in range(N_KV_HEADS)]
    lo = lax.broadcasted_iota(I32, (ATT_BLOCK, LANES), 1) < HEAD_DIM
    zero = jnp.zeros((ATT_BLOCK, LANES), BF16)
    heads = [(j, par) for j in range(ATTN_WIDTH // LANES) for par in range(2)]
    nt_dims = (((1,), (1,)), ((), ()))

    def staged(fn, *lists):
        out = []
        for args in zip(*lists):
            between()
            out.append(fn(*args))
        return out

    def score(head):
        j, par = head
        qt = q_ref[:, j * LANES:(j + 1) * LANES]
        qm = jnp.where(lo if par == 0 else jnp.logical_not(lo), qt, zero)
        return lax.dot_general(qm, ks[j // 2], nt_dims, preferred_element_type=F32)

    scores = staged(score, heads)
    scores = staged(lambda s: jnp.where(valid, s, NEG_BIG), scores)
    sinks = [sink_ref[2 * j + par] for j, par in heads]
    maxes = staged(lambda s, sk: jnp.maximum(jnp.max(s, axis=-1, keepdims=True), sk), scores, sinks)
    probs = staged(lambda s, m: jnp.exp(s - m), scores, maxes)
    dens = staged(lambda p, m, sk: jnp.sum(p, axis=-1, keepdims=True) + jnp.exp(sk - m), probs, maxes, sinks)
    outs = staged(lambda p, head: jnp.dot(p.astype(BF16), vs[head[0] // 2], preferred_element_type=F32), probs, heads)
    outs = staged(lambda o, d: o / d, outs, dens)
    tiles = [jnp.where(lo, outs[2 * j], outs[2 * j + 1]) for j in range(ATTN_WIDTH // LANES)]
    o = jnp.concatenate(tiles, axis=1)
    o_ref[...] = _rms(o, g_ref[...]).astype(BF16)


ATTN_STAGES = 7


def _attn_kernel(sink_ref, q_ref, kvp, kvc, kvn, g_ref, o_ref, *, bps):
    _attention_block(sink_ref, q_ref, kvp, kvc, kvn, g_ref, o_ref, pl.program_id(0), bps)


def _attn_scatter_kernel(sink_ref, pad_start_ref, pad_cnt_ref, q_ref, kvp, kvc, kvn, g_ref, rows_ref, h_ref,
                         o_ref, xb_hbm, zero_ref, stage_ref, sems, zsem, *, bps, tm):
    i = pl.program_id(0)
    nt = pl.num_programs(0)
    slot = i % 2
    n_dma = tm * TOP_K

    def tile_wait(s):
        whole = xb_hbm.at[pl.ds(0, n_dma * SUBLANES)]
        pltpu.make_async_copy(whole, whole, sems.at[s]).wait()

    @pl.when(i == 0)
    def _():
        _zero_fill(pad_start_ref, pad_cnt_ref, xb_hbm, zero_ref, zsem)

    @pl.when(i >= 2)
    def _():
        tile_wait(slot)

    stage_ref[slot] = h_ref[...]
    points = ATTN_STAGES * N_Q_HEADS
    per_point = -(-tm // points)
    started = [0]

    def between():
        lo, hi = started[0], min(tm, started[0] + per_point)
        started[0] = hi
        if hi <= lo:
            return
        dst = [rows_ref[0, 0, k * tm + r] for r in range(lo, hi) for k in range(TOP_K)]
        for n, (r, k) in enumerate((r, k) for r in range(lo, hi) for k in range(TOP_K)):
            src = stage_ref.at[slot].at[pl.ds(r * SUBLANES, SUBLANES)]
            pltpu.make_async_copy(src, _tile_at(xb_hbm, dst[n]), sems.at[slot]).start(priority=k % 2)

    _attention_block(sink_ref, q_ref, kvp, kvc, kvn, g_ref, o_ref, i, bps, between)
    while started[0] < tm:
        between()

    @pl.when(i == nt - 1)
    def _():
        @pl.when(nt > 1)
        def _():
            tile_wait(1 - slot)
        tile_wait(slot)


def _attention(q, kv, sink, g, seq_len):
    t = q.shape[0]
    nblk = t // ATT_BLOCK
    bps = seq_len // ATT_BLOCK
    cur = lambda i, s: (i, 0)
    prv = lambda i, s: (jnp.maximum(i - 1, 0), 0)
    nxt = lambda i, s: (jnp.minimum(i + 1, nblk - 1), 0)
    grid_spec = pltpu.PrefetchScalarGridSpec(
        num_scalar_prefetch=1,
        grid=(nblk,),
        in_specs=[pl.BlockSpec((ATT_BLOCK, ATTN_WIDTH), cur),
                  pl.BlockSpec((ATT_BLOCK, KV_COLS), prv), pl.BlockSpec((ATT_BLOCK, KV_COLS), cur),
                  pl.BlockSpec((ATT_BLOCK, KV_COLS), nxt),
                  pl.BlockSpec((1, ATTN_WIDTH), lambda i, s: (0, 0))],
        out_specs=pl.BlockSpec((ATT_BLOCK, ATTN_WIDTH), cur),
    )
    return pl.pallas_call(
        functools.partial(_attn_kernel, bps=bps),
        grid_spec=grid_spec,
        out_shape=jax.ShapeDtypeStruct((t, ATTN_WIDTH), BF16),
        compiler_params=_params(("parallel",)),
        name="banded_attention",
    )(sink, q, kv, kv, kv, g)


def _attention_scatter(q, kv, sink, g, seq_len, pad_start, pad_cnt, rows, h2, n_pad):
    t = q.shape[0]
    nblk = t // ATT_BLOCK
    bps = seq_len // ATT_BLOCK
    t_other = h2.shape[0] // SUBLANES
    assert t_other % nblk == 0 and nblk >= 2
    tm = t_other // nblk
    cur = lambda i, s, a, b: (i, 0)
    prv = lambda i, s, a, b: (jnp.maximum(i - 1, 0), 0)
    nxt = lambda i, s, a, b: (jnp.minimum(i + 1, nblk - 1), 0)
    grid_spec = pltpu.PrefetchScalarGridSpec(
        num_scalar_prefetch=3,
        grid=(nblk,),
        in_specs=[pl.BlockSpec((ATT_BLOCK, ATTN_WIDTH), cur),
                  pl.BlockSpec((ATT_BLOCK, KV_COLS), prv), pl.BlockSpec((ATT_BLOCK, KV_COLS), cur),
                  pl.BlockSpec((ATT_BLOCK, KV_COLS), nxt),
                  pl.BlockSpec((1, ATTN_WIDTH), lambda i, s, a, b: (0, 0)),
                  pl.BlockSpec((1, 1, TOP_K * tm), lambda i, s, a, b: (i, 0, 0), memory_space=pltpu.SMEM),
                  pl.BlockSpec((tm * SUBLANES, LANES), lambda i, s, a, b: (i, 0))],
        out_specs=[pl.BlockSpec((ATT_BLOCK, ATTN_WIDTH), cur), pl.BlockSpec(memory_space=pl.ANY)],
        scratch_shapes=[pltpu.VMEM((MOE_ROWS * SUBLANES, LANES), F32),
                        pltpu.VMEM((2, tm * SUBLANES, LANES), F32),
                        pltpu.SemaphoreType.DMA((2,)), pltpu.SemaphoreType.DMA(())],
    )
    return pl.pallas_call(
        functools.partial(_attn_scatter_kernel, bps=bps, tm=tm),
        grid_spec=grid_spec,
        out_shape=[jax.ShapeDtypeStruct((t, ATTN_WIDTH), BF16),
                   jax.ShapeDtypeStruct((n_pad * SUBLANES, LANES), F32)],
        compiler_params=_params(("arbitrary",)),
        name="attention_scatter",
    )(sink, pad_start, pad_cnt, q, kv, kv, kv, g, _rows_per_step(rows, tm), h2)


def _uproj_kernel(x_ref, g_ref, w_ref, o_ref, stage_ref):
    lt = x_ref.shape[0]
    h = _rms(x_ref[...].reshape(lt * SUBLANES, D_MODEL), g_ref[...])
    for j in range(D_MODEL // LANES):
        stage_ref[j] = h[:, j * LANES:(j + 1) * LANES]
    for bl in range(SUBLANES):
        hb = jnp.concatenate([stage_ref[j, pl.ds(bl, lt, stride=SUBLANES), :] for j in range(D_MODEL // LANES)],
                             axis=1).astype(BF16)
        ut = lax.dot_general(w_ref[...], hb, (((1,), (1,)), ((), ())), preferred_element_type=F32)
        o_ref[bl] = ut.astype(BF16)


def _uproj(x2d, g, w_t, lt):
    t = x2d.shape[0]
    nc = t // SSM_CHUNK
    xv = x2d.reshape(nc, SSM_CHUNK, D_MODEL)
    return pl.pallas_call(
        _uproj_kernel,
        grid=(nc // lt, SSM_CHUNK // SUBLANES),
        in_specs=[pl.BlockSpec((lt, SUBLANES, D_MODEL), lambda i, b: (i, b, 0)),
                  pl.BlockSpec((1, D_MODEL), lambda i, b: (0, 0)),
                  pl.BlockSpec((SSM_WIDTH, D_MODEL), lambda i, b: (0, 0))],
        out_specs=pl.BlockSpec((SUBLANES, SSM_WIDTH, lt), lambda i, b: (b, 0, i)),
        out_shape=jax.ShapeDtypeStruct((SSM_CHUNK, SSM_WIDTH, nc), BF16),
        scratch_shapes=[pltpu.VMEM((D_MODEL // LANES, lt * SUBLANES, LANES), F32)],
        compiler_params=_params(("parallel", "parallel")),
        name="u_projection",
    )(xv, g, w_t)


def _gelu_tanh(x):
    return 0.5 * x * (1.0 + jnp.tanh(math.sqrt(2.0 / math.pi) * (x + 0.044715 * (x * x * x))))


LAG_STRIP = 8 * LANES


def _toeplitz_from_strip(strip):
    per_tile = LANES // SSM_GROUP
    rolled = [strip if q == 0 else pltpu.roll(strip, LAG_STRIP - q * SSM_GROUP, 1) for q in range(per_tile)]
    blocks = []
    for b in range(SSM_CHUNK):
        m, q = divmod(SSM_CHUNK - 1 - b, per_tile)
        blocks.append(rolled[q][:, m * LANES:m * LANES + TOEP])
    return jnp.concatenate(blocks, axis=0)


def _ssm_kernel(a_ref, wt_ref, ws_ref, wc_ref, pw_ref, y_ref, *, chunks_per_seq, nsteps):
    nc = a_ref.shape[2]
    a = a_ref[...].reshape(TOEP, nc)
    w_toep = _toeplitz_from_strip(wt_ref[0]).astype(BF16)
    y = jnp.dot(w_toep, a, preferred_element_type=F32)
    s = jnp.dot(ws_ref[0], a, preferred_element_type=F32)
    pos = lax.broadcasted_iota(I32, (SSM_STATE, nc), 1) % chunks_per_seq
    carries = []
    for d in range(2):
        hr = s[2 * d * SSM_STATE:(2 * d + 1) * SSM_STATE]
        hi = s[(2 * d + 1) * SSM_STATE:(2 * d + 2) * SSM_STATE]
        for k in range(nsteps):
            sh = 1 << k
            pr = pw_ref[0, 2 * d * SSM_STATE:(2 * d + 1) * SSM_STATE, k:k + 1]
            pi = pw_ref[0, (2 * d + 1) * SSM_STATE:(2 * d + 2) * SSM_STATE, k:k + 1]
            if d == 0:
                ok = pos >= sh
                sr = pltpu.roll(hr, sh, 1)
                si = pltpu.roll(hi, sh, 1)
            else:
                ok = pos < chunks_per_seq - sh
                sr = pltpu.roll(hr, nc - sh, 1)
                si = pltpu.roll(hi, nc - sh, 1)
            hr, hi = (hr + jnp.where(ok, pr * sr - pi * si, 0.0),
                      hi + jnp.where(ok, pr * si + pi * sr, 0.0))
        if d == 0:
            ok = pos >= 1
            cr = pltpu.roll(hr, 1, 1)
            ci = pltpu.roll(hi, 1, 1)
        else:
            ok = pos < chunks_per_seq - 1
            cr = pltpu.roll(hr, nc - 1, 1)
            ci = pltpu.roll(hi, nc - 1, 1)
        carries += [jnp.where(ok, cr, 0.0), jnp.where(ok, ci, 0.0)]
    carry = jnp.concatenate(carries, axis=0).astype(BF16)
    y = y + jnp.dot(wc_ref[0], carry, preferred_element_type=F32)
    y_ref[...] = _gelu_tanh(y).reshape(SSM_CHUNK, SSM_GROUP, nc)


def _ssm(ut, w_toep, w_state, w_carry, pw, chunks_per_seq):
    nc = ut.shape[2]
    nsteps = max(1, (chunks_per_seq - 1).bit_length())
    g3 = lambda g: (g, 0, 0)
    return pl.pallas_call(
        functools.partial(_ssm_kernel, chunks_per_seq=chunks_per_seq, nsteps=nsteps),
        grid=(SSM_N_GROUPS,),
        in_specs=[pl.BlockSpec((SSM_CHUNK, SSM_GROUP, nc), lambda g: (0, g, 0)),
                  pl.BlockSpec((1, SSM_GROUP, LAG_STRIP), g3),
                  pl.BlockSpec((1, 4 * SSM_STATE, TOEP), g3),
                  pl.BlockSpec((1, TOEP, 4 * SSM_STATE), g3),
                  pl.BlockSpec((1, 4 * SSM_STATE, pw.shape[2]), g3)],
        out_specs=pl.BlockSpec((SSM_CHUNK, SSM_GROUP, nc), lambda g: (0, g, 0)),
        out_shape=jax.ShapeDtypeStruct((SSM_CHUNK, SSM_WIDTH, nc), F32),
        compiler_params=_params(("parallel",)),
        name="s5_core",
    )(ut, w_toep, w_state, w_carry, pw)


def _glu_kernel(y_ref, w_ref, b_ref, g_ref, o_ref, stage_ref):
    lt = o_ref.shape[0]
    for bl in range(SUBLANES):
        y = y_ref[bl]
        z = jnp.dot(w_ref[...], y.astype(BF16), preferred_element_type=F32) + b_ref[...]
        s = y * jax.nn.sigmoid(z)
        ms = jnp.mean(s * s, axis=0, keepdims=True)
        sn = (s * lax.rsqrt(ms + EPS) * g_ref[...]).T
        for j in range(SSM_WIDTH // LANES):
            stage_ref[j, pl.ds(bl, lt, stride=SUBLANES), :] = sn[:, j * LANES:(j + 1) * LANES]
    for j in range(SSM_WIDTH // LANES):
        o_ref[:, :, j * LANES:(j + 1) * LANES] = stage_ref[j].reshape(lt, SUBLANES, LANES)


def _glu(yt, w_t, b_col, g_col, lt):
    nc = yt.shape[2]
    out = pl.pallas_call(
        _glu_kernel,
        grid=(nc // lt, SSM_CHUNK // SUBLANES),
        in_specs=[pl.BlockSpec((SUBLANES, SSM_WIDTH, lt), lambda i, b: (b, 0, i)),
                  pl.BlockSpec((SSM_WIDTH, SSM_WIDTH), lambda i, b: (0, 0)),
                  pl.BlockSpec((SSM_WIDTH, 1), lambda i, b: (0, 0)),
                  pl.BlockSpec((SSM_WIDTH, 1), lambda i, b: (0, 0))],
        out_specs=pl.BlockSpec((lt, SUBLANES, SSM_WIDTH), lambda i, b: (i, b, 0)),
        out_shape=jax.ShapeDtypeStruct((nc, SSM_CHUNK, SSM_WIDTH), F32),
        scratch_shapes=[pltpu.VMEM((SSM_WIDTH // LANES, lt * SUBLANES, LANES), F32)],
        compiler_params=_params(("parallel", "parallel")),
        name="glu_norm",
    )(yt, w_t, b_col, g_col)
    return out.reshape(nc * SSM_CHUNK, SSM_WIDTH)


def _outproj_kernel(x_ref, a_ref, s_ref, wa_ref, ws_ref, g_ref, wr_ref, br_ref, tri_ref,
                    x2_ref, h2_ref, rt_ref, gt_ref, cnt_ref, run_ref):
    i = pl.program_id(0)

    @pl.when(i == 0)
    def _():
        run_ref[...] = jnp.zeros_like(run_ref)

    x2 = (x_ref[...] + jnp.dot(a_ref[...], wa_ref[...], preferred_element_type=F32)
          + jnp.dot(s_ref[...].astype(BF16), ws_ref[...], preferred_element_type=F32))
    x2_ref[...] = x2
    h2 = _rms(x2, g_ref[...])
    _tile_rows_store(h2_ref, h2)
    logits = lax.dot_general(wr_ref[...], h2.astype(BF16), (((1,), (1,)), ((), ())),
                             preferred_element_type=F32) + br_ref[...]
    tm = logits.shape[1]
    sub = lax.broadcasted_iota(I32, (N_EXPERTS, tm), 0)
    sub_f = sub.astype(F32)
    work = logits
    sel = jnp.zeros((N_EXPERTS, tm), F32)
    top_v, top_i = [], []
    for _ in range(TOP_K):
        m = jnp.max(work, axis=0, keepdims=True)
        idx = jnp.min(jnp.where(work == m, sub_f, float(N_EXPERTS)), axis=0, keepdims=True).astype(I32)
        hit = sub == idx
        sel = jnp.where(hit, 1.0, sel)
        work = jnp.where(hit, -jnp.inf, work)
        top_v.append(m)
        top_i.append(idx)
    ex = [jnp.exp(v - top_v[0]) for v in top_v]
    den = ex[0] + ex[1] + ex[2] + ex[3]
    before = jnp.dot(sel.astype(BF16), tri_ref[...], preferred_element_type=F32) + run_ref[:, 0:1]
    ranks = [jnp.sum(jnp.where(sub == top_i[k], before, 0.0), axis=0, keepdims=True).astype(I32) for k in range(TOP_K)]
    rt_ref[...] = jnp.concatenate(top_i + ranks, axis=0)
    gt_ref[...] = jnp.concatenate([e / den for e in ex] + [jnp.zeros((SUBLANES - TOP_K, tm), F32)], axis=0)
    run = run_ref[...] + jnp.sum(sel, axis=1, keepdims=True)
    run_ref[...] = run
    cnt_ref[...] = run


def _outproj(x2d, attn_n, ssm_n, w_a, w_s, g, w_r, b_r, tri):
    t = x2d.shape[0]
    tm = ROUTE_TM
    row = lambda i: (i, 0)
    full = lambda i: (0, 0)
    return pl.pallas_call(
        _outproj_kernel,
        grid=(t // tm,),
        in_specs=[pl.BlockSpec((tm, D_MODEL), row), pl.BlockSpec((tm, ATTN_WIDTH), row),
                  pl.BlockSpec((tm, SSM_WIDTH), row),
                  pl.BlockSpec((ATTN_WIDTH, D_MODEL), full), pl.BlockSpec((SSM_WIDTH, D_MODEL), full),
                  pl.BlockSpec((1, D_MODEL), full),
                  pl.BlockSpec((N_EXPERTS, D_MODEL), full), pl.BlockSpec((N_EXPERTS, 1), full),
                  pl.BlockSpec((tm, tm), full)],
        out_specs=[pl.BlockSpec((tm, D_MODEL), row), pl.BlockSpec((tm * SUBLANES, LANES), row),
                   pl.BlockSpec((2 * TOP_K, tm), lambda i: (0, i)), pl.BlockSpec((SUBLANES, tm), lambda i: (0, i)),
                   pl.BlockSpec((N_EXPERTS, LANES), full)],
        out_shape=[jax.ShapeDtypeStruct((t, D_MODEL), F32), jax.ShapeDtypeStruct((t * SUBLANES, LANES), F32),
                   jax.ShapeDtypeStruct((2 * TOP_K, t), I32), jax.ShapeDtypeStruct((SUBLANES, t), F32),
                   jax.ShapeDtypeStruct((N_EXPERTS, LANES), F32)],
        scratch_shapes=[pltpu.VMEM((N_EXPERTS, LANES), F32)],
        compiler_params=_params(("arbitrary",)),
        name="out_projection_router",
    )(x2d, attn_n, ssm_n, w_a, w_s, g, w_r, b_r, tri)


DMA_GROUP = 4


def _rows_per_step(rows, tokens):
    nt = rows.shape[1] // tokens
    return (rows * SUBLANES).reshape(TOP_K, nt, tokens).transpose(1, 0, 2).reshape(nt, 1, TOP_K * tokens)


def _zero_fill(pad_start_ref, pad_cnt_ref, xb_hbm, zero_ref, zsem):
    zero_ref[...] = jnp.zeros_like(zero_ref)

    def fill(row, n_rows):
        return pltpu.make_async_copy(zero_ref.at[pl.ds(0, n_rows * SUBLANES)],
                                     xb_hbm.at[pl.ds(pl.multiple_of(row * SUBLANES, SUBLANES), n_rows * SUBLANES)],
                                     zsem)

    for phase in range(2):
        def per_expert(e, c):
            cnt = pad_cnt_ref[e]
            row = pad_start_ref[e]
            run = MOE_ROWS // 2
            while run >= 1:
                below = cnt & ~(2 * run - 1)

                @pl.when((cnt & run) != 0)
                def _(run=run, below=below):
                    cp = fill(row + below, run)
                    cp.start() if phase == 0 else cp.wait()
                run //= 2
            return c
        lax.fori_loop(0, N_EXPERTS, per_expert, 0)

        def per_tail_block(b, c):
            cp = fill(pad_start_ref[N_EXPERTS] + b * MOE_ROWS, MOE_ROWS)
            cp.start() if phase == 0 else cp.wait()
            return c
        lax.fori_loop(0, pad_cnt_ref[N_EXPERTS] // MOE_ROWS, per_tail_block, 0)


def _scatter_kernel(pad_start_ref, pad_cnt_ref, rows_ref, h_ref, xb_hbm, zero_ref, stage_ref, sems, zsem, *, tm):
    i = pl.program_id(0)
    nt = pl.num_programs(0)
    n_dma = tm * TOP_K

    def tile_wait(slot):
        whole = xb_hbm.at[pl.ds(0, n_dma * SUBLANES)]
        pltpu.make_async_copy(whole, whole, sems.at[slot]).wait()

    @pl.when(i == 0)
    def _():
        _zero_fill(pad_start_ref, pad_cnt_ref, xb_hbm, zero_ref, zsem)

    for slot in range(2):
        @pl.when(i > 0)
        def _(slot=slot):
            tile_wait(slot)

        stage_ref[slot] = h_ref[pl.ds(slot * tm * SUBLANES, tm * SUBLANES), :]

        def issue(grp, c, slot=slot):
            r0 = grp * DMA_GROUP
            dst = [rows_ref[0, 0, k * (2 * tm) + slot * tm + r0 + r] for r in range(DMA_GROUP) for k in range(TOP_K)]
            for r in range(DMA_GROUP):
                src = _tile_row(stage_ref.at[slot], r0 + r)
                for k in range(TOP_K):
                    pltpu.make_async_copy(src, _tile_at(xb_hbm, dst[r * TOP_K + k]),
                                          sems.at[slot]).start(priority=k % 2)
            return c
        lax.fori_loop(0, tm // DMA_GROUP, issue, 0)

    @pl.when(i == nt - 1)
    def _():
        tile_wait(0)
        tile_wait(1)


def _scatter_rows(pad_start, pad_cnt, rows, h2, n_pad):
    tm = ROUTE_TM
    nt = h2.shape[0] // SUBLANES // (2 * tm)
    grid_spec = pltpu.PrefetchScalarGridSpec(
        num_scalar_prefetch=2,
        grid=(nt,),
        in_specs=[pl.BlockSpec((1, 1, 2 * tm * TOP_K), lambda i, a, b: (i, 0, 0), memory_space=pltpu.SMEM),
                  pl.BlockSpec((2 * tm * SUBLANES, LANES), lambda i, a, b: (i, 0))],
        out_specs=pl.BlockSpec(memory_space=pl.ANY),
        scratch_shapes=[pltpu.VMEM((MOE_ROWS * SUBLANES, LANES), F32),
                        pltpu.VMEM((2, tm * SUBLANES, LANES), F32),
                        pltpu.SemaphoreType.DMA((2,)), pltpu.SemaphoreType.DMA(())],
    )
    return pl.pallas_call(
        functools.partial(_scatter_kernel, tm=tm),
        grid_spec=grid_spec,
        out_shape=jax.ShapeDtypeStruct((n_pad * SUBLANES, LANES), F32),
        compiler_params=_params(("arbitrary",)),
        name="scatter_rows",
    )(pad_start, pad_cnt, _rows_per_step(rows, 2 * tm), h2)


def _expert_kernel(be_ref, grp_ref, ia_ref, ib_ref, nv_ref, xa_ref, xb_ref, wgu_ref, bgu_ref, wd_ref, bd_ref,
                   o_ref, x_scr, wgu_scr, wd_scr):
    i = pl.program_id(0)
    live = i < nv_ref[0]

    @pl.when(live & ((i == 0) | (be_ref[i] != be_ref[jnp.maximum(i - 1, 0)])))
    def _():
        wgu_scr[...] = wgu_ref[0].astype(BF16)
        wd_scr[...] = wd_ref[0].astype(BF16)

    @pl.when(live & (grp_ref[i] == 0))
    def _():
        x_scr[...] = _tile_rows_load(xa_ref, MOE_ROWS).astype(BF16)

    @pl.when(live & (grp_ref[i] != 0))
    def _():
        x_scr[...] = _tile_rows_load(xb_ref, MOE_ROWS).astype(BF16)

    @pl.when(live)
    def _():
        gu = jnp.dot(x_scr[...], wgu_scr[...], preferred_element_type=F32) + bgu_ref[0]
        gate = jnp.minimum(gu[:, :D_MODEL], SWIGLU_LIMIT)
        up = jnp.clip(gu[:, D_MODEL:], -SWIGLU_LIMIT, SWIGLU_LIMIT)
        act = (up + 1.0) * (gate * jax.nn.sigmoid(SWIGLU_ALPHA * gate))
        _tile_rows_store(o_ref, jnp.dot(act.astype(BF16), wd_scr[...], preferred_element_type=F32) + bd_ref[0])

    @pl.when(i >= nv_ref[0])
    def _():
        o_ref[...] = jnp.zeros_like(o_ref)


def _experts(block_e, block_grp, blk_a, blk_b, n_valid, xa, xb, wgu, bgu, wd, bd):
    nb = block_e.shape[0]
    e3 = lambda i, be, gr, ia, ib, nv: (be[i], 0, 0)
    grid_spec = pltpu.PrefetchScalarGridSpec(
        num_scalar_prefetch=5,
        grid=(nb,),
        in_specs=[pl.BlockSpec((MOE_ROWS * SUBLANES, LANES), lambda i, be, gr, ia, ib, nv: (ia[i], 0)),
                  pl.BlockSpec((MOE_ROWS * SUBLANES, LANES), lambda i, be, gr, ia, ib, nv: (ib[i], 0)),
                  pl.BlockSpec((1, D_MODEL, 2 * D_MODEL), e3), pl.BlockSpec((1, 1, 2 * D_MODEL), e3),
                  pl.BlockSpec((1, D_MODEL, D_MODEL), e3), pl.BlockSpec((1, 1, D_MODEL), e3)],
        out_specs=pl.BlockSpec((MOE_ROWS * SUBLANES, LANES), lambda i, be, gr, ia, ib, nv: (i, 0)),
        scratch_shapes=[pltpu.VMEM((MOE_ROWS, D_MODEL), BF16), pltpu.VMEM((D_MODEL, 2 * D_MODEL), BF16),
                        pltpu.VMEM((D_MODEL, D_MODEL), BF16)],
    )
    return pl.pallas_call(
        _expert_kernel,
        grid_spec=grid_spec,
        out_shape=jax.ShapeDtypeStruct((nb * MOE_ROWS * SUBLANES, LANES), F32),
        compiler_params=_params(("arbitrary",), vmem=EXPERT_VMEM_LIMIT),
        name="routed_experts",
    )(block_e, block_grp, blk_a, blk_b, n_valid, xa, xb, wgu, bgu, wd, bd)


GATHER_PITCH = 12


def _combine_kernel(rows_cur, rows_nxt, x_ref, gt_ref, g_ref, y_hbm, o_ref, buf, sems, *, tm):
    i = pl.program_id(0)
    nt = pl.num_programs(0)
    units = SUBLANES * TOP_K
    per_unit = tm // units
    ahead = COMBINE_TILES // 2

    def tile_wait(slot):
        whole = y_hbm.at[pl.ds(0, TOP_K * tm * SUBLANES)]
        pltpu.make_async_copy(whole, whole, sems.at[slot]).wait()

    def start_rows(rows_ref, tile, r_lo, r_hi):
        idx = [rows_ref[0, 0, k * (COMBINE_TILES * tm) + tile * tm + r] for r in range(r_lo, r_hi) for k in range(TOP_K)]
        for n, (r, k) in enumerate((r, k) for r in range(r_lo, r_hi) for k in range(TOP_K)):
            dst = buf.at[tile, k].at[pl.ds(r * GATHER_PITCH, SUBLANES)]
            pltpu.make_async_copy(_tile_at(y_hbm, idx[n]), dst, sems.at[tile]).start(priority=k % 2)

    def finish(tile):
        nxt = tile + ahead
        nxt_rows, nxt_tile = (rows_cur, nxt) if nxt < COMBINE_TILES else (rows_nxt, nxt - COMBINE_TILES)
        tile_wait(tile)
        rows = pl.ds(tile * tm, tm)
        g8 = gt_ref[:, tile * tm:(tile + 1) * tm]
        gt = jnp.concatenate([g8, jnp.zeros((tm - SUBLANES, tm), F32)], axis=0).T
        gk = [gt[:, k:k + 1] for k in range(TOP_K)]
        ss = jnp.zeros((tm, 1), F32)
        for j in range(SUBLANES):
            cols = slice(j * LANES, (j + 1) * LANES)
            a = x_ref[rows, cols]
            for k in range(TOP_K):
                u = j * TOP_K + k
                start_rows(nxt_rows, nxt_tile, u * per_unit, (u + 1) * per_unit)
                a = a + buf[tile, k, pl.ds(j, tm, stride=GATHER_PITCH), :] * gk[k]
            ss = ss + jnp.sum(a * a, axis=-1, keepdims=True)
            o_ref[rows, cols] = a
        scale = lax.rsqrt(ss * (1.0 / D_MODEL) + EPS)
        o_ref[rows, :] = o_ref[rows, :] * scale * g_ref[...]

    @pl.when(i == 0)
    def _():
        for tile in range(ahead):
            start_rows(rows_cur, tile, 0, tm)

    for tile in range(COMBINE_TILES):
        finish(tile)

    @pl.when(i == nt - 1)
    def _():
        for tile in range(ahead):
            tile_wait(tile)


def _combine(rows, x2, gates, g, yb):
    t = x2.shape[0]
    tm = COMBINE_TM
    step = COMBINE_TILES * tm
    nt = t // step
    rows3 = _rows_per_step(rows, step)
    row = lambda i: (i, 0)
    return pl.pallas_call(
        functools.partial(_combine_kernel, tm=tm),
        grid=(nt,),
        in_specs=[pl.BlockSpec((1, 1, step * TOP_K), lambda i: (i, 0, 0), memory_space=pltpu.SMEM),
                  pl.BlockSpec((1, 1, step * TOP_K), lambda i: (jnp.minimum(i + 1, nt - 1), 0, 0),
                               memory_space=pltpu.SMEM),
                  pl.BlockSpec((step, D_MODEL), row), pl.BlockSpec((SUBLANES, step), lambda i: (0, i)),
                  pl.BlockSpec((1, D_MODEL), lambda i: (0, 0)),
                  pl.BlockSpec(memory_space=pl.ANY)],
        out_specs=pl.BlockSpec((step, D_MODEL), row),
        out_shape=jax.ShapeDtypeStruct((t, D_MODEL), F32),
        scratch_shapes=[pltpu.VMEM((COMBINE_TILES, TOP_K, tm * GATHER_PITCH, LANES), F32),
                        pltpu.SemaphoreType.DMA((COMBINE_TILES,))],
        compiler_params=_params(("arbitrary",)),
        name="combine_final_norm",
    )(rows3, rows3, x2, gates, g, yb)


def _rope_tables(seq_len):
    inv_freq = ROPE_THETA ** (-np.arange(ROPE_HALF, dtype=np.float64) * 2.0 / ROPE_DIM)
    ang = np.arange(seq_len, dtype=np.float64)[:, None] * inv_freq[None, :]
    cos = np.cos(ang).astype(np.float32)
    sin = np.sin(ang).astype(np.float32)
    pad = HEAD_DIM - ROPE_DIM
    ones = np.ones((seq_len, pad), np.float32)
    zer_h = np.zeros((seq_len, ROPE_HALF), np.float32)
    zer_p = np.zeros((seq_len, pad), np.float32)
    c = np.concatenate([cos, cos, ones], axis=1)
    s1 = np.concatenate([zer_h, sin, zer_p], axis=1)
    s2 = np.concatenate([-sin, zer_h, zer_p], axis=1)
    rep = LANES // HEAD_DIM
    return tuple(jnp.asarray(np.tile(t, (1, rep))) for t in (c, s1, s2))


def _ssm_weights(a_re, a_im, log_dt, b_re, b_im, c_re, c_im, ssm_d, nsteps):
    r = SSM_CHUNK
    dt = jnp.exp(log_dt)[..., None]
    lr, li = a_re * dt, a_im * dt

    taus = jnp.arange(r + 1, dtype=F32)
    mag = jnp.exp(lr[..., None] * taus)
    p_re, p_im = mag * jnp.cos(li[..., None] * taus), mag * jnp.sin(li[..., None] * taus)

    ab_re, ab_im = p_re[..., 1], p_im[..., 1]
    den = a_re * a_re + a_im * a_im
    num_re, num_im = ab_re - 1.0, ab_im
    f_re = (num_re * a_re + num_im * a_im) / den
    f_im = (num_im * a_re - num_re * a_im) / den
    bb_re = f_re[..., None] * b_re - f_im[..., None] * b_im
    bb_im = f_re[..., None] * b_im + f_im[..., None] * b_re

    m_re = p_re[..., None] * bb_re[:, :, :, None, :] - p_im[..., None] * bb_im[:, :, :, None, :]
    m_im = p_re[..., None] * bb_im[:, :, :, None, :] + p_im[..., None] * bb_re[:, :, :, None, :]
    kern = (jnp.einsum('dgcp,dgptk->dgctk', c_re, m_re) - jnp.einsum('dgcp,dgptk->dgctk', c_im, m_im))
    center = kern[0, :, :, 0] + kern[1, :, :, 0] + jnp.eye(SSM_GROUP, dtype=F32) * ssm_d[:, :, None]
    lags = jnp.concatenate([kern[0, :, :, r - 1:0:-1], center[:, :, None], kern[1, :, :, 1:r]], axis=2)
    strip = lags.reshape(SSM_N_GROUPS, SSM_GROUP, (2 * r - 1) * SSM_GROUP)
    w_toep = jnp.pad(strip, ((0, 0), (0, 0), (0, LAG_STRIP - (2 * r - 1) * SSM_GROUP)))

    flat = lambda a: a.reshape(SSM_N_GROUPS, SSM_STATE, TOEP)
    w_state = jnp.concatenate([flat(m_re[0, :, :, r - 1::-1]), flat(m_im[0, :, :, r - 1::-1]),
                               flat(m_re[1, :, :, :r]), flat(m_im[1, :, :, :r])], axis=1)

    def c_times_pow(d, pr, pi):
        pr, pi = pr.transpose(0, 2, 1)[:, :, None, :], pi.transpose(0, 2, 1)[:, :, None, :]
        zr = c_re[d][:, None] * pr - c_im[d][:, None] * pi
        zi = c_re[d][:, None] * pi + c_im[d][:, None] * pr
        return zr, -zi
    cf_re, cf_im = c_times_pow(0, p_re[0, :, :, 1:r + 1], p_im[0, :, :, 1:r + 1])
    cb_re, cb_im = c_times_pow(1, p_re[1, :, :, r:0:-1], p_im[1, :, :, r:0:-1])
    w_carry = jnp.concatenate([cf_re, cf_im, cb_re, cb_im], axis=3).reshape(SSM_N_GROUPS, TOEP, 4 * SSM_STATE)

    qr, qi = p_re[..., r], p_im[..., r]
    cols = []
    for _ in range(nsteps):
        cols.append(jnp.stack([qr[0], qi[0], qr[1], qi[1]], axis=1))
        qr, qi = qr * qr - qi * qi, 2.0 * qr * qi
    pw = jnp.stack(cols, axis=-1).reshape(SSM_N_GROUPS, 4 * SSM_STATE, nsteps)
    return w_toep, w_state.astype(BF16), w_carry.astype(BF16), pw


def _front(x, prm, other_scatter=None):
    n, seq_len, _ = x.shape
    t = n * seq_len
    x2d = x.reshape(t, D_MODEL)
    q, kv = _qkv(x2d, prm['norm1_g'], prm['w_qkv'], *_rope_tables(seq_len), seq_len)
    if other_scatter is None:
        attn_n, other_buf = _attention(q, kv, prm['sink'], prm['attn_out_g'], seq_len), None
    else:
        attn_n, other_buf = _attention_scatter(q, kv, prm['sink'], prm['attn_out_g'], seq_len, *other_scatter)

    chunks_per_seq = seq_len // SSM_CHUNK
    nsteps = max(1, (chunks_per_seq - 1).bit_length())
    nc = t // SSM_CHUNK
    lt = min(MXU_WIDTH, nc)
    ut = _uproj(x2d, prm['norm1_g'], prm['w_u_t'], lt)
    yt = _ssm(ut, prm['w_toep'], prm['w_state'], prm['w_carry'], prm['pw'][:, :, :nsteps], chunks_per_seq)
    ssm_n = _glu(yt, prm['glu_w_t'], prm['glu_b'], prm['ssm_out_g'], lt)
    outs = _outproj(x2d, attn_n, ssm_n, prm['w_out_a'], prm['w_out_s'], prm['norm2_g'],
                    prm['router_w'], prm['router_b'], prm['tri'])
    return outs, other_buf


def _cumsum_small(x):
    n = x.shape[0]
    keep = jnp.arange(n)[None, :] <= jnp.arange(n)[:, None]
    return jnp.sum(jnp.where(keep, x[None, :], 0), axis=1)


def _by_expert(idx, table):
    hit = idx[None] == jnp.arange(N_EXPERTS, dtype=I32)[:, None, None]
    return jnp.sum(jnp.where(hit, table[:, None, None], 0), axis=0)


def kernel(x_prompt, x_sample, norm1_g, w_in, attn_sink, ssm_a_re, ssm_a_im, ssm_log_dt, ssm_b_re, ssm_b_im, ssm_c_re, ssm_c_im, ssm_d, glu_w, glu_b, attn_out_g, ssm_out_g, w_out, norm2_g, router_w, router_b, w_gate_up, b_gate_up, w_down, b_down, final_g):
    assert norm1_g.shape[0] == 1, "single-layer problem"
    l = 0
    xs = [x_prompt, x_sample]
    max_chunks = max(x.shape[1] for x in xs) // SSM_CHUNK
    max_steps = max(1, (max_chunks - 1).bit_length())
    wq, wk, wv, wu = jnp.split(w_in[l], [ATTN_WIDTH, ATTN_WIDTH + KV_WIDTH, ATTN_WIDTH + 2 * KV_WIDTH], axis=1)
    dup = lambda w: jnp.concatenate([w[:, :HEAD_DIM], w[:, :HEAD_DIM], w[:, HEAD_DIM:], w[:, HEAD_DIM:]], axis=1)
    w_toep, w_state, w_carry, pw = _ssm_weights(ssm_a_re[l], ssm_a_im[l], ssm_log_dt[l], ssm_b_re[l], ssm_b_im[l],
                                                ssm_c_re[l], ssm_c_im[l], ssm_d[l], max_steps)
    tri_i = lax.broadcasted_iota(I32, (ROUTE_TM, ROUTE_TM), 0)
    tri_j = lax.broadcasted_iota(I32, (ROUTE_TM, ROUTE_TM), 1)
    prm = dict(
        norm1_g=norm1_g[l].reshape(1, D_MODEL),
        w_qkv=jnp.concatenate([wq, dup(wk), dup(wv)], axis=1).astype(BF16),
        w_u_t=wu.T.astype(BF16),
        sink=attn_sink[l].astype(F32),
        attn_out_g=attn_out_g[l].reshape(1, ATTN_WIDTH),
        w_toep=w_toep, w_state=w_state, w_carry=w_carry, pw=pw,
        glu_w_t=glu_w[l].T.astype(BF16),
        glu_b=glu_b[l].reshape(SSM_WIDTH, 1),
        ssm_out_g=ssm_out_g[l].reshape(SSM_WIDTH, 1),
        w_out_a=w_out[l][:ATTN_WIDTH].astype(BF16),
        w_out_s=w_out[l][ATTN_WIDTH:].astype(BF16),
        norm2_g=norm2_g[l].reshape(1, D_MODEL),
        router_w=router_w[l].T.astype(BF16),
        router_b=router_b[l].reshape(N_EXPERTS, 1),
        tri=(tri_i < tri_j).astype(BF16),
    )
    def scatter_plan(f):
        cnt = f[4][:, 0].astype(I32)
        pad = (cnt + MOE_ROWS - 1) // MOE_ROWS * MOE_ROWS
        pend = _cumsum_small(pad)
        pstart = pend - pad
        nb_g = f[0].shape[0] * TOP_K // MOE_ROWS + N_EXPERTS
        n_pad = nb_g * MOE_ROWS
        pad_start = jnp.concatenate([pstart + cnt, pend[-1:]]).astype(I32)
        pad_cnt = jnp.concatenate([pad - cnt, n_pad - pend[-1:]]).astype(I32)
        xrows = (_by_expert(f[2][:TOP_K], pstart) + f[2][TOP_K:]).astype(I32)
        return dict(padded=pad, pstart=pstart, nb=nb_g, job=(pad_start, pad_cnt, xrows, f[1], n_pad))

    f_a, _ = _front(xs[0], prm)
    plan_a = scatter_plan(f_a)
    f_b, xbuf_a = _front(xs[1], prm, other_scatter=plan_a['job'])
    plan_b = scatter_plan(f_b)
    xbufs = [xbuf_a, _scatter_rows(*plan_b['job'])]
    fronts, plans = [f_a, f_b], [plan_a, plan_b]
    padded = [p['padded'] for p in plans]
    nbs = [p['nb'] for p in plans]
    seg_blocks = jnp.stack([p // MOE_ROWS for p in padded], axis=1).reshape(-1)
    seg_end = _cumsum_small(seg_blocks)
    seg_start = seg_end - seg_blocks
    nb = sum(nbs)
    bi = jnp.arange(nb, dtype=I32)
    seg = jnp.minimum(jnp.sum(seg_end[None, :] <= bi[:, None], axis=1), 2 * N_EXPERTS - 1).astype(I32)
    block_e = seg // 2
    block_grp = seg % 2
    n_valid = seg_end[-1].astype(I32).reshape(1)
    seg_ids = jnp.arange(2 * N_EXPERTS, dtype=I32)
    in_seg = seg[:, None] == seg_ids[None, :]
    pick = lambda table: jnp.sum(jnp.where(in_seg, table[None, :], 0), axis=1)
    within = bi - pick(seg_start) + 1
    blk = []
    for g in range(2):
        mine = jnp.where(seg_ids % 2 == g, seg_blocks, 0)
        before = _cumsum_small(mine) - mine
        seen = pick(before) + jnp.where(block_grp == g, within, 0)
        blk.append(jnp.maximum(jnp.minimum(seen, jnp.sum(mine)) - 1, 0).astype(I32))

    yrows = [(_by_expert(f[2][:TOP_K], seg_start[g::2] * MOE_ROWS) + f[2][TOP_K:]).astype(I32)
             for g, f in enumerate(fronts)]
    yb = _experts(block_e, block_grp, blk[0], blk[1], n_valid, xbufs[0], xbufs[1],
                  w_gate_up.reshape(N_EXPERTS, D_MODEL, 2 * D_MODEL), b_gate_up[l][:, None, :],
                  w_down.reshape(N_EXPERTS, D_MODEL, D_MODEL), b_down[l][:, None, :])
    gfin = final_g.reshape(1, D_MODEL)
    outs = [_combine(r, f[0], f[3], gfin, yb).reshape(x.shape) for x, f, r in zip(xs, fronts, yrows)]
    return tuple(outs)
```
